```python
import math
import jax, jax.numpy as jnp
from jax import lax
import numpy as np

D_MODEL = 1024
BATCH = 16
SEQ = 256
DEPTH = 2
DEC_BATCH = 8
DEC_SEQ = 2048
PAST_LEN = 512

GRID_W = 64
BRANCH_W = 512
N_BRANCH = 4
H_M = 4
DK_M = 128
DV_M = 128
H_D = 4
DQK_D = 64
DV_D = 128
H_G = 4
DK_G = 64
DV_G = 128
GATE_RANK = 16
GLA_TAU = 16.0
CONV_W = 31
N_GROUPS = 4
EXPERTS_PER_GROUP = 4
TOP_K = 2
D_EXPERT = 512
CHUNK = 64
QBLOCK = 128
ROPE_BASE = 10000.0
EPS = 1e-6
N_MOD = 6

IN_SIZES = (H_M * DK_M, H_M * DK_M, H_M * DV_M, H_M * DV_M, 2 * H_M, 2 * H_M,
            H_D * 2 * DQK_D, H_D * 2 * DQK_D, H_D * DV_D,
            H_G * DK_G, H_G * DK_G, H_G * DV_G, 2 * GATE_RANK, H_G * DV_G,
            BRANCH_W, BRANCH_W,
            N_BRANCH * D_MODEL)
N_IN = sum(IN_SIZES)

kernel_name = 'hybrid_mlstm_diffattn_gla_conv_hmoe_dit_step'


def _split_offsets():
    out, acc = [], 0
    for s in IN_SIZES[:-1]:
        acc += s
        out.append(acc)
    return out


def _rms(x, g):
    xf = x.astype(jnp.float32)
    y = xf * lax.rsqrt(jnp.mean(xf * xf, axis=-1, keepdims=True) + EPS)
    return (y * g.astype(jnp.float32)).astype(x.dtype)


def _layernorm(x, g, b):
    xf = x.astype(jnp.float32)
    mu = jnp.mean(xf, axis=-1, keepdims=True)
    xc = xf - mu
    y = xc * lax.rsqrt(jnp.mean(xc * xc, axis=-1, keepdims=True) + EPS)
    return (y * g.astype(jnp.float32) + b.astype(jnp.float32)).astype(x.dtype)


def _grid_angles(S):
    rows = S // GRID_W
    r, col = jnp.meshgrid(jnp.arange(rows, dtype=jnp.float32), jnp.arange(GRID_W, dtype=jnp.float32), indexing='ij')
    r, col = r.reshape(-1), col.reshape(-1)
    n_freq = DQK_D // 4
    inv = ROPE_BASE ** (-jnp.arange(n_freq, dtype=jnp.float32) / n_freq)
    ang = jnp.concatenate([r[:, None] * inv, col[:, None] * inv], axis=-1)
    return jnp.cos(ang), jnp.sin(ang)


def _rope(x, cos, sin):
    half = x.shape[-1] // 2
    x1, x2 = x[..., :half], x[..., half:]
    shape = (1, cos.shape[0]) + (1,) * (x.ndim - 3) + (half,)
    cs, sn = cos.reshape(shape).astype(x.dtype), sin.reshape(shape).astype(x.dtype)
    return jnp.concatenate([x1 * cs - x2 * sn, x1 * sn + x2 * cs], axis=-1)


def _chunks(a):
    B, H, S = a.shape[:3]
    return jnp.moveaxis(a.reshape((B, H, S // CHUNK, CHUNK) + a.shape[3:]), 2, 0)


def _unchunk(a):
    a = jnp.moveaxis(a, 0, 2)
    return a.reshape((a.shape[0], a.shape[1], -1) + a.shape[4:])


def _mlstm_chunkwise(q, k, v, ig, lf, C0, n0, m0):
    causal = jnp.tril(jnp.ones((CHUNK, CHUNK), bool))

    def step(carry, inp):
        C, n, m = carry
        qc, kc, vc, ic, fc = inp
        b = jnp.cumsum(fc, axis=-1)
        log_d = jnp.where(causal, b[..., :, None] - b[..., None, :] + ic[..., None, :], -jnp.inf)
        log_inter = b + m[..., None]
        m_t = jnp.maximum(log_inter, jnp.max(log_d, axis=-1))
        a_inter = jnp.exp(log_inter - m_t)
        s = jnp.einsum('bhtd,bhsd->bhts', qc, kc) * jnp.exp(log_d - m_t[..., None])
        num = a_inter[..., None] * jnp.einsum('bhtd,bhde->bhte', qc, C) + jnp.einsum('bhts,bhse->bhte', s, vc)
        den = a_inter * jnp.einsum('bhtd,bhd->bht', qc, n) + jnp.sum(s, axis=-1)
        h = num / jnp.maximum(jnp.abs(den), jnp.exp(-m_t))[..., None]
        b_last = b[..., -1]
        log_src = b_last[..., None] - b + ic
        m_new = jnp.maximum(b_last + m, jnp.max(log_src, axis=-1))
        a_c = jnp.exp(b_last + m - m_new)
        w = jnp.exp(log_src - m_new[..., None])
        C = a_c[..., None, None] * C + jnp.einsum('bhs,bhsd,bhse->bhde', w, kc, vc)
        n = a_c[..., None] * n + jnp.einsum('bhs,bhsd->bhd', w, kc)
        return (C, n, m_new), h

    init = (C0.astype(jnp.float32), n0.astype(jnp.float32), m0.astype(jnp.float32))
    (C, n, m), hs = lax.scan(step, init, (_chunks(q), _chunks(k), _chunks(v), _chunks(ig), _chunks(lf)))
    return _unchunk(hs), C, n, m


def _gla_chunkwise(q, k, v, la, S0):
    causal = jnp.tril(jnp.ones((CHUNK, CHUNK), bool))[:, :, None]

    def step(st, inp):
        qc, kc, vc, lc = inp
        g = jnp.cumsum(lc, axis=2)
        dec = jnp.exp(jnp.where(causal, g[:, :, :, None, :] - g[:, :, None, :, :], -jnp.inf))
        att = jnp.einsum('bhtd,bhsd,bhtsd->bhts', qc, kc, dec)
        o = jnp.einsum('bhtd,bhde->bhte', qc * jnp.exp(g), st) + jnp.einsum('bhts,bhse->bhte', att, vc)
        g_last = g[:, :, -1:, :]
        st = jnp.exp(g_last[:, :, 0, :, None]) * st + jnp.einsum('bhsd,bhse->bhde', kc * jnp.exp(g_last - g), vc)
        return st, o

    S_fin, o = lax.scan(step, S0.astype(jnp.float32), (_chunks(q), _chunks(k), _chunks(v), _chunks(la)))
    return _unchunk(o), S_fin


def _diff_attention(q, k, v, lam):
    B, Sq, H = q.shape[:3]
    nb = Sq // QBLOCK
    qb = jnp.moveaxis(q.reshape(B, nb, QBLOCK, H, 2, DQK_D), 1, 0)
    scale = DQK_D ** -0.5

    def block(qi):
        s = jnp.einsum('bqhcd,bkhcd->bhcqk', qi, k).astype(jnp.float32) * scale
        pr = jax.nn.softmax(s, axis=-1)
        w = (pr[:, :, 0] - lam * pr[:, :, 1]).astype(v.dtype)
        return jnp.einsum('bhqk,bkhe->bqhe', w, v)

    o = lax.map(block, qb)
    return jnp.moveaxis(o, 0, 1).reshape(B, Sq, H, DV_D)


def _mixer(h, p, lam_init, ctx):
    B, S, _ = h.shape
    dt = h.dtype
    proj = h @ p['w_in']
    (m_q, m_k, m_v, m_o, m_i, m_f, d_q, d_k, d_v,
     g_q, g_k, g_v, g_a, g_r, c_a, c_b, gate) = jnp.split(proj, _split_offsets(), axis=-1)
    flip = lambda a: jnp.flip(a, axis=2)
    heads = lambda a, n: a.reshape(B, S, n, -1).transpose(0, 2, 1, 3).astype(jnp.float32)

    mq = heads(m_q, H_M)
    mk = heads(m_k, H_M) * DK_M ** -0.5
    mv = heads(m_v, H_M)
    mi = (m_i.reshape(B, S, 2, H_M) + p['b_m_i'].reshape(2, H_M)).astype(jnp.float32).transpose(0, 2, 3, 1)
    mf = jax.nn.log_sigmoid((m_f.reshape(B, S, 2, H_M) + p['b_m_f'].reshape(2, H_M)).astype(jnp.float32)).transpose(0, 2, 3, 1)
    if ctx is None:
        C0 = jnp.zeros((B, 2, H_M, DK_M, DV_M), jnp.float32)
        n0 = jnp.zeros((B, 2, H_M, DK_M), jnp.float32)
        m0 = jnp.zeros((B, 2, H_M), jnp.float32)
        S0 = jnp.zeros((B, 2, H_G, DK_G, DV_G), jnp.float32)
    else:
        C0, n0, m0, S0 = ctx['C'], ctx['n'], ctx['m'], ctx['S']
    hf, Cf, nf, mf_fin = _mlstm_chunkwise(mq, mk, mv, mi[:, 0], mf[:, 0], C0[:, 0], n0[:, 0], m0[:, 0])
    hb, Cb, nb, mb_fin = _mlstm_chunkwise(flip(mq), flip(mk), flip(mv), flip(mi[:, 1]), flip(mf[:, 1]),
                                          C0[:, 1], n0[:, 1], m0[:, 1])
    hm = (hf + flip(hb)).transpose(0, 2, 1, 3)
    y_m = _rms(hm, p['hnorm_m'].reshape(H_M, DV_M)).reshape(B, S, BRANCH_W).astype(dt) * jax.nn.sigmoid(m_o)

    dq = d_q.reshape(B, S, H_D, 2, DQK_D)
    dk = d_k.reshape(B, S, H_D, 2, DQK_D)
    dv = d_v.reshape(B, S, H_D, DV_D)
    if ctx is None:
        k_all, v_all = dk, dv
    else:
        cos, sin = _grid_angles(S)
        dq = _rope(dq, cos, sin)
        n_ctx = ctx['k'].shape[1]
        k_all = jnp.concatenate([_rope(dk, cos, sin), ctx['k'].reshape(B, n_ctx, H_D, 2, DQK_D).astype(dk.dtype)], axis=1)
        v_all = jnp.concatenate([dv, ctx['v'].astype(dv.dtype)], axis=1)
    lq1, lk1 = p['lam_q1'].astype(jnp.float32), p['lam_k1'].astype(jnp.float32)
    lq2, lk2 = p['lam_q2'].astype(jnp.float32), p['lam_k2'].astype(jnp.float32)
    lam = jnp.exp(jnp.sum(lq1 * lk1)) - jnp.exp(jnp.sum(lq2 * lk2)) + lam_init
    od = _diff_attention(dq, k_all, v_all, lam)
    y_d = (_rms(od, p['hnorm_d'].reshape(H_D, DV_D)) * (1.0 - lam_init)).reshape(B, S, BRANCH_W).astype(dt)

    gq = heads(g_q, H_G) * DK_G ** -0.5
    gk = heads(g_k, H_G)
    gv = heads(g_v, H_G)
    la = jnp.einsum('bsjr,jrk->bsjk', g_a.reshape(B, S, 2, GATE_RANK), p['w_gla_up']) + p['b_gla_gate']
    la = jax.nn.log_sigmoid(la.astype(jnp.float32)) / GLA_TAU
    la = la.reshape(B, S, 2, H_G, DK_G).transpose(0, 2, 3, 1, 4)
    of, Sf = _gla_chunkwise(gq, gk, gv, la[:, 0], S0[:, 0])
    ob, Sb = _gla_chunkwise(flip(gq), flip(gk), flip(gv), flip(la[:, 1]), S0[:, 1])
    og = (of + flip(ob)).transpose(0, 2, 1, 3)
    y_g = _rms(og, p['hnorm_g'].reshape(H_G, DV_G)).reshape(B, S, BRANCH_W).astype(dt) * jax.nn.silu(g_r)

    u = c_a * jax.nn.sigmoid(c_b)
    u = lax.conv_general_dilated(u, p['w_dw'][:, None, :].astype(u.dtype), window_strides=(1,),
                                 padding=[(CONV_W // 2, CONV_W // 2)],
                                 dimension_numbers=('NWC', 'WIO', 'NWC'), feature_group_count=BRANCH_W)
    y_c = jax.nn.silu(_layernorm(u, p['conv_ln_g'], p['conv_ln_b'])).astype(dt)

    ys = jnp.stack([y_m, y_d, y_g, y_c], axis=2)
    br = jnp.einsum('bsnw,nwd->bsnd', ys, p['w_branch'])
    gts = jax.nn.sigmoid(gate.reshape(B, S, N_BRANCH, D_MODEL))
    out = jnp.einsum('bsnd,bsnd->bsd', gts, br) @ p['w_out']
    if ctx is None:
        state = (d_k.reshape(B, S, H_D, 2 * DQK_D), dv,
                 jnp.stack([Cf, Cb], axis=1), jnp.stack([nf, nb], axis=1),
                 jnp.stack([mf_fin, mb_fin], axis=1), jnp.stack([Sf, Sb], axis=1))
        return out, state
    return out, None


def _hier_moe(h, p):
    B, S, _ = h.shape
    gl = (h @ p['w_group_router']).astype(jnp.float32)
    g_sel = jax.nn.one_hot(jnp.argmax(gl, axis=-1), N_GROUPS, dtype=jnp.float32)
    g_p = jnp.sum(jax.nn.softmax(gl, axis=-1) * g_sel, axis=-1)
    el = (h @ p['w_expert_router']).astype(jnp.float32).reshape(B, S, N_GROUPS, EXPERTS_PER_GROUP)
    el_sel = jnp.einsum('bsge,bsg->bse', el, g_sel)
    top_v, top_i = lax.top_k(el_sel, TOP_K)
    top_w = jax.nn.softmax(top_v, axis=-1) * g_p[..., None]
    w_loc = jnp.einsum('bsk,bske->bse', top_w, jax.nn.one_hot(top_i, EXPERTS_PER_GROUP, dtype=jnp.float32))
    w_all = (g_sel[..., None] * w_loc[..., None, :]).astype(h.dtype)
    y = jnp.zeros_like(h)
    for gi in range(N_GROUPS):
        a = jnp.einsum('bsd,edf->bsef', h, p['w_e1'][gi])
        b = jnp.einsum('bsd,edf->bsef', h, p['w_e3'][gi])
        y = y + jnp.einsum('bsef,efd->bsd', jax.nn.silu(a) * b * w_all[:, :, gi, :, None], p['w_e2'][gi])
    return y


def _layer(x, cond, p, lam_init, ctx):
    mod = (jax.nn.silu(cond) @ p['w_mod'] + p['b_mod'])[:, None, :]
    sh1, sc1, g1, sh2, sc2, g2 = jnp.split(mod, N_MOD, axis=-1)
    h = _rms(x, p['norm1']) * (1 + sc1) + sh1
    out, state = _mixer(h, p, lam_init, ctx)
    x = x + g1 * out
    h = _rms(x, p['norm2']) * (1 + sc2) + sh2
    x = x + g2 * _hier_moe(h, p)
    return x, state


def setup_inputs(seed: int = 0) -> dict:
    key = jax.random.key(seed)
    ks = iter(jax.random.split(key, 48))
    nrm = lambda shape, s: jax.random.normal(next(ks), shape, jnp.float32) * s
    D = D_MODEL
    inp = {}
    inp['x_prompt'] = nrm((BATCH, SEQ, D), 1.0)
    inp['x_sample'] = nrm((DEC_BATCH, DEC_SEQ, D), 1.0)
    inp['c'] = nrm((DEC_BATCH, D), 1.0)
    inp['cache_diff_k'] = nrm((DEC_BATCH, DEPTH, PAST_LEN, H_D, 2 * DQK_D), 1.0)
    inp['cache_diff_v'] = nrm((DEC_BATCH, DEPTH, PAST_LEN, H_D, DV_D), 1.0)
    inp['state_mlstm_C'] = nrm((DEC_BATCH, DEPTH, 2, H_M, DK_M, DV_M), 0.5)
    inp['state_mlstm_n'] = nrm((DEC_BATCH, DEPTH, 2, H_M, DK_M), 0.5)
    inp['state_mlstm_m'] = nrm((DEC_BATCH, DEPTH, 2, H_M), 0.5)
    inp['state_gla_S'] = nrm((DEC_BATCH, DEPTH, 2, H_G, DK_G, DV_G), 0.5)
    inp['c_ctx'] = nrm((D,), 1.0)
    inp['w_mod'] = nrm((DEPTH, D, N_MOD * D), 0.5 * D ** -0.5)
    inp['b_mod'] = nrm((DEPTH, N_MOD * D), 0.02)
    inp['norm1'] = 1.0 + nrm((DEPTH, D), 0.02)
    inp['w_in'] = nrm((DEPTH, D, N_IN), D ** -0.5)
    inp['b_m_i'] = nrm((DEPTH, 2 * H_M), 0.1)
    inp['b_m_f'] = 3.0 + nrm((DEPTH, 2 * H_M), 0.5)
    inp['lam_q1'] = nrm((DEPTH, DQK_D), 0.1)
    inp['lam_k1'] = nrm((DEPTH, DQK_D), 0.1)
    inp['lam_q2'] = nrm((DEPTH, DQK_D), 0.1)
    inp['lam_k2'] = nrm((DEPTH, DQK_D), 0.1)
    inp['w_gla_up'] = nrm((DEPTH, 2, GATE_RANK, H_G * DK_G), GATE_RANK ** -0.5)
    inp['b_gla_gate'] = nrm((DEPTH, 2, H_G * DK_G), 0.1)
    inp['w_dw'] = nrm((DEPTH, CONV_W, BRANCH_W), CONV_W ** -0.5)
    inp['conv_ln_g'] = 1.0 + nrm((DEPTH, BRANCH_W), 0.02)
    inp['conv_ln_b'] = nrm((DEPTH, BRANCH_W), 0.02)
    inp['hnorm_m'] = 1.0 + nrm((DEPTH, BRANCH_W), 0.02)
    inp['hnorm_d'] = 1.0 + nrm((DEPTH, BRANCH_W), 0.02)
    inp['hnorm_g'] = 1.0 + nrm((DEPTH, BRANCH_W), 0.02)
    inp['w_branch'] = nrm((DEPTH, N_BRANCH, BRANCH_W, D), BRANCH_W ** -0.5)
    inp['w_out'] = nrm((DEPTH, D, D), D ** -0.5)
    inp['norm2'] = 1.0 + nrm((DEPTH, D), 0.02)
    inp['w_group_router'] = nrm((DEPTH, D, N_GROUPS), D ** -0.5)
    inp['w_expert_router'] = nrm((DEPTH, D, N_GROUPS * EXPERTS_PER_GROUP), D ** -0.5)
    inp['w_e1'] = nrm((DEPTH, N_GROUPS, EXPERTS_PER_GROUP, D, D_EXPERT), D ** -0.5)
    inp['w_e3'] = nrm((DEPTH, N_GROUPS, EXPERTS_PER_GROUP, D, D_EXPERT), D ** -0.5)
    inp['w_e2'] = nrm((DEPTH, N_GROUPS, EXPERTS_PER_GROUP, D_EXPERT, D), D_EXPERT ** -0.5)
    inp['final_norm'] = 1.0 + nrm((D,), 0.02)
    return inp


def reference(x_prompt, x_sample, c, cache_diff_k, cache_diff_v, state_mlstm_C, state_mlstm_n,
              state_mlstm_m, state_gla_S, c_ctx, w_mod, b_mod, norm1, w_in, b_m_i, b_m_f,
              lam_q1, lam_k1, lam_q2, lam_k2, w_gla_up, b_gla_gate, w_dw, conv_ln_g, conv_ln_b,
              hnorm_m, hnorm_d, hnorm_g, w_branch, w_out, norm2, w_group_router, w_expert_router,
              w_e1, w_e3, w_e2, final_norm):
    cond_ctx = jnp.broadcast_to(c_ctx, (x_prompt.shape[0], D_MODEL))
    yp, ys = x_prompt, x_sample
    ks, vs, Cs, ns, ms, Ss = [], [], [], [], [], []
    for l in range(DEPTH):
        p = {'w_mod': w_mod[l], 'b_mod': b_mod[l], 'norm1': norm1[l], 'w_in': w_in[l],
             'b_m_i': b_m_i[l], 'b_m_f': b_m_f[l], 'lam_q1': lam_q1[l], 'lam_k1': lam_k1[l],
             'lam_q2': lam_q2[l], 'lam_k2': lam_k2[l], 'w_gla_up': w_gla_up[l], 'b_gla_gate': b_gla_gate[l],
             'w_dw': w_dw[l], 'conv_ln_g': conv_ln_g[l], 'conv_ln_b': conv_ln_b[l],
             'hnorm_m': hnorm_m[l], 'hnorm_d': hnorm_d[l], 'hnorm_g': hnorm_g[l],
             'w_branch': w_branch[l], 'w_out': w_out[l], 'norm2': norm2[l],
             'w_group_router': w_group_router[l], 'w_expert_router': w_expert_router[l],
             'w_e1': w_e1[l], 'w_e3': w_e3[l], 'w_e2': w_e2[l]}
        lam_init = 0.8 - 0.6 * math.exp(-0.3 * l)
        yp, st = _layer(yp, cond_ctx, p, lam_init, None)
        ks.append(st[0]); vs.append(st[1]); Cs.append(st[2]); ns.append(st[3]); ms.append(st[4]); Ss.append(st[5])
        ctx = {'k': cache_diff_k[:, l], 'v': cache_diff_v[:, l], 'C': state_mlstm_C[:, l],
               'n': state_mlstm_n[:, l], 'm': state_mlstm_m[:, l], 'S': state_gla_S[:, l]}
        ys, _ = _layer(ys, c, p, lam_init, ctx)
    y_prompt = _rms(yp, final_norm)
    y_sample = _rms(ys, final_norm)
    return (y_prompt, y_sample, jnp.stack(ks, axis=1), jnp.stack(vs, axis=1), jnp.stack(Cs, axis=1),
            jnp.stack(ns, axis=1), jnp.stack(ms, axis=1), jnp.stack(Ss, axis=1))
```

```python
import functools
import math

import jax
import jax.numpy as jnp
from jax import lax
from jax.experimental import pallas as pl
from jax.experimental.pallas import tpu as pltpu

F32 = jnp.float32
BF16 = jnp.bfloat16

D_MODEL = 1024
DEPTH = 2
GRID_W = 64
BRANCH_W = 512
N_BRANCH = 4
H_M, DK_M, DV_M = 4, 128, 128
H_D, DQK_D, DV_D = 4, 64, 128
H_G, DK_G, DV_G = 4, 64, 128
GATE_RANK = 16
GLA_TAU = 16.0
CONV_W = 31
N_GROUPS, EXPERTS_PER_GROUP, D_EXPERT = 4, 4, 512
N_EXPERTS = N_GROUPS * EXPERTS_PER_GROUP
ROPE_BASE = 10000.0
EPS = 1e-6
N_MOD = 6

LANES = 128
VMEM_LIMIT = 48 * 1024 * 1024

A16_MQ, A16_MK, A16_MV, A16_MO = 0, 512, 1024, 1536
A16_GATE, A16_GR, A16_CA, A16_CB = 2048, 6144, 6656, 7168
N_A16 = 7680
A32_DQ, A32_DK, A32_DV, A32_GQK, A32_GV, A32_SM = 0, 512, 1024, 1536, 2048, 2560
N_A32 = 2688
SM_MI, SM_MF, SM_GA = 0, 8, 16

MLSTM_CHUNK = 128
GLA_CHUNK = 64
GLA_SUB = 16
GLA_EXP_CLAMP = 80.0
CONV_ROWS = 64
CONV_PAD = 16


def _cparams(*sem):
    return pltpu.CompilerParams(dimension_semantics=sem, vmem_limit_bytes=VMEM_LIMIT)


def _log_sigmoid(x):
    return jnp.minimum(x, 0.0) - jnp.log1p(jnp.exp(-jnp.abs(x)))


def _sigmoid(x):
    return 1.0 / (1.0 + jnp.exp(-x))


def _dot(a, b):
    return jnp.dot(a, b, preferred_element_type=F32)


def _dot_nt(a, b):
    return lax.dot_general(a, b, (((1,), (1,)), ((), ())), preferred_element_type=F32)


def _dot_tn(a, b):
    return lax.dot_general(a, b, (((0,), (0,)), ((), ())), preferred_element_type=F32)


def _mod_kernel(c_ref, w_ref, b_ref, o_ref):
    c = c_ref[...]
    a = (c * _sigmoid(c)).astype(BF16)
    o_ref[...] = _dot(a, w_ref[...].astype(BF16)) + b_ref[...]


def _modulation(cond, w_mod, b_mod):
    R = cond.shape[0]
    tn = 512
    nmod = N_MOD * D_MODEL
    return pl.pallas_call(
        _mod_kernel,
        grid=(DEPTH, nmod // tn),
        in_specs=[
            pl.BlockSpec((R, D_MODEL), lambda l, j: (0, 0)),
            pl.BlockSpec((None, D_MODEL, tn), lambda l, j: (l, 0, j)),
            pl.BlockSpec((None, 1, tn), lambda l, j: (l, 0, j)),
        ],
        out_specs=pl.BlockSpec((None, R, tn), lambda l, j: (l, 0, j)),
        out_shape=jax.ShapeDtypeStruct((DEPTH, R, nmod), F32),
        compiler_params=_cparams("parallel", "parallel"),
        name="adaln_mod",
    )(cond, w_mod, b_mod.reshape(DEPTH, 1, nmod))


def _inproj_kernel(x_ref, mod_ref, g_ref, w_ref, o_ref, h_ref):
    @pl.when(pl.program_id(1) == 0)
    def _():
        x = x_ref[...]
        y = x * lax.rsqrt(jnp.mean(x * x, axis=-1, keepdims=True) + EPS) * g_ref[...]
        h_ref[...] = (y * (1.0 + mod_ref[1:2, :]) + mod_ref[0:1, :]).astype(BF16)

    o_ref[...] = _dot(h_ref[...], w_ref[...]).astype(o_ref.dtype)


def _inproj(x2d, mod, g, w, out_dtype, rows_per_mod, tm, tn):
    T = x2d.shape[0]
    N = w.shape[1]
    return pl.pallas_call(
        _inproj_kernel,
        grid=(T // tm, N // tn),
        in_specs=[
            pl.BlockSpec((tm, D_MODEL), lambda i, j: (i, 0)),
            pl.BlockSpec((None, N_MOD, D_MODEL), lambda i, j: ((i * tm) // rows_per_mod, 0, 0)),
            pl.BlockSpec((1, D_MODEL), lambda i, j: (0, 0)),
            pl.BlockSpec((D_MODEL, tn), lambda i, j: (0, j)),
        ],
        out_specs=pl.BlockSpec((tm, tn), lambda i, j: (i, j)),
        out_shape=jax.ShapeDtypeStruct((T, N), out_dtype),
        scratch_shapes=[pltpu.VMEM((tm, D_MODEL), BF16)],
        compiler_params=_cparams("parallel", "arbitrary"),
        name="norm_inproj",
    )(x2d, mod, g, w)


def _mlstm_kernel(q_ref, k_ref, v_ref, og_ref, gc_ref, gr_ref, bc_ref, br_ref, c0_ref, n0_ref,
                  m0_ref, hn_ref, y_ref, c_out_ref, n_out_ref, m_out_ref, hacc_ref, *, L, S):
    nch = S // L
    scale = DK_M ** -0.5
    ti = lax.broadcasted_iota(jnp.int32, (L, L), 0)
    si = lax.broadcasted_iota(jnp.int32, (L, L), 1)
    lower = si <= ti
    upper = si >= ti
    for d in range(2):
        rev = d == 1
        mask = upper if rev else lower
        mask_t = lower if rev else upper

        def body(ci, carry, rev=rev, mask=mask, mask_t=mask_t, d=d):
            C, n, m = carry
            c = (nch - 1 - ci) if rev else ci
            r0 = pl.multiple_of(c * L, L)
            q = q_ref[pl.ds(r0, L), :]
            k = k_ref[pl.ds(r0, L), :]
            v = v_ref[pl.ds(r0, L), :]
            gcol = gc_ref[pl.ds(r0, L), :] + bc_ref[...]
            grow = gr_ref[c] + br_ref[...]
            i_col = gcol[:, d:d + 1]
            f_col = _log_sigmoid(gcol[:, 2 + d:3 + d])
            i_row = grow[d:d + 1, :]
            f_row = _log_sigmoid(grow[2 + d:3 + d, :])
            b_col = jnp.sum(jnp.where(mask, f_row, 0.0), axis=1, keepdims=True)
            b_row = jnp.sum(jnp.where(mask_t, f_col, 0.0), axis=0, keepdims=True)
            log_d = jnp.where(mask, b_col - b_row + i_row, -jnp.inf)
            m_t = jnp.maximum(b_col + m, jnp.max(log_d, axis=1, keepdims=True))
            dmat = jnp.exp(log_d - m_t)
            smat = _dot_nt(q, k) * scale * dmat
            a_int = jnp.exp(b_col + m - m_t)
            num = a_int * _dot(q, C.astype(BF16)) + _dot(smat.astype(BF16), v)
            den = (a_int * jnp.sum(q.astype(F32) * n, axis=1, keepdims=True)
                   + jnp.sum(smat, axis=1, keepdims=True))
            h = num / jnp.maximum(jnp.abs(den), jnp.exp(-m_t))
            if rev:
                hacc_ref[pl.ds(r0, L), :] += h
            else:
                hacc_ref[pl.ds(r0, L), :] = h
            b_last = jnp.sum(f_row, axis=1, keepdims=True)
            ls_row = b_last - b_row + i_row
            ls_col = b_last - b_col + i_col
            m_new = jnp.maximum(b_last + m, jnp.max(ls_row, axis=1, keepdims=True))
            a_c = jnp.exp(b_last + m - m_new)
            kw = jnp.exp(ls_col - m_new) * k.astype(F32)
            C = a_c * C + scale * _dot_tn(kw.astype(BF16), v)
            n = a_c * n + scale * jnp.sum(kw, axis=0, keepdims=True)
            return C, n, m_new

        C, n, m = lax.fori_loop(0, nch, body, (c0_ref[d], n0_ref[d], m0_ref[d][:, 0:1]))
        c_out_ref[d] = C
        n_out_ref[d] = n
        m_out_ref[d] = jnp.broadcast_to(m, (1, LANES))

    hm = hacc_ref[...]
    y = hm * lax.rsqrt(jnp.mean(hm * hm, axis=-1, keepdims=True) + EPS) * hn_ref[...]
    y_ref[...] = (y * _sigmoid(og_ref[...].astype(F32))).astype(y_ref.dtype)


def _mlstm(a16, gcol, grow, bcol, brow, c0, n0, m0, hnorm, B, S):
    L = min(MLSTM_CHUNK, S)
    nch = S // L
    cb = lambda off: off // LANES
    kern = functools.partial(_mlstm_kernel, L=L, S=S)
    return pl.pallas_call(
        kern,
        grid=(B, H_M),
        in_specs=[
            pl.BlockSpec((S, LANES), lambda b, h: (b, cb(A16_MQ) + h)),
            pl.BlockSpec((S, LANES), lambda b, h: (b, cb(A16_MK) + h)),
            pl.BlockSpec((S, LANES), lambda b, h: (b, cb(A16_MV) + h)),
            pl.BlockSpec((S, LANES), lambda b, h: (b, cb(A16_MO) + h)),
            pl.BlockSpec((None, S, 4), lambda b, h: (h, b, 0)),
            pl.BlockSpec((None, nch, 4, L), lambda b, h: (h, b, 0, 0)),
            pl.BlockSpec((None, 1, 4), lambda b, h: (h, 0, 0)),
            pl.BlockSpec((None, 4, 1), lambda b, h: (h, 0, 0)),
            pl.BlockSpec((None, 2, None, DK_M, DV_M), lambda b, h: (b, 0, h, 0, 0)),
            pl.BlockSpec((None, 2, None, 1, DK_M), lambda b, h: (b, 0, h, 0, 0)),
            pl.BlockSpec((None, 2, None, 1, LANES), lambda b, h: (b, 0, h, 0, 0)),
            pl.BlockSpec((1, LANES), lambda b, h: (0, h)),
        ],
        out_specs=[
            pl.BlockSpec((S, LANES), lambda b, h: (b, h)),
            pl.BlockSpec((None, 2, None, DK_M, DV_M), lambda b, h: (b, 0, h, 0, 0)),
            pl.BlockSpec((None, 2, None, 1, DK_M), lambda b, h: (b, 0, h, 0, 0)),
            pl.BlockSpec((None, 2, None, 1, LANES), lambda b, h: (b, 0, h, 0, 0)),
        ],
        out_shape=[
            jax.ShapeDtypeStruct((B * S, BRANCH_W), BF16),
            jax.ShapeDtypeStruct((B, 2, H_M, DK_M, DV_M), F32),
            jax.ShapeDtypeStruct((B, 2, H_M, 1, DK_M), F32),
            jax.ShapeDtypeStruct((B, 2, H_M, 1, LANES), F32),
        ],
        scratch_shapes=[pltpu.VMEM((S, DV_M), F32)],
        compiler_params=_cparams("parallel", "parallel"),
        name="mlstm",
    )(a16, a16, a16, a16, gcol, grow, bcol, brow, c0, n0, m0, hnorm)


def _gla_kernel(qk_ref, v_ref, sm_ref, wup_ref, bup_ref, s0_ref, gr_ref, hn_ref,
                y_ref, s_out_ref, oacc_ref, *, L, S):
    nch = S // L
    nb = L // GLA_SUB
    ti = lax.broadcasted_iota(jnp.int32, (L, L), 0)
    si = lax.broadcasted_iota(jnp.int32, (L, L), 1)
    row = lax.broadcasted_iota(jnp.int32, (L, LANES), 0)
    lane = lax.broadcasted_iota(jnp.int32, (L, LANES), 1)
    lo_half = lane < DK_G
    eye = (lax.broadcasted_iota(jnp.int32, (DK_G, DK_G), 0)
           == lax.broadcasted_iota(jnp.int32, (DK_G, DK_G), 1))
    for d in range(2):
        rev = d == 1
        mask = (si >= ti) if rev else (si <= ti)
        tri = mask.astype(F32)

        def body(ci, st, rev=rev, mask=mask, tri=tri, d=d):
            c = (nch - 1 - ci) if rev else ci
            r0 = pl.multiple_of(c * L, L)
            qk = qk_ref[pl.ds(r0, L), :]
            qk_sw = pltpu.roll(qk, DK_G, 1)
            q2 = jnp.where(lo_half, qk, qk_sw) * (DK_G ** -0.5)
            k2 = jnp.where(lo_half, qk_sw, qk)
            v = v_ref[pl.ds(r0, L), :].astype(BF16)
            x2 = _dot(sm_ref[pl.ds(r0, L), :].astype(BF16), wup_ref[d]) + bup_ref[d]
            la2 = _log_sigmoid(x2) * (1.0 / GLA_TAU)
            g2 = jnp.dot(tri, la2, preferred_element_type=F32, precision=lax.Precision.HIGHEST)
            g = g2[:, :DK_G]
            o = _dot((q2 * jnp.exp(g2))[:, :DK_G].astype(BF16), st.astype(BF16))
            a_parts, b_parts = [], []
            for p in range(nb // 2):
                ia, ib = 2 * p, 2 * p + 1
                ra = ia * GLA_SUB + (GLA_SUB - 1 if rev else 0)
                rb = ib * GLA_SUB + (GLA_SUB - 1 if rev else 0)
                ref2 = jnp.where(lo_half, g2[ra:ra + 1, :], g2[rb:rb + 1, :])
                blk = jnp.where(lo_half, ia, ib)
                in_blk = (row // GLA_SUB) == blk
                key_ok = ((row // GLA_SUB) >= blk) if rev else ((row // GLA_SUB) <= blk)
                a_parts.append(jnp.where(in_blk, q2 * jnp.exp(jnp.minimum(g2 - ref2, 0.0)), 0.0))
                b_parts.append(jnp.where(
                    key_ok, k2 * jnp.exp(jnp.minimum(ref2 - g2, GLA_EXP_CLAMP)), 0.0))
            a_big = jnp.concatenate(a_parts, axis=1).astype(BF16)
            b_big = jnp.concatenate(b_parts, axis=1).astype(BF16)
            att = jnp.where(mask, _dot_nt(a_big, b_big), 0.0)
            o = o + _dot(att.astype(BF16), v)
            if rev:
                oacc_ref[pl.ds(r0, L), :] += o
            else:
                oacc_ref[pl.ds(r0, L), :] = o
            gl_row = 0 if rev else L - 1
            glast = g[gl_row:gl_row + 1, :]
            kd = k2[:, :DK_G] * jnp.exp(glast - g)
            glast_col = jnp.sum(jnp.where(eye, glast, 0.0), axis=1, keepdims=True)
            return jnp.exp(glast_col) * st + _dot_tn(kd.astype(BF16), v)

        st = lax.fori_loop(0, nch, body, s0_ref[d])
        s_out_ref[d] = st

    og = oacc_ref[...]
    y = og * lax.rsqrt(jnp.mean(og * og, axis=-1, keepdims=True) + EPS) * hn_ref[...]
    gr = gr_ref[...].astype(F32)
    y_ref[...] = (y * (gr * _sigmoid(gr))).astype(y_ref.dtype)


def _gla(a32, a16, wup, bup, s0, hnorm, B, S):
    L = min(GLA_CHUNK, S)
    cb = lambda off: off // LANES
    kern = functools.partial(_gla_kernel, L=L, S=S)
    return pl.pallas_call(
        kern,
        grid=(B, H_G),
        in_specs=[
            pl.BlockSpec((S, LANES), lambda b, h: (b, cb(A32_GQK) + h)),
            pl.BlockSpec((S, LANES), lambda b, h: (b, cb(A32_GV) + h)),
            pl.BlockSpec((S, LANES), lambda b, h: (b, cb(A32_SM))),
            pl.BlockSpec((None, 2, LANES, LANES), lambda b, h: (h, 0, 0, 0)),
            pl.BlockSpec((None, 2, 1, LANES), lambda b, h: (h, 0, 0, 0)),
            pl.BlockSpec((None, 2, None, DK_G, DV_G), lambda b, h: (b, 0, h, 0, 0)),
            pl.BlockSpec((S, LANES), lambda b, h: (b, cb(A16_GR) + h)),
            pl.BlockSpec((1, LANES), lambda b, h: (0, h)),
        ],
        out_specs=[
            pl.BlockSpec((S, LANES), lambda b, h: (b, h)),
            pl.BlockSpec((None, 2, None, DK_G, DV_G), lambda b, h: (b, 0, h, 0, 0)),
        ],
        out_shape=[
            jax.ShapeDtypeStruct((B * S, BRANCH_W), BF16),
            jax.ShapeDtypeStruct((B, 2, H_G, DK_G, DV_G), F32),
        ],
        scratch_shapes=[pltpu.VMEM((S, DV_G), F32)],
        compiler_params=_cparams("parallel", "parallel"),
        name="gla",
    )(a32, a32, a32, wup, bup, s0, a16, hnorm)


def _rope(x, cos, sin_signed):
    lane = lax.broadcasted_iota(jnp.int32, x.shape, 1)
    first = (lane % DQK_D) < (DQK_D // 2)
    partner = jnp.where(first, pltpu.roll(x, LANES - DQK_D // 2, 1), pltpu.roll(x, DQK_D // 2, 1))
    return x * cos + partner * sin_signed


def _attn_kernel(*refs, S, P, TQ, lam_init, has_ctx):
    if has_ctx:
        (q_ref, k_ref, v_ref, ck_ref, cv_ref, cos_ref, sin_ref, lam_ref, hn_ref,
         y_ref, kk_ref, vv_ref) = refs
    else:
        q_ref, k_ref, v_ref, lam_ref, hn_ref, y_ref, kk_ref, vv_ref = refs
    qi = pl.program_id(2)

    @pl.when(qi == 0)
    def _():
        k = k_ref[...]
        if has_ctx:
            k = _rope(k, cos_ref[...], sin_ref[...])
            kk_ref[S:S + P, :] = ck_ref[...].astype(BF16)
            vv_ref[S:S + P, :] = cv_ref[...].astype(BF16)
        kk_ref[0:S, :] = k.astype(BF16)
        vv_ref[0:S, :] = v_ref[...].astype(BF16)

    q = q_ref[...]
    if has_ctx:
        r0 = pl.multiple_of(qi * TQ, TQ)
        q = _rope(q, cos_ref[pl.ds(r0, TQ), :], sin_ref[pl.ds(r0, TQ), :])
    q = q * (DQK_D ** -0.5)
    lane = lax.broadcasted_iota(jnp.int32, q.shape, 1)
    kk = kk_ref[...]
    vv = vv_ref[...]
    outs = []
    for comp in range(2):
        sel = (lane < DQK_D) if comp == 0 else (lane >= DQK_D)
        s = _dot_nt(jnp.where(sel, q, 0.0).astype(BF16), kk)
        e = jnp.exp(s - jnp.max(s, axis=-1, keepdims=True))
        l = jnp.sum(e, axis=-1, keepdims=True)
        outs.append(_dot(e.astype(BF16), vv) / l)
    lv = lam_ref[...]
    lam = (jnp.exp(jnp.sum(lv[0:1, :] * lv[1:2, :], axis=-1, keepdims=True))
           - jnp.exp(jnp.sum(lv[2:3, :] * lv[3:4, :], axis=-1, keepdims=True)) + lam_init)
    o = outs[0] - lam * outs[1]
    y = o * lax.rsqrt(jnp.mean(o * o, axis=-1, keepdims=True) + EPS) * hn_ref[...]
    y_ref[...] = (y * (1.0 - lam_init)).astype(y_ref.dtype)


def _attn(a32, lamv, hnorm, B, S, lam_init, ctx=None):
    TQ = min(256, S)
    nq = S // TQ
    has_ctx = ctx is not None
    P = ctx[0].shape[2] if has_ctx else 0
    cb = lambda off: off // LANES
    kern = functools.partial(_attn_kernel, S=S, P=P, TQ=TQ, lam_init=lam_init, has_ctx=has_ctx)
    in_specs = [
        pl.BlockSpec((TQ, LANES), lambda b, h, i: (b * nq + i, cb(A32_DQ) + h)),
        pl.BlockSpec((S, LANES), lambda b, h, i: (b, cb(A32_DK) + h)),
        pl.BlockSpec((S, LANES), lambda b, h, i: (b, cb(A32_DV) + h)),
    ]
    args = [a32, a32, a32]
    if has_ctx:
        ck, cv, layer, cos, sin = ctx
        in_specs += [
            pl.BlockSpec((None, None, P, LANES), lambda b, h, i: (b, layer, 0, h)),
            pl.BlockSpec((None, None, P, LANES), lambda b, h, i: (b, layer, 0, h)),
            pl.BlockSpec((S, LANES), lambda b, h, i: (0, 0)),
            pl.BlockSpec((S, LANES), lambda b, h, i: (0, 0)),
        ]
        args += [ck, cv, cos, sin]
    in_specs += [
        pl.BlockSpec((4, DQK_D), lambda b, h, i: (0, 0)),
        pl.BlockSpec((1, LANES), lambda b, h, i: (0, h)),
    ]
    args += [lamv, hnorm]
    return pl.pallas_call(
        kern,
        grid=(B, H_D, nq),
        in_specs=in_specs,
        out_specs=pl.BlockSpec((TQ, LANES), lambda b, h, i: (b * nq + i, h)),
        out_shape=jax.ShapeDtypeStruct((B * S, BRANCH_W), BF16),
        scratch_shapes=[pltpu.VMEM((S + P, LANES), BF16), pltpu.VMEM((S + P, LANES), BF16)],
        compiler_params=_cparams("parallel", "parallel", "arbitrary"),
        name="diff_attn",
    )(*args)


def _conv_kernel(ca_ref, cb_ref, w_ref, g_ref, b_ref, y_ref, pad_ref, cv_ref, *, S):
    ca = ca_ref[...].astype(F32)
    cbv = cb_ref[...].astype(F32)
    zeros = jnp.zeros((CONV_PAD, BRANCH_W), F32)
    pad_ref[0:CONV_PAD, :] = zeros
    pad_ref[CONV_PAD + S:2 * CONV_PAD + S, :] = zeros
    pad_ref[CONV_PAD:CONV_PAD + S, :] = ca * _sigmoid(cbv)
    off = CONV_PAD - CONV_W // 2

    win_rows = CONV_ROWS + 2 * CONV_PAD

    def body(i, carry):
        base = pl.multiple_of(i * CONV_ROWS, CONV_ROWS)
        for lb in range(BRANCH_W // LANES):
            cols = slice(lb * LANES, (lb + 1) * LANES)
            win = pad_ref[pl.ds(base, win_rows), cols]
            acc = jnp.zeros((CONV_ROWS, LANES), F32)
            for r in range(8):
                rolled = win if r == 0 else pltpu.roll(win, win_rows - r, 0)
                for a in range(2 * CONV_PAD // 8):
                    j = 8 * a + r - off
                    if 0 <= j < CONV_W:
                        acc = acc + rolled[8 * a:8 * a + CONV_ROWS, :] * w_ref[j:j + 1, cols]
            cv_ref[:, cols] = acc
        acc = cv_ref[...]
        mu = jnp.mean(acc, axis=-1, keepdims=True)
        xc = acc - mu
        yn = xc * lax.rsqrt(jnp.mean(xc * xc, axis=-1, keepdims=True) + EPS) * g_ref[...] + b_ref[...]
        y_ref[pl.ds(base, CONV_ROWS), :] = (yn * _sigmoid(yn)).astype(y_ref.dtype)
        return carry

    lax.fori_loop(0, S // CONV_ROWS, body, 0)


def _conv(a16, w_dw, ln_g, ln_b, B, S):
    cb = lambda off: off // BRANCH_W
    kern = functools.partial(_conv_kernel, S=S)
    return pl.pallas_call(
        kern,
        grid=(B,),
        in_specs=[
            pl.BlockSpec((S, BRANCH_W), lambda b: (b, cb(A16_CA))),
            pl.BlockSpec((S, BRANCH_W), lambda b: (b, cb(A16_CB))),
            pl.BlockSpec((CONV_W + 1, BRANCH_W), lambda b: (0, 0)),
            pl.BlockSpec((1, BRANCH_W), lambda b: (0, 0)),
            pl.BlockSpec((1, BRANCH_W), lambda b: (0, 0)),
        ],
        out_specs=pl.BlockSpec((S, BRANCH_W), lambda b: (b, 0)),
        out_shape=jax.ShapeDtypeStruct((B * S, BRANCH_W), BF16),
        scratch_shapes=[pltpu.VMEM((S + 2 * CONV_PAD, BRANCH_W), F32),
                        pltpu.VMEM((CONV_ROWS, BRANCH_W), F32)],
        compiler_params=_cparams("parallel"),
        name="glu_conv_ln",
    )(a16, a16, w_dw, ln_g, ln_b)


def _merge_kernel(x_ref, mod_ref, ym_ref, yd_ref, yg_ref, yc_ref, g0_ref, g1_ref, g2_ref, g3_ref,
                  wb_ref, wo_ref, n2_ref, wr_ref, x1_ref, h2_ref, wall_ref):
    ys = (ym_ref, yd_ref, yg_ref, yc_ref)
    gs = (g0_ref, g1_ref, g2_ref, g3_ref)
    merged = None
    for nbr in range(N_BRANCH):
        br = _dot(ys[nbr][...], wb_ref[nbr])
        term = _sigmoid(gs[nbr][...].astype(F32)) * br
        merged = term if merged is None else merged + term
    out = _dot(merged.astype(BF16), wo_ref[...])
    x1 = x_ref[...] + mod_ref[2:3, :] * out
    x1_ref[...] = x1
    y = x1 * lax.rsqrt(jnp.mean(x1 * x1, axis=-1, keepdims=True) + EPS) * n2_ref[...]
    h2 = y * (1.0 + mod_ref[4:5, :]) + mod_ref[3:4, :]
    h2_ref[...] = h2.astype(BF16)
    logits = jnp.dot(h2, wr_ref[...], preferred_element_type=F32, precision=lax.Precision.HIGHEST)
    lane = lax.broadcasted_iota(jnp.int32, logits.shape, 1)
    neg = -jnp.inf
    big = jnp.int32(LANES)
    is_g = lane < N_GROUPS
    gl = jnp.where(is_g, logits, neg)
    gmax = jnp.max(gl, axis=-1, keepdims=True)
    gidx = jnp.min(jnp.where(is_g & (gl == gmax), lane, big), axis=-1, keepdims=True)
    g_p = 1.0 / jnp.sum(jnp.where(is_g, jnp.exp(gl - gmax), 0.0), axis=-1, keepdims=True)
    e_lane = lane - N_GROUPS
    in_grp = (e_lane >= 0) & (e_lane < N_EXPERTS) & ((e_lane // EXPERTS_PER_GROUP) == gidx)
    el = jnp.where(in_grp, logits, neg)
    v1 = jnp.max(el, axis=-1, keepdims=True)
    i1 = jnp.min(jnp.where(in_grp & (el == v1), lane, big), axis=-1, keepdims=True)
    el2 = jnp.where(lane == i1, neg, el)
    v2 = jnp.max(el2, axis=-1, keepdims=True)
    i2 = jnp.min(jnp.where(in_grp & (lane != i1) & (el2 == v2), lane, big), axis=-1, keepdims=True)
    e2 = jnp.exp(v2 - v1)
    w1 = g_p / (1.0 + e2)
    w2 = g_p * e2 / (1.0 + e2)
    wall_ref[...] = jnp.where(lane == i1, w1, jnp.where(lane == i2, w2, 0.0))


def _merge(x2d, mod, a16, ym, yd, yg, yc, wb, wo, n2, wr, rows_per_mod, tm):
    T = x2d.shape[0]
    gcb = A16_GATE // D_MODEL
    row = lambda i: (i, 0)
    return pl.pallas_call(
        _merge_kernel,
        grid=(T // tm,),
        in_specs=[
            pl.BlockSpec((tm, D_MODEL), row),
            pl.BlockSpec((None, N_MOD, D_MODEL), lambda i: ((i * tm) // rows_per_mod, 0, 0)),
            pl.BlockSpec((tm, BRANCH_W), row),
            pl.BlockSpec((tm, BRANCH_W), row),
            pl.BlockSpec((tm, BRANCH_W), row),
            pl.BlockSpec((tm, BRANCH_W), row),
            pl.BlockSpec((tm, D_MODEL), lambda i: (i, gcb + 0)),
            pl.BlockSpec((tm, D_MODEL), lambda i: (i, gcb + 1)),
            pl.BlockSpec((tm, D_MODEL), lambda i: (i, gcb + 2)),
            pl.BlockSpec((tm, D_MODEL), lambda i: (i, gcb + 3)),
            pl.BlockSpec((N_BRANCH, BRANCH_W, D_MODEL), lambda i: (0, 0, 0)),
            pl.BlockSpec((D_MODEL, D_MODEL), lambda i: (0, 0)),
            pl.BlockSpec((1, D_MODEL), lambda i: (0, 0)),
            pl.BlockSpec((D_MODEL, LANES), lambda i: (0, 0)),
        ],
        out_specs=[
            pl.BlockSpec((tm, D_MODEL), row),
            pl.BlockSpec((tm, D_MODEL), row),
            pl.BlockSpec((tm, LANES), row),
        ],
        out_shape=[
            jax.ShapeDtypeStruct((T, D_MODEL), F32),
            jax.ShapeDtypeStruct((T, D_MODEL), BF16),
            jax.ShapeDtypeStruct((T, LANES), F32),
        ],
        compiler_params=_cparams("parallel"),
        name="merge_outproj_route",
    )(x2d, mod, ym, yd, yg, yc, a16, a16, a16, a16, wb, wo, n2, wr)


def _moe_kernel(h_ref, wall_ref, x1_ref, mod_ref, w1_ref, w3_ref, w2_ref, fn_ref, o_ref, acc_ref,
                *, final_norm):
    e = pl.program_id(1)

    @pl.when(e == 0)
    def _():
        acc_ref[...] = jnp.zeros_like(acc_ref)

    h = h_ref[...]
    a = _dot(h, w1_ref[...])
    b = _dot(h, w3_ref[...])
    wall = wall_ref[...]
    lane = lax.broadcasted_iota(jnp.int32, wall.shape, 1)
    wcol = jnp.sum(jnp.where(lane == e + N_GROUPS, wall, 0.0), axis=-1, keepdims=True)
    s = (a * _sigmoid(a)) * b * wcol
    acc_ref[...] += _dot(s.astype(BF16), w2_ref[...])

    @pl.when(e == N_EXPERTS - 1)
    def _():
        x2 = x1_ref[...] + mod_ref[5:6, :] * acc_ref[...]
        if final_norm:
            x2 = x2 * lax.rsqrt(jnp.mean(x2 * x2, axis=-1, keepdims=True) + EPS) * fn_ref[...]
        o_ref[...] = x2


def _moe(h2, wall, x1, mod, w1, w3, w2, fn, rows_per_mod, tm, final_norm):
    T = h2.shape[0]
    kern = functools.partial(_moe_kernel, final_norm=final_norm)
    return pl.pallas_call(
        kern,
        grid=(T // tm, N_EXPERTS),
        in_specs=[
            pl.BlockSpec((tm, D_MODEL), lambda i, e: (i, 0)),
            pl.BlockSpec((tm, LANES), lambda i, e: (i, 0)),
            pl.BlockSpec((tm, D_MODEL), lambda i, e: (i, 0)),
            pl.BlockSpec((None, N_MOD, D_MODEL), lambda i, e: ((i * tm) // rows_per_mod, 0, 0)),
            pl.BlockSpec((None, D_MODEL, D_EXPERT), lambda i, e: (e, 0, 0)),
            pl.BlockSpec((None, D_MODEL, D_EXPERT), lambda i, e: (e, 0, 0)),
            pl.BlockSpec((None, D_EXPERT, D_MODEL), lambda i, e: (e, 0, 0)),
            pl.BlockSpec((1, D_MODEL), lambda i, e: (0, 0)),
        ],
        out_specs=pl.BlockSpec((tm, D_MODEL), lambda i, e: (i, 0)),
        out_shape=jax.ShapeDtypeStruct((T, D_MODEL), F32),
        scratch_shapes=[pltpu.VMEM((tm, D_MODEL), F32)],
        compiler_params=_cparams("parallel", "arbitrary"),
        name="moe_experts",
    )(h2, wall, x1, mod, w1, w3, w2, fn)


def _split_w_in(w):
    sizes = (H_M * DK_M, H_M * DK_M, H_M * DV_M, H_M * DV_M, 2 * H_M, 2 * H_M,
             H_D * 2 * DQK_D, H_D * 2 * DQK_D, H_D * DV_D,
             H_G * DK_G, H_G * DK_G, H_G * DV_G, 2 * GATE_RANK, H_G * DV_G,
             BRANCH_W, BRANCH_W, N_BRANCH * D_MODEL)
    outs, acc = [], 0
    for s in sizes:
        outs.append(w[:, acc:acc + s])
        acc += s
    return outs


def _pack_layer_params(p):
    (m_q, m_k, m_v, m_o, m_i, m_f, d_q, d_k, d_v, g_q, g_k, g_v, g_a, g_r, c_a, c_b, gate) = \
        _split_w_in(p['w_in'])
    w16 = jnp.concatenate([m_q, m_k, m_v, m_o, gate, g_r, c_a, c_b], axis=1).astype(BF16)
    gqk = jnp.concatenate([g_q.reshape(D_MODEL, H_G, DK_G), g_k.reshape(D_MODEL, H_G, DK_G)],
                          axis=2).reshape(D_MODEL, 2 * H_G * DK_G)
    small = jnp.concatenate(
        [m_i, m_f, g_a, jnp.zeros((D_MODEL, LANES - 4 * H_M - 2 * GATE_RANK), F32)], axis=1)
    w32 = jnp.concatenate([d_q, d_k, d_v, gqk, g_v, small], axis=1).astype(BF16)
    bi = p['b_m_i'].reshape(2, H_M)
    bf = p['b_m_f'].reshape(2, H_M)
    bcol = jnp.stack([bi[0], bi[1], bf[0], bf[1]], axis=-1)
    wup = p['w_gla_up'].reshape(2, GATE_RANK, H_G, DK_G)
    wup_pad = jnp.zeros((H_G, 2, LANES, LANES), F32)
    bup = p['b_gla_gate'].reshape(2, H_G, DK_G)
    for d in range(2):
        blk = jnp.transpose(wup[d], (1, 0, 2))
        blk = jnp.concatenate([blk, blk], axis=-1)
        r0 = SM_GA + d * GATE_RANK
        wup_pad = wup_pad.at[:, d, r0:r0 + GATE_RANK, :].set(blk)
    bup2 = jnp.transpose(jnp.concatenate([bup, bup], axis=-1), (1, 0, 2))[:, :, None, :]
    wr = jnp.concatenate([p['w_group_router'], p['w_expert_router'],
                          jnp.zeros((D_MODEL, LANES - N_GROUPS - N_EXPERTS), F32)], axis=1)
    return dict(
        w16=w16, w32=w32, bcol=bcol.reshape(H_M, 1, 4), brow=bcol.reshape(H_M, 4, 1),
        wup=wup_pad.astype(BF16), bup=bup2,
        wdw=jnp.concatenate([p['w_dw'], jnp.zeros((1, BRANCH_W), F32)], axis=0),
        ln_g=p['conv_ln_g'].reshape(1, BRANCH_W), ln_b=p['conv_ln_b'].reshape(1, BRANCH_W),
        hn_m=p['hnorm_m'].reshape(1, BRANCH_W), hn_d=p['hnorm_d'].reshape(1, BRANCH_W),
        hn_g=p['hnorm_g'].reshape(1, BRANCH_W),
        lamv=jnp.stack([p['lam_q1'], p['lam_k1'], p['lam_q2'], p['lam_k2']], axis=0),
        wb=p['w_branch'].astype(BF16), wo=p['w_out'].astype(BF16),
        n1=p['norm1'].reshape(1, D_MODEL), n2=p['norm2'].reshape(1, D_MODEL), wr=wr,
        w1=p['w_e1'].reshape(N_EXPERTS, D_MODEL, D_EXPERT).astype(BF16),
        w3=p['w_e3'].reshape(N_EXPERTS, D_MODEL, D_EXPERT).astype(BF16),
        w2=p['w_e2'].reshape(N_EXPERTS, D_EXPERT, D_MODEL).astype(BF16),
    )


def _rope_tables(S):
    rows = S // GRID_W
    r, col = jnp.meshgrid(jnp.arange(rows, dtype=F32), jnp.arange(GRID_W, dtype=F32), indexing='ij')
    r, col = r.reshape(-1), col.reshape(-1)
    n_freq = DQK_D // 4
    inv = ROPE_BASE ** (-jnp.arange(n_freq, dtype=F32) / n_freq)
    ang = jnp.concatenate([r[:, None] * inv, col[:, None] * inv], axis=-1)
    cos, sin = jnp.cos(ang), jnp.sin(ang)
    cos_t = jnp.tile(cos, (1, LANES // (DQK_D // 2)))
    sin_t = jnp.tile(jnp.concatenate([-sin, sin], axis=-1), (1, LANES // DQK_D))
    return cos_t, sin_t


def _pick_tile(T, cap):
    t = min(T, cap)
    while T % t:
        t //= 2
    return t


def _layer(x2d, mod, pk, B, S, lam_init, ctx, final_norm, fn):
    T = B * S
    rows_per_mod = T // mod.shape[0]
    tm = _pick_tile(rows_per_mod, 1024)
    a16 = _inproj(x2d, mod, pk['n1'], pk['w16'], BF16, rows_per_mod, tm, 768)
    a32 = _inproj(x2d, mod, pk['n1'], pk['w32'], F32, rows_per_mod, tm, 896)

    L = min(MLSTM_CHUNK, S)
    sm = a32[:, A32_SM:A32_SM + 4 * H_M]
    gcol = jnp.transpose(sm.reshape(T, 2, 2, H_M), (3, 0, 1, 2)).reshape(H_M, T, 4)
    grow = jnp.transpose(gcol.reshape(H_M, T // L, L, 4), (0, 1, 3, 2))
    if ctx is None:
        c0 = jnp.zeros((B, 2, H_M, DK_M, DV_M), F32)
        n0 = jnp.zeros((B, 2, H_M, 1, DK_M), F32)
        m0 = jnp.zeros((B, 2, H_M, 1, LANES), F32)
        s0 = jnp.zeros((B, 2, H_G, DK_G, DV_G), F32)
        attn_ctx = None
    else:
        c0 = ctx['C']
        n0 = ctx['n'][:, :, :, None, :]
        m0 = jnp.broadcast_to(ctx['m'][:, :, :, None, None], (B, 2, H_M, 1, LANES))
        s0 = ctx['S']
        attn_ctx = (ctx['k'], ctx['v'], ctx['layer'], ctx['cos'], ctx['sin'])
    ym, c_f, n_f, m_f = _mlstm(a16, gcol, grow, pk['bcol'], pk['brow'], c0, n0, m0, pk['hn_m'], B, S)
    yd = _attn(a32, pk['lamv'], pk['hn_d'], B, S, lam_init, attn_ctx)
    yg, s_f = _gla(a32, a16, pk['wup'], pk['bup'], s0, pk['hn_g'], B, S)
    yc = _conv(a16, pk['wdw'], pk['ln_g'], pk['ln_b'], B, S)
    x1, h2, wall = _merge(x2d, mod, a16, ym, yd, yg, yc, pk['wb'], pk['wo'], pk['n2'], pk['wr'],
                          rows_per_mod, _pick_tile(rows_per_mod, 512))
    x2 = _moe(h2, wall, x1, mod, pk['w1'], pk['w3'], pk['w2'], fn, rows_per_mod,
              _pick_tile(rows_per_mod, 1024), final_norm)
    state = None
    if ctx is None:
        state = (a32[:, A32_DK:A32_DK + H_D * 2 * DQK_D].reshape(B, S, H_D, 2 * DQK_D),
                 a32[:, A32_DV:A32_DV + H_D * DV_D].reshape(B, S, H_D, DV_D),
                 c_f, n_f[:, :, :, 0, :], m_f[:, :, :, 0, 0], s_f)
    return x2, state


def kernel(x_prompt, x_sample, c, cache_diff_k, cache_diff_v, state_mlstm_C, state_mlstm_n, state_mlstm_m, state_gla_S, c_ctx, w_mod, b_mod, norm1, w_in, b_m_i, b_m_f, lam_q1, lam_k1, lam_q2, lam_k2, w_gla_up, b_gla_gate, w_dw, conv_ln_g, conv_ln_b, hnorm_m, hnorm_d, hnorm_g, w_branch, w_out, norm2, w_group_router, w_expert_router, w_e1, w_e3, w_e2, final_norm):
    Bp, Sp, _ = x_prompt.shape
    Bs, Ss, _ = x_sample.shape
    P = cache_diff_k.shape[2]
    n_cond = 8 * ((1 + Bs + 7) // 8)
    cond = jnp.concatenate([c_ctx[None, :], c, jnp.zeros((n_cond - 1 - Bs, D_MODEL), F32)], axis=0)
    mod_all = _modulation(cond, w_mod, b_mod).reshape(DEPTH, n_cond, N_MOD, D_MODEL)
    cos_t, sin_t = _rope_tables(Ss)
    ck4 = cache_diff_k.reshape(Bs, DEPTH, P, H_D * 2 * DQK_D)
    cv4 = cache_diff_v.reshape(Bs, DEPTH, P, H_D * DV_D)
    fn = final_norm.reshape(1, D_MODEL)
    yp = x_prompt.reshape(Bp * Sp, D_MODEL)
    ys = x_sample.reshape(Bs * Ss, D_MODEL)
    states = []
    for l in range(DEPTH):
        p = {'w_in': w_in[l], 'b_m_i': b_m_i[l], 'b_m_f': b_m_f[l], 'lam_q1': lam_q1[l],
             'lam_k1': lam_k1[l], 'lam_q2': lam_q2[l], 'lam_k2': lam_k2[l],
             'w_gla_up': w_gla_up[l], 'b_gla_gate': b_gla_gate[l], 'w_dw': w_dw[l],
             'conv_ln_g': conv_ln_g[l], 'conv_ln_b': conv_ln_b[l], 'hnorm_m': hnorm_m[l],
             'hnorm_d': hnorm_d[l], 'hnorm_g': hnorm_g[l], 'w_branch': w_branch[l],
             'w_out': w_out[l], 'norm1': norm1[l], 'norm2': norm2[l],
             'w_group_router': w_group_router[l], 'w_expert_router': w_expert_router[l],
             'w_e1': w_e1[l], 'w_e3': w_e3[l], 'w_e2': w_e2[l]}
        pk = _pack_layer_params(p)
        lam_init = 0.8 - 0.6 * math.exp(-0.3 * l)
        last = l == DEPTH - 1
        yp, st = _layer(yp, mod_all[l, 0:1], pk, Bp, Sp, lam_init, None, last, fn)
        states.append(st)
        ctx = {'k': ck4, 'v': cv4, 'layer': l, 'cos': cos_t, 'sin': sin_t,
               'C': state_mlstm_C[:, l], 'n': state_mlstm_n[:, l], 'm': state_mlstm_m[:, l],
               'S': state_gla_S[:, l]}
        ys, _ = _layer(ys, mod_all[l, 1:1 + Bs], pk, Bs, Ss, lam_init, ctx, last, fn)
    stack = lambda i: jnp.stack([s[i] for s in states], axis=1)
    return (yp.reshape(Bp, Sp, D_MODEL), ys.reshape(Bs, Ss, D_MODEL),
            stack(0), stack(1), stack(2), stack(3), stack(4), stack(5))
```

```python
import functools
import math

import jax
import jax.numpy as jnp
from jax import lax
from jax.experimental import pallas as pl
from jax.experimental.pallas import tpu as pltpu

F32 = jnp.float32
BF16 = jnp.bfloat16

D_MODEL = 1024
DEPTH = 2
GRID_W = 64
BRANCH_W = 512
N_BRANCH = 4
H_M, DK_M, DV_M = 4, 128, 128
H_D, DQK_D, DV_D = 4, 64, 128
H_G, DK_G, DV_G = 4, 64, 128
GATE_RANK = 16
GLA_TAU = 16.0
CONV_W = 31
N_GROUPS, EXPERTS_PER_GROUP, D_EXPERT = 4, 4, 512
N_EXPERTS = N_GROUPS * EXPERTS_PER_GROUP
ROPE_BASE = 10000.0
EPS = 1e-6
N_MOD = 6

LANES = 128
VMEM_LIMIT = 48 * 1024 * 1024

A16_MQ, A16_MK, A16_MV, A16_MO = 0, 512, 1024, 1536
A16_GATE, A16_GR, A16_CA, A16_CB = 2048, 6144, 6656, 7168
N_A16 = 7680
A32_DQ, A32_DK, A32_DV, A32_GQK, A32_GV, A32_SM = 0, 512, 1024, 1536, 2048, 2560
N_A32 = 2688
SM_MI, SM_MF, SM_GA = 0, 8, 16

MLSTM_CHUNK = 128
GLA_CHUNK = 64
GLA_SUB = 16
GLA_EXP_CLAMP = 80.0
CONV_ROWS = 64
CONV_PAD = 16


def _cparams(*sem):
    return pltpu.CompilerParams(dimension_semantics=sem, vmem_limit_bytes=VMEM_LIMIT)


def _log_sigmoid(x):
    return jnp.minimum(x, 0.0) - jnp.log1p(jnp.exp(-jnp.abs(x)))


def _sigmoid(x):
    return 1.0 / (1.0 + jnp.exp(-x))


def _dot(a, b):
    return jnp.dot(a, b, preferred_element_type=F32)


def _dot_nt(a, b):
    return lax.dot_general(a, b, (((1,), (1,)), ((), ())), preferred_element_type=F32)


def _dot_tn(a, b):
    return lax.dot_general(a, b, (((0,), (0,)), ((), ())), preferred_element_type=F32)


def _mod_kernel(c_ref, w_ref, b_ref, o_ref):
    c = c_ref[...]
    a = (c * _sigmoid(c)).astype(BF16)
    o_ref[...] = _dot(a, w_ref[...].astype(BF16)) + b_ref[...]


def _modulation(cond, w_mod, b_mod):
    R = cond.shape[0]
    tn = 512
    nmod = N_MOD * D_MODEL
    return pl.pallas_call(
        _mod_kernel,
        grid=(DEPTH, nmod // tn),
        in_specs=[
            pl.BlockSpec((R, D_MODEL), lambda l, j: (0, 0)),
            pl.BlockSpec((None, D_MODEL, tn), lambda l, j: (l, 0, j)),
            pl.BlockSpec((None, 1, tn), lambda l, j: (l, 0, j)),
        ],
        out_specs=pl.BlockSpec((None, R, tn), lambda l, j: (l, 0, j)),
        out_shape=jax.ShapeDtypeStruct((DEPTH, R, nmod), F32),
        compiler_params=_cparams("parallel", "parallel"),
        name="adaln_mod",
    )(cond, w_mod, b_mod.reshape(DEPTH, 1, nmod))


def _inproj_kernel(x_ref, mod_ref, g_ref, w_ref, o_ref, h_ref):
    @pl.when(pl.program_id(1) == 0)
    def _():
        x = x_ref[...]
        y = x * lax.rsqrt(jnp.mean(x * x, axis=-1, keepdims=True) + EPS) * g_ref[...]
        h_ref[...] = (y * (1.0 + mod_ref[1:2, :]) + mod_ref[0:1, :]).astype(BF16)

    o_ref[...] = _dot(h_ref[...], w_ref[...]).astype(o_ref.dtype)


def _inproj(x2d, mod, g, w, out_dtype, rows_per_mod, tm, tn):
    T = x2d.shape[0]
    N = w.shape[1]
    return pl.pallas_call(
        _inproj_kernel,
        grid=(T // tm, N // tn),
        in_specs=[
            pl.BlockSpec((tm, D_MODEL), lambda i, j: (i, 0)),
            pl.BlockSpec((None, N_MOD, D_MODEL), lambda i, j: ((i * tm) // rows_per_mod, 0, 0)),
            pl.BlockSpec((1, D_MODEL), lambda i, j: (0, 0)),
            pl.BlockSpec((D_MODEL, tn), lambda i, j: (0, j)),
        ],
        out_specs=pl.BlockSpec((tm, tn), lambda i, j: (i, j)),
        out_shape=jax.ShapeDtypeStruct((T, N), out_dtype),
        scratch_shapes=[pltpu.VMEM((tm, D_MODEL), BF16)],
        compiler_params=_cparams("parallel", "arbitrary"),
        name="norm_inproj",
    )(x2d, mod, g, w)


def _mlstm_local(c, q_ref, k_ref, v_ref, gc_ref, gr_ref, bc_ref, br_ref, p_ref, cp_ref, kv_ref,
                 rp_ref, L):
    scale = DK_M ** -0.5
    ti = lax.broadcasted_iota(jnp.int32, (L, L), 0)
    si = lax.broadcasted_iota(jnp.int32, (L, L), 1)
    lane = lax.broadcasted_iota(jnp.int32, (L, LANES), 1)
    sub = lax.broadcasted_iota(jnp.int32, (8, LANES), 0)
    rows = pl.ds(pl.multiple_of(c * L, L), L)
    q = q_ref[rows, :]
    k = k_ref[rows, :]
    v = v_ref[rows, :]
    kf = k.astype(F32)
    qk = _dot_nt(q, k) * scale
    gcol = gc_ref[rows, :] + bc_ref[...]
    grow = gr_ref[c] + br_ref[...]
    for d in range(2):
        rev = d == 1
        mask = (si >= ti) if rev else (si <= ti)
        mask_t = (si <= ti) if rev else (si >= ti)
        i_col = gcol[:, d:d + 1]
        f_col = _log_sigmoid(gcol[:, 2 + d:3 + d])
        i_row = grow[d:d + 1, :]
        f_row = _log_sigmoid(grow[2 + d:3 + d, :])
        b_col = jnp.sum(jnp.where(mask, f_row, 0.0), axis=1, keepdims=True)
        b_row = jnp.sum(jnp.where(mask_t, f_col, 0.0), axis=0, keepdims=True)
        log_d = jnp.where(mask, b_col - b_row + i_row, -jnp.inf)
        m_loc = jnp.max(log_d, axis=1, keepdims=True)
        smat = qk * jnp.exp(log_d - m_loc)
        p_ref[d, rows, :] = _dot(smat.astype(BF16), v)
        r = jnp.sum(smat, axis=1, keepdims=True)
        cp_ref[d, rows, :] = jnp.where(lane == 0, m_loc, jnp.where(lane == 1, r, b_col))
        b_last = jnp.sum(f_row, axis=1, keepdims=True)
        ls_row = b_last - b_row + i_row
        ls_col = b_last - b_col + i_col
        m2 = jnp.max(ls_row, axis=1, keepdims=True)
        kw = jnp.exp(ls_col - m2) * kf
        kv_ref[d, c] = scale * _dot_tn(kw.astype(BF16), v)
        kn = scale * jnp.sum(kw, axis=0, keepdims=True)
        rp_ref[d, c] = jnp.where(sub == 0, kn, jnp.where(sub == 1, b_last, m2))


def _mlstm_carry(c, d, carry, q_ref, p_ref, cp_ref, kv_ref, rp_ref, h_ref, L):
    C, n, m = carry
    rows = pl.ds(pl.multiple_of(c * L, L), L)
    q = q_ref[rows, :]
    cp = cp_ref[d, rows, :]
    m_loc, r, b_col = cp[:, 0:1], cp[:, 1:2], cp[:, 2:3]
    m_t = jnp.maximum(b_col + m, m_loc)
    a_int = jnp.exp(b_col + m - m_t)
    e_loc = jnp.exp(m_loc - m_t)
    num = a_int * _dot(q, C.astype(BF16)) + e_loc * p_ref[d, rows, :]
    den = a_int * jnp.sum(q.astype(F32) * n, axis=1, keepdims=True) + e_loc * r
    h_ref[d, rows, :] = num / jnp.maximum(jnp.abs(den), jnp.exp(-m_t))
    rp = rp_ref[d, c]
    kn, b_last, m2 = rp[0:1, :], rp[1:2, 0:1], rp[2:3, 0:1]
    m_new = jnp.maximum(b_last + m, m2)
    a_c = jnp.exp(b_last + m - m_new)
    e2 = jnp.exp(m2 - m_new)
    return a_c * C + e2 * kv_ref[d, c], a_c * n + e2 * kn, m_new


def _mlstm_kernel(q_ref, k_ref, v_ref, og_ref, gc_ref, gr_ref, bc_ref, br_ref, c0_ref, n0_ref,
                  m0_ref, hn_ref, y_ref, c_out_ref, n_out_ref, m_out_ref,
                  p_ref, cp_ref, kv_ref, rp_ref, h_ref, *, L, S):
    nch = S // L

    def local(ci, carry):
        _mlstm_local(ci, q_ref, k_ref, v_ref, gc_ref, gr_ref, bc_ref, br_ref, p_ref, cp_ref,
                     kv_ref, rp_ref, L)
        return carry

    lax.fori_loop(0, nch, local, 0, unroll=2)
    step = functools.partial(_mlstm_carry, q_ref=q_ref, p_ref=p_ref, cp_ref=cp_ref, kv_ref=kv_ref,
                             rp_ref=rp_ref, h_ref=h_ref, L=L)

    def body(ci, carry):
        return step(ci, 0, carry[0]), step(nch - 1 - ci, 1, carry[1])

    init = tuple((c0_ref[d], n0_ref[d], m0_ref[d][:, 0:1]) for d in range(2))
    fin = lax.fori_loop(0, nch, body, init, unroll=2)
    for d in range(2):
        C, n, m = fin[d]
        c_out_ref[d] = C
        n_out_ref[d] = n
        m_out_ref[d] = jnp.broadcast_to(m, (1, LANES))

    hm = h_ref[0] + h_ref[1]
    y = hm * lax.rsqrt(jnp.mean(hm * hm, axis=-1, keepdims=True) + EPS) * hn_ref[...]
    y_ref[...] = (y * _sigmoid(og_ref[...].astype(F32))).astype(y_ref.dtype)


def _mlstm(a16, gcol, grow, bcol, brow, c0, n0, m0, hnorm, B, S):
    L = min(MLSTM_CHUNK, S)
    nch = S // L
    cb = lambda off: off // LANES
    kern = functools.partial(_mlstm_kernel, L=L, S=S)
    return pl.pallas_call(
        kern,
        grid=(B, H_M),
        in_specs=[
            pl.BlockSpec((S, LANES), lambda b, h: (b, cb(A16_MQ) + h)),
            pl.BlockSpec((S, LANES), lambda b, h: (b, cb(A16_MK) + h)),
            pl.BlockSpec((S, LANES), lambda b, h: (b, cb(A16_MV) + h)),
            pl.BlockSpec((S, LANES), lambda b, h: (b, cb(A16_MO) + h)),
            pl.BlockSpec((None, S, 4), lambda b, h: (h, b, 0)),
            pl.BlockSpec((None, nch, 4, L), lambda b, h: (h, b, 0, 0)),
            pl.BlockSpec((None, 1, 4), lambda b, h: (h, 0, 0)),
            pl.BlockSpec((None, 4, 1), lambda b, h: (h, 0, 0)),
            pl.BlockSpec((None, 2, None, DK_M, DV_M), lambda b, h: (b, 0, h, 0, 0)),
            pl.BlockSpec((None, 2, None, 1, DK_M), lambda b, h: (b, 0, h, 0, 0)),
            pl.BlockSpec((None, 2, None, 1, LANES), lambda b, h: (b, 0, h, 0, 0)),
            pl.BlockSpec((1, LANES), lambda b, h: (0, h)),
        ],
        out_specs=[
            pl.BlockSpec((S, LANES), lambda b, h: (b, h)),
            pl.BlockSpec((None, 2, None, DK_M, DV_M), lambda b, h: (b, 0, h, 0, 0)),
            pl.BlockSpec((None, 2, None, 1, DK_M), lambda b, h: (b, 0, h, 0, 0)),
            pl.BlockSpec((None, 2, None, 1, LANES), lambda b, h: (b, 0, h, 0, 0)),
        ],
        out_shape=[
            jax.ShapeDtypeStruct((B * S, BRANCH_W), BF16),
            jax.ShapeDtypeStruct((B, 2, H_M, DK_M, DV_M), F32),
            jax.ShapeDtypeStruct((B, 2, H_M, 1, DK_M), F32),
            jax.ShapeDtypeStruct((B, 2, H_M, 1, LANES), F32),
        ],
        scratch_shapes=[pltpu.VMEM((2, S, DV_M), F32), pltpu.VMEM((2, S, LANES), F32),
                        pltpu.VMEM((2, nch, DK_M, DV_M), F32), pltpu.VMEM((2, nch, 8, LANES), F32),
                        pltpu.VMEM((2, S, DV_M), F32)],
        compiler_params=_cparams("parallel", "parallel"),
        name="mlstm",
    )(a16, a16, a16, a16, gcol, grow, bcol, brow, c0, n0, m0, hnorm)


def _gla_local(c, q2_ref, k2_ref, v_ref, la_ref, oa_ref, qt_ref, u_ref, dec_ref, L):
    nb = L // GLA_SUB
    ti = lax.broadcasted_iota(jnp.int32, (L, L), 0)
    si = lax.broadcasted_iota(jnp.int32, (L, L), 1)
    row_blk = lax.broadcasted_iota(jnp.int32, (L, LANES), 0) // GLA_SUB
    lo_half = lax.broadcasted_iota(jnp.int32, (L, LANES), 1) < DK_G
    eye = (lax.broadcasted_iota(jnp.int32, (DK_G, LANES), 0)
           == lax.broadcasted_iota(jnp.int32, (DK_G, LANES), 1))
    rows = pl.ds(pl.multiple_of(c * L, L), L)
    q2 = q2_ref[rows, :]
    k2 = k2_ref[rows, :]
    v = v_ref[rows, :].astype(BF16)
    for d in range(2):
        rev = d == 1
        mask = (si >= ti) if rev else (si <= ti)
        tri = mask.astype(BF16)
        la2 = la_ref[d, rows, :]
        la_hi = la2.astype(BF16)
        la_lo = (la2 - la_hi.astype(F32)).astype(BF16)
        g2 = _dot(tri, la_hi) + _dot(tri, la_lo)
        qt_ref[d, rows, :] = (q2 * jnp.exp(g2))[:, :DK_G].astype(BF16)
        a_parts, b_parts = [], []
        for p in range(nb // 2):
            ia, ib = 2 * p, 2 * p + 1
            ra = ia * GLA_SUB + (GLA_SUB - 1 if rev else 0)
            rb = ib * GLA_SUB + (GLA_SUB - 1 if rev else 0)
            ref2 = jnp.where(lo_half, g2[ra:ra + 1, :], g2[rb:rb + 1, :])
            blk = jnp.where(lo_half, ia, ib)
            in_blk = row_blk == blk
            key_ok = (row_blk >= blk) if rev else (row_blk <= blk)
            a_parts.append(jnp.where(in_blk, q2 * jnp.exp(jnp.minimum(g2 - ref2, 0.0)), 0.0))
            b_parts.append(
                jnp.where(key_ok, k2 * jnp.exp(jnp.minimum(ref2 - g2, GLA_EXP_CLAMP)), 0.0))
        a_big = jnp.concatenate(a_parts, axis=1).astype(BF16)
        b_big = jnp.concatenate(b_parts, axis=1).astype(BF16)
        att = jnp.where(mask, _dot_nt(a_big, b_big), 0.0)
        oa_ref[d, rows, :] = _dot(att.astype(BF16), v)
        gl_row = 0 if rev else L - 1
        glast = g2[gl_row:gl_row + 1, :]
        kd = (k2 * jnp.exp(glast - g2))[:, :DK_G]
        u_ref[d, c] = _dot_tn(kd.astype(BF16), v)
        glast_col = jnp.sum(jnp.where(eye, glast, 0.0), axis=1, keepdims=True)
        dec_ref[d, c] = jnp.broadcast_to(jnp.exp(glast_col), (DK_G, DV_G))


def _gla_kernel(qk_ref, v_ref, sm_ref, wup_ref, bup_ref, s0_ref, gr_ref, hn_ref,
                y_ref, s_out_ref, la_ref, q2_ref, k2_ref, oa_ref, oi_ref, qt_ref, u_ref, dec_ref,
                *, L, S):
    nch = S // L
    sm = sm_ref[...].astype(BF16)
    for d in range(2):
        la_ref[d] = _log_sigmoid(_dot(sm, wup_ref[d]) + bup_ref[d]) * (1.0 / GLA_TAU)
    qk = qk_ref[...]
    qk_sw = pltpu.roll(qk, DK_G, 1)
    lo_half = lax.broadcasted_iota(jnp.int32, qk.shape, 1) < DK_G
    q2_ref[...] = jnp.where(lo_half, qk, qk_sw) * (DK_G ** -0.5)
    k2_ref[...] = jnp.where(lo_half, qk_sw, qk)

    def local(ci, carry):
        _gla_local(ci, q2_ref, k2_ref, v_ref, la_ref, oa_ref, qt_ref, u_ref, dec_ref, L)
        return carry

    lax.fori_loop(0, nch, local, 0, unroll=4)

    def body(ci, carry):
        out = []
        for d, c in ((0, ci), (1, nch - 1 - ci)):
            rows = pl.ds(pl.multiple_of(c * L, L), L)
            st = carry[d]
            oi_ref[d, rows, :] = _dot(qt_ref[d, rows, :], st.astype(BF16))
            out.append(dec_ref[d, c] * st + u_ref[d, c])
        return tuple(out)

    st_f, st_b = lax.fori_loop(0, nch, body, (s0_ref[0], s0_ref[1]), unroll=2)
    s_out_ref[0] = st_f
    s_out_ref[1] = st_b

    og = (oa_ref[0] + oi_ref[0]) + (oa_ref[1] + oi_ref[1])
    y = og * lax.rsqrt(jnp.mean(og * og, axis=-1, keepdims=True) + EPS) * hn_ref[...]
    gr = gr_ref[...].astype(F32)
    y_ref[...] = (y * (gr * _sigmoid(gr))).astype(y_ref.dtype)


def _gla(a32, a16, wup, bup, s0, hnorm, B, S):
    L = min(GLA_CHUNK, S)
    nch = S // L
    cb = lambda off: off // LANES
    kern = functools.partial(_gla_kernel, L=L, S=S)
    return pl.pallas_call(
        kern,
        grid=(B, H_G),
        in_specs=[
            pl.BlockSpec((S, LANES), lambda b, h: (b, cb(A32_GQK) + h)),
            pl.BlockSpec((S, LANES), lambda b, h: (b, cb(A32_GV) + h)),
            pl.BlockSpec((S, LANES), lambda b, h: (b, cb(A32_SM))),
            pl.BlockSpec((None, 2, LANES, LANES), lambda b, h: (h, 0, 0, 0)),
            pl.BlockSpec((None, 2, 1, LANES), lambda b, h: (h, 0, 0, 0)),
            pl.BlockSpec((None, 2, None, DK_G, DV_G), lambda b, h: (b, 0, h, 0, 0)),
            pl.BlockSpec((S, LANES), lambda b, h: (b, cb(A16_GR) + h)),
            pl.BlockSpec((1, LANES), lambda b, h: (0, h)),
        ],
        out_specs=[
            pl.BlockSpec((S, LANES), lambda b, h: (b, h)),
            pl.BlockSpec((None, 2, None, DK_G, DV_G), lambda b, h: (b, 0, h, 0, 0)),
        ],
        out_shape=[
            jax.ShapeDtypeStruct((B * S, BRANCH_W), BF16),
            jax.ShapeDtypeStruct((B, 2, H_G, DK_G, DV_G), F32),
        ],
        scratch_shapes=[pltpu.VMEM((2, S, LANES), F32), pltpu.VMEM((S, LANES), F32),
                        pltpu.VMEM((S, LANES), F32), pltpu.VMEM((2, S, DV_G), F32),
                        pltpu.VMEM((2, S, DV_G), F32), pltpu.VMEM((2, S, DK_G), BF16),
                        pltpu.VMEM((2, nch, DK_G, DV_G), F32), pltpu.VMEM((2, nch, DK_G, DV_G), F32)],
        compiler_params=_cparams("parallel", "parallel"),
        name="gla",
    )(a32, a32, a32, wup, bup, s0, a16, hnorm)


def _rope(x, cos, sin_signed):
    lane = lax.broadcasted_iota(jnp.int32, x.shape, 1)
    first = (lane % DQK_D) < (DQK_D // 2)
    partner = jnp.where(first, pltpu.roll(x, LANES - DQK_D // 2, 1), pltpu.roll(x, DQK_D // 2, 1))
    return x * cos + partner * sin_signed


def _attn_kernel(*refs, S, P, TQ, lam_init, has_ctx):
    if has_ctx:
        (q_ref, k_ref, v_ref, ck_ref, cv_ref, cos_ref, sin_ref, lam_ref, hn_ref,
         y_ref, kk_ref, vv_ref) = refs
    else:
        q_ref, k_ref, v_ref, lam_ref, hn_ref, y_ref, kk_ref, vv_ref = refs
    qi = pl.program_id(2)

    @pl.when(qi == 0)
    def _():
        k = k_ref[...]
        if has_ctx:
            k = _rope(k, cos_ref[...], sin_ref[...])
            kk_ref[S:S + P, :] = ck_ref[...].astype(BF16)
            vv_ref[S:S + P, :] = cv_ref[...].astype(BF16)
        kk_ref[0:S, :] = k.astype(BF16)
        vv_ref[0:S, :] = v_ref[...].astype(BF16)

    q = q_ref[...]
    if has_ctx:
        r0 = pl.multiple_of(qi * TQ, TQ)
        q = _rope(q, cos_ref[pl.ds(r0, TQ), :], sin_ref[pl.ds(r0, TQ), :])
    q = q * (DQK_D ** -0.5 * math.log2(math.e))
    lane = lax.broadcasted_iota(jnp.int32, q.shape, 1)
    kk = kk_ref[...]
    vv = vv_ref[...]
    outs = []
    for comp in range(2):
        sel = (lane < DQK_D) if comp == 0 else (lane >= DQK_D)
        s = _dot_nt(jnp.where(sel, q, 0.0).astype(BF16), kk)
        e = jnp.exp2(s - jnp.max(s, axis=-1, keepdims=True))
        l = jnp.sum(e, axis=-1, keepdims=True)
        outs.append(_dot(e.astype(BF16), vv) / l)
    lv = lam_ref[...]
    lam = (jnp.exp(jnp.sum(lv[0:1, :] * lv[1:2, :], axis=-1, keepdims=True))
           - jnp.exp(jnp.sum(lv[2:3, :] * lv[3:4, :], axis=-1, keepdims=True)) + lam_init)
    o = outs[0] - lam * outs[1]
    y = o * lax.rsqrt(jnp.mean(o * o, axis=-1, keepdims=True) + EPS) * hn_ref[...]
    y_ref[...] = (y * (1.0 - lam_init)).astype(y_ref.dtype)


def _attn(a32, lamv, hnorm, B, S, lam_init, ctx=None):
    TQ = min(256, S)
    nq = S // TQ
    has_ctx = ctx is not None
    P = ctx[0].shape[2] if has_ctx else 0
    cb = lambda off: off // LANES
    kern = functools.partial(_attn_kernel, S=S, P=P, TQ=TQ, lam_init=lam_init, has_ctx=has_ctx)
    in_specs = [
        pl.BlockSpec((TQ, LANES), lambda b, h, i: (b * nq + i, cb(A32_DQ) + h)),
        pl.BlockSpec((S, LANES), lambda b, h, i: (b, cb(A32_DK) + h)),
        pl.BlockSpec((S, LANES), lambda b, h, i: (b, cb(A32_DV) + h)),
    ]
    args = [a32, a32, a32]
    if has_ctx:
        ck, cv, layer, cos, sin = ctx
        in_specs += [
            pl.BlockSpec((None, None, P, LANES), lambda b, h, i: (b, layer, 0, h)),
            pl.BlockSpec((None, None, P, LANES), lambda b, h, i: (b, layer, 0, h)),
            pl.BlockSpec((S, LANES), lambda b, h, i: (0, 0)),
            pl.BlockSpec((S, LANES), lambda b, h, i: (0, 0)),
        ]
        args += [ck, cv, cos, sin]
    in_specs += [
        pl.BlockSpec((4, DQK_D), lambda b, h, i: (0, 0)),
        pl.BlockSpec((1, LANES), lambda b, h, i: (0, h)),
    ]
    args += [lamv, hnorm]
    return pl.pallas_call(
        kern,
        grid=(B, H_D, nq),
        in_specs=in_specs,
        out_specs=pl.BlockSpec((TQ, LANES), lambda b, h, i: (b * nq + i, h)),
        out_shape=jax.ShapeDtypeStruct((B * S, BRANCH_W), BF16),
        scratch_shapes=[pltpu.VMEM((S + P, LANES), BF16), pltpu.VMEM((S + P, LANES), BF16)],
        compiler_params=_cparams("parallel", "parallel", "arbitrary"),
        name="diff_attn",
    )(*args)


def _conv_kernel(ca_ref, cb_ref, w_ref, g_ref, b_ref, y_ref, pad_ref, cv_ref, *, S):
    ca = ca_ref[...].astype(F32)
    cbv = cb_ref[...].astype(F32)
    zeros = jnp.zeros((CONV_PAD, BRANCH_W), F32)
    pad_ref[0:CONV_PAD, :] = zeros
    pad_ref[CONV_PAD + S:2 * CONV_PAD + S, :] = zeros
    pad_ref[CONV_PAD:CONV_PAD + S, :] = ca * _sigmoid(cbv)
    off = CONV_PAD - CONV_W // 2

    win_rows = CONV_ROWS + 2 * CONV_PAD

    def body(i, carry):
        base = pl.multiple_of(i * CONV_ROWS, CONV_ROWS)
        for lb in range(BRANCH_W // LANES):
            cols = slice(lb * LANES, (lb + 1) * LANES)
            win = pad_ref[pl.ds(base, win_rows), cols]
            acc = jnp.zeros((CONV_ROWS, LANES), F32)
            for r in range(8):
                rolled = win if r == 0 else pltpu.roll(win, win_rows - r, 0)
                for a in range(2 * CONV_PAD // 8):
                    j = 8 * a + r - off
                    if 0 <= j < CONV_W:
                        acc = acc + rolled[8 * a:8 * a + CONV_ROWS, :] * w_ref[j:j + 1, cols]
            cv_ref[:, cols] = acc
        acc = cv_ref[...]
        mu = jnp.mean(acc, axis=-1, keepdims=True)
        xc = acc - mu
        yn = xc * lax.rsqrt(jnp.mean(xc * xc, axis=-1, keepdims=True) + EPS) * g_ref[...] + b_ref[...]
        y_ref[pl.ds(base, CONV_ROWS), :] = (yn * _sigmoid(yn)).astype(y_ref.dtype)
        return carry

    lax.fori_loop(0, S // CONV_ROWS, body, 0)


def _conv(a16, w_dw, ln_g, ln_b, B, S):
    cb = lambda off: off // BRANCH_W
    kern = functools.partial(_conv_kernel, S=S)
    return pl.pallas_call(
        kern,
        grid=(B,),
        in_specs=[
            pl.BlockSpec((S, BRANCH_W), lambda b: (b, cb(A16_CA))),
            pl.BlockSpec((S, BRANCH_W), lambda b: (b, cb(A16_CB))),
            pl.BlockSpec((CONV_W + 1, BRANCH_W), lambda b: (0, 0)),
            pl.BlockSpec((1, BRANCH_W), lambda b: (0, 0)),
            pl.BlockSpec((1, BRANCH_W), lambda b: (0, 0)),
        ],
        out_specs=pl.BlockSpec((S, BRANCH_W), lambda b: (b, 0)),
        out_shape=jax.ShapeDtypeStruct((B * S, BRANCH_W), BF16),
        scratch_shapes=[pltpu.VMEM((S + 2 * CONV_PAD, BRANCH_W), F32),
                        pltpu.VMEM((CONV_ROWS, BRANCH_W), F32)],
        compiler_params=_cparams("parallel"),
        name="glu_conv_ln",
    )(a16, a16, w_dw, ln_g, ln_b)


def _merge_kernel(x_ref, mod_ref, ym_ref, yd_ref, yg_ref, yc_ref, g0_ref, g1_ref, g2_ref, g3_ref,
                  wb_ref, wo_ref, n2_ref, wr_ref, x1_ref, h2_ref, wall_ref):
    ys = (ym_ref, yd_ref, yg_ref, yc_ref)
    gs = (g0_ref, g1_ref, g2_ref, g3_ref)
    merged = None
    for nbr in range(N_BRANCH):
        br = _dot(ys[nbr][...], wb_ref[nbr])
        term = _sigmoid(gs[nbr][...].astype(F32)) * br
        merged = term if merged is None else merged + term
    out = _dot(merged.astype(BF16), wo_ref[...])
    x1 = x_ref[...] + mod_ref[2:3, :] * out
    x1_ref[...] = x1
    y = x1 * lax.rsqrt(jnp.mean(x1 * x1, axis=-1, keepdims=True) + EPS) * n2_ref[...]
    h2 = y * (1.0 + mod_ref[4:5, :]) + mod_ref[3:4, :]
    h2_ref[...] = h2.astype(BF16)
    logits = jnp.dot(h2, wr_ref[...], preferred_element_type=F32, precision=lax.Precision.HIGHEST)
    lane = lax.broadcasted_iota(jnp.int32, logits.shape, 1)
    neg = -jnp.inf
    big = jnp.int32(LANES)
    is_g = lane < N_GROUPS
    gl = jnp.where(is_g, logits, neg)
    gmax = jnp.max(gl, axis=-1, keepdims=True)
    gidx = jnp.min(jnp.where(is_g & (gl == gmax), lane, big), axis=-1, keepdims=True)
    g_p = 1.0 / jnp.sum(jnp.where(is_g, jnp.exp(gl - gmax), 0.0), axis=-1, keepdims=True)
    e_lane = lane - N_GROUPS
    in_grp = (e_lane >= 0) & (e_lane < N_EXPERTS) & ((e_lane // EXPERTS_PER_GROUP) == gidx)
    el = jnp.where(in_grp, logits, neg)
    v1 = jnp.max(el, axis=-1, keepdims=True)
    i1 = jnp.min(jnp.where(in_grp & (el == v1), lane, big), axis=-1, keepdims=True)
    el2 = jnp.where(lane == i1, neg, el)
    v2 = jnp.max(el2, axis=-1, keepdims=True)
    i2 = jnp.min(jnp.where(in_grp & (lane != i1) & (el2 == v2), lane, big), axis=-1, keepdims=True)
    e2 = jnp.exp(v2 - v1)
    w1 = g_p / (1.0 + e2)
    w2 = g_p * e2 / (1.0 + e2)
    wall_ref[...] = jnp.where(lane == i1, w1, jnp.where(lane == i2, w2, 0.0))


def _merge(x2d, mod, a16, ym, yd, yg, yc, wb, wo, n2, wr, rows_per_mod, tm):
    T = x2d.shape[0]
    gcb = A16_GATE // D_MODEL
    row = lambda i: (i, 0)
    return pl.pallas_call(
        _merge_kernel,
        grid=(T // tm,),
        in_specs=[
            pl.BlockSpec((tm, D_MODEL), row),
            pl.BlockSpec((None, N_MOD, D_MODEL), lambda i: ((i * tm) // rows_per_mod, 0, 0)),
            pl.BlockSpec((tm, BRANCH_W), row),
            pl.BlockSpec((tm, BRANCH_W), row),
            pl.BlockSpec((tm, BRANCH_W), row),
            pl.BlockSpec((tm, BRANCH_W), row),
            pl.BlockSpec((tm, D_MODEL), lambda i: (i, gcb + 0)),
            pl.BlockSpec((tm, D_MODEL), lambda i: (i, gcb + 1)),
            pl.BlockSpec((tm, D_MODEL), lambda i: (i, gcb + 2)),
            pl.BlockSpec((tm, D_MODEL), lambda i: (i, gcb + 3)),
            pl.BlockSpec((N_BRANCH, BRANCH_W, D_MODEL), lambda i: (0, 0, 0)),
            pl.BlockSpec((D_MODEL, D_MODEL), lambda i: (0, 0)),
            pl.BlockSpec((1, D_MODEL), lambda i: (0, 0)),
            pl.BlockSpec((D_MODEL, LANES), lambda i: (0, 0)),
        ],
        out_specs=[
            pl.BlockSpec((tm, D_MODEL), row),
            pl.BlockSpec((tm, D_MODEL), row),
            pl.BlockSpec((tm, LANES), row),
        ],
        out_shape=[
            jax.ShapeDtypeStruct((T, D_MODEL), F32),
            jax.ShapeDtypeStruct((T, D_MODEL), BF16),
            jax.ShapeDtypeStruct((T, LANES), F32),
        ],
        compiler_params=_cparams("parallel"),
        name="merge_outproj_route",
    )(x2d, mod, ym, yd, yg, yc, a16, a16, a16, a16, wb, wo, n2, wr)


def _moe_kernel(h_ref, wall_ref, x1_ref, mod_ref, w1_ref, w3_ref, w2_ref, fn_ref, o_ref, acc_ref,
                *, final_norm):
    e = pl.program_id(1)

    @pl.when(e == 0)
    def _():
        acc_ref[...] = jnp.zeros_like(acc_ref)

    h = h_ref[...]
    a = _dot(h, w1_ref[...])
    b = _dot(h, w3_ref[...])
    wall = wall_ref[...]
    lane = lax.broadcasted_iota(jnp.int32, wall.shape, 1)
    wcol = jnp.sum(jnp.where(lane == e + N_GROUPS, wall, 0.0), axis=-1, keepdims=True)
    s = (a * _sigmoid(a)) * b * wcol
    acc_ref[...] += _dot(s.astype(BF16), w2_ref[...])

    @pl.when(e == N_EXPERTS - 1)
    def _():
        x2 = x1_ref[...] + mod_ref[5:6, :] * acc_ref[...]
        if final_norm:
            x2 = x2 * lax.rsqrt(jnp.mean(x2 * x2, axis=-1, keepdims=True) + EPS) * fn_ref[...]
        o_ref[...] = x2


def _moe(h2, wall, x1, mod, w1, w3, w2, fn, rows_per_mod, tm, final_norm):
    T = h2.shape[0]
    kern = functools.partial(_moe_kernel, final_norm=final_norm)
    return pl.pallas_call(
        kern,
        grid=(T // tm, N_EXPERTS),
        in_specs=[
            pl.BlockSpec((tm, D_MODEL), lambda i, e: (i, 0)),
            pl.BlockSpec((tm, LANES), lambda i, e: (i, 0)),
            pl.BlockSpec((tm, D_MODEL), lambda i, e: (i, 0)),
            pl.BlockSpec((None, N_MOD, D_MODEL), lambda i, e: ((i * tm) // rows_per_mod, 0, 0)),
            pl.BlockSpec((None, D_MODEL, D_EXPERT), lambda i, e: (e, 0, 0)),
            pl.BlockSpec((None, D_MODEL, D_EXPERT), lambda i, e: (e, 0, 0)),
            pl.BlockSpec((None, D_EXPERT, D_MODEL), lambda i, e: (e, 0, 0)),
            pl.BlockSpec((1, D_MODEL), lambda i, e: (0, 0)),
        ],
        out_specs=pl.BlockSpec((tm, D_MODEL), lambda i, e: (i, 0)),
        out_shape=jax.ShapeDtypeStruct((T, D_MODEL), F32),
        scratch_shapes=[pltpu.VMEM((tm, D_MODEL), F32)],
        compiler_params=_cparams("parallel", "arbitrary"),
        name="moe_experts",
    )(h2, wall, x1, mod, w1, w3, w2, fn)


def _split_w_in(w):
    sizes = (H_M * DK_M, H_M * DK_M, H_M * DV_M, H_M * DV_M, 2 * H_M, 2 * H_M,
             H_D * 2 * DQK_D, H_D * 2 * DQK_D, H_D * DV_D,
             H_G * DK_G, H_G * DK_G, H_G * DV_G, 2 * GATE_RANK, H_G * DV_G,
             BRANCH_W, BRANCH_W, N_BRANCH * D_MODEL)
    outs, acc = [], 0
    for s in sizes:
        outs.append(w[:, acc:acc + s])
        acc += s
    return outs


def _pack_layer_params(p):
    (m_q, m_k, m_v, m_o, m_i, m_f, d_q, d_k, d_v, g_q, g_k, g_v, g_a, g_r, c_a, c_b, gate) = \
        _split_w_in(p['w_in'])
    w16 = jnp.concatenate([m_q, m_k, m_v, m_o, gate, g_r, c_a, c_b], axis=1).astype(BF16)
    gqk = jnp.concatenate([g_q.reshape(D_MODEL, H_G, DK_G), g_k.reshape(D_MODEL, H_G, DK_G)],
                          axis=2).reshape(D_MODEL, 2 * H_G * DK_G)
    small = jnp.concatenate(
        [m_i, m_f, g_a, jnp.zeros((D_MODEL, LANES - 4 * H_M - 2 * GATE_RANK), F32)], axis=1)
    w32 = jnp.concatenate([d_q, d_k, d_v, gqk, g_v, small], axis=1).astype(BF16)
    bi = p['b_m_i'].reshape(2, H_M)
    bf = p['b_m_f'].reshape(2, H_M)
    bcol = jnp.stack([bi[0], bi[1], bf[0], bf[1]], axis=-1)
    wup = p['w_gla_up'].reshape(2, GATE_RANK, H_G, DK_G)
    wup_pad = jnp.zeros((H_G, 2, LANES, LANES), F32)
    bup = p['b_gla_gate'].reshape(2, H_G, DK_G)
    for d in range(2):
        blk = jnp.transpose(wup[d], (1, 0, 2))
        blk = jnp.concatenate([blk, blk], axis=-1)
        r0 = SM_GA + d * GATE_RANK
        wup_pad = wup_pad.at[:, d, r0:r0 + GATE_RANK, :].set(blk)
    bup2 = jnp.transpose(jnp.concatenate([bup, bup], axis=-1), (1, 0, 2))[:, :, None, :]
    wr = jnp.concatenate([p['w_group_router'], p['w_expert_router'],
                          jnp.zeros((D_MODEL, LANES - N_GROUPS - N_EXPERTS), F32)], axis=1)
    return dict(
        w16=w16, w32=w32, bcol=bcol.reshape(H_M, 1, 4), brow=bcol.reshape(H_M, 4, 1),
        wup=wup_pad.astype(BF16), bup=bup2,
        wdw=jnp.concatenate([p['w_dw'], jnp.zeros((1, BRANCH_W), F32)], axis=0),
        ln_g=p['conv_ln_g'].reshape(1, BRANCH_W), ln_b=p['conv_ln_b'].reshape(1, BRANCH_W),
        hn_m=p['hnorm_m'].reshape(1, BRANCH_W), hn_d=p['hnorm_d'].reshape(1, BRANCH_W),
        hn_g=p['hnorm_g'].reshape(1, BRANCH_W),
        lamv=jnp.stack([p['lam_q1'], p['lam_k1'], p['lam_q2'], p['lam_k2']], axis=0),
        wb=p['w_branch'].astype(BF16), wo=p['w_out'].astype(BF16),
        n1=p['norm1'].reshape(1, D_MODEL), n2=p['norm2'].reshape(1, D_MODEL), wr=wr,
        w1=p['w_e1'].reshape(N_EXPERTS, D_MODEL, D_EXPERT).astype(BF16),
        w3=p['w_e3'].reshape(N_EXPERTS, D_MODEL, D_EXPERT).astype(BF16),
        w2=p['w_e2'].reshape(N_EXPERTS, D_EXPERT, D_MODEL).astype(BF16),
    )


def _rope_tables(S):
    rows = S // GRID_W
    r, col = jnp.meshgrid(jnp.arange(rows, dtype=F32), jnp.arange(GRID_W, dtype=F32), indexing='ij')
    r, col = r.reshape(-1), col.reshape(-1)
    n_freq = DQK_D // 4
    inv = ROPE_BASE ** (-jnp.arange(n_freq, dtype=F32) / n_freq)
    ang = jnp.concatenate([r[:, None] * inv, col[:, None] * inv], axis=-1)
    cos, sin = jnp.cos(ang), jnp.sin(ang)
    cos_t = jnp.tile(cos, (1, LANES // (DQK_D // 2)))
    sin_t = jnp.tile(jnp.concatenate([-sin, sin], axis=-1), (1, LANES // DQK_D))
    return cos_t, sin_t


def _pick_tile(T, cap):
    t = min(T, cap)
    while T % t:
        t //= 2
    return t


def _layer(x2d, mod, pk, B, S, lam_init, ctx, final_norm, fn):
    T = B * S
    rows_per_mod = T // mod.shape[0]
    tm = _pick_tile(rows_per_mod, 1024)
    a16 = _inproj(x2d, mod, pk['n1'], pk['w16'], BF16, rows_per_mod, tm, 768)
    a32 = _inproj(x2d, mod, pk['n1'], pk['w32'], F32, rows_per_mod, tm, 896)

    L = min(MLSTM_CHUNK, S)
    sm = a32[:, A32_SM:A32_SM + 4 * H_M]
    gcol = jnp.transpose(sm.reshape(T, 2, 2, H_M), (3, 0, 1, 2)).reshape(H_M, T, 4)
    grow = jnp.transpose(gcol.reshape(H_M, T // L, L, 4), (0, 1, 3, 2))
    if ctx is None:
        c0 = jnp.zeros((B, 2, H_M, DK_M, DV_M), F32)
        n0 = jnp.zeros((B, 2, H_M, 1, DK_M), F32)
        m0 = jnp.zeros((B, 2, H_M, 1, LANES), F32)
        s0 = jnp.zeros((B, 2, H_G, DK_G, DV_G), F32)
        attn_ctx = None
    else:
        c0 = ctx['C']
        n0 = ctx['n'][:, :, :, None, :]
        m0 = jnp.broadcast_to(ctx['m'][:, :, :, None, None], (B, 2, H_M, 1, LANES))
        s0 = ctx['S']
        attn_ctx = (ctx['k'], ctx['v'], ctx['layer'], ctx['cos'], ctx['sin'])
    ym, c_f, n_f, m_f = _mlstm(a16, gcol, grow, pk['bcol'], pk['brow'], c0, n0, m0, pk['hn_m'], B, S)
    yd = _attn(a32, pk['lamv'], pk['hn_d'], B, S, lam_init, attn_ctx)
    yg, s_f = _gla(a32, a16, pk['wup'], pk['bup'], s0, pk['hn_g'], B, S)
    yc = _conv(a16, pk['wdw'], pk['ln_g'], pk['ln_b'], B, S)
    x1, h2, wall = _merge(x2d, mod, a16, ym, yd, yg, yc, pk['wb'], pk['wo'], pk['n2'], pk['wr'],
                          rows_per_mod, _pick_tile(rows_per_mod, 512))
    x2 = _moe(h2, wall, x1, mod, pk['w1'], pk['w3'], pk['w2'], fn, rows_per_mod,
              _pick_tile(rows_per_mod, 1024), final_norm)
    state = None
    if ctx is None:
        state = (a32[:, A32_DK:A32_DK + H_D * 2 * DQK_D].reshape(B, S, H_D, 2 * DQK_D),
                 a32[:, A32_DV:A32_DV + H_D * DV_D].reshape(B, S, H_D, DV_D),
                 c_f, n_f[:, :, :, 0, :], m_f[:, :, :, 0, 0], s_f)
    return x2, state


def kernel(x_prompt, x_sample, c, cache_diff_k, cache_diff_v, state_mlstm_C, state_mlstm_n, state_mlstm_m, state_gla_S, c_ctx, w_mod, b_mod, norm1, w_in, b_m_i, b_m_f, lam_q1, lam_k1, lam_q2, lam_k2, w_gla_up, b_gla_gate, w_dw, conv_ln_g, conv_ln_b, hnorm_m, hnorm_d, hnorm_g, w_branch, w_out, norm2, w_group_router, w_expert_router, w_e1, w_e3, w_e2, final_norm):
    Bp, Sp, _ = x_prompt.shape
    Bs, Ss, _ = x_sample.shape
    P = cache_diff_k.shape[2]
    n_cond = 8 * ((1 + Bs + 7) // 8)
    cond = jnp.concatenate([c_ctx[None, :], c, jnp.zeros((n_cond - 1 - Bs, D_MODEL), F32)], axis=0)
    mod_all = _modulation(cond, w_mod, b_mod).reshape(DEPTH, n_cond, N_MOD, D_MODEL)
    cos_t, sin_t = _rope_tables(Ss)
    ck4 = cache_diff_k.reshape(Bs, DEPTH, P, H_D * 2 * DQK_D)
    cv4 = cache_diff_v.reshape(Bs, DEPTH, P, H_D * DV_D)
    fn = final_norm.reshape(1, D_MODEL)
    yp = x_prompt.reshape(Bp * Sp, D_MODEL)
    ys = x_sample.reshape(Bs * Ss, D_MODEL)
    states = []
    for l in range(DEPTH):
        p = {'w_in': w_in[l], 'b_m_i': b_m_i[l], 'b_m_f': b_m_f[l], 'lam_q1': lam_q1[l],
             'lam_k1': lam_k1[l], 'lam_q2': lam_q2[l], 'lam_k2': lam_k2[l],
             'w_gla_up': w_gla_up[l], 'b_gla_gate': b_gla_gate[l], 'w_dw': w_dw[l],
             'conv_ln_g': conv_ln_g[l], 'conv_ln_b': conv_ln_b[l], 'hnorm_m': hnorm_m[l],
             'hnorm_d': hnorm_d[l], 'hnorm_g': hnorm_g[l], 'w_branch': w_branch[l],
             'w_out': w_out[l], 'norm1': norm1[l], 'norm2': norm2[l],
             'w_group_router': w_group_router[l], 'w_expert_router': w_expert_router[l],
             'w_e1': w_e1[l], 'w_e3': w_e3[l], 'w_e2': w_e2[l]}
        pk = _pack_layer_params(p)
        lam_init = 0.8 - 0.6 * math.exp(-0.3 * l)
        last = l == DEPTH - 1
        yp, st = _layer(yp, mod_all[l, 0:1], pk, Bp, Sp, lam_init, None, last, fn)
        states.append(st)
        ctx = {'k': ck4, 'v': cv4, 'layer': l, 'cos': cos_t, 'sin': sin_t,
               'C': state_mlstm_C[:, l], 'n': state_mlstm_n[:, l], 'm': state_mlstm_m[:, l],
               'S': state_gla_S[:, l]}
        ys, _ = _layer(ys, mod_all[l, 1:1 + Bs], pk, Bs, Ss, lam_init, ctx, last, fn)
    stack = lambda i: jnp.stack([s[i] for s in states], axis=1)
    return (yp.reshape(Bp, Sp, D_MODEL), ys.reshape(Bs, Ss, D_MODEL),
            stack(0), stack(1), stack(2), stack(3), stack(4), stack(5))
```

```python
import functools
import math

import jax
import jax.numpy as jnp
from jax import lax
from jax.experimental import pallas as pl
from jax.experimental.pallas import tpu as pltpu

F32 = jnp.float32
BF16 = jnp.bfloat16

D_MODEL = 1024
DEPTH = 2
GRID_W = 64
BRANCH_W = 512
N_BRANCH = 4
H_M, DK_M, DV_M = 4, 128, 128
H_D, DQK_D, DV_D = 4, 64, 128
H_G, DK_G, DV_G = 4, 64, 128
GATE_RANK = 16
GLA_TAU = 16.0
CONV_W = 31
N_GROUPS, EXPERTS_PER_GROUP, D_EXPERT = 4, 4, 512
N_EXPERTS = N_GROUPS * EXPERTS_PER_GROUP
ROPE_BASE = 10000.0
EPS = 1e-6
N_MOD = 6

LANES = 128
VMEM_LIMIT = 48 * 1024 * 1024

A16_MQ, A16_MK, A16_MV, A16_MO = 0, 512, 1024, 1536
A16_GATE, A16_GR, A16_CA, A16_CB = 2048, 6144, 6656, 7168
N_A16 = 7680
A32_DQ, A32_DK, A32_DV, A32_GQK, A32_GV, A32_SM = 0, 512, 1024, 1536, 2048, 2560
N_A32 = 2688
SM_MI, SM_MF, SM_GA = 0, 8, 16

MLSTM_CHUNK = 128
GLA_CHUNK = 64
GLA_SUB = 16
GLA_EXP_CLAMP = 80.0
CONV_ROWS = 64
CONV_PAD = 16


def _cparams(*sem):
    return pltpu.CompilerParams(dimension_semantics=sem, vmem_limit_bytes=VMEM_LIMIT)


def _log_sigmoid(x):
    return jnp.minimum(x, 0.0) - jnp.log1p(jnp.exp(-jnp.abs(x)))


def _sigmoid(x):
    return 1.0 / (1.0 + jnp.exp(-x))


def _dot(a, b):
    return jnp.dot(a, b, preferred_element_type=F32)


def _dot_nt(a, b):
    return lax.dot_general(a, b, (((1,), (1,)), ((), ())), preferred_element_type=F32)


def _dot_tn(a, b):
    return lax.dot_general(a, b, (((0,), (0,)), ((), ())), preferred_element_type=F32)


def _mod_kernel(c_ref, w_ref, b_ref, o_ref):
    c = c_ref[...]
    a = (c * _sigmoid(c)).astype(BF16)
    o_ref[...] = _dot(a, w_ref[...].astype(BF16)) + b_ref[...]


def _modulation(cond, w_mod, b_mod):
    R = cond.shape[0]
    tn = 512
    nmod = N_MOD * D_MODEL
    return pl.pallas_call(
        _mod_kernel,
        grid=(DEPTH, nmod // tn),
        in_specs=[
            pl.BlockSpec((R, D_MODEL), lambda l, j: (0, 0)),
            pl.BlockSpec((None, D_MODEL, tn), lambda l, j: (l, 0, j)),
            pl.BlockSpec((None, 1, tn), lambda l, j: (l, 0, j)),
        ],
        out_specs=pl.BlockSpec((None, R, tn), lambda l, j: (l, 0, j)),
        out_shape=jax.ShapeDtypeStruct((DEPTH, R, nmod), F32),
        compiler_params=_cparams("parallel", "parallel"),
        name="adaln_mod",
    )(cond, w_mod, b_mod.reshape(DEPTH, 1, nmod))


def _inproj_kernel(x_ref, mod_ref, g_ref, w_ref, o_ref, h_ref):
    @pl.when(pl.program_id(1) == 0)
    def _():
        x = x_ref[...]
        y = x * lax.rsqrt(jnp.mean(x * x, axis=-1, keepdims=True) + EPS) * g_ref[...]
        h_ref[...] = (y * (1.0 + mod_ref[1:2, :]) + mod_ref[0:1, :]).astype(BF16)

    o_ref[...] = _dot(h_ref[...], w_ref[...]).astype(o_ref.dtype)


def _inproj(x2d, mod, g, w, out_dtype, rows_per_mod, tm, tn):
    T = x2d.shape[0]
    N = w.shape[1]
    return pl.pallas_call(
        _inproj_kernel,
        grid=(T // tm, N // tn),
        in_specs=[
            pl.BlockSpec((tm, D_MODEL), lambda i, j: (i, 0)),
            pl.BlockSpec((None, N_MOD, D_MODEL), lambda i, j: ((i * tm) // rows_per_mod, 0, 0)),
            pl.BlockSpec((1, D_MODEL), lambda i, j: (0, 0)),
            pl.BlockSpec((D_MODEL, tn), lambda i, j: (0, j)),
        ],
        out_specs=pl.BlockSpec((tm, tn), lambda i, j: (i, j)),
        out_shape=jax.ShapeDtypeStruct((T, N), out_dtype),
        scratch_shapes=[pltpu.VMEM((tm, D_MODEL), BF16)],
        compiler_params=_cparams("parallel", "arbitrary"),
        name="norm_inproj",
    )(x2d, mod, g, w)


def _mlstm_local(c, q_ref, k_ref, v_ref, gr_ref, br_ref, pr_ref, bb_ref, mb_ref, kv_ref, rp_ref, L):
    scale = DK_M ** -0.5
    ti = lax.broadcasted_iota(jnp.int32, (L, L), 0)
    si = lax.broadcasted_iota(jnp.int32, (L, L), 1)
    sub = lax.broadcasted_iota(jnp.int32, (8, LANES), 0)
    rows = pl.ds(pl.multiple_of(c * L, L), L)
    q = q_ref[rows, :]
    v_ext = jnp.concatenate([v_ref[rows, :], jnp.ones((L, LANES), BF16)], axis=1)
    k_t = k_ref[rows, :].astype(F32).T
    qk = _dot(q, k_t.astype(BF16)) * scale
    grow = gr_ref[c] + br_ref[...]
    for d in range(2):
        rev = d == 1
        mask = (si >= ti) if rev else (si <= ti)
        src = ((si <= ti) if rev else (si >= ti)).astype(BF16)
        i_row = grow[d:d + 1, :]
        f_row = _log_sigmoid(grow[2 + d:3 + d, :])
        f8 = jnp.broadcast_to(f_row, (8, L))
        f_hi = f8.astype(BF16)
        f_r1 = f8 - f_hi.astype(F32)
        f_mid = f_r1.astype(BF16)
        f_lo = (f_r1 - f_mid.astype(F32)).astype(BF16)
        b_row = (_dot(f_hi, src) + _dot(f_mid, src) + _dot(f_lo, src))[0:1, :]
        b_col = jnp.sum(jnp.where(mask, f_row, 0.0), axis=1, keepdims=True)
        log_d = jnp.where(mask, b_col + (i_row - b_row), -jnp.inf)
        m_loc = jnp.max(log_d, axis=1, keepdims=True)
        smat = qk * jnp.exp(log_d - m_loc)
        pr_ref[d, rows, :] = _dot(smat.astype(BF16), v_ext)
        bb_ref[d, rows, :] = jnp.broadcast_to(b_col, (L, LANES))
        mb_ref[d, rows, :] = jnp.broadcast_to(m_loc, (L, LANES))
        b_last = jnp.sum(f_row, axis=1, keepdims=True)
        ls_row = b_last - b_row + i_row
        m2 = jnp.max(ls_row, axis=1, keepdims=True)
        kw_t = (k_t * jnp.exp(ls_row - m2)).astype(BF16)
        kv_ref[d, c] = scale * _dot(kw_t, v_ext)
        rp_ref[d, c] = jnp.where(sub == 0, b_last, m2)


def _mlstm_carry(c, d, carry, q_ref, pr_ref, bb_ref, mb_ref, kv_ref, rp_ref, h_ref, L):
    cn, m = carry
    two = lambda x: jnp.concatenate([x, x], axis=1)
    rows = pl.ds(pl.multiple_of(c * L, L), L)
    bb = bb_ref[d, rows, :]
    mb = mb_ref[d, rows, :]
    m_t = jnp.maximum(bb + m, mb)
    a_int = jnp.exp(bb + m - m_t)
    e_loc = jnp.exp(mb - m_t)
    nd = two(a_int) * _dot(q_ref[rows, :], cn.astype(BF16)) + two(e_loc) * pr_ref[d, rows, :]
    h_ref[d, rows, :] = nd[:, :DV_M] / jnp.maximum(jnp.abs(nd[:, DV_M:]), jnp.exp(-m_t))
    rp = rp_ref[d, c]
    b_last, m2 = rp[0:1, :], rp[1:2, :]
    m_new = jnp.maximum(b_last + m, m2)
    a_c = jnp.exp(b_last + m - m_new)
    e2 = jnp.exp(m2 - m_new)
    return two(a_c) * cn + two(e2) * kv_ref[d, c], m_new


def _mlstm_kernel(q_ref, k_ref, v_ref, og_ref, gr_ref, br_ref, c0_ref, n0_ref, m0_ref, hn_ref,
                  y_ref, c_out_ref, n_out_ref, m_out_ref,
                  pr_ref, bb_ref, mb_ref, kv_ref, rp_ref, h_ref, *, L, S):
    nch = S // L

    def local(ci, carry):
        _mlstm_local(ci, q_ref, k_ref, v_ref, gr_ref, br_ref, pr_ref, bb_ref, mb_ref, kv_ref,
                     rp_ref, L)
        return carry

    lax.fori_loop(0, nch, local, 0, unroll=2)
    step = functools.partial(_mlstm_carry, q_ref=q_ref, pr_ref=pr_ref, bb_ref=bb_ref, mb_ref=mb_ref,
                             kv_ref=kv_ref, rp_ref=rp_ref, h_ref=h_ref, L=L)

    def body(ci, carry):
        return step(ci, 0, carry[0]), step(nch - 1 - ci, 1, carry[1])

    def init(d):
        n_rep = jnp.broadcast_to(n0_ref[d], (DK_M, DK_M)).T
        return jnp.concatenate([c0_ref[d], n_rep], axis=1), m0_ref[d]

    fin = lax.fori_loop(0, nch, body, (init(0), init(1)), unroll=2)
    for d in range(2):
        cn, m = fin[d]
        c_out_ref[d] = cn[:, :DV_M]
        n_out_ref[d] = cn[:, DV_M:].T[0:1, :]
        m_out_ref[d] = m

    hm = h_ref[0] + h_ref[1]
    y = hm * lax.rsqrt(jnp.mean(hm * hm, axis=-1, keepdims=True) + EPS) * hn_ref[...]
    y_ref[...] = (y * _sigmoid(og_ref[...].astype(F32))).astype(y_ref.dtype)


def _mlstm(a16, grow, brow, c0, n0, m0, hnorm, B, S):
    L = min(MLSTM_CHUNK, S)
    nch = S // L
    cb = lambda off: off // LANES
    kern = functools.partial(_mlstm_kernel, L=L, S=S)
    return pl.pallas_call(
        kern,
        grid=(B, H_M),
        in_specs=[
            pl.BlockSpec((S, LANES), lambda b, h: (b, cb(A16_MQ) + h)),
            pl.BlockSpec((S, LANES), lambda b, h: (b, cb(A16_MK) + h)),
            pl.BlockSpec((S, LANES), lambda b, h: (b, cb(A16_MV) + h)),
            pl.BlockSpec((S, LANES), lambda b, h: (b, cb(A16_MO) + h)),
            pl.BlockSpec((None, nch, 4, L), lambda b, h: (h, b, 0, 0)),
            pl.BlockSpec((None, 4, 1), lambda b, h: (h, 0, 0)),
            pl.BlockSpec((None, 2, None, DK_M, DV_M), lambda b, h: (b, 0, h, 0, 0)),
            pl.BlockSpec((None, 2, None, 1, DK_M), lambda b, h: (b, 0, h, 0, 0)),
            pl.BlockSpec((None, 2, None, 1, LANES), lambda b, h: (b, 0, h, 0, 0)),
            pl.BlockSpec((1, LANES), lambda b, h: (0, h)),
        ],
        out_specs=[
            pl.BlockSpec((S, LANES), lambda b, h: (b, h)),
            pl.BlockSpec((None, 2, None, DK_M, DV_M), lambda b, h: (b, 0, h, 0, 0)),
            pl.BlockSpec((None, 2, None, 1, DK_M), lambda b, h: (b, 0, h, 0, 0)),
            pl.BlockSpec((None, 2, None, 1, LANES), lambda b, h: (b, 0, h, 0, 0)),
        ],
        out_shape=[
            jax.ShapeDtypeStruct((B * S, BRANCH_W), BF16),
            jax.ShapeDtypeStruct((B, 2, H_M, DK_M, DV_M), F32),
            jax.ShapeDtypeStruct((B, 2, H_M, 1, DK_M), F32),
            jax.ShapeDtypeStruct((B, 2, H_M, 1, LANES), F32),
        ],
        scratch_shapes=[pltpu.VMEM((2, S, 2 * DV_M), F32), pltpu.VMEM((2, S, LANES), F32),
                        pltpu.VMEM((2, S, LANES), F32), pltpu.VMEM((2, nch, DK_M, 2 * DV_M), F32),
                        pltpu.VMEM((2, nch, 8, LANES), F32), pltpu.VMEM((2, S, DV_M), F32)],
        compiler_params=_cparams("parallel", "parallel"),
        name="mlstm",
    )(a16, a16, a16, a16, grow, brow, c0, n0, m0, hnorm)


def _gla_local(c, q2_ref, k2_ref, v_ref, la_ref, oa_ref, qt_ref, u_ref, dec_ref, L):
    nb = L // GLA_SUB
    ti = lax.broadcasted_iota(jnp.int32, (L, L), 0)
    si = lax.broadcasted_iota(jnp.int32, (L, L), 1)
    row_blk = lax.broadcasted_iota(jnp.int32, (L, LANES), 0) // GLA_SUB
    lo_half = lax.broadcasted_iota(jnp.int32, (L, LANES), 1) < DK_G
    eye = (lax.broadcasted_iota(jnp.int32, (DK_G, LANES), 0)
           == lax.broadcasted_iota(jnp.int32, (DK_G, LANES), 1))
    rows = pl.ds(pl.multiple_of(c * L, L), L)
    q2 = q2_ref[rows, :]
    k2 = k2_ref[rows, :]
    v = v_ref[rows, :].astype(BF16)
    for d in range(2):
        rev = d == 1
        mask = (si >= ti) if rev else (si <= ti)
        tri = mask.astype(BF16)
        la2 = la_ref[d, rows, :]
        la_hi = la2.astype(BF16)
        la_lo = (la2 - la_hi.astype(F32)).astype(BF16)
        g2 = _dot(tri, la_hi) + _dot(tri, la_lo)
        qt_ref[d, rows, :] = (q2 * jnp.exp(g2))[:, :DK_G].astype(BF16)
        a_parts, b_parts = [], []
        for p in range(nb // 2):
            ia, ib = 2 * p, 2 * p + 1
            ra = ia * GLA_SUB + (GLA_SUB - 1 if rev else 0)
            rb = ib * GLA_SUB + (GLA_SUB - 1 if rev else 0)
            ref2 = jnp.where(lo_half, g2[ra:ra + 1, :], g2[rb:rb + 1, :])
            blk = jnp.where(lo_half, ia, ib)
            in_blk = row_blk == blk
            key_ok = (row_blk >= blk) if rev else (row_blk <= blk)
            a_parts.append(jnp.where(in_blk, q2 * jnp.exp(jnp.minimum(g2 - ref2, 0.0)), 0.0))
            b_parts.append(
                jnp.where(key_ok, k2 * jnp.exp(jnp.minimum(ref2 - g2, GLA_EXP_CLAMP)), 0.0))
        a_big = jnp.concatenate(a_parts, axis=1).astype(BF16)
        b_big = jnp.concatenate(b_parts, axis=1).astype(BF16)
        att = jnp.where(mask, _dot_nt(a_big, b_big), 0.0)
        oa_ref[d, rows, :] = _dot(att.astype(BF16), v)
        gl_row = 0 if rev else L - 1
        glast = g2[gl_row:gl_row + 1, :]
        kd = (k2 * jnp.exp(glast - g2))[:, :DK_G]
        u_ref[d, c] = _dot_tn(kd.astype(BF16), v)
        glast_col = jnp.sum(jnp.where(eye, glast, 0.0), axis=1, keepdims=True)
        dec_ref[d, c] = jnp.broadcast_to(jnp.exp(glast_col), (DK_G, DV_G))


def _gla_kernel(qk_ref, v_ref, sm_ref, wup_ref, bup_ref, s0_ref, gr_ref, hn_ref,
                y_ref, s_out_ref, la_ref, q2_ref, k2_ref, oa_ref, oi_ref, qt_ref, u_ref, dec_ref,
                *, L, S):
    nch = S // L
    sm = sm_ref[...].astype(BF16)
    for d in range(2):
        la_ref[d] = _log_sigmoid(_dot(sm, wup_ref[d]) + bup_ref[d]) * (1.0 / GLA_TAU)
    qk = qk_ref[...]
    qk_sw = pltpu.roll(qk, DK_G, 1)
    lo_half = lax.broadcasted_iota(jnp.int32, qk.shape, 1) < DK_G
    q2_ref[...] = jnp.where(lo_half, qk, qk_sw) * (DK_G ** -0.5)
    k2_ref[...] = jnp.where(lo_half, qk_sw, qk)

    def local(ci, carry):
        _gla_local(ci, q2_ref, k2_ref, v_ref, la_ref, oa_ref, qt_ref, u_ref, dec_ref, L)
        return carry

    lax.fori_loop(0, nch, local, 0, unroll=4)

    def body(ci, carry):
        out = []
        for d, c in ((0, ci), (1, nch - 1 - ci)):
            rows = pl.ds(pl.multiple_of(c * L, L), L)
            st = carry[d]
            oi_ref[d, rows, :] = _dot(qt_ref[d, rows, :], st.astype(BF16))
            out.append(dec_ref[d, c] * st + u_ref[d, c])
        return tuple(out)

    st_f, st_b = lax.fori_loop(0, nch, body, (s0_ref[0], s0_ref[1]), unroll=2)
    s_out_ref[0] = st_f
    s_out_ref[1] = st_b

    og = (oa_ref[0] + oi_ref[0]) + (oa_ref[1] + oi_ref[1])
    y = og * lax.rsqrt(jnp.mean(og * og, axis=-1, keepdims=True) + EPS) * hn_ref[...]
    gr = gr_ref[...].astype(F32)
    y_ref[...] = (y * (gr * _sigmoid(gr))).astype(y_ref.dtype)


def _gla(a32, a16, wup, bup, s0, hnorm, B, S):
    L = min(GLA_CHUNK, S)
    nch = S // L
    cb = lambda off: off // LANES
    kern = functools.partial(_gla_kernel, L=L, S=S)
    return pl.pallas_call(
        kern,
        grid=(B, H_G),
        in_specs=[
            pl.BlockSpec((S, LANES), lambda b, h: (b, cb(A32_GQK) + h)),
            pl.BlockSpec((S, LANES), lambda b, h: (b, cb(A32_GV) + h)),
            pl.BlockSpec((S, LANES), lambda b, h: (b, cb(A32_SM))),
            pl.BlockSpec((None, 2, LANES, LANES), lambda b, h: (h, 0, 0, 0)),
            pl.BlockSpec((None, 2, 1, LANES), lambda b, h: (h, 0, 0, 0)),
            pl.BlockSpec((None, 2, None, DK_G, DV_G), lambda b, h: (b, 0, h, 0, 0)),
            pl.BlockSpec((S, LANES), lambda b, h: (b, cb(A16_GR) + h)),
            pl.BlockSpec((1, LANES), lambda b, h: (0, h)),
        ],
        out_specs=[
            pl.BlockSpec((S, LANES), lambda b, h: (b, h)),
            pl.BlockSpec((None, 2, None, DK_G, DV_G), lambda b, h: (b, 0, h, 0, 0)),
        ],
        out_shape=[
            jax.ShapeDtypeStruct((B * S, BRANCH_W), BF16),
            jax.ShapeDtypeStruct((B, 2, H_G, DK_G, DV_G), F32),
        ],
        scratch_shapes=[pltpu.VMEM((2, S, LANES), F32), pltpu.VMEM((S, LANES), F32),
                        pltpu.VMEM((S, LANES), F32), pltpu.VMEM((2, S, DV_G), F32),
                        pltpu.VMEM((2, S, DV_G), F32), pltpu.VMEM((2, S, DK_G), BF16),
                        pltpu.VMEM((2, nch, DK_G, DV_G), F32), pltpu.VMEM((2, nch, DK_G, DV_G), F32)],
        compiler_params=_cparams("parallel", "parallel"),
        name="gla",
    )(a32, a32, a32, wup, bup, s0, a16, hnorm)


def _rope(x, cos, sin_signed):
    lane = lax.broadcasted_iota(jnp.int32, x.shape, 1)
    first = (lane % DQK_D) < (DQK_D // 2)
    partner = jnp.where(first, pltpu.roll(x, LANES - DQK_D // 2, 1), pltpu.roll(x, DQK_D // 2, 1))
    return x * cos + partner * sin_signed


def _attn_kernel(*refs, S, P, TQ, lam_init, has_ctx):
    if has_ctx:
        (q_ref, k_ref, v_ref, ck_ref, cv_ref, cos_ref, sin_ref, lam_ref, hn_ref,
         y_ref, kk_ref, vv_ref) = refs
    else:
        q_ref, k_ref, v_ref, lam_ref, hn_ref, y_ref, kk_ref, vv_ref = refs
    qi = pl.program_id(2)

    @pl.when(qi == 0)
    def _():
        k = k_ref[...]
        if has_ctx:
            k = _rope(k, cos_ref[...], sin_ref[...])
            kk_ref[S:S + P, :] = ck_ref[...].astype(BF16)
            vv_ref[S:S + P, :] = cv_ref[...].astype(BF16)
        kk_ref[0:S, :] = k.astype(BF16)
        vv_ref[0:S, :] = v_ref[...].astype(BF16)

    q = q_ref[...]
    if has_ctx:
        r0 = pl.multiple_of(qi * TQ, TQ)
        q = _rope(q, cos_ref[pl.ds(r0, TQ), :], sin_ref[pl.ds(r0, TQ), :])
    q = q * (DQK_D ** -0.5 * math.log2(math.e))
    lane = lax.broadcasted_iota(jnp.int32, q.shape, 1)
    kk = kk_ref[...]
    vv = vv_ref[...]
    lv = lam_ref[...]
    lam = (jnp.exp(jnp.sum(lv[0:1, :] * lv[1:2, :], axis=-1, keepdims=True))
           - jnp.exp(jnp.sum(lv[2:3, :] * lv[3:4, :], axis=-1, keepdims=True)) + lam_init)
    es, ls = [], []
    for comp in range(2):
        sel = (lane < DQK_D) if comp == 0 else (lane >= DQK_D)
        s = _dot_nt(jnp.where(sel, q, 0.0).astype(BF16), kk)
        e = jnp.exp2(s - jnp.max(s, axis=-1, keepdims=True))
        es.append(e)
        ls.append(jnp.sum(e, axis=-1, keepdims=True))
    w = es[0] * (1.0 / ls[0]) - es[1] * (lam / ls[1])
    o = _dot(w.astype(BF16), vv)
    y = o * lax.rsqrt(jnp.mean(o * o, axis=-1, keepdims=True) + EPS) * hn_ref[...]
    y_ref[...] = (y * (1.0 - lam_init)).astype(y_ref.dtype)


def _attn(a32, lamv, hnorm, B, S, lam_init, ctx=None):
    TQ = min(256, S)
    nq = S // TQ
    has_ctx = ctx is not None
    P = ctx[0].shape[2] if has_ctx else 0
    cb = lambda off: off // LANES
    kern = functools.partial(_attn_kernel, S=S, P=P, TQ=TQ, lam_init=lam_init, has_ctx=has_ctx)
    in_specs = [
        pl.BlockSpec((TQ, LANES), lambda b, h, i: (b * nq + i, cb(A32_DQ) + h)),
        pl.BlockSpec((S, LANES), lambda b, h, i: (b, cb(A32_DK) + h)),
        pl.BlockSpec((S, LANES), lambda b, h, i: (b, cb(A32_DV) + h)),
    ]
    args = [a32, a32, a32]
    if has_ctx:
        ck, cv, layer, cos, sin = ctx
        in_specs += [
            pl.BlockSpec((None, None, P, LANES), lambda b, h, i: (b, layer, 0, h)),
            pl.BlockSpec((None, None, P, LANES), lambda b, h, i: (b, layer, 0, h)),
            pl.BlockSpec((S, LANES), lambda b, h, i: (0, 0)),
            pl.BlockSpec((S, LANES), lambda b, h, i: (0, 0)),
        ]
        args += [ck, cv, cos, sin]
    in_specs += [
        pl.BlockSpec((4, DQK_D), lambda b, h, i: (0, 0)),
        pl.BlockSpec((1, LANES), lambda b, h, i: (0, h)),
    ]
    args += [lamv, hnorm]
    return pl.pallas_call(
        kern,
        grid=(B, H_D, nq),
        in_specs=in_specs,
        out_specs=pl.BlockSpec((TQ, LANES), lambda b, h, i: (b * nq + i, h)),
        out_shape=jax.ShapeDtypeStruct((B * S, BRANCH_W), BF16),
        scratch_shapes=[pltpu.VMEM((S + P, LANES), BF16), pltpu.VMEM((S + P, LANES), BF16)],
        compiler_params=_cparams("parallel", "parallel", "arbitrary"),
        name="diff_attn",
    )(*args)


def _conv_kernel(ca_ref, cb_ref, w_ref, g_ref, b_ref, y_ref, pad_ref, cv_ref, *, S):
    ca = ca_ref[...].astype(F32)
    cbv = cb_ref[...].astype(F32)
    zeros = jnp.zeros((CONV_PAD, BRANCH_W), F32)
    pad_ref[0:CONV_PAD, :] = zeros
    pad_ref[CONV_PAD + S:2 * CONV_PAD + S, :] = zeros
    pad_ref[CONV_PAD:CONV_PAD + S, :] = ca * _sigmoid(cbv)
    off = CONV_PAD - CONV_W // 2

    win_rows = CONV_ROWS + 2 * CONV_PAD

    def body(i, carry):
        base = pl.multiple_of(i * CONV_ROWS, CONV_ROWS)
        for lb in range(BRANCH_W // LANES):
            cols = slice(lb * LANES, (lb + 1) * LANES)
            win = pad_ref[pl.ds(base, win_rows), cols]
            acc = jnp.zeros((CONV_ROWS, LANES), F32)
            for r in range(8):
                rolled = win if r == 0 else pltpu.roll(win, win_rows - r, 0)
                for a in range(2 * CONV_PAD // 8):
                    j = 8 * a + r - off
                    if 0 <= j < CONV_W:
                        acc = acc + rolled[8 * a:8 * a + CONV_ROWS, :] * w_ref[j:j + 1, cols]
            cv_ref[:, cols] = acc
        acc = cv_ref[...]
        mu = jnp.mean(acc, axis=-1, keepdims=True)
        xc = acc - mu
        yn = xc * lax.rsqrt(jnp.mean(xc * xc, axis=-1, keepdims=True) + EPS) * g_ref[...] + b_ref[...]
        y_ref[pl.ds(base, CONV_ROWS), :] = (yn * _sigmoid(yn)).astype(y_ref.dtype)
        return carry

    lax.fori_loop(0, S // CONV_ROWS, body, 0)


def _conv(a16, w_dw, ln_g, ln_b, B, S):
    cb = lambda off: off // BRANCH_W
    kern = functools.partial(_conv_kernel, S=S)
    return pl.pallas_call(
        kern,
        grid=(B,),
        in_specs=[
            pl.BlockSpec((S, BRANCH_W), lambda b: (b, cb(A16_CA))),
            pl.BlockSpec((S, BRANCH_W), lambda b: (b, cb(A16_CB))),
            pl.BlockSpec((CONV_W + 1, BRANCH_W), lambda b: (0, 0)),
            pl.BlockSpec((1, BRANCH_W), lambda b: (0, 0)),
            pl.BlockSpec((1, BRANCH_W), lambda b: (0, 0)),
        ],
        out_specs=pl.BlockSpec((S, BRANCH_W), lambda b: (b, 0)),
        out_shape=jax.ShapeDtypeStruct((B * S, BRANCH_W), BF16),
        scratch_shapes=[pltpu.VMEM((S + 2 * CONV_PAD, BRANCH_W), F32),
                        pltpu.VMEM((CONV_ROWS, BRANCH_W), F32)],
        compiler_params=_cparams("parallel"),
        name="glu_conv_ln",
    )(a16, a16, w_dw, ln_g, ln_b)


def _merge_kernel(x_ref, mod_ref, ym_ref, yd_ref, yg_ref, yc_ref, g0_ref, g1_ref, g2_ref, g3_ref,
                  wb_ref, wo_ref, n2_ref, wr_ref, x1_ref, h2_ref, wall_ref):
    ys = (ym_ref, yd_ref, yg_ref, yc_ref)
    gs = (g0_ref, g1_ref, g2_ref, g3_ref)
    merged = None
    for nbr in range(N_BRANCH):
        br = _dot(ys[nbr][...], wb_ref[nbr])
        term = _sigmoid(gs[nbr][...].astype(F32)) * br
        merged = term if merged is None else merged + term
    out = _dot(merged.astype(BF16), wo_ref[...])
    x1 = x_ref[...] + mod_ref[2:3, :] * out
    x1_ref[...] = x1
    y = x1 * lax.rsqrt(jnp.mean(x1 * x1, axis=-1, keepdims=True) + EPS) * n2_ref[...]
    h2 = y * (1.0 + mod_ref[4:5, :]) + mod_ref[3:4, :]
    h2_ref[...] = h2.astype(BF16)
    logits = jnp.dot(h2, wr_ref[...], preferred_element_type=F32, precision=lax.Precision.HIGHEST)
    lane = lax.broadcasted_iota(jnp.int32, logits.shape, 1)
    neg = -jnp.inf
    big = jnp.int32(LANES)
    is_g = lane < N_GROUPS
    gl = jnp.where(is_g, logits, neg)
    gmax = jnp.max(gl, axis=-1, keepdims=True)
    gidx = jnp.min(jnp.where(is_g & (gl == gmax), lane, big), axis=-1, keepdims=True)
    g_p = 1.0 / jnp.sum(jnp.where(is_g, jnp.exp(gl - gmax), 0.0), axis=-1, keepdims=True)
    e_lane = lane - N_GROUPS
    in_grp = (e_lane >= 0) & (e_lane < N_EXPERTS) & ((e_lane // EXPERTS_PER_GROUP) == gidx)
    el = jnp.where(in_grp, logits, neg)
    v1 = jnp.max(el, axis=-1, keepdims=True)
    i1 = jnp.min(jnp.where(in_grp & (el == v1), lane, big), axis=-1, keepdims=True)
    el2 = jnp.where(lane == i1, neg, el)
    v2 = jnp.max(el2, axis=-1, keepdims=True)
    i2 = jnp.min(jnp.where(in_grp & (lane != i1) & (el2 == v2), lane, big), axis=-1, keepdims=True)
    e2 = jnp.exp(v2 - v1)
    w1 = g_p / (1.0 + e2)
    w2 = g_p * e2 / (1.0 + e2)
    wall_ref[...] = jnp.where(lane == i1, w1, jnp.where(lane == i2, w2, 0.0))


def _merge(x2d, mod, a16, ym, yd, yg, yc, wb, wo, n2, wr, rows_per_mod, tm):
    T = x2d.shape[0]
    gcb = A16_GATE // D_MODEL
    row = lambda i: (i, 0)
    return pl.pallas_call(
        _merge_kernel,
        grid=(T // tm,),
        in_specs=[
            pl.BlockSpec((tm, D_MODEL), row),
            pl.BlockSpec((None, N_MOD, D_MODEL), lambda i: ((i * tm) // rows_per_mod, 0, 0)),
            pl.BlockSpec((tm, BRANCH_W), row),
            pl.BlockSpec((tm, BRANCH_W), row),
            pl.BlockSpec((tm, BRANCH_W), row),
            pl.BlockSpec((tm, BRANCH_W), row),
            pl.BlockSpec((tm, D_MODEL), lambda i: (i, gcb + 0)),
            pl.BlockSpec((tm, D_MODEL), lambda i: (i, gcb + 1)),
            pl.BlockSpec((tm, D_MODEL), lambda i: (i, gcb + 2)),
            pl.BlockSpec((tm, D_MODEL), lambda i: (i, gcb + 3)),
            pl.BlockSpec((N_BRANCH, BRANCH_W, D_MODEL), lambda i: (0, 0, 0)),
            pl.BlockSpec((D_MODEL, D_MODEL), lambda i: (0, 0)),
            pl.BlockSpec((1, D_MODEL), lambda i: (0, 0)),
            pl.BlockSpec((D_MODEL, LANES), lambda i: (0, 0)),
        ],
        out_specs=[
            pl.BlockSpec((tm, D_MODEL), row),
            pl.BlockSpec((tm, D_MODEL), row),
            pl.BlockSpec((tm, LANES), row),
        ],
        out_shape=[
            jax.ShapeDtypeStruct((T, D_MODEL), F32),
            jax.ShapeDtypeStruct((T, D_MODEL), BF16),
            jax.ShapeDtypeStruct((T, LANES), F32),
        ],
        compiler_params=_cparams("parallel"),
        name="merge_outproj_route",
    )(x2d, mod, ym, yd, yg, yc, a16, a16, a16, a16, wb, wo, n2, wr)


def _moe_kernel(h_ref, wall_ref, x1_ref, mod_ref, w1_ref, w3_ref, w2_ref, fn_ref, o_ref, acc_ref,
                *, final_norm):
    e = pl.program_id(1)

    @pl.when(e == 0)
    def _():
        acc_ref[...] = jnp.zeros_like(acc_ref)

    h = h_ref[...]
    a = _dot(h, w1_ref[...])
    b = _dot(h, w3_ref[...])
    wall = wall_ref[...]
    lane = lax.broadcasted_iota(jnp.int32, wall.shape, 1)
    wcol = jnp.sum(jnp.where(lane == e + N_GROUPS, wall, 0.0), axis=-1, keepdims=True)
    s = (a * _sigmoid(a)) * b * wcol
    acc_ref[...] += _dot(s.astype(BF16), w2_ref[...])

    @pl.when(e == N_EXPERTS - 1)
    def _():
        x2 = x1_ref[...] + mod_ref[5:6, :] * acc_ref[...]
        if final_norm:
            x2 = x2 * lax.rsqrt(jnp.mean(x2 * x2, axis=-1, keepdims=True) + EPS) * fn_ref[...]
        o_ref[...] = x2


def _moe(h2, wall, x1, mod, w1, w3, w2, fn, rows_per_mod, tm, final_norm):
    T = h2.shape[0]
    kern = functools.partial(_moe_kernel, final_norm=final_norm)
    return pl.pallas_call(
        kern,
        grid=(T // tm, N_EXPERTS),
        in_specs=[
            pl.BlockSpec((tm, D_MODEL), lambda i, e: (i, 0)),
            pl.BlockSpec((tm, LANES), lambda i, e: (i, 0)),
            pl.BlockSpec((tm, D_MODEL), lambda i, e: (i, 0)),
            pl.BlockSpec((None, N_MOD, D_MODEL), lambda i, e: ((i * tm) // rows_per_mod, 0, 0)),
            pl.BlockSpec((None, D_MODEL, D_EXPERT), lambda i, e: (e, 0, 0)),
            pl.BlockSpec((None, D_MODEL, D_EXPERT), lambda i, e: (e, 0, 0)),
            pl.BlockSpec((None, D_EXPERT, D_MODEL), lambda i, e: (e, 0, 0)),
            pl.BlockSpec((1, D_MODEL), lambda i, e: (0, 0)),
        ],
        out_specs=pl.BlockSpec((tm, D_MODEL), lambda i, e: (i, 0)),
        out_shape=jax.ShapeDtypeStruct((T, D_MODEL), F32),
        scratch_shapes=[pltpu.VMEM((tm, D_MODEL), F32)],
        compiler_params=_cparams("parallel", "arbitrary"),
        name="moe_experts",
    )(h2, wall, x1, mod, w1, w3, w2, fn)


def _split_w_in(w):
    sizes = (H_M * DK_M, H_M * DK_M, H_M * DV_M, H_M * DV_M, 2 * H_M, 2 * H_M,
             H_D * 2 * DQK_D, H_D * 2 * DQK_D, H_D * DV_D,
             H_G * DK_G, H_G * DK_G, H_G * DV_G, 2 * GATE_RANK, H_G * DV_G,
             BRANCH_W, BRANCH_W, N_BRANCH * D_MODEL)
    outs, acc = [], 0
    for s in sizes:
        outs.append(w[:, acc:acc + s])
        acc += s
    return outs


def _pack_layer_params(p):
    (m_q, m_k, m_v, m_o, m_i, m_f, d_q, d_k, d_v, g_q, g_k, g_v, g_a, g_r, c_a, c_b, gate) = \
        _split_w_in(p['w_in'])
    w16 = jnp.concatenate([m_q, m_k, m_v, m_o, gate, g_r, c_a, c_b], axis=1).astype(BF16)
    gqk = jnp.concatenate([g_q.reshape(D_MODEL, H_G, DK_G), g_k.reshape(D_MODEL, H_G, DK_G)],
                          axis=2).reshape(D_MODEL, 2 * H_G * DK_G)
    small = jnp.concatenate(
        [m_i, m_f, g_a, jnp.zeros((D_MODEL, LANES - 4 * H_M - 2 * GATE_RANK), F32)], axis=1)
    w32 = jnp.concatenate([d_q, d_k, d_v, gqk, g_v, small], axis=1).astype(BF16)
    bi = p['b_m_i'].reshape(2, H_M)
    bf = p['b_m_f'].reshape(2, H_M)
    bcol = jnp.stack([bi[0], bi[1], bf[0], bf[1]], axis=-1)
    wup = p['w_gla_up'].reshape(2, GATE_RANK, H_G, DK_G)
    wup_pad = jnp.zeros((H_G, 2, LANES, LANES), F32)
    bup = p['b_gla_gate'].reshape(2, H_G, DK_G)
    for d in range(2):
        blk = jnp.transpose(wup[d], (1, 0, 2))
        blk = jnp.concatenate([blk, blk], axis=-1)
        r0 = SM_GA + d * GATE_RANK
        wup_pad = wup_pad.at[:, d, r0:r0 + GATE_RANK, :].set(blk)
    bup2 = jnp.transpose(jnp.concatenate([bup, bup], axis=-1), (1, 0, 2))[:, :, None, :]
    wr = jnp.concatenate([p['w_group_router'], p['w_expert_router'],
                          jnp.zeros((D_MODEL, LANES - N_GROUPS - N_EXPERTS), F32)], axis=1)
    return dict(
        w16=w16, w32=w32, brow=bcol.reshape(H_M, 4, 1),
        wup=wup_pad.astype(BF16), bup=bup2,
        wdw=jnp.concatenate([p['w_dw'], jnp.zeros((1, BRANCH_W), F32)], axis=0),
        ln_g=p['conv_ln_g'].reshape(1, BRANCH_W), ln_b=p['conv_ln_b'].reshape(1, BRANCH_W),
        hn_m=p['hnorm_m'].reshape(1, BRANCH_W), hn_d=p['hnorm_d'].reshape(1, BRANCH_W),
        hn_g=p['hnorm_g'].reshape(1, BRANCH_W),
        lamv=jnp.stack([p['lam_q1'], p['lam_k1'], p['lam_q2'], p['lam_k2']], axis=0),
        wb=p['w_branch'].astype(BF16), wo=p['w_out'].astype(BF16),
        n1=p['norm1'].reshape(1, D_MODEL), n2=p['norm2'].reshape(1, D_MODEL), wr=wr,
        w1=p['w_e1'].reshape(N_EXPERTS, D_MODEL, D_EXPERT).astype(BF16),
        w3=p['w_e3'].reshape(N_EXPERTS, D_MODEL, D_EXPERT).astype(BF16),
        w2=p['w_e2'].reshape(N_EXPERTS, D_EXPERT, D_MODEL).astype(BF16),
    )


def _rope_tables(S):
    rows = S // GRID_W
    r, col = jnp.meshgrid(jnp.arange(rows, dtype=F32), jnp.arange(GRID_W, dtype=F32), indexing='ij')
    r, col = r.reshape(-1), col.reshape(-1)
    n_freq = DQK_D // 4
    inv = ROPE_BASE ** (-jnp.arange(n_freq, dtype=F32) / n_freq)
    ang = jnp.concatenate([r[:, None] * inv, col[:, None] * inv], axis=-1)
    cos, sin = jnp.cos(ang), jnp.sin(ang)
    cos_t = jnp.tile(cos, (1, LANES // (DQK_D // 2)))
    sin_t = jnp.tile(jnp.concatenate([-sin, sin], axis=-1), (1, LANES // DQK_D))
    return cos_t, sin_t


def _pick_tile(T, cap):
    t = min(T, cap)
    while T % t:
        t //= 2
    return t


def _layer(x2d, mod, pk, B, S, lam_init, ctx, final_norm, fn):
    T = B * S
    rows_per_mod = T // mod.shape[0]
    tm = _pick_tile(rows_per_mod, 1024)
    a16 = _inproj(x2d, mod, pk['n1'], pk['w16'], BF16, rows_per_mod, tm, 768)
    a32 = _inproj(x2d, mod, pk['n1'], pk['w32'], F32, rows_per_mod, tm, 896)

    L = min(MLSTM_CHUNK, S)
    sm = a32[:, A32_SM:A32_SM + 4 * H_M]
    grow = jnp.transpose(sm.reshape(T // L, L, 4, H_M), (3, 0, 2, 1))
    if ctx is None:
        c0 = jnp.zeros((B, 2, H_M, DK_M, DV_M), F32)
        n0 = jnp.zeros((B, 2, H_M, 1, DK_M), F32)
        m0 = jnp.zeros((B, 2, H_M, 1, LANES), F32)
        s0 = jnp.zeros((B, 2, H_G, DK_G, DV_G), F32)
        attn_ctx = None
    else:
        c0 = ctx['C']
        n0 = ctx['n'][:, :, :, None, :]
        m0 = jnp.broadcast_to(ctx['m'][:, :, :, None, None], (B, 2, H_M, 1, LANES))
        s0 = ctx['S']
        attn_ctx = (ctx['k'], ctx['v'], ctx['layer'], ctx['cos'], ctx['sin'])
    ym, c_f, n_f, m_f = _mlstm(a16, grow, pk['brow'], c0, n0, m0, pk['hn_m'], B, S)
    yd = _attn(a32, pk['lamv'], pk['hn_d'], B, S, lam_init, attn_ctx)
    yg, s_f = _gla(a32, a16, pk['wup'], pk['bup'], s0, pk['hn_g'], B, S)
    yc = _conv(a16, pk['wdw'], pk['ln_g'], pk['ln_b'], B, S)
    x1, h2, wall = _merge(x2d, mod, a16, ym, yd, yg, yc, pk['wb'], pk['wo'], pk['n2'], pk['wr'],
                          rows_per_mod, _pick_tile(rows_per_mod, 512))
    x2 = _moe(h2, wall, x1, mod, pk['w1'], pk['w3'], pk['w2'], fn, rows_per_mod,
              _pick_tile(rows_per_mod, 1024), final_norm)
    state = None
    if ctx is None:
        state = (a32[:, A32_DK:A32_DK + H_D * 2 * DQK_D].reshape(B, S, H_D, 2 * DQK_D),
                 a32[:, A32_DV:A32_DV + H_D * DV_D].reshape(B, S, H_D, DV_D),
                 c_f, n_f[:, :, :, 0, :], m_f[:, :, :, 0, 0], s_f)
    return x2, state


def kernel(x_prompt, x_sample, c, cache_diff_k, cache_diff_v, state_mlstm_C, state_mlstm_n, state_mlstm_m, state_gla_S, c_ctx, w_mod, b_mod, norm1, w_in, b_m_i, b_m_f, lam_q1, lam_k1, lam_q2, lam_k2, w_gla_up, b_gla_gate, w_dw, conv_ln_g, conv_ln_b, hnorm_m, hnorm_d, hnorm_g, w_branch, w_out, norm2, w_group_router, w_expert_router, w_e1, w_e3, w_e2, final_norm):
    Bp, Sp, _ = x_prompt.shape
    Bs, Ss, _ = x_sample.shape
    P = cache_diff_k.shape[2]
    n_cond = 8 * ((1 + Bs + 7) // 8)
    cond = jnp.concatenate([c_ctx[None, :], c, jnp.zeros((n_cond - 1 - Bs, D_MODEL), F32)], axis=0)
    mod_all = _modulation(cond, w_mod, b_mod).reshape(DEPTH, n_cond, N_MOD, D_MODEL)
    cos_t, sin_t = _rope_tables(Ss)
    ck4 = cache_diff_k.reshape(Bs, DEPTH, P, H_D * 2 * DQK_D)
    cv4 = cache_diff_v.reshape(Bs, DEPTH, P, H_D * DV_D)
    fn = final_norm.reshape(1, D_MODEL)
    yp = x_prompt.reshape(Bp * Sp, D_MODEL)
    ys = x_sample.reshape(Bs * Ss, D_MODEL)
    states = []
    for l in range(DEPTH):
        p = {'w_in': w_in[l], 'b_m_i': b_m_i[l], 'b_m_f': b_m_f[l], 'lam_q1': lam_q1[l],
             'lam_k1': lam_k1[l], 'lam_q2': lam_q2[l], 'lam_k2': lam_k2[l],
             'w_gla_up': w_gla_up[l], 'b_gla_gate': b_gla_gate[l], 'w_dw': w_dw[l],
             'conv_ln_g': conv_ln_g[l], 'conv_ln_b': conv_ln_b[l], 'hnorm_m': hnorm_m[l],
             'hnorm_d': hnorm_d[l], 'hnorm_g': hnorm_g[l], 'w_branch': w_branch[l],
             'w_out': w_out[l], 'norm1': norm1[l], 'norm2': norm2[l],
             'w_group_router': w_group_router[l], 'w_expert_router': w_expert_router[l],
             'w_e1': w_e1[l], 'w_e3': w_e3[l], 'w_e2': w_e2[l]}
        pk = _pack_layer_params(p)
        lam_init = 0.8 - 0.6 * math.exp(-0.3 * l)
        last = l == DEPTH - 1
        yp, st = _layer(yp, mod_all[l, 0:1], pk, Bp, Sp, lam_init, None, last, fn)
        states.append(st)
        ctx = {'k': ck4, 'v': cv4, 'layer': l, 'cos': cos_t, 'sin': sin_t,
               'C': state_mlstm_C[:, l], 'n': state_mlstm_n[:, l], 'm': state_mlstm_m[:, l],
               'S': state_gla_S[:, l]}
        ys, _ = _layer(ys, mod_all[l, 1:1 + Bs], pk, Bs, Ss, lam_init, ctx, last, fn)
    stack = lambda i: jnp.stack([s[i] for s in states], axis=1)
    return (yp.reshape(Bp, Sp, D_MODEL), ys.reshape(Bs, Ss, D_MODEL),
            stack(0), stack(1), stack(2), stack(3), stack(4), stack(5))
```

```python
import functools
import math

import jax
import jax.numpy as jnp
from jax import lax
from jax.experimental import pallas as pl
from jax.experimental.pallas import tpu as pltpu

F32 = jnp.float32
BF16 = jnp.bfloat16

D_MODEL = 1024
DEPTH = 2
GRID_W = 64
BRANCH_W = 512
N_BRANCH = 4
H_M, DK_M, DV_M = 4, 128, 128
H_D, DQK_D, DV_D = 4, 64, 128
H_G, DK_G, DV_G = 4, 64, 128
GATE_RANK = 16
GLA_TAU = 16.0
CONV_W = 31
N_GROUPS, EXPERTS_PER_GROUP, D_EXPERT = 4, 4, 512
N_EXPERTS = N_GROUPS * EXPERTS_PER_GROUP
ROPE_BASE = 10000.0
EPS = 1e-6
N_MOD = 6

LANES = 128
VMEM_LIMIT = 48 * 1024 * 1024

A16_MQ, A16_MK, A16_MV, A16_MO = 0, 512, 1024, 1536
A16_GATE, A16_GR, A16_CA, A16_CB = 2048, 6144, 6656, 7168
N_A16 = 7680
A32_DQ, A32_DK, A32_DV, A32_GQK, A32_GV, A32_SM = 0, 512, 1024, 1536, 2048, 2560
N_A32 = 2688
SM_MI, SM_MF, SM_GA = 0, 8, 16

MLSTM_CHUNK = 128
GLA_CHUNK = 64
GLA_SUB = 16
GLA_EXP_CLAMP = 80.0
CONV_ROWS = 64
CONV_PAD = 16
TOK_SUB = D_MODEL // LANES
MOE_TILE = 256
ROW_DMA_UNROLL = 8


def _cparams(*sem):
    return pltpu.CompilerParams(dimension_semantics=sem, vmem_limit_bytes=VMEM_LIMIT)


def _log_sigmoid(x):
    return jnp.minimum(x, 0.0) - jnp.log1p(jnp.exp(-jnp.abs(x)))


def _sigmoid(x):
    return 1.0 / (1.0 + jnp.exp(-x))


def _dot(a, b):
    return jnp.dot(a, b, preferred_element_type=F32)


def _dot_nt(a, b):
    return lax.dot_general(a, b, (((1,), (1,)), ((), ())), preferred_element_type=F32)


def _dot_tn(a, b):
    return lax.dot_general(a, b, (((0,), (0,)), ((), ())), preferred_element_type=F32)


def _mod_kernel(c_ref, w_ref, b_ref, o_ref):
    c = c_ref[...]
    a = (c * _sigmoid(c)).astype(BF16)
    o_ref[...] = _dot(a, w_ref[...].astype(BF16)) + b_ref[...]


def _modulation(cond, w_mod, b_mod):
    R = cond.shape[0]
    tn = 512
    nmod = N_MOD * D_MODEL
    return pl.pallas_call(
        _mod_kernel,
        grid=(DEPTH, nmod // tn),
        in_specs=[
            pl.BlockSpec((R, D_MODEL), lambda l, j: (0, 0)),
            pl.BlockSpec((None, D_MODEL, tn), lambda l, j: (l, 0, j)),
            pl.BlockSpec((None, 1, tn), lambda l, j: (l, 0, j)),
        ],
        out_specs=pl.BlockSpec((None, R, tn), lambda l, j: (l, 0, j)),
        out_shape=jax.ShapeDtypeStruct((DEPTH, R, nmod), F32),
        compiler_params=_cparams("parallel", "parallel"),
        name="adaln_mod",
    )(cond, w_mod, b_mod.reshape(DEPTH, 1, nmod))


def _inproj_kernel(x_ref, mod_ref, g_ref, w_ref, o_ref, h_ref):
    @pl.when(pl.program_id(1) == 0)
    def _():
        x = x_ref[...]
        y = x * lax.rsqrt(jnp.mean(x * x, axis=-1, keepdims=True) + EPS) * g_ref[...]
        h_ref[...] = (y * (1.0 + mod_ref[1:2, :]) + mod_ref[0:1, :]).astype(BF16)

    o_ref[...] = _dot(h_ref[...], w_ref[...]).astype(o_ref.dtype)


def _inproj(x2d, mod, g, w, out_dtype, rows_per_mod, tm, tn):
    T = x2d.shape[0]
    N = w.shape[1]
    return pl.pallas_call(
        _inproj_kernel,
        grid=(T // tm, N // tn),
        in_specs=[
            pl.BlockSpec((tm, D_MODEL), lambda i, j: (i, 0)),
            pl.BlockSpec((None, N_MOD, D_MODEL), lambda i, j: ((i * tm) // rows_per_mod, 0, 0)),
            pl.BlockSpec((1, D_MODEL), lambda i, j: (0, 0)),
            pl.BlockSpec((D_MODEL, tn), lambda i, j: (0, j)),
        ],
        out_specs=pl.BlockSpec((tm, tn), lambda i, j: (i, j)),
        out_shape=jax.ShapeDtypeStruct((T, N), out_dtype),
        scratch_shapes=[pltpu.VMEM((tm, D_MODEL), BF16)],
        compiler_params=_cparams("parallel", "arbitrary"),
        name="norm_inproj",
    )(x2d, mod, g, w)


def _mlstm_local(c, q_ref, k_ref, v_ref, gr_ref, br_ref, pr_ref, bb_ref, mb_ref, kv_ref, rp_ref, L):
    scale = DK_M ** -0.5
    ti = lax.broadcasted_iota(jnp.int32, (L, L), 0)
    si = lax.broadcasted_iota(jnp.int32, (L, L), 1)
    sub = lax.broadcasted_iota(jnp.int32, (8, LANES), 0)
    rows = pl.ds(pl.multiple_of(c * L, L), L)
    q = q_ref[rows, :]
    v_ext = jnp.concatenate([v_ref[rows, :], jnp.ones((L, LANES), BF16)], axis=1)
    k_t = k_ref[rows, :].astype(F32).T
    qk = _dot(q, k_t.astype(BF16)) * scale
    grow = gr_ref[c] + br_ref[...]
    for d in range(2):
        rev = d == 1
        mask = (si >= ti) if rev else (si <= ti)
        src = ((si <= ti) if rev else (si >= ti)).astype(BF16)
        i_row = grow[d:d + 1, :]
        f_row = _log_sigmoid(grow[2 + d:3 + d, :])
        f8 = jnp.broadcast_to(f_row, (8, L))
        f_hi = f8.astype(BF16)
        f_r1 = f8 - f_hi.astype(F32)
        f_mid = f_r1.astype(BF16)
        f_lo = (f_r1 - f_mid.astype(F32)).astype(BF16)
        b_row = (_dot(f_hi, src) + _dot(f_mid, src) + _dot(f_lo, src))[0:1, :]
        b_col = jnp.sum(jnp.where(mask, f_row, 0.0), axis=1, keepdims=True)
        log_d = jnp.where(mask, b_col + (i_row - b_row), -jnp.inf)
        m_loc = jnp.max(log_d, axis=1, keepdims=True)
        smat = qk * jnp.exp(log_d - m_loc)
        pr_ref[d, rows, :] = _dot(smat.astype(BF16), v_ext)
        bb_ref[d, rows, :] = jnp.broadcast_to(b_col, (L, LANES))
        mb_ref[d, rows, :] = jnp.broadcast_to(m_loc, (L, LANES))
        b_last = jnp.sum(f_row, axis=1, keepdims=True)
        ls_row = b_last - b_row + i_row
        m2 = jnp.max(ls_row, axis=1, keepdims=True)
        kw_t = (k_t * jnp.exp(ls_row - m2)).astype(BF16)
        kv_ref[d, c] = scale * _dot(kw_t, v_ext)
        rp_ref[d, c] = jnp.where(sub == 0, b_last, m2)


def _mlstm_carry(c, d, carry, q_ref, pr_ref, bb_ref, mb_ref, kv_ref, rp_ref, h_ref, L):
    cn, m = carry
    two = lambda x: jnp.concatenate([x, x], axis=1)
    rows = pl.ds(pl.multiple_of(c * L, L), L)
    bb = bb_ref[d, rows, :]
    mb = mb_ref[d, rows, :]
    m_t = jnp.maximum(bb + m, mb)
    a_int = jnp.exp(bb + m - m_t)
    e_loc = jnp.exp(mb - m_t)
    nd = two(a_int) * _dot(q_ref[rows, :], cn.astype(BF16)) + two(e_loc) * pr_ref[d, rows, :]
    h_ref[d, rows, :] = nd[:, :DV_M] / jnp.maximum(jnp.abs(nd[:, DV_M:]), jnp.exp(-m_t))
    rp = rp_ref[d, c]
    b_last, m2 = rp[0:1, :], rp[1:2, :]
    m_new = jnp.maximum(b_last + m, m2)
    a_c = jnp.exp(b_last + m - m_new)
    e2 = jnp.exp(m2 - m_new)
    return two(a_c) * cn + two(e2) * kv_ref[d, c], m_new


def _mlstm_kernel(q_ref, k_ref, v_ref, og_ref, gr_ref, br_ref, c0_ref, n0_ref, m0_ref, hn_ref,
                  y_ref, c_out_ref, n_out_ref, m_out_ref,
                  pr_ref, bb_ref, mb_ref, kv_ref, rp_ref, h_ref, *, L, S):
    nch = S // L

    def local(ci, carry):
        _mlstm_local(ci, q_ref, k_ref, v_ref, gr_ref, br_ref, pr_ref, bb_ref, mb_ref, kv_ref,
                     rp_ref, L)
        return carry

    lax.fori_loop(0, nch, local, 0, unroll=2)
    step = functools.partial(_mlstm_carry, q_ref=q_ref, pr_ref=pr_ref, bb_ref=bb_ref, mb_ref=mb_ref,
                             kv_ref=kv_ref, rp_ref=rp_ref, h_ref=h_ref, L=L)

    def body(ci, carry):
        return step(ci, 0, carry[0]), step(nch - 1 - ci, 1, carry[1])

    def init(d):
        n_rep = jnp.broadcast_to(n0_ref[d], (DK_M, DK_M)).T
        return jnp.concatenate([c0_ref[d], n_rep], axis=1), m0_ref[d]

    fin = lax.fori_loop(0, nch, body, (init(0), init(1)), unroll=2)
    for d in range(2):
        cn, m = fin[d]
        c_out_ref[d] = cn[:, :DV_M]
        n_out_ref[d] = cn[:, DV_M:].T[0:1, :]
        m_out_ref[d] = m

    hm = h_ref[0] + h_ref[1]
    y = hm * lax.rsqrt(jnp.mean(hm * hm, axis=-1, keepdims=True) + EPS) * hn_ref[...]
    y_ref[...] = (y * _sigmoid(og_ref[...].astype(F32))).astype(y_ref.dtype)


def _mlstm(a16, grow, brow, c0, n0, m0, hnorm, B, S):
    L = min(MLSTM_CHUNK, S)
    nch = S // L
    cb = lambda off: off // LANES
    kern = functools.partial(_mlstm_kernel, L=L, S=S)
    return pl.pallas_call(
        kern,
        grid=(B, H_M),
        in_specs=[
            pl.BlockSpec((S, LANES), lambda b, h: (b, cb(A16_MQ) + h)),
            pl.BlockSpec((S, LANES), lambda b, h: (b, cb(A16_MK) + h)),
            pl.BlockSpec((S, LANES), lambda b, h: (b, cb(A16_MV) + h)),
            pl.BlockSpec((S, LANES), lambda b, h: (b, cb(A16_MO) + h)),
            pl.BlockSpec((None, nch, 4, L), lambda b, h: (h, b, 0, 0)),
            pl.BlockSpec((None, 4, 1), lambda b, h: (h, 0, 0)),
            pl.BlockSpec((None, 2, None, DK_M, DV_M), lambda b, h: (b, 0, h, 0, 0)),
            pl.BlockSpec((None, 2, None, 1, DK_M), lambda b, h: (b, 0, h, 0, 0)),
            pl.BlockSpec((None, 2, None, 1, LANES), lambda b, h: (b, 0, h, 0, 0)),
            pl.BlockSpec((1, LANES), lambda b, h: (0, h)),
        ],
        out_specs=[
            pl.BlockSpec((S, LANES), lambda b, h: (b, h)),
            pl.BlockSpec((None, 2, None, DK_M, DV_M), lambda b, h: (b, 0, h, 0, 0)),
            pl.BlockSpec((None, 2, None, 1, DK_M), lambda b, h: (b, 0, h, 0, 0)),
            pl.BlockSpec((None, 2, None, 1, LANES), lambda b, h: (b, 0, h, 0, 0)),
        ],
        out_shape=[
            jax.ShapeDtypeStruct((B * S, BRANCH_W), BF16),
            jax.ShapeDtypeStruct((B, 2, H_M, DK_M, DV_M), F32),
            jax.ShapeDtypeStruct((B, 2, H_M, 1, DK_M), F32),
            jax.ShapeDtypeStruct((B, 2, H_M, 1, LANES), F32),
        ],
        scratch_shapes=[pltpu.VMEM((2, S, 2 * DV_M), F32), pltpu.VMEM((2, S, LANES), F32),
                        pltpu.VMEM((2, S, LANES), F32), pltpu.VMEM((2, nch, DK_M, 2 * DV_M), F32),
                        pltpu.VMEM((2, nch, 8, LANES), F32), pltpu.VMEM((2, S, DV_M), F32)],
        compiler_params=_cparams("parallel", "parallel"),
        name="mlstm",
    )(a16, a16, a16, a16, grow, brow, c0, n0, m0, hnorm)


def _gla_local(c, q2_ref, k2_ref, v_ref, la_ref, oa_ref, qt_ref, u_ref, dec_ref, L):
    nb = L // GLA_SUB
    ti = lax.broadcasted_iota(jnp.int32, (L, L), 0)
    si = lax.broadcasted_iota(jnp.int32, (L, L), 1)
    row_blk = lax.broadcasted_iota(jnp.int32, (L, LANES), 0) // GLA_SUB
    lo_half = lax.broadcasted_iota(jnp.int32, (L, LANES), 1) < DK_G
    eye = (lax.broadcasted_iota(jnp.int32, (DK_G, LANES), 0)
           == lax.broadcasted_iota(jnp.int32, (DK_G, LANES), 1))
    rows = pl.ds(pl.multiple_of(c * L, L), L)
    q2 = q2_ref[rows, :]
    k2 = k2_ref[rows, :]
    v = v_ref[rows, :].astype(BF16)
    for d in range(2):
        rev = d == 1
        mask = (si >= ti) if rev else (si <= ti)
        tri = mask.astype(BF16)
        la2 = la_ref[d, rows, :]
        la_hi = la2.astype(BF16)
        la_lo = (la2 - la_hi.astype(F32)).astype(BF16)
        g2 = _dot(tri, la_hi) + _dot(tri, la_lo)
        qt_ref[d, rows, :] = (q2 * jnp.exp(g2))[:, :DK_G].astype(BF16)
        a_parts, b_parts = [], []
        for p in range(nb // 2):
            ia, ib = 2 * p, 2 * p + 1
            ra = ia * GLA_SUB + (GLA_SUB - 1 if rev else 0)
            rb = ib * GLA_SUB + (GLA_SUB - 1 if rev else 0)
            ref2 = jnp.where(lo_half, g2[ra:ra + 1, :], g2[rb:rb + 1, :])
            blk = jnp.where(lo_half, ia, ib)
            in_blk = row_blk == blk
            key_ok = (row_blk >= blk) if rev else (row_blk <= blk)
            a_parts.append(jnp.where(in_blk, q2 * jnp.exp(jnp.minimum(g2 - ref2, 0.0)), 0.0))
            b_parts.append(
                jnp.where(key_ok, k2 * jnp.exp(jnp.minimum(ref2 - g2, GLA_EXP_CLAMP)), 0.0))
        a_big = jnp.concatenate(a_parts, axis=1).astype(BF16)
        b_big = jnp.concatenate(b_parts, axis=1).astype(BF16)
        att = jnp.where(mask, _dot_nt(a_big, b_big), 0.0)
        oa_ref[d, rows, :] = _dot(att.astype(BF16), v)
        gl_row = 0 if rev else L - 1
        glast = g2[gl_row:gl_row + 1, :]
        kd = (k2 * jnp.exp(glast - g2))[:, :DK_G]
        u_ref[d, c] = _dot_tn(kd.astype(BF16), v)
        glast_col = jnp.sum(jnp.where(eye, glast, 0.0), axis=1, keepdims=True)
        dec_ref[d, c] = jnp.broadcast_to(jnp.exp(glast_col), (DK_G, DV_G))


def _gla_kernel(qk_ref, v_ref, sm_ref, wup_ref, bup_ref, s0_ref, gr_ref, hn_ref,
                y_ref, s_out_ref, la_ref, q2_ref, k2_ref, oa_ref, oi_ref, qt_ref, u_ref, dec_ref,
                *, L, S):
    nch = S // L
    sm = sm_ref[...].astype(BF16)
    for d in range(2):
        la_ref[d] = _log_sigmoid(_dot(sm, wup_ref[d]) + bup_ref[d]) * (1.0 / GLA_TAU)
    qk = qk_ref[...]
    qk_sw = pltpu.roll(qk, DK_G, 1)
    lo_half = lax.broadcasted_iota(jnp.int32, qk.shape, 1) < DK_G
    q2_ref[...] = jnp.where(lo_half, qk, qk_sw) * (DK_G ** -0.5)
    k2_ref[...] = jnp.where(lo_half, qk_sw, qk)

    def local(ci, carry):
        _gla_local(ci, q2_ref, k2_ref, v_ref, la_ref, oa_ref, qt_ref, u_ref, dec_ref, L)
        return carry

    lax.fori_loop(0, nch, local, 0, unroll=4)

    def body(ci, carry):
        out = []
        for d, c in ((0, ci), (1, nch - 1 - ci)):
            rows = pl.ds(pl.multiple_of(c * L, L), L)
            st = carry[d]
            oi_ref[d, rows, :] = _dot(qt_ref[d, rows, :], st.astype(BF16))
            out.append(dec_ref[d, c] * st + u_ref[d, c])
        return tuple(out)

    st_f, st_b = lax.fori_loop(0, nch, body, (s0_ref[0], s0_ref[1]), unroll=2)
    s_out_ref[0] = st_f
    s_out_ref[1] = st_b

    og = (oa_ref[0] + oi_ref[0]) + (oa_ref[1] + oi_ref[1])
    y = og * lax.rsqrt(jnp.mean(og * og, axis=-1, keepdims=True) + EPS) * hn_ref[...]
    gr = gr_ref[...].astype(F32)
    y_ref[...] = (y * (gr * _sigmoid(gr))).astype(y_ref.dtype)


def _gla(a32, a16, wup, bup, s0, hnorm, B, S):
    L = min(GLA_CHUNK, S)
    nch = S // L
    cb = lambda off: off // LANES
    kern = functools.partial(_gla_kernel, L=L, S=S)
    return pl.pallas_call(
        kern,
        grid=(B, H_G),
        in_specs=[
            pl.BlockSpec((S, LANES), lambda b, h: (b, cb(A32_GQK) + h)),
            pl.BlockSpec((S, LANES), lambda b, h: (b, cb(A32_GV) + h)),
            pl.BlockSpec((S, LANES), lambda b, h: (b, cb(A32_SM))),
            pl.BlockSpec((None, 2, LANES, LANES), lambda b, h: (h, 0, 0, 0)),
            pl.BlockSpec((None, 2, 1, LANES), lambda b, h: (h, 0, 0, 0)),
            pl.BlockSpec((None, 2, None, DK_G, DV_G), lambda b, h: (b, 0, h, 0, 0)),
            pl.BlockSpec((S, LANES), lambda b, h: (b, cb(A16_GR) + h)),
            pl.BlockSpec((1, LANES), lambda b, h: (0, h)),
        ],
        out_specs=[
            pl.BlockSpec((S, LANES), lambda b, h: (b, h)),
            pl.BlockSpec((None, 2, None, DK_G, DV_G), lambda b, h: (b, 0, h, 0, 0)),
        ],
        out_shape=[
            jax.ShapeDtypeStruct((B * S, BRANCH_W), BF16),
            jax.ShapeDtypeStruct((B, 2, H_G, DK_G, DV_G), F32),
        ],
        scratch_shapes=[pltpu.VMEM((2, S, LANES), F32), pltpu.VMEM((S, LANES), F32),
                        pltpu.VMEM((S, LANES), F32), pltpu.VMEM((2, S, DV_G), F32),
                        pltpu.VMEM((2, S, DV_G), F32), pltpu.VMEM((2, S, DK_G), BF16),
                        pltpu.VMEM((2, nch, DK_G, DV_G), F32), pltpu.VMEM((2, nch, DK_G, DV_G), F32)],
        compiler_params=_cparams("parallel", "parallel"),
        name="gla",
    )(a32, a32, a32, wup, bup, s0, a16, hnorm)


def _rope(x, cos, sin_signed):
    lane = lax.broadcasted_iota(jnp.int32, x.shape, 1)
    first = (lane % DQK_D) < (DQK_D // 2)
    partner = jnp.where(first, pltpu.roll(x, LANES - DQK_D // 2, 1), pltpu.roll(x, DQK_D // 2, 1))
    return x * cos + partner * sin_signed


def _attn_kernel(*refs, S, P, TQ, lam_init, has_ctx):
    if has_ctx:
        (q_ref, k_ref, v_ref, ck_ref, cv_ref, cos_ref, sin_ref, lam_ref, hn_ref,
         y_ref, kk_ref, vv_ref) = refs
    else:
        q_ref, k_ref, v_ref, lam_ref, hn_ref, y_ref, kk_ref, vv_ref = refs
    qi = pl.program_id(2)

    @pl.when(qi == 0)
    def _():
        k = k_ref[...]
        if has_ctx:
            k = _rope(k, cos_ref[...], sin_ref[...])
            kk_ref[S:S + P, :] = ck_ref[...].astype(BF16)
            vv_ref[S:S + P, :] = cv_ref[...].astype(BF16)
        kk_ref[0:S, :] = k.astype(BF16)
        vv_ref[0:S, :] = v_ref[...].astype(BF16)

    q = q_ref[...]
    if has_ctx:
        r0 = pl.multiple_of(qi * TQ, TQ)
        q = _rope(q, cos_ref[pl.ds(r0, TQ), :], sin_ref[pl.ds(r0, TQ), :])
    q = q * (DQK_D ** -0.5 * math.log2(math.e))
    lane = lax.broadcasted_iota(jnp.int32, q.shape, 1)
    kk = kk_ref[...]
    vv = vv_ref[...]
    lv = lam_ref[...]
    lam = (jnp.exp(jnp.sum(lv[0:1, :] * lv[1:2, :], axis=-1, keepdims=True))
           - jnp.exp(jnp.sum(lv[2:3, :] * lv[3:4, :], axis=-1, keepdims=True)) + lam_init)
    es, ls = [], []
    for comp in range(2):
        sel = (lane < DQK_D) if comp == 0 else (lane >= DQK_D)
        s = _dot_nt(jnp.where(sel, q, 0.0).astype(BF16), kk)
        e = jnp.exp2(s - jnp.max(s, axis=-1, keepdims=True))
        es.append(e)
        ls.append(jnp.sum(e, axis=-1, keepdims=True))
    w = es[0] * (1.0 / ls[0]) - es[1] * (lam / ls[1])
    o = _dot(w.astype(BF16), vv)
    y = o * lax.rsqrt(jnp.mean(o * o, axis=-1, keepdims=True) + EPS) * hn_ref[...]
    y_ref[...] = (y * (1.0 - lam_init)).astype(y_ref.dtype)


def _attn(a32, lamv, hnorm, B, S, lam_init, ctx=None):
    TQ = min(256, S)
    nq = S // TQ
    has_ctx = ctx is not None
    P = ctx[0].shape[2] if has_ctx else 0
    cb = lambda off: off // LANES
    kern = functools.partial(_attn_kernel, S=S, P=P, TQ=TQ, lam_init=lam_init, has_ctx=has_ctx)
    in_specs = [
        pl.BlockSpec((TQ, LANES), lambda b, h, i: (b * nq + i, cb(A32_DQ) + h)),
        pl.BlockSpec((S, LANES), lambda b, h, i: (b, cb(A32_DK) + h)),
        pl.BlockSpec((S, LANES), lambda b, h, i: (b, cb(A32_DV) + h)),
    ]
    args = [a32, a32, a32]
    if has_ctx:
        ck, cv, layer, cos, sin = ctx
        in_specs += [
            pl.BlockSpec((None, None, P, LANES), lambda b, h, i: (b, layer, 0, h)),
            pl.BlockSpec((None, None, P, LANES), lambda b, h, i: (b, layer, 0, h)),
            pl.BlockSpec((S, LANES), lambda b, h, i: (0, 0)),
            pl.BlockSpec((S, LANES), lambda b, h, i: (0, 0)),
        ]
        args += [ck, cv, cos, sin]
    in_specs += [
        pl.BlockSpec((4, DQK_D), lambda b, h, i: (0, 0)),
        pl.BlockSpec((1, LANES), lambda b, h, i: (0, h)),
    ]
    args += [lamv, hnorm]
    return pl.pallas_call(
        kern,
        grid=(B, H_D, nq),
        in_specs=in_specs,
        out_specs=pl.BlockSpec((TQ, LANES), lambda b, h, i: (b * nq + i, h)),
        out_shape=jax.ShapeDtypeStruct((B * S, BRANCH_W), BF16),
        scratch_shapes=[pltpu.VMEM((S + P, LANES), BF16), pltpu.VMEM((S + P, LANES), BF16)],
        compiler_params=_cparams("parallel", "parallel", "arbitrary"),
        name="diff_attn",
    )(*args)


def _conv_kernel(ca_ref, cb_ref, w_ref, g_ref, b_ref, y_ref, pad_ref, cv_ref, *, S):
    ca = ca_ref[...].astype(F32)
    cbv = cb_ref[...].astype(F32)
    zeros = jnp.zeros((CONV_PAD, BRANCH_W), F32)
    pad_ref[0:CONV_PAD, :] = zeros
    pad_ref[CONV_PAD + S:2 * CONV_PAD + S, :] = zeros
    pad_ref[CONV_PAD:CONV_PAD + S, :] = ca * _sigmoid(cbv)
    off = CONV_PAD - CONV_W // 2

    win_rows = CONV_ROWS + 2 * CONV_PAD

    def body(i, carry):
        base = pl.multiple_of(i * CONV_ROWS, CONV_ROWS)
        for lb in range(BRANCH_W // LANES):
            cols = slice(lb * LANES, (lb + 1) * LANES)
            win = pad_ref[pl.ds(base, win_rows), cols]
            acc = jnp.zeros((CONV_ROWS, LANES), F32)
            for r in range(8):
                rolled = win if r == 0 else pltpu.roll(win, win_rows - r, 0)
                for a in range(2 * CONV_PAD // 8):
                    j = 8 * a + r - off
                    if 0 <= j < CONV_W:
                        acc = acc + rolled[8 * a:8 * a + CONV_ROWS, :] * w_ref[j:j + 1, cols]
            cv_ref[:, cols] = acc
        acc = cv_ref[...]
        mu = jnp.mean(acc, axis=-1, keepdims=True)
        xc = acc - mu
        yn = xc * lax.rsqrt(jnp.mean(xc * xc, axis=-1, keepdims=True) + EPS) * g_ref[...] + b_ref[...]
        y_ref[pl.ds(base, CONV_ROWS), :] = (yn * _sigmoid(yn)).astype(y_ref.dtype)
        return carry

    lax.fori_loop(0, S // CONV_ROWS, body, 0)


def _conv(a16, w_dw, ln_g, ln_b, B, S):
    cb = lambda off: off // BRANCH_W
    kern = functools.partial(_conv_kernel, S=S)
    return pl.pallas_call(
        kern,
        grid=(B,),
        in_specs=[
            pl.BlockSpec((S, BRANCH_W), lambda b: (b, cb(A16_CA))),
            pl.BlockSpec((S, BRANCH_W), lambda b: (b, cb(A16_CB))),
            pl.BlockSpec((CONV_W + 1, BRANCH_W), lambda b: (0, 0)),
            pl.BlockSpec((1, BRANCH_W), lambda b: (0, 0)),
            pl.BlockSpec((1, BRANCH_W), lambda b: (0, 0)),
        ],
        out_specs=pl.BlockSpec((S, BRANCH_W), lambda b: (b, 0)),
        out_shape=jax.ShapeDtypeStruct((B * S, BRANCH_W), BF16),
        scratch_shapes=[pltpu.VMEM((S + 2 * CONV_PAD, BRANCH_W), F32),
                        pltpu.VMEM((CONV_ROWS, BRANCH_W), F32)],
        compiler_params=_cparams("parallel"),
        name="glu_conv_ln",
    )(a16, a16, w_dw, ln_g, ln_b)


def _merge_kernel(x_ref, mod_ref, ym_ref, yd_ref, yg_ref, yc_ref, g0_ref, g1_ref, g2_ref, g3_ref,
                  wb_ref, wo_ref, n2_ref, wr_ref, x1_ref, h2_ref, route_ref):
    ys = (ym_ref, yd_ref, yg_ref, yc_ref)
    gs = (g0_ref, g1_ref, g2_ref, g3_ref)
    merged = None
    for nbr in range(N_BRANCH):
        br = _dot(ys[nbr][...], wb_ref[nbr])
        term = _sigmoid(gs[nbr][...].astype(F32)) * br
        merged = term if merged is None else merged + term
    out = _dot(merged.astype(BF16), wo_ref[...])
    x1 = x_ref[...] + mod_ref[2:3, :] * out
    x1_ref[...] = x1
    y = x1 * lax.rsqrt(jnp.mean(x1 * x1, axis=-1, keepdims=True) + EPS) * n2_ref[...]
    h2 = y * (1.0 + mod_ref[4:5, :]) + mod_ref[3:4, :]
    h2_ref[...] = h2.reshape(h2_ref.shape)
    logits = jnp.dot(h2, wr_ref[...], preferred_element_type=F32, precision=lax.Precision.HIGHEST)
    lane = lax.broadcasted_iota(jnp.int32, logits.shape, 1)
    neg = -jnp.inf
    big = jnp.int32(LANES)
    is_g = lane < N_GROUPS
    gl = jnp.where(is_g, logits, neg)
    gmax = jnp.max(gl, axis=-1, keepdims=True)
    gidx = jnp.min(jnp.where(is_g & (gl == gmax), lane, big), axis=-1, keepdims=True)
    g_p = 1.0 / jnp.sum(jnp.where(is_g, jnp.exp(gl - gmax), 0.0), axis=-1, keepdims=True)
    e_lane = lane - N_GROUPS
    in_grp = (e_lane >= 0) & (e_lane < N_EXPERTS) & ((e_lane // EXPERTS_PER_GROUP) == gidx)
    el = jnp.where(in_grp, logits, neg)
    v1 = jnp.max(el, axis=-1, keepdims=True)
    i1 = jnp.min(jnp.where(in_grp & (el == v1), lane, big), axis=-1, keepdims=True)
    el2 = jnp.where(lane == i1, neg, el)
    v2 = jnp.max(el2, axis=-1, keepdims=True)
    i2 = jnp.min(jnp.where(in_grp & (lane != i1) & (el2 == v2), lane, big), axis=-1, keepdims=True)
    e2 = jnp.exp(v2 - v1)
    w1 = g_p / (1.0 + e2)
    w2 = g_p * e2 / (1.0 + e2)
    id1 = (i1 - N_GROUPS).astype(F32)
    id2 = (i2 - N_GROUPS).astype(F32)
    route_ref[...] = jnp.where(lane == 0, id1, jnp.where(lane == 1, id2,
                               jnp.where(lane == 2, w1, jnp.where(lane == 3, w2, 0.0))))


def _merge(x2d, mod, a16, ym, yd, yg, yc, wb, wo, n2, wr, rows_per_mod, tm):
    T = x2d.shape[0]
    gcb = A16_GATE // D_MODEL
    row = lambda i: (i, 0)
    return pl.pallas_call(
        _merge_kernel,
        grid=(T // tm,),
        in_specs=[
            pl.BlockSpec((tm, D_MODEL), row),
            pl.BlockSpec((None, N_MOD, D_MODEL), lambda i: ((i * tm) // rows_per_mod, 0, 0)),
            pl.BlockSpec((tm, BRANCH_W), row),
            pl.BlockSpec((tm, BRANCH_W), row),
            pl.BlockSpec((tm, BRANCH_W), row),
            pl.BlockSpec((tm, BRANCH_W), row),
            pl.BlockSpec((tm, D_MODEL), lambda i: (i, gcb + 0)),
            pl.BlockSpec((tm, D_MODEL), lambda i: (i, gcb + 1)),
            pl.BlockSpec((tm, D_MODEL), lambda i: (i, gcb + 2)),
            pl.BlockSpec((tm, D_MODEL), lambda i: (i, gcb + 3)),
            pl.BlockSpec((N_BRANCH, BRANCH_W, D_MODEL), lambda i: (0, 0, 0)),
            pl.BlockSpec((D_MODEL, D_MODEL), lambda i: (0, 0)),
            pl.BlockSpec((1, D_MODEL), lambda i: (0, 0)),
            pl.BlockSpec((D_MODEL, LANES), lambda i: (0, 0)),
        ],
        out_specs=[
            pl.BlockSpec((tm, D_MODEL), row),
            pl.BlockSpec((tm, TOK_SUB, LANES), lambda i: (i, 0, 0)),
            pl.BlockSpec((tm, LANES), row),
        ],
        out_shape=[
            jax.ShapeDtypeStruct((T, D_MODEL), F32),
            jax.ShapeDtypeStruct((T, TOK_SUB, LANES), F32),
            jax.ShapeDtypeStruct((T, LANES), F32),
        ],
        compiler_params=_cparams("parallel"),
        name="merge_outproj_route",
    )(x2d, mod, ym, yd, yg, yc, a16, a16, a16, a16, wb, wo, n2, wr)


def _gather_rows(idx_ref, src_hbm, dst, sem, n):
    def body(j, carry):
        for u in range(ROW_DMA_UNROLL):
            r = j * ROW_DMA_UNROLL + u
            pltpu.make_async_copy(src_hbm.at[idx_ref[0, r]], dst.at[r], sem).start()
        return carry

    lax.fori_loop(0, n // ROW_DMA_UNROLL, body, 0)


def _wait_rows(dst, sem):
    pltpu.make_async_copy(dst, dst, sem).wait()


def _moe_ffn_kernel(te_ref, idx_ref, idxn_ref, h_hbm, w1_ref, w3_ref, w2_ref, o_ref, buf, sem,
                    *, ntiles):
    i = pl.program_id(0)
    slot = i % 2

    @pl.when(i == 0)
    def _():
        _gather_rows(idx_ref, h_hbm, buf.at[0], sem.at[0], MOE_TILE)

    @pl.when(i + 1 < ntiles)
    def _():
        _gather_rows(idxn_ref, h_hbm, buf.at[1 - slot], sem.at[1 - slot], MOE_TILE)

    _wait_rows(buf.at[slot], sem.at[slot])
    x = buf[slot].reshape(MOE_TILE, D_MODEL).astype(BF16)
    a = _dot(x, w1_ref[...].astype(BF16))
    b = _dot(x, w3_ref[...].astype(BF16))
    s = (a * _sigmoid(a)) * b
    y = _dot(s.astype(BF16), w2_ref[...].astype(BF16))
    o_ref[...] = y.reshape(o_ref.shape)


def _moe_ffn(h3, src_idx, tile_e, w1, w3, w2):
    ntiles = src_idx.shape[0]
    kern = functools.partial(_moe_ffn_kernel, ntiles=ntiles)
    smem_row = lambda f: pl.BlockSpec((None, 1, MOE_TILE), f, memory_space=pltpu.SMEM)
    return pl.pallas_call(
        kern,
        grid_spec=pltpu.PrefetchScalarGridSpec(
            num_scalar_prefetch=1,
            grid=(ntiles,),
            in_specs=[
                smem_row(lambda i, te: (i, 0, 0)),
                smem_row(lambda i, te: (jnp.minimum(i + 1, ntiles - 1), 0, 0)),
                pl.BlockSpec(memory_space=pl.ANY),
                pl.BlockSpec((None, D_MODEL, D_EXPERT), lambda i, te: (te[i], 0, 0)),
                pl.BlockSpec((None, D_MODEL, D_EXPERT), lambda i, te: (te[i], 0, 0)),
                pl.BlockSpec((None, D_EXPERT, D_MODEL), lambda i, te: (te[i], 0, 0)),
            ],
            out_specs=pl.BlockSpec((MOE_TILE, TOK_SUB, LANES), lambda i, te: (i, 0, 0)),
            scratch_shapes=[pltpu.VMEM((2, MOE_TILE, TOK_SUB, LANES), F32),
                            pltpu.SemaphoreType.DMA((2,))],
        ),
        out_shape=jax.ShapeDtypeStruct((ntiles * MOE_TILE, TOK_SUB, LANES), F32),
        compiler_params=_cparams("arbitrary"),
        name="moe_grouped_experts",
    )(tile_e, src_idx, src_idx, h3, w1, w3, w2)


def _moe_combine_kernel(d0_ref, d1_ref, y_hbm, route_ref, x1_ref, mod_ref, fn_ref, o_ref,
                        ga, gb, sem, *, tm, final_norm):
    _gather_rows(d0_ref, y_hbm, ga, sem.at[0], tm)
    _gather_rows(d1_ref, y_hbm, gb, sem.at[1], tm)
    rt = route_ref[...]
    _wait_rows(ga, sem.at[0])
    _wait_rows(gb, sem.at[1])
    y = rt[:, 2:3] * ga[...].reshape(tm, D_MODEL) + rt[:, 3:4] * gb[...].reshape(tm, D_MODEL)
    x2 = x1_ref[...] + mod_ref[5:6, :] * y
    if final_norm:
        x2 = x2 * lax.rsqrt(jnp.mean(x2 * x2, axis=-1, keepdims=True) + EPS) * fn_ref[...]
    o_ref[...] = x2


def _moe_combine(yg, dest, route, x1, mod, fn, rows_per_mod, tm, final_norm):
    T = x1.shape[0]
    kern = functools.partial(_moe_combine_kernel, tm=tm, final_norm=final_norm)
    return pl.pallas_call(
        kern,
        grid=(T // tm,),
        in_specs=[
            pl.BlockSpec((None, None, 1, tm), lambda i: (0, i, 0, 0), memory_space=pltpu.SMEM),
            pl.BlockSpec((None, None, 1, tm), lambda i: (1, i, 0, 0), memory_space=pltpu.SMEM),
            pl.BlockSpec(memory_space=pl.ANY),
            pl.BlockSpec((tm, LANES), lambda i: (i, 0)),
            pl.BlockSpec((tm, D_MODEL), lambda i: (i, 0)),
            pl.BlockSpec((None, N_MOD, D_MODEL), lambda i: ((i * tm) // rows_per_mod, 0, 0)),
            pl.BlockSpec((1, D_MODEL), lambda i: (0, 0)),
        ],
        out_specs=pl.BlockSpec((tm, D_MODEL), lambda i: (i, 0)),
        out_shape=jax.ShapeDtypeStruct((T, D_MODEL), F32),
        scratch_shapes=[pltpu.VMEM((tm, TOK_SUB, LANES), F32), pltpu.VMEM((tm, TOK_SUB, LANES), F32),
                        pltpu.SemaphoreType.DMA((2,))],
        compiler_params=_cparams("arbitrary"),
        name="moe_combine",
    )(dest, dest, yg, route, x1, mod, fn)


def _route_plan(route, T):
    ntiles = (2 * T + N_EXPERTS * (MOE_TILE - 1) + MOE_TILE - 1) // MOE_TILE
    ef = route[:, 0:2].astype(jnp.int32).reshape(-1)
    oh = (ef[:, None] == jnp.arange(N_EXPERTS, dtype=jnp.int32)[None, :]).astype(jnp.int32)
    csum = jnp.cumsum(oh, axis=0)
    rank = jnp.sum((csum - oh) * oh, axis=1)
    counts = csum[-1]
    padded = ((counts + MOE_TILE - 1) // MOE_TILE) * MOE_TILE
    seg_end = jnp.cumsum(padded)
    dest = (seg_end - padded)[ef] + rank
    src = jnp.zeros((ntiles * MOE_TILE,), jnp.int32).at[dest].set(
        jnp.arange(2 * T, dtype=jnp.int32) // 2)
    tile_e = jnp.minimum(
        jnp.searchsorted(seg_end, jnp.arange(ntiles, dtype=jnp.int32) * MOE_TILE, side='right'),
        N_EXPERTS - 1).astype(jnp.int32)
    return src.reshape(ntiles, 1, MOE_TILE), tile_e, dest.reshape(T, 2)


def _split_w_in(w):
    sizes = (H_M * DK_M, H_M * DK_M, H_M * DV_M, H_M * DV_M, 2 * H_M, 2 * H_M,
             H_D * 2 * DQK_D, H_D * 2 * DQK_D, H_D * DV_D,
             H_G * DK_G, H_G * DK_G, H_G * DV_G, 2 * GATE_RANK, H_G * DV_G,
             BRANCH_W, BRANCH_W, N_BRANCH * D_MODEL)
    outs, acc = [], 0
    for s in sizes:
        outs.append(w[:, acc:acc + s])
        acc += s
    return outs


def _pack_layer_params(p):
    (m_q, m_k, m_v, m_o, m_i, m_f, d_q, d_k, d_v, g_q, g_k, g_v, g_a, g_r, c_a, c_b, gate) = \
        _split_w_in(p['w_in'])
    w16 = jnp.concatenate([m_q, m_k, m_v, m_o, gate, g_r, c_a, c_b], axis=1).astype(BF16)
    gqk = jnp.concatenate([g_q.reshape(D_MODEL, H_G, DK_G), g_k.reshape(D_MODEL, H_G, DK_G)],
                          axis=2).reshape(D_MODEL, 2 * H_G * DK_G)
    small = jnp.concatenate(
        [m_i, m_f, g_a, jnp.zeros((D_MODEL, LANES - 4 * H_M - 2 * GATE_RANK), F32)], axis=1)
    w32 = jnp.concatenate([d_q, d_k, d_v, gqk, g_v, small], axis=1).astype(BF16)
    bi = p['b_m_i'].reshape(2, H_M)
    bf = p['b_m_f'].reshape(2, H_M)
    bcol = jnp.stack([bi[0], bi[1], bf[0], bf[1]], axis=-1)
    wup = p['w_gla_up'].reshape(2, GATE_RANK, H_G, DK_G)
    wup_pad = jnp.zeros((H_G, 2, LANES, LANES), F32)
    bup = p['b_gla_gate'].reshape(2, H_G, DK_G)
    for d in range(2):
        blk = jnp.transpose(wup[d], (1, 0, 2))
        blk = jnp.concatenate([blk, blk], axis=-1)
        r0 = SM_GA + d * GATE_RANK
        wup_pad = wup_pad.at[:, d, r0:r0 + GATE_RANK, :].set(blk)
    bup2 = jnp.transpose(jnp.concatenate([bup, bup], axis=-1), (1, 0, 2))[:, :, None, :]
    wr = jnp.concatenate([p['w_group_router'], p['w_expert_router'],
                          jnp.zeros((D_MODEL, LANES - N_GROUPS - N_EXPERTS), F32)], axis=1)
    return dict(
        w16=w16, w32=w32, brow=bcol.reshape(H_M, 4, 1),
        wup=wup_pad.astype(BF16), bup=bup2,
        wdw=jnp.concatenate([p['w_dw'], jnp.zeros((1, BRANCH_W), F32)], axis=0),
        ln_g=p['conv_ln_g'].reshape(1, BRANCH_W), ln_b=p['conv_ln_b'].reshape(1, BRANCH_W),
        hn_m=p['hnorm_m'].reshape(1, BRANCH_W), hn_d=p['hnorm_d'].reshape(1, BRANCH_W),
        hn_g=p['hnorm_g'].reshape(1, BRANCH_W),
        lamv=jnp.stack([p['lam_q1'], p['lam_k1'], p['lam_q2'], p['lam_k2']], axis=0),
        wb=p['w_branch'].astype(BF16), wo=p['w_out'].astype(BF16),
        n1=p['norm1'].reshape(1, D_MODEL), n2=p['norm2'].reshape(1, D_MODEL), wr=wr,
        w1=p['w_e1'].reshape(N_EXPERTS, D_MODEL, D_EXPERT),
        w3=p['w_e3'].reshape(N_EXPERTS, D_MODEL, D_EXPERT),
        w2=p['w_e2'].reshape(N_EXPERTS, D_EXPERT, D_MODEL),
    )


def _rope_tables(S):
    rows = S // GRID_W
    r, col = jnp.meshgrid(jnp.arange(rows, dtype=F32), jnp.arange(GRID_W, dtype=F32), indexing='ij')
    r, col = r.reshape(-1), col.reshape(-1)
    n_freq = DQK_D // 4
    inv = ROPE_BASE ** (-jnp.arange(n_freq, dtype=F32) / n_freq)
    ang = jnp.concatenate([r[:, None] * inv, col[:, None] * inv], axis=-1)
    cos, sin = jnp.cos(ang), jnp.sin(ang)
    cos_t = jnp.tile(cos, (1, LANES // (DQK_D // 2)))
    sin_t = jnp.tile(jnp.concatenate([-sin, sin], axis=-1), (1, LANES // DQK_D))
    return cos_t, sin_t


def _pick_tile(T, cap):
    t = min(T, cap)
    while T % t:
        t //= 2
    return t


def _layer(x2d, mod, pk, B, S, lam_init, ctx, final_norm, fn):
    T = B * S
    rows_per_mod = T // mod.shape[0]
    tm = _pick_tile(rows_per_mod, 1024)
    a16 = _inproj(x2d, mod, pk['n1'], pk['w16'], BF16, rows_per_mod, tm, 768)
    a32 = _inproj(x2d, mod, pk['n1'], pk['w32'], F32, rows_per_mod, tm, 896)

    L = min(MLSTM_CHUNK, S)
    sm = a32[:, A32_SM:A32_SM + 4 * H_M]
    grow = jnp.transpose(sm.reshape(T // L, L, 4, H_M), (3, 0, 2, 1))
    if ctx is None:
        c0 = jnp.zeros((B, 2, H_M, DK_M, DV_M), F32)
        n0 = jnp.zeros((B, 2, H_M, 1, DK_M), F32)
        m0 = jnp.zeros((B, 2, H_M, 1, LANES), F32)
        s0 = jnp.zeros((B, 2, H_G, DK_G, DV_G), F32)
        attn_ctx = None
    else:
        c0 = ctx['C']
        n0 = ctx['n'][:, :, :, None, :]
        m0 = jnp.broadcast_to(ctx['m'][:, :, :, None, None], (B, 2, H_M, 1, LANES))
        s0 = ctx['S']
        attn_ctx = (ctx['k'], ctx['v'], ctx['layer'], ctx['cos'], ctx['sin'])
    ym, c_f, n_f, m_f = _mlstm(a16, grow, pk['brow'], c0, n0, m0, pk['hn_m'], B, S)
    yd = _attn(a32, pk['lamv'], pk['hn_d'], B, S, lam_init, attn_ctx)
    yg, s_f = _gla(a32, a16, pk['wup'], pk['bup'], s0, pk['hn_g'], B, S)
    yc = _conv(a16, pk['wdw'], pk['ln_g'], pk['ln_b'], B, S)
    x1, h3, route = _merge(x2d, mod, a16, ym, yd, yg, yc, pk['wb'], pk['wo'], pk['n2'], pk['wr'],
                           rows_per_mod, _pick_tile(rows_per_mod, 512))
    src_idx, tile_e, dest = _route_plan(route, T)
    y_grouped = _moe_ffn(h3, src_idx, tile_e, pk['w1'], pk['w3'], pk['w2'])
    tmc = _pick_tile(rows_per_mod, 256)
    dest4 = jnp.transpose(dest).reshape(2, T // tmc, 1, tmc)
    x2 = _moe_combine(y_grouped, dest4, route, x1, mod, fn, rows_per_mod, tmc, final_norm)
    state = None
    if ctx is None:
        state = (a32[:, A32_DK:A32_DK + H_D * 2 * DQK_D].reshape(B, S, H_D, 2 * DQK_D),
                 a32[:, A32_DV:A32_DV + H_D * DV_D].reshape(B, S, H_D, DV_D),
                 c_f, n_f[:, :, :, 0, :], m_f[:, :, :, 0, 0], s_f)
    return x2, state


def kernel(x_prompt, x_sample, c, cache_diff_k, cache_diff_v, state_mlstm_C, state_mlstm_n, state_mlstm_m, state_gla_S, c_ctx, w_mod, b_mod, norm1, w_in, b_m_i, b_m_f, lam_q1, lam_k1, lam_q2, lam_k2, w_gla_up, b_gla_gate, w_dw, conv_ln_g, conv_ln_b, hnorm_m, hnorm_d, hnorm_g, w_branch, w_out, norm2, w_group_router, w_expert_router, w_e1, w_e3, w_e2, final_norm):
    Bp, Sp, _ = x_prompt.shape
    Bs, Ss, _ = x_sample.shape
    P = cache_diff_k.shape[2]
    n_cond = 8 * ((1 + Bs + 7) // 8)
    cond = jnp.concatenate([c_ctx[None, :], c, jnp.zeros((n_cond - 1 - Bs, D_MODEL), F32)], axis=0)
    mod_all = _modulation(cond, w_mod, b_mod).reshape(DEPTH, n_cond, N_MOD, D_MODEL)
    cos_t, sin_t = _rope_tables(Ss)
    ck4 = cache_diff_k.reshape(Bs, DEPTH, P, H_D * 2 * DQK_D)
    cv4 = cache_diff_v.reshape(Bs, DEPTH, P, H_D * DV_D)
    fn = final_norm.reshape(1, D_MODEL)
    yp = x_prompt.reshape(Bp * Sp, D_MODEL)
    ys = x_sample.reshape(Bs * Ss, D_MODEL)
    states = []
    for l in range(DEPTH):
        p = {'w_in': w_in[l], 'b_m_i': b_m_i[l], 'b_m_f': b_m_f[l], 'lam_q1': lam_q1[l],
             'lam_k1': lam_k1[l], 'lam_q2': lam_q2[l], 'lam_k2': lam_k2[l],
             'w_gla_up': w_gla_up[l], 'b_gla_gate': b_gla_gate[l], 'w_dw': w_dw[l],
             'conv_ln_g': conv_ln_g[l], 'conv_ln_b': conv_ln_b[l], 'hnorm_m': hnorm_m[l],
             'hnorm_d': hnorm_d[l], 'hnorm_g': hnorm_g[l], 'w_branch': w_branch[l],
             'w_out': w_out[l], 'norm1': norm1[l], 'norm2': norm2[l],
             'w_group_router': w_group_router[l], 'w_expert_router': w_expert_router[l],
             'w_e1': w_e1[l], 'w_e3': w_e3[l], 'w_e2': w_e2[l]}
        pk = _pack_layer_params(p)
        lam_init = 0.8 - 0.6 * math.exp(-0.3 * l)
        last = l == DEPTH - 1
        yp, st = _layer(yp, mod_all[l, 0:1], pk, Bp, Sp, lam_init, None, last, fn)
        states.append(st)
        ctx = {'k': ck4, 'v': cv4, 'layer': l, 'cos': cos_t, 'sin': sin_t,
               'C': state_mlstm_C[:, l], 'n': state_mlstm_n[:, l], 'm': state_mlstm_m[:, l],
               'S': state_gla_S[:, l]}
        ys, _ = _layer(ys, mod_all[l, 1:1 + Bs], pk, Bs, Ss, lam_init, ctx, last, fn)
    stack = lambda i: jnp.stack([s[i] for s in states], axis=1)
    return (yp.reshape(Bp, Sp, D_MODEL), ys.reshape(Bs, Ss, D_MODEL),
            stack(0), stack(1), stack(2), stack(3), stack(4), stack(5))
```

```python
import functools
import math

import jax
import jax.numpy as jnp
from jax import lax
from jax.experimental import pallas as pl
from jax.experimental.pallas import tpu as pltpu

F32 = jnp.float32
BF16 = jnp.bfloat16

D_MODEL = 1024
DEPTH = 2
GRID_W = 64
BRANCH_W = 512
N_BRANCH = 4
H_M, DK_M, DV_M = 4, 128, 128
H_D, DQK_D, DV_D = 4, 64, 128
H_G, DK_G, DV_G = 4, 64, 128
GATE_RANK = 16
GLA_TAU = 16.0
CONV_W = 31
N_GROUPS, EXPERTS_PER_GROUP, D_EXPERT = 4, 4, 512
N_EXPERTS = N_GROUPS * EXPERTS_PER_GROUP
ROPE_BASE = 10000.0
EPS = 1e-6
N_MOD = 6

LANES = 128
VMEM_LIMIT = 48 * 1024 * 1024

A16_MQ, A16_MK, A16_MV, A16_MO = 0, 512, 1024, 1536
A16_GATE, A16_GR, A16_CA, A16_CB = 2048, 6144, 6656, 7168
N_A16 = 7680
A32_DQ, A32_DK, A32_DV, A32_GQK, A32_GV, A32_SM = 0, 512, 1024, 1536, 2048, 2560
N_A32 = 2688
SM_MI, SM_MF, SM_GA = 0, 8, 16

MLSTM_CHUNK = 128
GLA_CHUNK = 64
GLA_SUB = 16
GLA_EXP_CLAMP = 80.0
CONV_ROWS = 64
CONV_PAD = 16
TOK_SUB = D_MODEL // LANES
MOE_TILE = 256
ROW_DMA_UNROLL = 8


def _cparams(*sem):
    return pltpu.CompilerParams(dimension_semantics=sem, vmem_limit_bytes=VMEM_LIMIT)


def _log_sigmoid(x):
    return jnp.minimum(x, 0.0) - jnp.log1p(jnp.exp(-jnp.abs(x)))


def _sigmoid(x):
    return 1.0 / (1.0 + jnp.exp(-x))


def _dot(a, b):
    return jnp.dot(a, b, preferred_element_type=F32)


def _dot_nt(a, b):
    return lax.dot_general(a, b, (((1,), (1,)), ((), ())), preferred_element_type=F32)


def _dot_tn(a, b):
    return lax.dot_general(a, b, (((0,), (0,)), ((), ())), preferred_element_type=F32)


def _mod_kernel(c_ref, w_ref, b_ref, o_ref):
    c = c_ref[...]
    a = (c * _sigmoid(c)).astype(BF16)
    o_ref[...] = _dot(a, w_ref[...].astype(BF16)) + b_ref[...]


def _modulation(cond, w_mod, b_mod):
    R = cond.shape[0]
    tn = 512
    nmod = N_MOD * D_MODEL
    return pl.pallas_call(
        _mod_kernel,
        grid=(DEPTH, nmod // tn),
        in_specs=[
            pl.BlockSpec((R, D_MODEL), lambda l, j: (0, 0)),
            pl.BlockSpec((None, D_MODEL, tn), lambda l, j: (l, 0, j)),
            pl.BlockSpec((None, 1, tn), lambda l, j: (l, 0, j)),
        ],
        out_specs=pl.BlockSpec((None, R, tn), lambda l, j: (l, 0, j)),
        out_shape=jax.ShapeDtypeStruct((DEPTH, R, nmod), F32),
        compiler_params=_cparams("parallel", "parallel"),
        name="adaln_mod",
    )(cond, w_mod, b_mod.reshape(DEPTH, 1, nmod))


def _inproj_kernel(x_ref, mod_ref, g_ref, w_ref, o_ref, h_ref):
    @pl.when(pl.program_id(1) == 0)
    def _():
        x = x_ref[...]
        y = x * lax.rsqrt(jnp.mean(x * x, axis=-1, keepdims=True) + EPS) * g_ref[...]
        h_ref[...] = (y * (1.0 + mod_ref[1:2, :]) + mod_ref[0:1, :]).astype(BF16)

    o_ref[...] = _dot(h_ref[...], w_ref[...]).astype(o_ref.dtype)


def _inproj(x2d, mod, g, w, out_dtype, rows_per_mod, tm, tn):
    T = x2d.shape[0]
    N = w.shape[1]
    return pl.pallas_call(
        _inproj_kernel,
        grid=(T // tm, N // tn),
        in_specs=[
            pl.BlockSpec((tm, D_MODEL), lambda i, j: (i, 0)),
            pl.BlockSpec((None, N_MOD, D_MODEL), lambda i, j: ((i * tm) // rows_per_mod, 0, 0)),
            pl.BlockSpec((1, D_MODEL), lambda i, j: (0, 0)),
            pl.BlockSpec((D_MODEL, tn), lambda i, j: (0, j)),
        ],
        out_specs=pl.BlockSpec((tm, tn), lambda i, j: (i, j)),
        out_shape=jax.ShapeDtypeStruct((T, N), out_dtype),
        scratch_shapes=[pltpu.VMEM((tm, D_MODEL), BF16)],
        compiler_params=_cparams("parallel", "arbitrary"),
        name="norm_inproj",
    )(x2d, mod, g, w)


def _mlstm_local(c, q_ref, k_ref, v_ref, gr_ref, br_ref, pr_ref, bb_ref, mb_ref, kv_ref, rp_ref, L):
    scale = DK_M ** -0.5
    ti = lax.broadcasted_iota(jnp.int32, (L, L), 0)
    si = lax.broadcasted_iota(jnp.int32, (L, L), 1)
    sub = lax.broadcasted_iota(jnp.int32, (8, LANES), 0)
    rows = pl.ds(pl.multiple_of(c * L, L), L)
    q = q_ref[rows, :]
    v_ext = jnp.concatenate([v_ref[rows, :], jnp.ones((L, LANES), BF16)], axis=1)
    k_t = k_ref[rows, :].astype(F32).T
    qk = _dot(q, k_t.astype(BF16)) * scale
    grow = gr_ref[c] + br_ref[...]
    for d in range(2):
        rev = d == 1
        mask = (si >= ti) if rev else (si <= ti)
        src = ((si <= ti) if rev else (si >= ti)).astype(BF16)
        i_row = grow[d:d + 1, :]
        f_row = _log_sigmoid(grow[2 + d:3 + d, :])
        f8 = jnp.broadcast_to(f_row, (8, L))
        f_hi = f8.astype(BF16)
        f_r1 = f8 - f_hi.astype(F32)
        f_mid = f_r1.astype(BF16)
        f_lo = (f_r1 - f_mid.astype(F32)).astype(BF16)
        b_row = (_dot(f_hi, src) + _dot(f_mid, src) + _dot(f_lo, src))[0:1, :]
        b_col = jnp.sum(jnp.where(mask, f_row, 0.0), axis=1, keepdims=True)
        log_d = jnp.where(mask, b_col + (i_row - b_row), -jnp.inf)
        m_loc = jnp.max(log_d, axis=1, keepdims=True)
        smat = qk * jnp.exp(log_d - m_loc)
        pr_ref[d, rows, :] = _dot(smat.astype(BF16), v_ext)
        bb_ref[d, rows, :] = jnp.broadcast_to(b_col, (L, LANES))
        mb_ref[d, rows, :] = jnp.broadcast_to(m_loc, (L, LANES))
        b_last = jnp.sum(f_row, axis=1, keepdims=True)
        ls_row = b_last - b_row + i_row
        m2 = jnp.max(ls_row, axis=1, keepdims=True)
        kw_t = (k_t * jnp.exp(ls_row - m2)).astype(BF16)
        kv_ref[d, c] = scale * _dot(kw_t, v_ext)
        rp_ref[d, c] = jnp.where(sub == 0, b_last, m2)


def _mlstm_carry(c, d, carry, q_ref, pr_ref, bb_ref, mb_ref, kv_ref, rp_ref, h_ref, L):
    cn, m = carry
    two = lambda x: jnp.concatenate([x, x], axis=1)
    rows = pl.ds(pl.multiple_of(c * L, L), L)
    bb = bb_ref[d, rows, :]
    mb = mb_ref[d, rows, :]
    m_t = jnp.maximum(bb + m, mb)
    a_int = jnp.exp(bb + m - m_t)
    e_loc = jnp.exp(mb - m_t)
    nd = two(a_int) * _dot(q_ref[rows, :], cn.astype(BF16)) + two(e_loc) * pr_ref[d, rows, :]
    h_ref[d, rows, :] = nd[:, :DV_M] / jnp.maximum(jnp.abs(nd[:, DV_M:]), jnp.exp(-m_t))
    rp = rp_ref[d, c]
    b_last, m2 = rp[0:1, :], rp[1:2, :]
    m_new = jnp.maximum(b_last + m, m2)
    a_c = jnp.exp(b_last + m - m_new)
    e2 = jnp.exp(m2 - m_new)
    return two(a_c) * cn + two(e2) * kv_ref[d, c], m_new


def _mlstm_kernel(q_ref, k_ref, v_ref, og_ref, gr_ref, br_ref, c0_ref, n0_ref, m0_ref, hn_ref,
                  y_ref, c_out_ref, n_out_ref, m_out_ref,
                  pr_ref, bb_ref, mb_ref, kv_ref, rp_ref, h_ref, *, L, S):
    nch = S // L

    def local(ci, carry):
        _mlstm_local(ci, q_ref, k_ref, v_ref, gr_ref, br_ref, pr_ref, bb_ref, mb_ref, kv_ref,
                     rp_ref, L)
        return carry

    lax.fori_loop(0, nch, local, 0, unroll=2)
    step = functools.partial(_mlstm_carry, q_ref=q_ref, pr_ref=pr_ref, bb_ref=bb_ref, mb_ref=mb_ref,
                             kv_ref=kv_ref, rp_ref=rp_ref, h_ref=h_ref, L=L)

    def body(ci, carry):
        return step(ci, 0, carry[0]), step(nch - 1 - ci, 1, carry[1])

    def init(d):
        n_rep = jnp.broadcast_to(n0_ref[d], (DK_M, DK_M)).T
        return jnp.concatenate([c0_ref[d], n_rep], axis=1), m0_ref[d]

    fin = lax.fori_loop(0, nch, body, (init(0), init(1)), unroll=2)
    for d in range(2):
        cn, m = fin[d]
        c_out_ref[d] = cn[:, :DV_M]
        n_out_ref[d] = cn[:, DV_M:].T[0:1, :]
        m_out_ref[d] = m

    hm = h_ref[0] + h_ref[1]
    y = hm * lax.rsqrt(jnp.mean(hm * hm, axis=-1, keepdims=True) + EPS) * hn_ref[...]
    y_ref[...] = (y * _sigmoid(og_ref[...].astype(F32))).astype(y_ref.dtype)


def _mlstm(a16, grow, brow, c0, n0, m0, hnorm, B, S):
    L = min(MLSTM_CHUNK, S)
    nch = S // L
    cb = lambda off: off // LANES
    kern = functools.partial(_mlstm_kernel, L=L, S=S)
    return pl.pallas_call(
        kern,
        grid=(B, H_M),
        in_specs=[
            pl.BlockSpec((S, LANES), lambda b, h: (b, cb(A16_MQ) + h)),
            pl.BlockSpec((S, LANES), lambda b, h: (b, cb(A16_MK) + h)),
            pl.BlockSpec((S, LANES), lambda b, h: (b, cb(A16_MV) + h)),
            pl.BlockSpec((S, LANES), lambda b, h: (b, cb(A16_MO) + h)),
            pl.BlockSpec((None, nch, 4, L), lambda b, h: (h, b, 0, 0)),
            pl.BlockSpec((None, 4, 1), lambda b, h: (h, 0, 0)),
            pl.BlockSpec((None, 2, None, DK_M, DV_M), lambda b, h: (b, 0, h, 0, 0)),
            pl.BlockSpec((None, 2, None, 1, DK_M), lambda b, h: (b, 0, h, 0, 0)),
            pl.BlockSpec((None, 2, None, 1, LANES), lambda b, h: (b, 0, h, 0, 0)),
            pl.BlockSpec((1, LANES), lambda b, h: (0, h)),
        ],
        out_specs=[
            pl.BlockSpec((S, LANES), lambda b, h: (b, h)),
            pl.BlockSpec((None, 2, None, DK_M, DV_M), lambda b, h: (b, 0, h, 0, 0)),
            pl.BlockSpec((None, 2, None, 1, DK_M), lambda b, h: (b, 0, h, 0, 0)),
            pl.BlockSpec((None, 2, None, 1, LANES), lambda b, h: (b, 0, h, 0, 0)),
        ],
        out_shape=[
            jax.ShapeDtypeStruct((B * S, BRANCH_W), BF16),
            jax.ShapeDtypeStruct((B, 2, H_M, DK_M, DV_M), F32),
            jax.ShapeDtypeStruct((B, 2, H_M, 1, DK_M), F32),
            jax.ShapeDtypeStruct((B, 2, H_M, 1, LANES), F32),
        ],
        scratch_shapes=[pltpu.VMEM((2, S, 2 * DV_M), F32), pltpu.VMEM((2, S, LANES), F32),
                        pltpu.VMEM((2, S, LANES), F32), pltpu.VMEM((2, nch, DK_M, 2 * DV_M), F32),
                        pltpu.VMEM((2, nch, 8, LANES), F32), pltpu.VMEM((2, S, DV_M), F32)],
        compiler_params=_cparams("parallel", "parallel"),
        name="mlstm",
    )(a16, a16, a16, a16, grow, brow, c0, n0, m0, hnorm)


def _gla_local(c, q2_ref, k2_ref, v_ref, la_ref, oa_ref, qt_ref, u_ref, dec_ref, L):
    nb = L // GLA_SUB
    ti = lax.broadcasted_iota(jnp.int32, (L, L), 0)
    si = lax.broadcasted_iota(jnp.int32, (L, L), 1)
    row_blk = lax.broadcasted_iota(jnp.int32, (L, LANES), 0) // GLA_SUB
    lo_half = lax.broadcasted_iota(jnp.int32, (L, LANES), 1) < DK_G
    eye = (lax.broadcasted_iota(jnp.int32, (DK_G, LANES), 0)
           == lax.broadcasted_iota(jnp.int32, (DK_G, LANES), 1))
    rows = pl.ds(pl.multiple_of(c * L, L), L)
    q2 = q2_ref[rows, :]
    k2 = k2_ref[rows, :]
    v = v_ref[rows, :].astype(BF16)
    for d in range(2):
        rev = d == 1
        mask = (si >= ti) if rev else (si <= ti)
        tri = mask.astype(BF16)
        la2 = la_ref[d, rows, :]
        la_hi = la2.astype(BF16)
        la_lo = (la2 - la_hi.astype(F32)).astype(BF16)
        g2 = _dot(tri, la_hi) + _dot(tri, la_lo)
        qt_ref[d, rows, :] = (q2 * jnp.exp(g2))[:, :DK_G].astype(BF16)
        a_parts, b_parts = [], []
        for p in range(nb // 2):
            ia, ib = 2 * p, 2 * p + 1
            ra = ia * GLA_SUB + (GLA_SUB - 1 if rev else 0)
            rb = ib * GLA_SUB + (GLA_SUB - 1 if rev else 0)
            ref2 = jnp.where(lo_half, g2[ra:ra + 1, :], g2[rb:rb + 1, :])
            blk = jnp.where(lo_half, ia, ib)
            in_blk = row_blk == blk
            key_ok = (row_blk >= blk) if rev else (row_blk <= blk)
            a_parts.append(jnp.where(in_blk, q2 * jnp.exp(jnp.minimum(g2 - ref2, 0.0)), 0.0))
            b_parts.append(
                jnp.where(key_ok, k2 * jnp.exp(jnp.minimum(ref2 - g2, GLA_EXP_CLAMP)), 0.0))
        a_big = jnp.concatenate(a_parts, axis=1).astype(BF16)
        b_big = jnp.concatenate(b_parts, axis=1).astype(BF16)
        att = jnp.where(mask, _dot_nt(a_big, b_big), 0.0)
        oa_ref[d, rows, :] = _dot(att.astype(BF16), v)
        gl_row = 0 if rev else L - 1
        glast = g2[gl_row:gl_row + 1, :]
        kd = (k2 * jnp.exp(glast - g2))[:, :DK_G]
        u_ref[d, c] = _dot_tn(kd.astype(BF16), v)
        glast_col = jnp.sum(jnp.where(eye, glast, 0.0), axis=1, keepdims=True)
        dec_ref[d, c] = jnp.broadcast_to(jnp.exp(glast_col), (DK_G, DV_G))


def _gla_kernel(qk_ref, v_ref, sm_ref, wup_ref, bup_ref, s0_ref, gr_ref, hn_ref,
                y_ref, s_out_ref, la_ref, q2_ref, k2_ref, oa_ref, oi_ref, qt_ref, u_ref, dec_ref,
                *, L, S):
    nch = S // L
    sm = sm_ref[...].astype(BF16)
    for d in range(2):
        la_ref[d] = _log_sigmoid(_dot(sm, wup_ref[d]) + bup_ref[d]) * (1.0 / GLA_TAU)
    qk = qk_ref[...]
    qk_sw = pltpu.roll(qk, DK_G, 1)
    lo_half = lax.broadcasted_iota(jnp.int32, qk.shape, 1) < DK_G
    q2_ref[...] = jnp.where(lo_half, qk, qk_sw) * (DK_G ** -0.5)
    k2_ref[...] = jnp.where(lo_half, qk_sw, qk)

    def local(ci, carry):
        _gla_local(ci, q2_ref, k2_ref, v_ref, la_ref, oa_ref, qt_ref, u_ref, dec_ref, L)
        return carry

    lax.fori_loop(0, nch, local, 0, unroll=4)

    def body(ci, carry):
        out = []
        for d, c in ((0, ci), (1, nch - 1 - ci)):
            rows = pl.ds(pl.multiple_of(c * L, L), L)
            st = carry[d]
            oi_ref[d, rows, :] = _dot(qt_ref[d, rows, :], st.astype(BF16))
            out.append(dec_ref[d, c] * st + u_ref[d, c])
        return tuple(out)

    st_f, st_b = lax.fori_loop(0, nch, body, (s0_ref[0], s0_ref[1]), unroll=2)
    s_out_ref[0] = st_f
    s_out_ref[1] = st_b

    og = (oa_ref[0] + oi_ref[0]) + (oa_ref[1] + oi_ref[1])
    y = og * lax.rsqrt(jnp.mean(og * og, axis=-1, keepdims=True) + EPS) * hn_ref[...]
    gr = gr_ref[...].astype(F32)
    y_ref[...] = (y * (gr * _sigmoid(gr))).astype(y_ref.dtype)


def _gla(a32, a16, wup, bup, s0, hnorm, B, S):
    L = min(GLA_CHUNK, S)
    nch = S // L
    cb = lambda off: off // LANES
    kern = functools.partial(_gla_kernel, L=L, S=S)
    return pl.pallas_call(
        kern,
        grid=(B, H_G),
        in_specs=[
            pl.BlockSpec((S, LANES), lambda b, h: (b, cb(A32_GQK) + h)),
            pl.BlockSpec((S, LANES), lambda b, h: (b, cb(A32_GV) + h)),
            pl.BlockSpec((S, LANES), lambda b, h: (b, cb(A32_SM))),
            pl.BlockSpec((None, 2, LANES, LANES), lambda b, h: (h, 0, 0, 0)),
            pl.BlockSpec((None, 2, 1, LANES), lambda b, h: (h, 0, 0, 0)),
            pl.BlockSpec((None, 2, None, DK_G, DV_G), lambda b, h: (b, 0, h, 0, 0)),
            pl.BlockSpec((S, LANES), lambda b, h: (b, cb(A16_GR) + h)),
            pl.BlockSpec((1, LANES), lambda b, h: (0, h)),
        ],
        out_specs=[
            pl.BlockSpec((S, LANES), lambda b, h: (b, h)),
            pl.BlockSpec((None, 2, None, DK_G, DV_G), lambda b, h: (b, 0, h, 0, 0)),
        ],
        out_shape=[
            jax.ShapeDtypeStruct((B * S, BRANCH_W), BF16),
            jax.ShapeDtypeStruct((B, 2, H_G, DK_G, DV_G), F32),
        ],
        scratch_shapes=[pltpu.VMEM((2, S, LANES), F32), pltpu.VMEM((S, LANES), F32),
                        pltpu.VMEM((S, LANES), F32), pltpu.VMEM((2, S, DV_G), F32),
                        pltpu.VMEM((2, S, DV_G), F32), pltpu.VMEM((2, S, DK_G), BF16),
                        pltpu.VMEM((2, nch, DK_G, DV_G), F32), pltpu.VMEM((2, nch, DK_G, DV_G), F32)],
        compiler_params=_cparams("parallel", "parallel"),
        name="gla",
    )(a32, a32, a32, wup, bup, s0, a16, hnorm)


def _rope(x, cos, sin_signed):
    lane = lax.broadcasted_iota(jnp.int32, x.shape, 1)
    first = (lane % DQK_D) < (DQK_D // 2)
    partner = jnp.where(first, pltpu.roll(x, LANES - DQK_D // 2, 1), pltpu.roll(x, DQK_D // 2, 1))
    return x * cos + partner * sin_signed


def _attn_kernel(*refs, S, P, TQ, lam_init, has_ctx):
    if has_ctx:
        (q_ref, k_ref, v_ref, ck_ref, cv_ref, cos_ref, sin_ref, lam_ref, hn_ref,
         y_ref, kk_ref, vv_ref) = refs
    else:
        q_ref, k_ref, v_ref, lam_ref, hn_ref, y_ref, kk_ref, vv_ref = refs
    qi = pl.program_id(2)

    @pl.when(qi == 0)
    def _():
        k = k_ref[...]
        if has_ctx:
            k = _rope(k, cos_ref[...], sin_ref[...])
            kk_ref[S:S + P, :] = ck_ref[...].astype(BF16)
            vv_ref[S:S + P, :] = cv_ref[...].astype(BF16)
        kk_ref[0:S, :] = k.astype(BF16)
        vv_ref[0:S, :] = v_ref[...].astype(BF16)

    q = q_ref[...]
    if has_ctx:
        r0 = pl.multiple_of(qi * TQ, TQ)
        q = _rope(q, cos_ref[pl.ds(r0, TQ), :], sin_ref[pl.ds(r0, TQ), :])
    q = q * (DQK_D ** -0.5 * math.log2(math.e))
    lane = lax.broadcasted_iota(jnp.int32, q.shape, 1)
    kk = kk_ref[...]
    vv = vv_ref[...]
    lv = lam_ref[...]
    lam = (jnp.exp(jnp.sum(lv[0:1, :] * lv[1:2, :], axis=-1, keepdims=True))
           - jnp.exp(jnp.sum(lv[2:3, :] * lv[3:4, :], axis=-1, keepdims=True)) + lam_init)
    es, ls = [], []
    for comp in range(2):
        sel = (lane < DQK_D) if comp == 0 else (lane >= DQK_D)
        s = _dot_nt(jnp.where(sel, q, 0.0).astype(BF16), kk)
        e = jnp.exp2(s - jnp.max(s, axis=-1, keepdims=True))
        es.append(e)
        ls.append(jnp.sum(e, axis=-1, keepdims=True))
    w = es[0] * (1.0 / ls[0]) - es[1] * (lam / ls[1])
    o = _dot(w.astype(BF16), vv)
    y = o * lax.rsqrt(jnp.mean(o * o, axis=-1, keepdims=True) + EPS) * hn_ref[...]
    y_ref[...] = (y * (1.0 - lam_init)).astype(y_ref.dtype)


def _attn(a32, lamv, hnorm, B, S, lam_init, ctx=None):
    TQ = min(256, S)
    nq = S // TQ
    has_ctx = ctx is not None
    P = ctx[0].shape[2] if has_ctx else 0
    cb = lambda off: off // LANES
    kern = functools.partial(_attn_kernel, S=S, P=P, TQ=TQ, lam_init=lam_init, has_ctx=has_ctx)
    in_specs = [
        pl.BlockSpec((TQ, LANES), lambda b, h, i: (b * nq + i, cb(A32_DQ) + h)),
        pl.BlockSpec((S, LANES), lambda b, h, i: (b, cb(A32_DK) + h)),
        pl.BlockSpec((S, LANES), lambda b, h, i: (b, cb(A32_DV) + h)),
    ]
    args = [a32, a32, a32]
    if has_ctx:
        ck, cv, layer, cos, sin = ctx
        in_specs += [
            pl.BlockSpec((None, None, P, LANES), lambda b, h, i: (b, layer, 0, h)),
            pl.BlockSpec((None, None, P, LANES), lambda b, h, i: (b, layer, 0, h)),
            pl.BlockSpec((S, LANES), lambda b, h, i: (0, 0)),
            pl.BlockSpec((S, LANES), lambda b, h, i: (0, 0)),
        ]
        args += [ck, cv, cos, sin]
    in_specs += [
        pl.BlockSpec((4, DQK_D), lambda b, h, i: (0, 0)),
        pl.BlockSpec((1, LANES), lambda b, h, i: (0, h)),
    ]
    args += [lamv, hnorm]
    return pl.pallas_call(
        kern,
        grid=(B, H_D, nq),
        in_specs=in_specs,
        out_specs=pl.BlockSpec((TQ, LANES), lambda b, h, i: (b * nq + i, h)),
        out_shape=jax.ShapeDtypeStruct((B * S, BRANCH_W), BF16),
        scratch_shapes=[pltpu.VMEM((S + P, LANES), BF16), pltpu.VMEM((S + P, LANES), BF16)],
        compiler_params=_cparams("parallel", "parallel", "arbitrary"),
        name="diff_attn",
    )(*args)


def _conv_kernel(ca_ref, cb_ref, w_ref, g_ref, b_ref, y_ref, pad_ref, cv_ref, *, S):
    ca = ca_ref[...].astype(F32)
    cbv = cb_ref[...].astype(F32)
    zeros = jnp.zeros((CONV_PAD, BRANCH_W), F32)
    pad_ref[0:CONV_PAD, :] = zeros
    pad_ref[CONV_PAD + S:2 * CONV_PAD + S, :] = zeros
    pad_ref[CONV_PAD:CONV_PAD + S, :] = ca * _sigmoid(cbv)
    off = CONV_PAD - CONV_W // 2

    win_rows = CONV_ROWS + 2 * CONV_PAD

    def body(i, carry):
        base = pl.multiple_of(i * CONV_ROWS, CONV_ROWS)
        for lb in range(BRANCH_W // LANES):
            cols = slice(lb * LANES, (lb + 1) * LANES)
            win = pad_ref[pl.ds(base, win_rows), cols]
            acc = jnp.zeros((CONV_ROWS, LANES), F32)
            for r in range(8):
                rolled = win if r == 0 else pltpu.roll(win, win_rows - r, 0)
                for a in range(2 * CONV_PAD // 8):
                    j = 8 * a + r - off
                    if 0 <= j < CONV_W:
                        acc = acc + rolled[8 * a:8 * a + CONV_ROWS, :] * w_ref[j:j + 1, cols]
            cv_ref[:, cols] = acc
        acc = cv_ref[...]
        mu = jnp.mean(acc, axis=-1, keepdims=True)
        xc = acc - mu
        yn = xc * lax.rsqrt(jnp.mean(xc * xc, axis=-1, keepdims=True) + EPS) * g_ref[...] + b_ref[...]
        y_ref[pl.ds(base, CONV_ROWS), :] = (yn * _sigmoid(yn)).astype(y_ref.dtype)
        return carry

    lax.fori_loop(0, S // CONV_ROWS, body, 0)


def _conv(a16, w_dw, ln_g, ln_b, B, S):
    cb = lambda off: off // BRANCH_W
    kern = functools.partial(_conv_kernel, S=S)
    return pl.pallas_call(
        kern,
        grid=(B,),
        in_specs=[
            pl.BlockSpec((S, BRANCH_W), lambda b: (b, cb(A16_CA))),
            pl.BlockSpec((S, BRANCH_W), lambda b: (b, cb(A16_CB))),
            pl.BlockSpec((CONV_W + 1, BRANCH_W), lambda b: (0, 0)),
            pl.BlockSpec((1, BRANCH_W), lambda b: (0, 0)),
            pl.BlockSpec((1, BRANCH_W), lambda b: (0, 0)),
        ],
        out_specs=pl.BlockSpec((S, BRANCH_W), lambda b: (b, 0)),
        out_shape=jax.ShapeDtypeStruct((B * S, BRANCH_W), BF16),
        scratch_shapes=[pltpu.VMEM((S + 2 * CONV_PAD, BRANCH_W), F32),
                        pltpu.VMEM((CONV_ROWS, BRANCH_W), F32)],
        compiler_params=_cparams("parallel"),
        name="glu_conv_ln",
    )(a16, a16, w_dw, ln_g, ln_b)


def _merge_kernel(x_ref, mod_ref, ym_ref, yd_ref, yg_ref, yc_ref, g0_ref, g1_ref, g2_ref, g3_ref,
                  wb_ref, wo_ref, n2_ref, wr_ref, x1_ref, h2_ref, route_ref):
    ys = (ym_ref, yd_ref, yg_ref, yc_ref)
    gs = (g0_ref, g1_ref, g2_ref, g3_ref)
    merged = None
    for nbr in range(N_BRANCH):
        br = _dot(ys[nbr][...], wb_ref[nbr])
        term = _sigmoid(gs[nbr][...].astype(F32)) * br
        merged = term if merged is None else merged + term
    out = _dot(merged.astype(BF16), wo_ref[...])
    x1 = x_ref[...] + mod_ref[2:3, :] * out
    x1_ref[...] = x1
    y = x1 * lax.rsqrt(jnp.mean(x1 * x1, axis=-1, keepdims=True) + EPS) * n2_ref[...]
    h2 = y * (1.0 + mod_ref[4:5, :]) + mod_ref[3:4, :]
    h2_ref[...] = h2.reshape(h2_ref.shape)
    logits = jnp.dot(h2, wr_ref[...], preferred_element_type=F32, precision=lax.Precision.HIGHEST)
    lane = lax.broadcasted_iota(jnp.int32, logits.shape, 1)
    neg = -jnp.inf
    big = jnp.int32(LANES)
    is_g = lane < N_GROUPS
    gl = jnp.where(is_g, logits, neg)
    gmax = jnp.max(gl, axis=-1, keepdims=True)
    gidx = jnp.min(jnp.where(is_g & (gl == gmax), lane, big), axis=-1, keepdims=True)
    g_p = 1.0 / jnp.sum(jnp.where(is_g, jnp.exp(gl - gmax), 0.0), axis=-1, keepdims=True)
    e_lane = lane - N_GROUPS
    in_grp = (e_lane >= 0) & (e_lane < N_EXPERTS) & ((e_lane // EXPERTS_PER_GROUP) == gidx)
    el = jnp.where(in_grp, logits, neg)
    v1 = jnp.max(el, axis=-1, keepdims=True)
    i1 = jnp.min(jnp.where(in_grp & (el == v1), lane, big), axis=-1, keepdims=True)
    el2 = jnp.where(lane == i1, neg, el)
    v2 = jnp.max(el2, axis=-1, keepdims=True)
    i2 = jnp.min(jnp.where(in_grp & (lane != i1) & (el2 == v2), lane, big), axis=-1, keepdims=True)
    e2 = jnp.exp(v2 - v1)
    w1 = g_p / (1.0 + e2)
    w2 = g_p * e2 / (1.0 + e2)
    id1 = (i1 - N_GROUPS).astype(F32)
    id2 = (i2 - N_GROUPS).astype(F32)
    route_ref[...] = jnp.where(lane == 0, id1, jnp.where(lane == 1, id2,
                               jnp.where(lane == 2, w1, jnp.where(lane == 3, w2, 0.0))))


def _merge(x2d, mod, a16, ym, yd, yg, yc, wb, wo, n2, wr, rows_per_mod, tm):
    T = x2d.shape[0]
    gcb = A16_GATE // D_MODEL
    row = lambda i: (i, 0)
    return pl.pallas_call(
        _merge_kernel,
        grid=(T // tm,),
        in_specs=[
            pl.BlockSpec((tm, D_MODEL), row),
            pl.BlockSpec((None, N_MOD, D_MODEL), lambda i: ((i * tm) // rows_per_mod, 0, 0)),
            pl.BlockSpec((tm, BRANCH_W), row),
            pl.BlockSpec((tm, BRANCH_W), row),
            pl.BlockSpec((tm, BRANCH_W), row),
            pl.BlockSpec((tm, BRANCH_W), row),
            pl.BlockSpec((tm, D_MODEL), lambda i: (i, gcb + 0)),
            pl.BlockSpec((tm, D_MODEL), lambda i: (i, gcb + 1)),
            pl.BlockSpec((tm, D_MODEL), lambda i: (i, gcb + 2)),
            pl.BlockSpec((tm, D_MODEL), lambda i: (i, gcb + 3)),
            pl.BlockSpec((N_BRANCH, BRANCH_W, D_MODEL), lambda i: (0, 0, 0)),
            pl.BlockSpec((D_MODEL, D_MODEL), lambda i: (0, 0)),
            pl.BlockSpec((1, D_MODEL), lambda i: (0, 0)),
            pl.BlockSpec((D_MODEL, LANES), lambda i: (0, 0)),
        ],
        out_specs=[
            pl.BlockSpec((tm, D_MODEL), row),
            pl.BlockSpec((tm, TOK_SUB, LANES), lambda i: (i, 0, 0)),
            pl.BlockSpec((tm, LANES), row),
        ],
        out_shape=[
            jax.ShapeDtypeStruct((T, D_MODEL), F32),
            jax.ShapeDtypeStruct((T, TOK_SUB, LANES), F32),
            jax.ShapeDtypeStruct((T, LANES), F32),
        ],
        compiler_params=_cparams("parallel"),
        name="merge_outproj_route",
    )(x2d, mod, ym, yd, yg, yc, a16, a16, a16, a16, wb, wo, n2, wr)


def _gather_rows(idx_ref, src_hbm, dst, sem, n):
    def body(j, carry):
        for u in range(ROW_DMA_UNROLL):
            r = j * ROW_DMA_UNROLL + u
            pltpu.make_async_copy(src_hbm.at[idx_ref[0, r]], dst.at[r], sem).start(priority=u % 2)
        return carry

    lax.fori_loop(0, n // ROW_DMA_UNROLL, body, 0)


def _scatter_rows(idx_ref, src, dst_hbm, sem, n):
    def body(j, carry):
        for u in range(ROW_DMA_UNROLL):
            r = j * ROW_DMA_UNROLL + u
            pltpu.make_async_copy(src.at[r], dst_hbm.at[idx_ref[0, r]], sem).start(priority=u % 2)
        return carry

    lax.fori_loop(0, n // ROW_DMA_UNROLL, body, 0)


def _wait_rows(buf, sem):
    pltpu.make_async_copy(buf, buf, sem).wait()


def _moe_dispatch_kernel(d0_ref, d1_ref, h_ref, xg_in, xg_out, sem, *, tm):
    del xg_in
    _scatter_rows(d0_ref, h_ref, xg_out, sem.at[0], tm)
    _scatter_rows(d1_ref, h_ref, xg_out, sem.at[1], tm)
    _wait_rows(h_ref, sem.at[0])
    _wait_rows(h_ref, sem.at[1])


def _moe_dispatch(h3, dest, n_rows, tm):
    T = h3.shape[0]
    kern = functools.partial(_moe_dispatch_kernel, tm=tm)
    return pl.pallas_call(
        kern,
        grid=(T // tm,),
        in_specs=[
            pl.BlockSpec((None, None, 1, tm), lambda i: (0, i, 0, 0), memory_space=pltpu.SMEM),
            pl.BlockSpec((None, None, 1, tm), lambda i: (1, i, 0, 0), memory_space=pltpu.SMEM),
            pl.BlockSpec((tm, TOK_SUB, LANES), lambda i: (i, 0, 0)),
            pl.BlockSpec(memory_space=pl.ANY),
        ],
        out_specs=pl.BlockSpec(memory_space=pl.ANY),
        out_shape=jax.ShapeDtypeStruct((n_rows, TOK_SUB, LANES), F32),
        input_output_aliases={3: 0},
        scratch_shapes=[pltpu.SemaphoreType.DMA((2,))],
        compiler_params=_cparams("arbitrary"),
        name="moe_dispatch",
    )(dest, dest, h3, jnp.zeros((n_rows, TOK_SUB, LANES), F32))


def _moe_ffn_kernel(te_ref, x_ref, w1_ref, w3_ref, w2_ref, o_ref):
    del te_ref
    x = x_ref[...].reshape(MOE_TILE, D_MODEL).astype(BF16)
    a = _dot(x, w1_ref[...].astype(BF16))
    b = _dot(x, w3_ref[...].astype(BF16))
    s = (a * _sigmoid(a)) * b
    y = _dot(s.astype(BF16), w2_ref[...].astype(BF16))
    o_ref[...] = y.reshape(o_ref.shape)


def _moe_ffn(xg, tile_e, w1, w3, w2):
    ntiles = xg.shape[0] // MOE_TILE
    tile = pl.BlockSpec((MOE_TILE, TOK_SUB, LANES), lambda i, te: (i, 0, 0))
    return pl.pallas_call(
        _moe_ffn_kernel,
        grid_spec=pltpu.PrefetchScalarGridSpec(
            num_scalar_prefetch=1,
            grid=(ntiles,),
            in_specs=[
                tile,
                pl.BlockSpec((None, D_MODEL, D_EXPERT), lambda i, te: (te[i], 0, 0)),
                pl.BlockSpec((None, D_MODEL, D_EXPERT), lambda i, te: (te[i], 0, 0)),
                pl.BlockSpec((None, D_EXPERT, D_MODEL), lambda i, te: (te[i], 0, 0)),
            ],
            out_specs=tile,
        ),
        out_shape=jax.ShapeDtypeStruct(xg.shape, F32),
        compiler_params=_cparams("parallel"),
        name="moe_grouped_experts",
    )(tile_e, xg, w1, w3, w2)


def _moe_combine_kernel(d0_ref, d1_ref, y_hbm, route_ref, x1_ref, mod_ref, fn_ref, o_ref,
                        ga, gb, sem, *, tm, final_norm):
    _gather_rows(d0_ref, y_hbm, ga, sem.at[0], tm)
    _gather_rows(d1_ref, y_hbm, gb, sem.at[1], tm)
    rt = route_ref[...]
    _wait_rows(ga, sem.at[0])
    _wait_rows(gb, sem.at[1])
    y = rt[:, 2:3] * ga[...].reshape(tm, D_MODEL) + rt[:, 3:4] * gb[...].reshape(tm, D_MODEL)
    x2 = x1_ref[...] + mod_ref[5:6, :] * y
    if final_norm:
        x2 = x2 * lax.rsqrt(jnp.mean(x2 * x2, axis=-1, keepdims=True) + EPS) * fn_ref[...]
    o_ref[...] = x2


def _moe_combine(yg, dest, route, x1, mod, fn, rows_per_mod, tm, final_norm):
    T = x1.shape[0]
    kern = functools.partial(_moe_combine_kernel, tm=tm, final_norm=final_norm)
    return pl.pallas_call(
        kern,
        grid=(T // tm,),
        in_specs=[
            pl.BlockSpec((None, None, 1, tm), lambda i: (0, i, 0, 0), memory_space=pltpu.SMEM),
            pl.BlockSpec((None, None, 1, tm), lambda i: (1, i, 0, 0), memory_space=pltpu.SMEM),
            pl.BlockSpec(memory_space=pl.ANY),
            pl.BlockSpec((tm, LANES), lambda i: (i, 0)),
            pl.BlockSpec((tm, D_MODEL), lambda i: (i, 0)),
            pl.BlockSpec((None, N_MOD, D_MODEL), lambda i: ((i * tm) // rows_per_mod, 0, 0)),
            pl.BlockSpec((1, D_MODEL), lambda i: (0, 0)),
        ],
        out_specs=pl.BlockSpec((tm, D_MODEL), lambda i: (i, 0)),
        out_shape=jax.ShapeDtypeStruct((T, D_MODEL), F32),
        scratch_shapes=[pltpu.VMEM((tm, TOK_SUB, LANES), F32), pltpu.VMEM((tm, TOK_SUB, LANES), F32),
                        pltpu.SemaphoreType.DMA((2,))],
        compiler_params=_cparams("arbitrary"),
        name="moe_combine",
    )(dest, dest, yg, route, x1, mod, fn)


def _route_plan(route, T):
    ntiles = (2 * T + N_EXPERTS * (MOE_TILE - 1) + MOE_TILE - 1) // MOE_TILE
    ef = route[:, 0:2].astype(jnp.int32).reshape(-1)
    oh = (ef[:, None] == jnp.arange(N_EXPERTS, dtype=jnp.int32)[None, :]).astype(jnp.int32)
    csum = jnp.cumsum(oh, axis=0)
    rank = jnp.sum((csum - oh) * oh, axis=1)
    counts = csum[-1]
    padded = ((counts + MOE_TILE - 1) // MOE_TILE) * MOE_TILE
    seg_end = jnp.cumsum(padded)
    dest = jnp.sum(oh * (seg_end - padded)[None, :], axis=1) + rank
    tile_row = jnp.arange(ntiles, dtype=jnp.int32) * MOE_TILE
    tile_e = jnp.minimum(jnp.sum((tile_row[:, None] >= seg_end[None, :]).astype(jnp.int32), axis=1),
                         N_EXPERTS - 1)
    return jnp.transpose(dest.reshape(T, 2)), tile_e, ntiles * MOE_TILE


def _split_w_in(w):
    sizes = (H_M * DK_M, H_M * DK_M, H_M * DV_M, H_M * DV_M, 2 * H_M, 2 * H_M,
             H_D * 2 * DQK_D, H_D * 2 * DQK_D, H_D * DV_D,
             H_G * DK_G, H_G * DK_G, H_G * DV_G, 2 * GATE_RANK, H_G * DV_G,
             BRANCH_W, BRANCH_W, N_BRANCH * D_MODEL)
    outs, acc = [], 0
    for s in sizes:
        outs.append(w[:, acc:acc + s])
        acc += s
    return outs


def _pack_layer_params(p):
    (m_q, m_k, m_v, m_o, m_i, m_f, d_q, d_k, d_v, g_q, g_k, g_v, g_a, g_r, c_a, c_b, gate) = \
        _split_w_in(p['w_in'])
    w16 = jnp.concatenate([m_q, m_k, m_v, m_o, gate, g_r, c_a, c_b], axis=1).astype(BF16)
    gqk = jnp.concatenate([g_q.reshape(D_MODEL, H_G, DK_G), g_k.reshape(D_MODEL, H_G, DK_G)],
                          axis=2).reshape(D_MODEL, 2 * H_G * DK_G)
    small = jnp.concatenate(
        [m_i, m_f, g_a, jnp.zeros((D_MODEL, LANES - 4 * H_M - 2 * GATE_RANK), F32)], axis=1)
    w32 = jnp.concatenate([d_q, d_k, d_v, gqk, g_v, small], axis=1).astype(BF16)
    bi = p['b_m_i'].reshape(2, H_M)
    bf = p['b_m_f'].reshape(2, H_M)
    bcol = jnp.stack([bi[0], bi[1], bf[0], bf[1]], axis=-1)
    wup = p['w_gla_up'].reshape(2, GATE_RANK, H_G, DK_G)
    wup_pad = jnp.zeros((H_G, 2, LANES, LANES), F32)
    bup = p['b_gla_gate'].reshape(2, H_G, DK_G)
    for d in range(2):
        blk = jnp.transpose(wup[d], (1, 0, 2))
        blk = jnp.concatenate([blk, blk], axis=-1)
        r0 = SM_GA + d * GATE_RANK
        wup_pad = wup_pad.at[:, d, r0:r0 + GATE_RANK, :].set(blk)
    bup2 = jnp.transpose(jnp.concatenate([bup, bup], axis=-1), (1, 0, 2))[:, :, None, :]
    wr = jnp.concatenate([p['w_group_router'], p['w_expert_router'],
                          jnp.zeros((D_MODEL, LANES - N_GROUPS - N_EXPERTS), F32)], axis=1)
    return dict(
        w16=w16, w32=w32, brow=bcol.reshape(H_M, 4, 1),
        wup=wup_pad.astype(BF16), bup=bup2,
        wdw=jnp.concatenate([p['w_dw'], jnp.zeros((1, BRANCH_W), F32)], axis=0),
        ln_g=p['conv_ln_g'].reshape(1, BRANCH_W), ln_b=p['conv_ln_b'].reshape(1, BRANCH_W),
        hn_m=p['hnorm_m'].reshape(1, BRANCH_W), hn_d=p['hnorm_d'].reshape(1, BRANCH_W),
        hn_g=p['hnorm_g'].reshape(1, BRANCH_W),
        lamv=jnp.stack([p['lam_q1'], p['lam_k1'], p['lam_q2'], p['lam_k2']], axis=0),
        wb=p['w_branch'].astype(BF16), wo=p['w_out'].astype(BF16),
        n1=p['norm1'].reshape(1, D_MODEL), n2=p['norm2'].reshape(1, D_MODEL), wr=wr,
        w1=p['w_e1'].reshape(N_EXPERTS, D_MODEL, D_EXPERT),
        w3=p['w_e3'].reshape(N_EXPERTS, D_MODEL, D_EXPERT),
        w2=p['w_e2'].reshape(N_EXPERTS, D_EXPERT, D_MODEL),
    )


def _rope_tables(S):
    rows = S // GRID_W
    r, col = jnp.meshgrid(jnp.arange(rows, dtype=F32), jnp.arange(GRID_W, dtype=F32), indexing='ij')
    r, col = r.reshape(-1), col.reshape(-1)
    n_freq = DQK_D // 4
    inv = ROPE_BASE ** (-jnp.arange(n_freq, dtype=F32) / n_freq)
    ang = jnp.concatenate([r[:, None] * inv, col[:, None] * inv], axis=-1)
    cos, sin = jnp.cos(ang), jnp.sin(ang)
    cos_t = jnp.tile(cos, (1, LANES // (DQK_D // 2)))
    sin_t = jnp.tile(jnp.concatenate([-sin, sin], axis=-1), (1, LANES // DQK_D))
    return cos_t, sin_t


def _pick_tile(T, cap):
    t = min(T, cap)
    while T % t:
        t //= 2
    return t


def _layer(x2d, mod, pk, B, S, lam_init, ctx, final_norm, fn):
    T = B * S
    rows_per_mod = T // mod.shape[0]
    tm = _pick_tile(rows_per_mod, 1024)
    a16 = _inproj(x2d, mod, pk['n1'], pk['w16'], BF16, rows_per_mod, tm, 768)
    a32 = _inproj(x2d, mod, pk['n1'], pk['w32'], F32, rows_per_mod, tm, 896)

    L = min(MLSTM_CHUNK, S)
    sm = a32[:, A32_SM:A32_SM + 4 * H_M]
    grow = jnp.transpose(sm.reshape(T // L, L, 4, H_M), (3, 0, 2, 1))
    if ctx is None:
        c0 = jnp.zeros((B, 2, H_M, DK_M, DV_M), F32)
        n0 = jnp.zeros((B, 2, H_M, 1, DK_M), F32)
        m0 = jnp.zeros((B, 2, H_M, 1, LANES), F32)
        s0 = jnp.zeros((B, 2, H_G, DK_G, DV_G), F32)
        attn_ctx = None
    else:
        c0 = ctx['C']
        n0 = ctx['n'][:, :, :, None, :]
        m0 = jnp.broadcast_to(ctx['m'][:, :, :, None, None], (B, 2, H_M, 1, LANES))
        s0 = ctx['S']
        attn_ctx = (ctx['k'], ctx['v'], ctx['layer'], ctx['cos'], ctx['sin'])
    ym, c_f, n_f, m_f = _mlstm(a16, grow, pk['brow'], c0, n0, m0, pk['hn_m'], B, S)
    yd = _attn(a32, pk['lamv'], pk['hn_d'], B, S, lam_init, attn_ctx)
    yg, s_f = _gla(a32, a16, pk['wup'], pk['bup'], s0, pk['hn_g'], B, S)
    yc = _conv(a16, pk['wdw'], pk['ln_g'], pk['ln_b'], B, S)
    x1, h3, route = _merge(x2d, mod, a16, ym, yd, yg, yc, pk['wb'], pk['wo'], pk['n2'], pk['wr'],
                           rows_per_mod, _pick_tile(rows_per_mod, 512))
    dest, tile_e, n_rows = _route_plan(route, T)
    tmd = _pick_tile(T, 512)
    xg = _moe_dispatch(h3, dest.reshape(2, T // tmd, 1, tmd), n_rows, tmd)
    y_grouped = _moe_ffn(xg, tile_e, pk['w1'], pk['w3'], pk['w2'])
    tmc = _pick_tile(rows_per_mod, 256)
    x2 = _moe_combine(y_grouped, dest.reshape(2, T // tmc, 1, tmc), route, x1, mod, fn,
                      rows_per_mod, tmc, final_norm)
    state = None
    if ctx is None:
        state = (a32[:, A32_DK:A32_DK + H_D * 2 * DQK_D].reshape(B, S, H_D, 2 * DQK_D),
                 a32[:, A32_DV:A32_DV + H_D * DV_D].reshape(B, S, H_D, DV_D),
                 c_f, n_f[:, :, :, 0, :], m_f[:, :, :, 0, 0], s_f)
    return x2, state


def kernel(x_prompt, x_sample, c, cache_diff_k, cache_diff_v, state_mlstm_C, state_mlstm_n, state_mlstm_m, state_gla_S, c_ctx, w_mod, b_mod, norm1, w_in, b_m_i, b_m_f, lam_q1, lam_k1, lam_q2, lam_k2, w_gla_up, b_gla_gate, w_dw, conv_ln_g, conv_ln_b, hnorm_m, hnorm_d, hnorm_g, w_branch, w_out, norm2, w_group_router, w_expert_router, w_e1, w_e3, w_e2, final_norm):
    Bp, Sp, _ = x_prompt.shape
    Bs, Ss, _ = x_sample.shape
    P = cache_diff_k.shape[2]
    n_cond = 8 * ((1 + Bs + 7) // 8)
    cond = jnp.concatenate([c_ctx[None, :], c, jnp.zeros((n_cond - 1 - Bs, D_MODEL), F32)], axis=0)
    mod_all = _modulation(cond, w_mod, b_mod).reshape(DEPTH, n_cond, N_MOD, D_MODEL)
    cos_t, sin_t = _rope_tables(Ss)
    ck4 = cache_diff_k.reshape(Bs, DEPTH, P, H_D * 2 * DQK_D)
    cv4 = cache_diff_v.reshape(Bs, DEPTH, P, H_D * DV_D)
    fn = final_norm.reshape(1, D_MODEL)
    yp = x_prompt.reshape(Bp * Sp, D_MODEL)
    ys = x_sample.reshape(Bs * Ss, D_MODEL)
    states = []
    for l in range(DEPTH):
        p = {'w_in': w_in[l], 'b_m_i': b_m_i[l], 'b_m_f': b_m_f[l], 'lam_q1': lam_q1[l],
             'lam_k1': lam_k1[l], 'lam_q2': lam_q2[l], 'lam_k2': lam_k2[l],
             'w_gla_up': w_gla_up[l], 'b_gla_gate': b_gla_gate[l], 'w_dw': w_dw[l],
             'conv_ln_g': conv_ln_g[l], 'conv_ln_b': conv_ln_b[l], 'hnorm_m': hnorm_m[l],
             'hnorm_d': hnorm_d[l], 'hnorm_g': hnorm_g[l], 'w_branch': w_branch[l],
             'w_out': w_out[l], 'norm1': norm1[l], 'norm2': norm2[l],
             'w_group_router': w_group_router[l], 'w_expert_router': w_expert_router[l],
             'w_e1': w_e1[l], 'w_e3': w_e3[l], 'w_e2': w_e2[l]}
        pk = _pack_layer_params(p)
        lam_init = 0.8 - 0.6 * math.exp(-0.3 * l)
        last = l == DEPTH - 1
        yp, st = _layer(yp, mod_all[l, 0:1], pk, Bp, Sp, lam_init, None, last, fn)
        states.append(st)
        ctx = {'k': ck4, 'v': cv4, 'layer': l, 'cos': cos_t, 'sin': sin_t,
               'C': state_mlstm_C[:, l], 'n': state_mlstm_n[:, l], 'm': state_mlstm_m[:, l],
               'S': state_gla_S[:, l]}
        ys, _ = _layer(ys, mod_all[l, 1:1 + Bs], pk, Bs, Ss, lam_init, ctx, last, fn)
    stack = lambda i: jnp.stack([s[i] for s in states], axis=1)
    return (yp.reshape(Bp, Sp, D_MODEL), ys.reshape(Bs, Ss, D_MODEL),
            stack(0), stack(1), stack(2), stack(3), stack(4), stack(5))
```

```python
import functools
import math

import jax
import jax.numpy as jnp
from jax import lax
from jax.experimental import pallas as pl
from jax.experimental.pallas import tpu as pltpu

F32 = jnp.float32
BF16 = jnp.bfloat16

D_MODEL = 1024
DEPTH = 2
GRID_W = 64
BRANCH_W = 512
N_BRANCH = 4
H_M, DK_M, DV_M = 4, 128, 128
H_D, DQK_D, DV_D = 4, 64, 128
H_G, DK_G, DV_G = 4, 64, 128
GATE_RANK = 16
GLA_TAU = 16.0
CONV_W = 31
N_GROUPS, EXPERTS_PER_GROUP, D_EXPERT = 4, 4, 512
N_EXPERTS = N_GROUPS * EXPERTS_PER_GROUP
ROPE_BASE = 10000.0
EPS = 1e-6
N_MOD = 6

LANES = 128
VMEM_LIMIT = 48 * 1024 * 1024

A16_MQ, A16_MK, A16_MV, A16_MO = 0, 512, 1024, 1536
A16_GATE, A16_GR, A16_CA, A16_CB = 2048, 6144, 6656, 7168
N_A16 = 7680
A32_DQ, A32_DK, A32_DV, A32_GQK, A32_GV, A32_SM = 0, 512, 1024, 1536, 2048, 2560
N_A32 = 2688
SM_MI, SM_MF, SM_GA = 0, 8, 16

MLSTM_CHUNK = 128
GLA_CHUNK = 64
GLA_SUB = 16
GLA_EXP_CLAMP = 80.0
CONV_ROWS = 64
CONV_PAD = 16
TOK_SUB = D_MODEL // LANES
MOE_TILE = 256
ROW_DMA_UNROLL = 8


def _cparams(*sem):
    return pltpu.CompilerParams(dimension_semantics=sem, vmem_limit_bytes=VMEM_LIMIT)


def _log_sigmoid(x):
    return jnp.minimum(x, 0.0) - jnp.log1p(jnp.exp(-jnp.abs(x)))


def _sigmoid(x):
    return 1.0 / (1.0 + jnp.exp(-x))


def _dot(a, b):
    return jnp.dot(a, b, preferred_element_type=F32)


def _dot_nt(a, b):
    return lax.dot_general(a, b, (((1,), (1,)), ((), ())), preferred_element_type=F32)


def _dot_tn(a, b):
    return lax.dot_general(a, b, (((0,), (0,)), ((), ())), preferred_element_type=F32)


def _mod_kernel(c_ref, w_ref, b_ref, o_ref):
    c = c_ref[...]
    a = (c * _sigmoid(c)).astype(BF16)
    o_ref[...] = _dot(a, w_ref[...].astype(BF16)) + b_ref[...]


def _modulation(cond, w_mod, b_mod):
    R = cond.shape[0]
    tn = 512
    nmod = N_MOD * D_MODEL
    return pl.pallas_call(
        _mod_kernel,
        grid=(DEPTH, nmod // tn),
        in_specs=[
            pl.BlockSpec((R, D_MODEL), lambda l, j: (0, 0)),
            pl.BlockSpec((None, D_MODEL, tn), lambda l, j: (l, 0, j)),
            pl.BlockSpec((None, 1, tn), lambda l, j: (l, 0, j)),
        ],
        out_specs=pl.BlockSpec((None, R, tn), lambda l, j: (l, 0, j)),
        out_shape=jax.ShapeDtypeStruct((DEPTH, R, nmod), F32),
        compiler_params=_cparams("parallel", "parallel"),
        name="adaln_mod",
    )(cond, w_mod, b_mod.reshape(DEPTH, 1, nmod))


def _inproj_kernel(x_ref, mod_ref, g_ref, w_ref, o_ref, h_ref):
    @pl.when(pl.program_id(1) == 0)
    def _():
        x = x_ref[...]
        y = x * lax.rsqrt(jnp.mean(x * x, axis=-1, keepdims=True) + EPS) * g_ref[...]
        h_ref[...] = (y * (1.0 + mod_ref[1:2, :]) + mod_ref[0:1, :]).astype(BF16)

    o_ref[...] = _dot(h_ref[...], w_ref[...]).astype(o_ref.dtype)


def _inproj(x2d, mod, g, w, out_dtype, rows_per_mod, tm, tn):
    T = x2d.shape[0]
    N = w.shape[1]
    return pl.pallas_call(
        _inproj_kernel,
        grid=(T // tm, N // tn),
        in_specs=[
            pl.BlockSpec((tm, D_MODEL), lambda i, j: (i, 0)),
            pl.BlockSpec((None, N_MOD, D_MODEL), lambda i, j: ((i * tm) // rows_per_mod, 0, 0)),
            pl.BlockSpec((1, D_MODEL), lambda i, j: (0, 0)),
            pl.BlockSpec((D_MODEL, tn), lambda i, j: (0, j)),
        ],
        out_specs=pl.BlockSpec((tm, tn), lambda i, j: (i, j)),
        out_shape=jax.ShapeDtypeStruct((T, N), out_dtype),
        scratch_shapes=[pltpu.VMEM((tm, D_MODEL), BF16)],
        compiler_params=_cparams("parallel", "arbitrary"),
        name="norm_inproj",
    )(x2d, mod, g, w)


def _mlstm_local(c, q_ref, k_ref, v_ref, gr_ref, br_ref, pr_ref, bb_ref, mb_ref, kv_ref, rp_ref, L):
    scale = DK_M ** -0.5
    ti = lax.broadcasted_iota(jnp.int32, (L, L), 0)
    si = lax.broadcasted_iota(jnp.int32, (L, L), 1)
    sub = lax.broadcasted_iota(jnp.int32, (8, LANES), 0)
    rows = pl.ds(pl.multiple_of(c * L, L), L)
    q = q_ref[rows, :]
    v_ext = jnp.concatenate([v_ref[rows, :], jnp.ones((L, LANES), BF16)], axis=1)
    k_t = k_ref[rows, :].astype(F32).T
    qk = _dot(q, k_t.astype(BF16)) * scale
    grow = gr_ref[c] + br_ref[...]
    for d in range(2):
        rev = d == 1
        mask = (si >= ti) if rev else (si <= ti)
        src = ((si <= ti) if rev else (si >= ti)).astype(BF16)
        i_row = grow[d:d + 1, :]
        f_row = _log_sigmoid(grow[2 + d:3 + d, :])
        f8 = jnp.broadcast_to(f_row, (8, L))
        f_hi = f8.astype(BF16)
        f_r1 = f8 - f_hi.astype(F32)
        f_mid = f_r1.astype(BF16)
        f_lo = (f_r1 - f_mid.astype(F32)).astype(BF16)
        b_row = (_dot(f_hi, src) + _dot(f_mid, src) + _dot(f_lo, src))[0:1, :]
        b_col = jnp.sum(jnp.where(mask, f_row, 0.0), axis=1, keepdims=True)
        log_d = jnp.where(mask, b_col + (i_row - b_row), -jnp.inf)
        m_loc = jnp.max(log_d, axis=1, keepdims=True)
        smat = qk * jnp.exp(log_d - m_loc)
        pr_ref[d, rows, :] = _dot(smat.astype(BF16), v_ext)
        bb_ref[d, rows, :] = jnp.broadcast_to(b_col, (L, LANES))
        mb_ref[d, rows, :] = jnp.broadcast_to(m_loc, (L, LANES))
        b_last = jnp.sum(f_row, axis=1, keepdims=True)
        ls_row = b_last - b_row + i_row
        m2 = jnp.max(ls_row, axis=1, keepdims=True)
        kw_t = (k_t * jnp.exp(ls_row - m2)).astype(BF16)
        kv_ref[d, c] = scale * _dot(kw_t, v_ext)
        rp_ref[d, c] = jnp.where(sub == 0, b_last, m2)


def _mlstm_carry(c, d, carry, q_ref, pr_ref, bb_ref, mb_ref, kv_ref, rp_ref, h_ref, L):
    cn, m = carry
    two = lambda x: jnp.concatenate([x, x], axis=1)
    rows = pl.ds(pl.multiple_of(c * L, L), L)
    bb = bb_ref[d, rows, :]
    mb = mb_ref[d, rows, :]
    m_t = jnp.maximum(bb + m, mb)
    a_int = jnp.exp(bb + m - m_t)
    e_loc = jnp.exp(mb - m_t)
    nd = two(a_int) * _dot(q_ref[rows, :], cn.astype(BF16)) + two(e_loc) * pr_ref[d, rows, :]
    h_ref[d, rows, :] = nd[:, :DV_M] / jnp.maximum(jnp.abs(nd[:, DV_M:]), jnp.exp(-m_t))
    rp = rp_ref[d, c]
    b_last, m2 = rp[0:1, :], rp[1:2, :]
    m_new = jnp.maximum(b_last + m, m2)
    a_c = jnp.exp(b_last + m - m_new)
    e2 = jnp.exp(m2 - m_new)
    return two(a_c) * cn + two(e2) * kv_ref[d, c], m_new


def _mlstm_kernel(q_ref, k_ref, v_ref, og_ref, gr_ref, br_ref, c0_ref, n0_ref, m0_ref, hn_ref,
                  y_ref, c_out_ref, n_out_ref, m_out_ref,
                  pr_ref, bb_ref, mb_ref, kv_ref, rp_ref, h_ref, *, L, S):
    nch = S // L

    def local(ci, carry):
        _mlstm_local(ci, q_ref, k_ref, v_ref, gr_ref, br_ref, pr_ref, bb_ref, mb_ref, kv_ref,
                     rp_ref, L)
        return carry

    lax.fori_loop(0, nch, local, 0, unroll=2)
    step = functools.partial(_mlstm_carry, q_ref=q_ref, pr_ref=pr_ref, bb_ref=bb_ref, mb_ref=mb_ref,
                             kv_ref=kv_ref, rp_ref=rp_ref, h_ref=h_ref, L=L)

    def body(ci, carry):
        return step(ci, 0, carry[0]), step(nch - 1 - ci, 1, carry[1])

    def init(d):
        n_rep = jnp.broadcast_to(n0_ref[d], (DK_M, DK_M)).T
        return jnp.concatenate([c0_ref[d], n_rep], axis=1), m0_ref[d]

    fin = lax.fori_loop(0, nch, body, (init(0), init(1)), unroll=2)
    for d in range(2):
        cn, m = fin[d]
        c_out_ref[d] = cn[:, :DV_M]
        n_out_ref[d] = cn[:, DV_M:].T[0:1, :]
        m_out_ref[d] = m

    hm = h_ref[0] + h_ref[1]
    y = hm * lax.rsqrt(jnp.mean(hm * hm, axis=-1, keepdims=True) + EPS) * hn_ref[...]
    y_ref[...] = (y * _sigmoid(og_ref[...].astype(F32))).astype(y_ref.dtype)


def _mlstm(a16, grow, brow, c0, n0, m0, hnorm, B, S):
    L = min(MLSTM_CHUNK, S)
    nch = S // L
    cb = lambda off: off // LANES
    kern = functools.partial(_mlstm_kernel, L=L, S=S)
    return pl.pallas_call(
        kern,
        grid=(B, H_M),
        in_specs=[
            pl.BlockSpec((S, LANES), lambda b, h: (b, cb(A16_MQ) + h)),
            pl.BlockSpec((S, LANES), lambda b, h: (b, cb(A16_MK) + h)),
            pl.BlockSpec((S, LANES), lambda b, h: (b, cb(A16_MV) + h)),
            pl.BlockSpec((S, LANES), lambda b, h: (b, cb(A16_MO) + h)),
            pl.BlockSpec((None, nch, 4, L), lambda b, h: (h, b, 0, 0)),
            pl.BlockSpec((None, 4, 1), lambda b, h: (h, 0, 0)),
            pl.BlockSpec((None, 2, None, DK_M, DV_M), lambda b, h: (b, 0, h, 0, 0)),
            pl.BlockSpec((None, 2, None, 1, DK_M), lambda b, h: (b, 0, h, 0, 0)),
            pl.BlockSpec((None, 2, None, 1, LANES), lambda b, h: (b, 0, h, 0, 0)),
            pl.BlockSpec((1, LANES), lambda b, h: (0, h)),
        ],
        out_specs=[
            pl.BlockSpec((S, LANES), lambda b, h: (b, h)),
            pl.BlockSpec((None, 2, None, DK_M, DV_M), lambda b, h: (b, 0, h, 0, 0)),
            pl.BlockSpec((None, 2, None, 1, DK_M), lambda b, h: (b, 0, h, 0, 0)),
            pl.BlockSpec((None, 2, None, 1, LANES), lambda b, h: (b, 0, h, 0, 0)),
        ],
        out_shape=[
            jax.ShapeDtypeStruct((B * S, BRANCH_W), BF16),
            jax.ShapeDtypeStruct((B, 2, H_M, DK_M, DV_M), F32),
            jax.ShapeDtypeStruct((B, 2, H_M, 1, DK_M), F32),
            jax.ShapeDtypeStruct((B, 2, H_M, 1, LANES), F32),
        ],
        scratch_shapes=[pltpu.VMEM((2, S, 2 * DV_M), F32), pltpu.VMEM((2, S, LANES), F32),
                        pltpu.VMEM((2, S, LANES), F32), pltpu.VMEM((2, nch, DK_M, 2 * DV_M), F32),
                        pltpu.VMEM((2, nch, 8, LANES), F32), pltpu.VMEM((2, S, DV_M), F32)],
        compiler_params=_cparams("parallel", "parallel"),
        name="mlstm",
    )(a16, a16, a16, a16, grow, brow, c0, n0, m0, hnorm)


def _gla_local(c, q2_ref, k2_ref, v_ref, la_ref, oa_ref, qt_ref, u_ref, dec_ref, L):
    nb = L // GLA_SUB
    ti = lax.broadcasted_iota(jnp.int32, (L, L), 0)
    si = lax.broadcasted_iota(jnp.int32, (L, L), 1)
    row_blk = lax.broadcasted_iota(jnp.int32, (L, LANES), 0) // GLA_SUB
    lo_half = lax.broadcasted_iota(jnp.int32, (L, LANES), 1) < DK_G
    eye = (lax.broadcasted_iota(jnp.int32, (DK_G, LANES), 0)
           == lax.broadcasted_iota(jnp.int32, (DK_G, LANES), 1))
    rows = pl.ds(pl.multiple_of(c * L, L), L)
    q2 = q2_ref[rows, :]
    k2 = k2_ref[rows, :]
    v = v_ref[rows, :].astype(BF16)
    for d in range(2):
        rev = d == 1
        mask = (si >= ti) if rev else (si <= ti)
        tri = mask.astype(BF16)
        la2 = la_ref[d, rows, :]
        la_hi = la2.astype(BF16)
        la_lo = (la2 - la_hi.astype(F32)).astype(BF16)
        g2 = _dot(tri, la_hi) + _dot(tri, la_lo)
        qt_ref[d, rows, :] = (q2 * jnp.exp(g2))[:, :DK_G].astype(BF16)
        a_parts, b_parts = [], []
        for p in range(nb // 2):
            ia, ib = 2 * p, 2 * p + 1
            ra = ia * GLA_SUB + (GLA_SUB - 1 if rev else 0)
            rb = ib * GLA_SUB + (GLA_SUB - 1 if rev else 0)
            ref2 = jnp.where(lo_half, g2[ra:ra + 1, :], g2[rb:rb + 1, :])
            blk = jnp.where(lo_half, ia, ib)
            in_blk = row_blk == blk
            key_ok = (row_blk >= blk) if rev else (row_blk <= blk)
            a_parts.append(jnp.where(in_blk, q2 * jnp.exp(jnp.minimum(g2 - ref2, 0.0)), 0.0))
            b_parts.append(
                jnp.where(key_ok, k2 * jnp.exp(jnp.minimum(ref2 - g2, GLA_EXP_CLAMP)), 0.0))
        a_big = jnp.concatenate(a_parts, axis=1).astype(BF16)
        b_big = jnp.concatenate(b_parts, axis=1).astype(BF16)
        att = jnp.where(mask, _dot_nt(a_big, b_big), 0.0)
        oa_ref[d, rows, :] = _dot(att.astype(BF16), v)
        gl_row = 0 if rev else L - 1
        glast = g2[gl_row:gl_row + 1, :]
        kd = (k2 * jnp.exp(glast - g2))[:, :DK_G]
        u_ref[d, c] = _dot_tn(kd.astype(BF16), v)
        glast_col = jnp.sum(jnp.where(eye, glast, 0.0), axis=1, keepdims=True)
        dec_ref[d, c] = jnp.broadcast_to(jnp.exp(glast_col), (DK_G, DV_G))


def _gla_kernel(qk_ref, v_ref, sm_ref, wup_ref, bup_ref, s0_ref, gr_ref, hn_ref,
                y_ref, s_out_ref, la_ref, q2_ref, k2_ref, oa_ref, oi_ref, qt_ref, u_ref, dec_ref,
                *, L, S):
    nch = S // L
    sm = sm_ref[...].astype(BF16)
    for d in range(2):
        la_ref[d] = _log_sigmoid(_dot(sm, wup_ref[d]) + bup_ref[d]) * (1.0 / GLA_TAU)
    qk = qk_ref[...]
    qk_sw = pltpu.roll(qk, DK_G, 1)
    lo_half = lax.broadcasted_iota(jnp.int32, qk.shape, 1) < DK_G
    q2_ref[...] = jnp.where(lo_half, qk, qk_sw) * (DK_G ** -0.5)
    k2_ref[...] = jnp.where(lo_half, qk_sw, qk)

    def local(ci, carry):
        _gla_local(ci, q2_ref, k2_ref, v_ref, la_ref, oa_ref, qt_ref, u_ref, dec_ref, L)
        return carry

    lax.fori_loop(0, nch, local, 0, unroll=4)

    def body(ci, carry):
        out = []
        for d, c in ((0, ci), (1, nch - 1 - ci)):
            rows = pl.ds(pl.multiple_of(c * L, L), L)
            st = carry[d]
            oi_ref[d, rows, :] = _dot(qt_ref[d, rows, :], st.astype(BF16))
            out.append(dec_ref[d, c] * st + u_ref[d, c])
        return tuple(out)

    st_f, st_b = lax.fori_loop(0, nch, body, (s0_ref[0], s0_ref[1]), unroll=2)
    s_out_ref[0] = st_f
    s_out_ref[1] = st_b

    og = (oa_ref[0] + oi_ref[0]) + (oa_ref[1] + oi_ref[1])
    y = og * lax.rsqrt(jnp.mean(og * og, axis=-1, keepdims=True) + EPS) * hn_ref[...]
    gr = gr_ref[...].astype(F32)
    y_ref[...] = (y * (gr * _sigmoid(gr))).astype(y_ref.dtype)


def _gla(a32, a16, wup, bup, s0, hnorm, B, S):
    L = min(GLA_CHUNK, S)
    nch = S // L
    cb = lambda off: off // LANES
    kern = functools.partial(_gla_kernel, L=L, S=S)
    return pl.pallas_call(
        kern,
        grid=(B, H_G),
        in_specs=[
            pl.BlockSpec((S, LANES), lambda b, h: (b, cb(A32_GQK) + h)),
            pl.BlockSpec((S, LANES), lambda b, h: (b, cb(A32_GV) + h)),
            pl.BlockSpec((S, LANES), lambda b, h: (b, cb(A32_SM))),
            pl.BlockSpec((None, 2, LANES, LANES), lambda b, h: (h, 0, 0, 0)),
            pl.BlockSpec((None, 2, 1, LANES), lambda b, h: (h, 0, 0, 0)),
            pl.BlockSpec((None, 2, None, DK_G, DV_G), lambda b, h: (b, 0, h, 0, 0)),
            pl.BlockSpec((S, LANES), lambda b, h: (b, cb(A16_GR) + h)),
            pl.BlockSpec((1, LANES), lambda b, h: (0, h)),
        ],
        out_specs=[
            pl.BlockSpec((S, LANES), lambda b, h: (b, h)),
            pl.BlockSpec((None, 2, None, DK_G, DV_G), lambda b, h: (b, 0, h, 0, 0)),
        ],
        out_shape=[
            jax.ShapeDtypeStruct((B * S, BRANCH_W), BF16),
            jax.ShapeDtypeStruct((B, 2, H_G, DK_G, DV_G), F32),
        ],
        scratch_shapes=[pltpu.VMEM((2, S, LANES), F32), pltpu.VMEM((S, LANES), F32),
                        pltpu.VMEM((S, LANES), F32), pltpu.VMEM((2, S, DV_G), F32),
                        pltpu.VMEM((2, S, DV_G), F32), pltpu.VMEM((2, S, DK_G), BF16),
                        pltpu.VMEM((2, nch, DK_G, DV_G), F32), pltpu.VMEM((2, nch, DK_G, DV_G), F32)],
        compiler_params=_cparams("parallel", "parallel"),
        name="gla",
    )(a32, a32, a32, wup, bup, s0, a16, hnorm)


def _rope(x, cos, sin_signed):
    lane = lax.broadcasted_iota(jnp.int32, x.shape, 1)
    first = (lane % DQK_D) < (DQK_D // 2)
    partner = jnp.where(first, pltpu.roll(x, LANES - DQK_D // 2, 1), pltpu.roll(x, DQK_D // 2, 1))
    return x * cos + partner * sin_signed


def _attn_kernel(*refs, S, P, TQ, lam_init, has_ctx):
    if has_ctx:
        (q_ref, k_ref, v_ref, ck_ref, cv_ref, cos_ref, sin_ref, lam_ref, hn_ref,
         y_ref, kk_ref, vv_ref) = refs
    else:
        q_ref, k_ref, v_ref, lam_ref, hn_ref, y_ref, kk_ref, vv_ref = refs
    qi = pl.program_id(2)

    @pl.when(qi == 0)
    def _():
        k = k_ref[...]
        if has_ctx:
            k = _rope(k, cos_ref[...], sin_ref[...])
            kk_ref[S:S + P, :] = ck_ref[...].astype(BF16)
            vv_ref[S:S + P, :] = cv_ref[...].astype(BF16)
        kk_ref[0:S, :] = k.astype(BF16)
        vv_ref[0:S, :] = v_ref[...].astype(BF16)

    q = q_ref[...]
    if has_ctx:
        r0 = pl.multiple_of(qi * TQ, TQ)
        q = _rope(q, cos_ref[pl.ds(r0, TQ), :], sin_ref[pl.ds(r0, TQ), :])
    q = q * (DQK_D ** -0.5 * math.log2(math.e))
    lane = lax.broadcasted_iota(jnp.int32, q.shape, 1)
    kk = kk_ref[...]
    vv = vv_ref[...]
    lv = lam_ref[...]
    lam = (jnp.exp(jnp.sum(lv[0:1, :] * lv[1:2, :], axis=-1, keepdims=True))
           - jnp.exp(jnp.sum(lv[2:3, :] * lv[3:4, :], axis=-1, keepdims=True)) + lam_init)
    es, ls = [], []
    for comp in range(2):
        sel = (lane < DQK_D) if comp == 0 else (lane >= DQK_D)
        s = _dot_nt(jnp.where(sel, q, 0.0).astype(BF16), kk)
        e = jnp.exp2(s - jnp.max(s, axis=-1, keepdims=True))
        es.append(e)
        ls.append(jnp.sum(e, axis=-1, keepdims=True))
    w = es[0] - es[1] * (lam * ls[0] / ls[1])
    o = _dot(w.astype(BF16), vv) * (1.0 / ls[0])
    y = o * lax.rsqrt(jnp.mean(o * o, axis=-1, keepdims=True) + EPS) * hn_ref[...]
    y_ref[...] = (y * (1.0 - lam_init)).astype(y_ref.dtype)


def _attn(a32, lamv, hnorm, B, S, lam_init, ctx=None):
    TQ = min(256, S)
    nq = S // TQ
    has_ctx = ctx is not None
    P = ctx[0].shape[2] if has_ctx else 0
    cb = lambda off: off // LANES
    kern = functools.partial(_attn_kernel, S=S, P=P, TQ=TQ, lam_init=lam_init, has_ctx=has_ctx)
    in_specs = [
        pl.BlockSpec((TQ, LANES), lambda b, h, i: (b * nq + i, cb(A32_DQ) + h)),
        pl.BlockSpec((S, LANES), lambda b, h, i: (b, cb(A32_DK) + h)),
        pl.BlockSpec((S, LANES), lambda b, h, i: (b, cb(A32_DV) + h)),
    ]
    args = [a32, a32, a32]
    if has_ctx:
        ck, cv, layer, cos, sin = ctx
        in_specs += [
            pl.BlockSpec((None, None, P, LANES), lambda b, h, i: (b, layer, 0, h)),
            pl.BlockSpec((None, None, P, LANES), lambda b, h, i: (b, layer, 0, h)),
            pl.BlockSpec((S, LANES), lambda b, h, i: (0, 0)),
            pl.BlockSpec((S, LANES), lambda b, h, i: (0, 0)),
        ]
        args += [ck, cv, cos, sin]
    in_specs += [
        pl.BlockSpec((4, DQK_D), lambda b, h, i: (0, 0)),
        pl.BlockSpec((1, LANES), lambda b, h, i: (0, h)),
    ]
    args += [lamv, hnorm]
    return pl.pallas_call(
        kern,
        grid=(B, H_D, nq),
        in_specs=in_specs,
        out_specs=pl.BlockSpec((TQ, LANES), lambda b, h, i: (b * nq + i, h)),
        out_shape=jax.ShapeDtypeStruct((B * S, BRANCH_W), BF16),
        scratch_shapes=[pltpu.VMEM((S + P, LANES), BF16), pltpu.VMEM((S + P, LANES), BF16)],
        compiler_params=_cparams("parallel", "parallel", "arbitrary"),
        name="diff_attn",
    )(*args)


def _conv_kernel(ca_ref, cb_ref, w_ref, g_ref, b_ref, y_ref, pad_ref, cv_ref, *, S):
    ca = ca_ref[...].astype(F32)
    cbv = cb_ref[...].astype(F32)
    zeros = jnp.zeros((CONV_PAD, BRANCH_W), F32)
    pad_ref[0:CONV_PAD, :] = zeros
    pad_ref[CONV_PAD + S:2 * CONV_PAD + S, :] = zeros
    pad_ref[CONV_PAD:CONV_PAD + S, :] = ca * _sigmoid(cbv)
    off = CONV_PAD - CONV_W // 2

    win_rows = CONV_ROWS + 2 * CONV_PAD

    def body(i, carry):
        base = pl.multiple_of(i * CONV_ROWS, CONV_ROWS)
        for lb in range(BRANCH_W // LANES):
            cols = slice(lb * LANES, (lb + 1) * LANES)
            win = pad_ref[pl.ds(base, win_rows), cols]
            acc = jnp.zeros((CONV_ROWS, LANES), F32)
            for r in range(8):
                rolled = win if r == 0 else pltpu.roll(win, win_rows - r, 0)
                for a in range(2 * CONV_PAD // 8):
                    j = 8 * a + r - off
                    if 0 <= j < CONV_W:
                        acc = acc + rolled[8 * a:8 * a + CONV_ROWS, :] * w_ref[j:j + 1, cols]
            cv_ref[:, cols] = acc
        acc = cv_ref[...]
        mu = jnp.mean(acc, axis=-1, keepdims=True)
        xc = acc - mu
        yn = xc * lax.rsqrt(jnp.mean(xc * xc, axis=-1, keepdims=True) + EPS) * g_ref[...] + b_ref[...]
        y_ref[pl.ds(base, CONV_ROWS), :] = (yn * _sigmoid(yn)).astype(y_ref.dtype)
        return carry

    lax.fori_loop(0, S // CONV_ROWS, body, 0)


def _conv(a16, w_dw, ln_g, ln_b, B, S):
    cb = lambda off: off // BRANCH_W
    kern = functools.partial(_conv_kernel, S=S)
    return pl.pallas_call(
        kern,
        grid=(B,),
        in_specs=[
            pl.BlockSpec((S, BRANCH_W), lambda b: (b, cb(A16_CA))),
            pl.BlockSpec((S, BRANCH_W), lambda b: (b, cb(A16_CB))),
            pl.BlockSpec((CONV_W + 1, BRANCH_W), lambda b: (0, 0)),
            pl.BlockSpec((1, BRANCH_W), lambda b: (0, 0)),
            pl.BlockSpec((1, BRANCH_W), lambda b: (0, 0)),
        ],
        out_specs=pl.BlockSpec((S, BRANCH_W), lambda b: (b, 0)),
        out_shape=jax.ShapeDtypeStruct((B * S, BRANCH_W), BF16),
        scratch_shapes=[pltpu.VMEM((S + 2 * CONV_PAD, BRANCH_W), F32),
                        pltpu.VMEM((CONV_ROWS, BRANCH_W), F32)],
        compiler_params=_cparams("parallel"),
        name="glu_conv_ln",
    )(a16, a16, w_dw, ln_g, ln_b)


def _merge_kernel(x_ref, mod_ref, ym_ref, yd_ref, yg_ref, yc_ref, g0_ref, g1_ref, g2_ref, g3_ref,
                  wb_ref, wo_ref, n2_ref, wr_ref, x1_ref, h2_ref, route_ref):
    ys = (ym_ref, yd_ref, yg_ref, yc_ref)
    gs = (g0_ref, g1_ref, g2_ref, g3_ref)
    merged = None
    for nbr in range(N_BRANCH):
        br = _dot(ys[nbr][...], wb_ref[nbr])
        term = _sigmoid(gs[nbr][...].astype(F32)) * br
        merged = term if merged is None else merged + term
    out = _dot(merged.astype(BF16), wo_ref[...])
    x1 = x_ref[...] + mod_ref[2:3, :] * out
    x1_ref[...] = x1
    y = x1 * lax.rsqrt(jnp.mean(x1 * x1, axis=-1, keepdims=True) + EPS) * n2_ref[...]
    h2 = y * (1.0 + mod_ref[4:5, :]) + mod_ref[3:4, :]
    h2_ref[...] = h2.reshape(h2_ref.shape)
    wr = wr_ref[...]
    h_hi, w_hi = h2.astype(BF16), wr.astype(BF16)
    h_lo = (h2 - h_hi.astype(F32)).astype(BF16)
    w_lo = (wr - w_hi.astype(F32)).astype(BF16)
    logits = _dot(h_hi, w_hi) + (_dot(h_hi, w_lo) + _dot(h_lo, w_hi))
    lane = lax.broadcasted_iota(jnp.int32, logits.shape, 1)
    neg = -jnp.inf
    big = jnp.int32(LANES)
    is_g = lane < N_GROUPS
    gl = jnp.where(is_g, logits, neg)
    gmax = jnp.max(gl, axis=-1, keepdims=True)
    gidx = jnp.min(jnp.where(is_g & (gl == gmax), lane, big), axis=-1, keepdims=True)
    g_p = 1.0 / jnp.sum(jnp.where(is_g, jnp.exp(gl - gmax), 0.0), axis=-1, keepdims=True)
    e_lane = lane - N_GROUPS
    in_grp = (e_lane >= 0) & (e_lane < N_EXPERTS) & ((e_lane // EXPERTS_PER_GROUP) == gidx)
    el = jnp.where(in_grp, logits, neg)
    v1 = jnp.max(el, axis=-1, keepdims=True)
    i1 = jnp.min(jnp.where(in_grp & (el == v1), lane, big), axis=-1, keepdims=True)
    el2 = jnp.where(lane == i1, neg, el)
    v2 = jnp.max(el2, axis=-1, keepdims=True)
    i2 = jnp.min(jnp.where(in_grp & (lane != i1) & (el2 == v2), lane, big), axis=-1, keepdims=True)
    e2 = jnp.exp(v2 - v1)
    w1 = g_p / (1.0 + e2)
    w2 = g_p * e2 / (1.0 + e2)
    id1 = (i1 - N_GROUPS).astype(F32)
    id2 = (i2 - N_GROUPS).astype(F32)
    route_ref[...] = jnp.where(lane == 0, id1, jnp.where(lane == 1, id2,
                               jnp.where(lane == 2, w1, jnp.where(lane == 3, w2, 0.0))))


def _merge(x2d, mod, a16, ym, yd, yg, yc, wb, wo, n2, wr, rows_per_mod, tm):
    T = x2d.shape[0]
    gcb = A16_GATE // D_MODEL
    row = lambda i: (i, 0)
    return pl.pallas_call(
        _merge_kernel,
        grid=(T // tm,),
        in_specs=[
            pl.BlockSpec((tm, D_MODEL), row),
            pl.BlockSpec((None, N_MOD, D_MODEL), lambda i: ((i * tm) // rows_per_mod, 0, 0)),
            pl.BlockSpec((tm, BRANCH_W), row),
            pl.BlockSpec((tm, BRANCH_W), row),
            pl.BlockSpec((tm, BRANCH_W), row),
            pl.BlockSpec((tm, BRANCH_W), row),
            pl.BlockSpec((tm, D_MODEL), lambda i: (i, gcb + 0)),
            pl.BlockSpec((tm, D_MODEL), lambda i: (i, gcb + 1)),
            pl.BlockSpec((tm, D_MODEL), lambda i: (i, gcb + 2)),
            pl.BlockSpec((tm, D_MODEL), lambda i: (i, gcb + 3)),
            pl.BlockSpec((N_BRANCH, BRANCH_W, D_MODEL), lambda i: (0, 0, 0)),
            pl.BlockSpec((D_MODEL, D_MODEL), lambda i: (0, 0)),
            pl.BlockSpec((1, D_MODEL), lambda i: (0, 0)),
            pl.BlockSpec((D_MODEL, LANES), lambda i: (0, 0)),
        ],
        out_specs=[
            pl.BlockSpec((tm, D_MODEL), row),
            pl.BlockSpec((tm, TOK_SUB, LANES), lambda i: (i, 0, 0)),
            pl.BlockSpec((tm, LANES), row),
        ],
        out_shape=[
            jax.ShapeDtypeStruct((T, D_MODEL), F32),
            jax.ShapeDtypeStruct((T, TOK_SUB, LANES), F32),
            jax.ShapeDtypeStruct((T, LANES), F32),
        ],
        compiler_params=_cparams("parallel"),
        name="merge_outproj_route",
    )(x2d, mod, ym, yd, yg, yc, a16, a16, a16, a16, wb, wo, n2, wr)


def _gather_rows(idx_ref, src_hbm, dst, sem, n):
    def body(j, carry):
        for u in range(ROW_DMA_UNROLL):
            r = j * ROW_DMA_UNROLL + u
            pltpu.make_async_copy(src_hbm.at[idx_ref[0, r]], dst.at[r], sem).start(priority=u % 2)
        return carry

    lax.fori_loop(0, n // ROW_DMA_UNROLL, body, 0)


def _scatter_rows(idx_ref, src, dst_hbm, sem, n):
    def body(j, carry):
        for u in range(ROW_DMA_UNROLL):
            r = j * ROW_DMA_UNROLL + u
            pltpu.make_async_copy(src.at[r], dst_hbm.at[idx_ref[0, r]], sem).start(priority=u % 2)
        return carry

    lax.fori_loop(0, n // ROW_DMA_UNROLL, body, 0)


def _wait_rows(buf, sem):
    pltpu.make_async_copy(buf, buf, sem).wait()


def _moe_dispatch_kernel(d0_ref, d1_ref, h_ref, xg_in, xg_out, sem, *, tm):
    del xg_in
    _scatter_rows(d0_ref, h_ref, xg_out, sem.at[0], tm)
    _scatter_rows(d1_ref, h_ref, xg_out, sem.at[1], tm)
    _wait_rows(h_ref, sem.at[0])
    _wait_rows(h_ref, sem.at[1])


def _moe_dispatch(h3, dest, n_rows, tm):
    T = h3.shape[0]
    kern = functools.partial(_moe_dispatch_kernel, tm=tm)
    return pl.pallas_call(
        kern,
        grid=(T // tm,),
        in_specs=[
            pl.BlockSpec((None, None, 1, tm), lambda i: (0, i, 0, 0), memory_space=pltpu.SMEM),
            pl.BlockSpec((None, None, 1, tm), lambda i: (1, i, 0, 0), memory_space=pltpu.SMEM),
            pl.BlockSpec((tm, TOK_SUB, LANES), lambda i: (i, 0, 0)),
            pl.BlockSpec(memory_space=pl.ANY),
        ],
        out_specs=pl.BlockSpec(memory_space=pl.ANY),
        out_shape=jax.ShapeDtypeStruct((n_rows, TOK_SUB, LANES), F32),
        input_output_aliases={3: 0},
        scratch_shapes=[pltpu.SemaphoreType.DMA((2,))],
        compiler_params=_cparams("arbitrary"),
        name="moe_dispatch",
    )(dest, dest, h3, jnp.zeros((n_rows, TOK_SUB, LANES), F32))


def _moe_ffn_kernel(te_ref, x_ref, w1_ref, w3_ref, w2_ref, o_ref):
    del te_ref
    x = x_ref[...].reshape(MOE_TILE, D_MODEL).astype(BF16)
    a = _dot(x, w1_ref[...].astype(BF16))
    b = _dot(x, w3_ref[...].astype(BF16))
    s = (a * _sigmoid(a)) * b
    y = _dot(s.astype(BF16), w2_ref[...].astype(BF16))
    o_ref[...] = y.reshape(o_ref.shape)


def _moe_ffn(xg, tile_e, w1, w3, w2):
    ntiles = xg.shape[0] // MOE_TILE
    tile = pl.BlockSpec((MOE_TILE, TOK_SUB, LANES), lambda i, te: (i, 0, 0))
    return pl.pallas_call(
        _moe_ffn_kernel,
        grid_spec=pltpu.PrefetchScalarGridSpec(
            num_scalar_prefetch=1,
            grid=(ntiles,),
            in_specs=[
                tile,
                pl.BlockSpec((None, D_MODEL, D_EXPERT), lambda i, te: (te[i], 0, 0)),
                pl.BlockSpec((None, D_MODEL, D_EXPERT), lambda i, te: (te[i], 0, 0)),
                pl.BlockSpec((None, D_EXPERT, D_MODEL), lambda i, te: (te[i], 0, 0)),
            ],
            out_specs=tile,
        ),
        out_shape=jax.ShapeDtypeStruct(xg.shape, F32),
        compiler_params=_cparams("parallel"),
        name="moe_grouped_experts",
    )(tile_e, xg, w1, w3, w2)


def _moe_combine_kernel(d0_ref, d1_ref, y_hbm, route_ref, x1_ref, mod_ref, fn_ref, o_ref,
                        ga, gb, sem, *, tm, final_norm):
    _gather_rows(d0_ref, y_hbm, ga, sem.at[0], tm)
    _gather_rows(d1_ref, y_hbm, gb, sem.at[1], tm)
    rt = route_ref[...]
    _wait_rows(ga, sem.at[0])
    _wait_rows(gb, sem.at[1])
    y = rt[:, 2:3] * ga[...].reshape(tm, D_MODEL) + rt[:, 3:4] * gb[...].reshape(tm, D_MODEL)
    x2 = x1_ref[...] + mod_ref[5:6, :] * y
    if final_norm:
        x2 = x2 * lax.rsqrt(jnp.mean(x2 * x2, axis=-1, keepdims=True) + EPS) * fn_ref[...]
    o_ref[...] = x2


def _moe_combine(yg, dest, route, x1, mod, fn, rows_per_mod, tm, final_norm):
    T = x1.shape[0]
    kern = functools.partial(_moe_combine_kernel, tm=tm, final_norm=final_norm)
    return pl.pallas_call(
        kern,
        grid=(T // tm,),
        in_specs=[
            pl.BlockSpec((None, None, 1, tm), lambda i: (0, i, 0, 0), memory_space=pltpu.SMEM),
            pl.BlockSpec((None, None, 1, tm), lambda i: (1, i, 0, 0), memory_space=pltpu.SMEM),
            pl.BlockSpec(memory_space=pl.ANY),
            pl.BlockSpec((tm, LANES), lambda i: (i, 0)),
            pl.BlockSpec((tm, D_MODEL), lambda i: (i, 0)),
            pl.BlockSpec((None, N_MOD, D_MODEL), lambda i: ((i * tm) // rows_per_mod, 0, 0)),
            pl.BlockSpec((1, D_MODEL), lambda i: (0, 0)),
        ],
        out_specs=pl.BlockSpec((tm, D_MODEL), lambda i: (i, 0)),
        out_shape=jax.ShapeDtypeStruct((T, D_MODEL), F32),
        scratch_shapes=[pltpu.VMEM((tm, TOK_SUB, LANES), F32), pltpu.VMEM((tm, TOK_SUB, LANES), F32),
                        pltpu.SemaphoreType.DMA((2,))],
        compiler_params=_cparams("arbitrary"),
        name="moe_combine",
    )(dest, dest, yg, route, x1, mod, fn)


def _route_plan(route, T):
    ntiles = (2 * T + N_EXPERTS * (MOE_TILE - 1) + MOE_TILE - 1) // MOE_TILE
    ef = route[:, 0:2].astype(jnp.int32).reshape(-1)
    oh = (ef[:, None] == jnp.arange(N_EXPERTS, dtype=jnp.int32)[None, :]).astype(jnp.int32)
    csum = jnp.cumsum(oh, axis=0)
    rank = jnp.sum((csum - oh) * oh, axis=1)
    counts = csum[-1]
    padded = ((counts + MOE_TILE - 1) // MOE_TILE) * MOE_TILE
    seg_end = jnp.cumsum(padded)
    dest = jnp.sum(oh * (seg_end - padded)[None, :], axis=1) + rank
    tile_row = jnp.arange(ntiles, dtype=jnp.int32) * MOE_TILE
    tile_e = jnp.minimum(jnp.sum((tile_row[:, None] >= seg_end[None, :]).astype(jnp.int32), axis=1),
                         N_EXPERTS - 1)
    return jnp.transpose(dest.reshape(T, 2)), tile_e, ntiles * MOE_TILE


def _split_w_in(w):
    sizes = (H_M * DK_M, H_M * DK_M, H_M * DV_M, H_M * DV_M, 2 * H_M, 2 * H_M,
             H_D * 2 * DQK_D, H_D * 2 * DQK_D, H_D * DV_D,
             H_G * DK_G, H_G * DK_G, H_G * DV_G, 2 * GATE_RANK, H_G * DV_G,
             BRANCH_W, BRANCH_W, N_BRANCH * D_MODEL)
    outs, acc = [], 0
    for s in sizes:
        outs.append(w[:, acc:acc + s])
        acc += s
    return outs


def _pack_layer_params(p):
    (m_q, m_k, m_v, m_o, m_i, m_f, d_q, d_k, d_v, g_q, g_k, g_v, g_a, g_r, c_a, c_b, gate) = \
        _split_w_in(p['w_in'])
    w16 = jnp.concatenate([m_q, m_k, m_v, m_o, gate, g_r, c_a, c_b], axis=1).astype(BF16)
    gqk = jnp.concatenate([g_q.reshape(D_MODEL, H_G, DK_G), g_k.reshape(D_MODEL, H_G, DK_G)],
                          axis=2).reshape(D_MODEL, 2 * H_G * DK_G)
    small = jnp.concatenate(
        [m_i, m_f, g_a, jnp.zeros((D_MODEL, LANES - 4 * H_M - 2 * GATE_RANK), F32)], axis=1)
    w32 = jnp.concatenate([d_q, d_k, d_v, gqk, g_v, small], axis=1).astype(BF16)
    bi = p['b_m_i'].reshape(2, H_M)
    bf = p['b_m_f'].reshape(2, H_M)
    bcol = jnp.stack([bi[0], bi[1], bf[0], bf[1]], axis=-1)
    wup = p['w_gla_up'].reshape(2, GATE_RANK, H_G, DK_G)
    wup_pad = jnp.zeros((H_G, 2, LANES, LANES), F32)
    bup = p['b_gla_gate'].reshape(2, H_G, DK_G)
    for d in range(2):
        blk = jnp.transpose(wup[d], (1, 0, 2))
        blk = jnp.concatenate([blk, blk], axis=-1)
        r0 = SM_GA + d * GATE_RANK
        wup_pad = wup_pad.at[:, d, r0:r0 + GATE_RANK, :].set(blk)
    bup2 = jnp.transpose(jnp.concatenate([bup, bup], axis=-1), (1, 0, 2))[:, :, None, :]
    wr = jnp.concatenate([p['w_group_router'], p['w_expert_router'],
                          jnp.zeros((D_MODEL, LANES - N_GROUPS - N_EXPERTS), F32)], axis=1)
    return dict(
        w16=w16, w32=w32, brow=bcol.reshape(H_M, 4, 1),
        wup=wup_pad.astype(BF16), bup=bup2,
        wdw=jnp.concatenate([p['w_dw'], jnp.zeros((1, BRANCH_W), F32)], axis=0),
        ln_g=p['conv_ln_g'].reshape(1, BRANCH_W), ln_b=p['conv_ln_b'].reshape(1, BRANCH_W),
        hn_m=p['hnorm_m'].reshape(1, BRANCH_W), hn_d=p['hnorm_d'].reshape(1, BRANCH_W),
        hn_g=p['hnorm_g'].reshape(1, BRANCH_W),
        lamv=jnp.stack([p['lam_q1'], p['lam_k1'], p['lam_q2'], p['lam_k2']], axis=0),
        wb=p['w_branch'].astype(BF16), wo=p['w_out'].astype(BF16),
        n1=p['norm1'].reshape(1, D_MODEL), n2=p['norm2'].reshape(1, D_MODEL), wr=wr,
        w1=p['w_e1'].reshape(N_EXPERTS, D_MODEL, D_EXPERT),
        w3=p['w_e3'].reshape(N_EXPERTS, D_MODEL, D_EXPERT),
        w2=p['w_e2'].reshape(N_EXPERTS, D_EXPERT, D_MODEL),
    )


def _rope_tables(S):
    rows = S // GRID_W
    r, col = jnp.meshgrid(jnp.arange(rows, dtype=F32), jnp.arange(GRID_W, dtype=F32), indexing='ij')
    r, col = r.reshape(-1), col.reshape(-1)
    n_freq = DQK_D // 4
    inv = ROPE_BASE ** (-jnp.arange(n_freq, dtype=F32) / n_freq)
    ang = jnp.concatenate([r[:, None] * inv, col[:, None] * inv], axis=-1)
    cos, sin = jnp.cos(ang), jnp.sin(ang)
    cos_t = jnp.tile(cos, (1, LANES // (DQK_D // 2)))
    sin_t = jnp.tile(jnp.concatenate([-sin, sin], axis=-1), (1, LANES // DQK_D))
    return cos_t, sin_t


def _pick_tile(T, cap):
    t = min(T, cap)
    while T % t:
        t //= 2
    return t


def _layer(x2d, mod, pk, B, S, lam_init, ctx, final_norm, fn):
    T = B * S
    rows_per_mod = T // mod.shape[0]
    tm = _pick_tile(rows_per_mod, 1024)
    a16 = _inproj(x2d, mod, pk['n1'], pk['w16'], BF16, rows_per_mod, tm, 1536)
    a32 = _inproj(x2d, mod, pk['n1'], pk['w32'], F32, rows_per_mod, tm, 896)

    L = min(MLSTM_CHUNK, S)
    sm = a32[:, A32_SM:A32_SM + 4 * H_M]
    grow = jnp.transpose(sm.reshape(T // L, L, 4, H_M), (3, 0, 2, 1))
    if ctx is None:
        c0 = jnp.zeros((B, 2, H_M, DK_M, DV_M), F32)
        n0 = jnp.zeros((B, 2, H_M, 1, DK_M), F32)
        m0 = jnp.zeros((B, 2, H_M, 1, LANES), F32)
        s0 = jnp.zeros((B, 2, H_G, DK_G, DV_G), F32)
        attn_ctx = None
    else:
        c0 = ctx['C']
        n0 = ctx['n'][:, :, :, None, :]
        m0 = jnp.broadcast_to(ctx['m'][:, :, :, None, None], (B, 2, H_M, 1, LANES))
        s0 = ctx['S']
        attn_ctx = (ctx['k'], ctx['v'], ctx['layer'], ctx['cos'], ctx['sin'])
    ym, c_f, n_f, m_f = _mlstm(a16, grow, pk['brow'], c0, n0, m0, pk['hn_m'], B, S)
    yd = _attn(a32, pk['lamv'], pk['hn_d'], B, S, lam_init, attn_ctx)
    yg, s_f = _gla(a32, a16, pk['wup'], pk['bup'], s0, pk['hn_g'], B, S)
    yc = _conv(a16, pk['wdw'], pk['ln_g'], pk['ln_b'], B, S)
    x1, h3, route = _merge(x2d, mod, a16, ym, yd, yg, yc, pk['wb'], pk['wo'], pk['n2'], pk['wr'],
                           rows_per_mod, _pick_tile(rows_per_mod, 512))
    dest, tile_e, n_rows = _route_plan(route, T)
    tmd = _pick_tile(T, 512)
    xg = _moe_dispatch(h3, dest.reshape(2, T // tmd, 1, tmd), n_rows, tmd)
    y_grouped = _moe_ffn(xg, tile_e, pk['w1'], pk['w3'], pk['w2'])
    tmc = _pick_tile(rows_per_mod, 256)
    x2 = _moe_combine(y_grouped, dest.reshape(2, T // tmc, 1, tmc), route, x1, mod, fn,
                      rows_per_mod, tmc, final_norm)
    state = None
    if ctx is None:
        state = (a32[:, A32_DK:A32_DK + H_D * 2 * DQK_D].reshape(B, S, H_D, 2 * DQK_D),
                 a32[:, A32_DV:A32_DV + H_D * DV_D].reshape(B, S, H_D, DV_D),
                 c_f, n_f[:, :, :, 0, :], m_f[:, :, :, 0, 0], s_f)
    return x2, state


def kernel(x_prompt, x_sample, c, cache_diff_k, cache_diff_v, state_mlstm_C, state_mlstm_n, state_mlstm_m, state_gla_S, c_ctx, w_mod, b_mod, norm1, w_in, b_m_i, b_m_f, lam_q1, lam_k1, lam_q2, lam_k2, w_gla_up, b_gla_gate, w_dw, conv_ln_g, conv_ln_b, hnorm_m, hnorm_d, hnorm_g, w_branch, w_out, norm2, w_group_router, w_expert_router, w_e1, w_e3, w_e2, final_norm):
    Bp, Sp, _ = x_prompt.shape
    Bs, Ss, _ = x_sample.shape
    P = cache_diff_k.shape[2]
    n_cond = 8 * ((1 + Bs + 7) // 8)
    cond = jnp.concatenate([c_ctx[None, :], c, jnp.zeros((n_cond - 1 - Bs, D_MODEL), F32)], axis=0)
    mod_all = _modulation(cond, w_mod, b_mod).reshape(DEPTH, n_cond, N_MOD, D_MODEL)
    cos_t, sin_t = _rope_tables(Ss)
    ck4 = cache_diff_k.reshape(Bs, DEPTH, P, H_D * 2 * DQK_D)
    cv4 = cache_diff_v.reshape(Bs, DEPTH, P, H_D * DV_D)
    fn = final_norm.reshape(1, D_MODEL)
    yp = x_prompt.reshape(Bp * Sp, D_MODEL)
    ys = x_sample.reshape(Bs * Ss, D_MODEL)
    states = []
    for l in range(DEPTH):
        p = {'w_in': w_in[l], 'b_m_i': b_m_i[l], 'b_m_f': b_m_f[l], 'lam_q1': lam_q1[l],
             'lam_k1': lam_k1[l], 'lam_q2': lam_q2[l], 'lam_k2': lam_k2[l],
             'w_gla_up': w_gla_up[l], 'b_gla_gate': b_gla_gate[l], 'w_dw': w_dw[l],
             'conv_ln_g': conv_ln_g[l], 'conv_ln_b': conv_ln_b[l], 'hnorm_m': hnorm_m[l],
             'hnorm_d': hnorm_d[l], 'hnorm_g': hnorm_g[l], 'w_branch': w_branch[l],
             'w_out': w_out[l], 'norm1': norm1[l], 'norm2': norm2[l],
             'w_group_router': w_group_router[l], 'w_expert_router': w_expert_router[l],
             'w_e1': w_e1[l], 'w_e3': w_e3[l], 'w_e2': w_e2[l]}
        pk = _pack_layer_params(p)
        lam_init = 0.8 - 0.6 * math.exp(-0.3 * l)
        last = l == DEPTH - 1
        yp, st = _layer(yp, mod_all[l, 0:1], pk, Bp, Sp, lam_init, None, last, fn)
        states.append(st)
        ctx = {'k': ck4, 'v': cv4, 'layer': l, 'cos': cos_t, 'sin': sin_t,
               'C': state_mlstm_C[:, l], 'n': state_mlstm_n[:, l], 'm': state_mlstm_m[:, l],
               'S': state_gla_S[:, l]}
        ys, _ = _layer(ys, mod_all[l, 1:1 + Bs], pk, Bs, Ss, lam_init, ctx, last, fn)
    stack = lambda i: jnp.stack([s[i] for s in states], axis=1)
    return (yp.reshape(Bp, Sp, D_MODEL), ys.reshape(Bs, Ss, D_MODEL),
            stack(0), stack(1), stack(2), stack(3), stack(4), stack(5))
```

```python
import functools
import math

import jax
import jax.numpy as jnp
from jax import lax
from jax.experimental import pallas as pl
from jax.experimental.pallas import tpu as pltpu

F32 = jnp.float32
BF16 = jnp.bfloat16

D_MODEL = 1024
DEPTH = 2
GRID_W = 64
BRANCH_W = 512
N_BRANCH = 4
H_M, DK_M, DV_M = 4, 128, 128
H_D, DQK_D, DV_D = 4, 64, 128
H_G, DK_G, DV_G = 4, 64, 128
GATE_RANK = 16
GLA_TAU = 16.0
CONV_W = 31
N_GROUPS, EXPERTS_PER_GROUP, D_EXPERT = 4, 4, 512
N_EXPERTS = N_GROUPS * EXPERTS_PER_GROUP
ROPE_BASE = 10000.0
EPS = 1e-6
N_MOD = 6

LANES = 128
VMEM_LIMIT = 48 * 1024 * 1024

A16_MQ, A16_MK, A16_MV, A16_MO = 0, 512, 1024, 1536
A16_GATE, A16_GR, A16_CA, A16_CB = 2048, 6144, 6656, 7168
A16_GQK, A16_GV = 7680, 8192
N_A16 = 8704
A32_DQ, A32_DK, A32_DV, A32_SM = 0, 512, 1024, 1536
N_A32 = 1664
SM_MI, SM_MF, SM_GA = 0, 8, 16

MLSTM_CHUNK = 128
GLA_CHUNK = 64
GLA_SUB = 16
GLA_EXP_CLAMP = 80.0
CONV_ROWS = 64
CONV_PAD = 16
TOK_SUB = D_MODEL // LANES
MOE_TILE = 256
ROW_DMA_UNROLL = 8


def _cparams(*sem):
    return pltpu.CompilerParams(dimension_semantics=sem, vmem_limit_bytes=VMEM_LIMIT)


def _log_sigmoid(x):
    return jnp.minimum(x, 0.0) - jnp.log1p(jnp.exp(-jnp.abs(x)))


def _sigmoid(x):
    return 1.0 / (1.0 + jnp.exp(-x))


def _dot(a, b):
    return jnp.dot(a, b, preferred_element_type=F32)


def _dot_nt(a, b):
    return lax.dot_general(a, b, (((1,), (1,)), ((), ())), preferred_element_type=F32)


def _dot_tn(a, b):
    return lax.dot_general(a, b, (((0,), (0,)), ((), ())), preferred_element_type=F32)


def _mod_kernel(c_ref, w_ref, b_ref, o_ref):
    c = c_ref[...]
    a = (c * _sigmoid(c)).astype(BF16)
    o_ref[...] = _dot(a, w_ref[...].astype(BF16)) + b_ref[...]


def _modulation(cond, w_mod, b_mod):
    R = cond.shape[0]
    tn = 512
    nmod = N_MOD * D_MODEL
    return pl.pallas_call(
        _mod_kernel,
        grid=(DEPTH, nmod // tn),
        in_specs=[
            pl.BlockSpec((R, D_MODEL), lambda l, j: (0, 0)),
            pl.BlockSpec((None, D_MODEL, tn), lambda l, j: (l, 0, j)),
            pl.BlockSpec((None, 1, tn), lambda l, j: (l, 0, j)),
        ],
        out_specs=pl.BlockSpec((None, R, tn), lambda l, j: (l, 0, j)),
        out_shape=jax.ShapeDtypeStruct((DEPTH, R, nmod), F32),
        compiler_params=_cparams("parallel", "parallel"),
        name="adaln_mod",
    )(cond, w_mod, b_mod.reshape(DEPTH, 1, nmod))


def _inproj_kernel(x_ref, mod_ref, g_ref, w_ref, o_ref, h_ref):
    @pl.when(pl.program_id(1) == 0)
    def _():
        x = x_ref[...]
        y = x * lax.rsqrt(jnp.mean(x * x, axis=-1, keepdims=True) + EPS) * g_ref[...]
        h_ref[...] = (y * (1.0 + mod_ref[1:2, :]) + mod_ref[0:1, :]).astype(BF16)

    o_ref[...] = _dot(h_ref[...], w_ref[...]).astype(o_ref.dtype)


def _inproj(x2d, mod, g, w, out_dtype, rows_per_mod, tm, tn):
    T = x2d.shape[0]
    N = w.shape[1]
    return pl.pallas_call(
        _inproj_kernel,
        grid=(T // tm, N // tn),
        in_specs=[
            pl.BlockSpec((tm, D_MODEL), lambda i, j: (i, 0)),
            pl.BlockSpec((None, N_MOD, D_MODEL), lambda i, j: ((i * tm) // rows_per_mod, 0, 0)),
            pl.BlockSpec((1, D_MODEL), lambda i, j: (0, 0)),
            pl.BlockSpec((D_MODEL, tn), lambda i, j: (0, j)),
        ],
        out_specs=pl.BlockSpec((tm, tn), lambda i, j: (i, j)),
        out_shape=jax.ShapeDtypeStruct((T, N), out_dtype),
        scratch_shapes=[pltpu.VMEM((tm, D_MODEL), BF16)],
        compiler_params=_cparams("parallel", "arbitrary"),
        name="norm_inproj",
    )(x2d, mod, g, w)


def _mlstm_local(c, q_ref, k_ref, v_ref, gr_ref, br_ref, pr_ref, bb_ref, mb_ref, kv_ref, rp_ref, L):
    scale = DK_M ** -0.5
    ti = lax.broadcasted_iota(jnp.int32, (L, L), 0)
    si = lax.broadcasted_iota(jnp.int32, (L, L), 1)
    sub = lax.broadcasted_iota(jnp.int32, (8, LANES), 0)
    rows = pl.ds(pl.multiple_of(c * L, L), L)
    q = q_ref[rows, :]
    v_ext = jnp.concatenate([v_ref[rows, :], jnp.ones((L, LANES), BF16)], axis=1)
    k_t = k_ref[rows, :].astype(F32).T
    qk = _dot(q, k_t.astype(BF16)) * scale
    grow = gr_ref[c] + br_ref[...]
    for d in range(2):
        rev = d == 1
        mask = (si >= ti) if rev else (si <= ti)
        src = ((si <= ti) if rev else (si >= ti)).astype(BF16)
        i_row = grow[d:d + 1, :]
        f_row = _log_sigmoid(grow[2 + d:3 + d, :])
        f8 = jnp.broadcast_to(f_row, (8, L))
        f_hi = f8.astype(BF16)
        f_r1 = f8 - f_hi.astype(F32)
        f_mid = f_r1.astype(BF16)
        f_lo = (f_r1 - f_mid.astype(F32)).astype(BF16)
        b_row = (_dot(f_hi, src) + _dot(f_mid, src) + _dot(f_lo, src))[0:1, :]
        b_col = jnp.sum(jnp.where(mask, f_row, 0.0), axis=1, keepdims=True)
        log_d = jnp.where(mask, b_col + (i_row - b_row), -jnp.inf)
        m_loc = jnp.max(log_d, axis=1, keepdims=True)
        smat = qk * jnp.exp(log_d - m_loc)
        pr_ref[d, rows, :] = _dot(smat.astype(BF16), v_ext)
        bb_ref[d, rows, :] = jnp.broadcast_to(b_col, (L, LANES))
        mb_ref[d, rows, :] = jnp.broadcast_to(m_loc, (L, LANES))
        b_last = jnp.sum(f_row, axis=1, keepdims=True)
        ls_row = b_last - b_row + i_row
        m2 = jnp.max(ls_row, axis=1, keepdims=True)
        kw_t = (k_t * jnp.exp(ls_row - m2)).astype(BF16)
        kv_ref[d, c] = scale * _dot(kw_t, v_ext)
        rp_ref[d, c] = jnp.where(sub == 0, b_last, m2)


def _mlstm_carry(c, d, carry, q_ref, pr_ref, bb_ref, mb_ref, kv_ref, rp_ref, h_ref, L):
    cn, m = carry
    two = lambda x: jnp.concatenate([x, x], axis=1)
    rows = pl.ds(pl.multiple_of(c * L, L), L)
    bb = bb_ref[d, rows, :]
    mb = mb_ref[d, rows, :]
    m_t = jnp.maximum(bb + m, mb)
    a_int = jnp.exp(bb + m - m_t)
    e_loc = jnp.exp(mb - m_t)
    nd = two(a_int) * _dot(q_ref[rows, :], cn.astype(BF16)) + two(e_loc) * pr_ref[d, rows, :]
    h_ref[d, rows, :] = nd[:, :DV_M] / jnp.maximum(jnp.abs(nd[:, DV_M:]), jnp.exp(-m_t))
    rp = rp_ref[d, c]
    b_last, m2 = rp[0:1, :], rp[1:2, :]
    m_new = jnp.maximum(b_last + m, m2)
    a_c = jnp.exp(b_last + m - m_new)
    e2 = jnp.exp(m2 - m_new)
    return two(a_c) * cn + two(e2) * kv_ref[d, c], m_new


def _mlstm_kernel(q_ref, k_ref, v_ref, og_ref, gr_ref, br_ref, c0_ref, n0_ref, m0_ref, hn_ref,
                  y_ref, c_out_ref, n_out_ref, m_out_ref,
                  pr_ref, bb_ref, mb_ref, kv_ref, rp_ref, h_ref, *, L, S):
    nch = S // L

    def local(ci, carry):
        _mlstm_local(ci, q_ref, k_ref, v_ref, gr_ref, br_ref, pr_ref, bb_ref, mb_ref, kv_ref,
                     rp_ref, L)
        return carry

    lax.fori_loop(0, nch, local, 0, unroll=2)
    step = functools.partial(_mlstm_carry, q_ref=q_ref, pr_ref=pr_ref, bb_ref=bb_ref, mb_ref=mb_ref,
                             kv_ref=kv_ref, rp_ref=rp_ref, h_ref=h_ref, L=L)

    def body(ci, carry):
        return step(ci, 0, carry[0]), step(nch - 1 - ci, 1, carry[1])

    def init(d):
        n_rep = jnp.broadcast_to(n0_ref[d], (DK_M, DK_M)).T
        return jnp.concatenate([c0_ref[d], n_rep], axis=1), m0_ref[d]

    fin = lax.fori_loop(0, nch, body, (init(0), init(1)), unroll=2)
    for d in range(2):
        cn, m = fin[d]
        c_out_ref[d] = cn[:, :DV_M]
        n_out_ref[d] = cn[:, DV_M:].T[0:1, :]
        m_out_ref[d] = m

    hm = h_ref[0] + h_ref[1]
    y = hm * lax.rsqrt(jnp.mean(hm * hm, axis=-1, keepdims=True) + EPS) * hn_ref[...]
    y_ref[...] = (y * _sigmoid(og_ref[...].astype(F32))).astype(y_ref.dtype)


def _mlstm(a16, grow, brow, c0, n0, m0, hnorm, B, S):
    L = min(MLSTM_CHUNK, S)
    nch = S // L
    cb = lambda off: off // LANES
    kern = functools.partial(_mlstm_kernel, L=L, S=S)
    return pl.pallas_call(
        kern,
        grid=(B, H_M),
        in_specs=[
            pl.BlockSpec((S, LANES), lambda b, h: (b, cb(A16_MQ) + h)),
            pl.BlockSpec((S, LANES), lambda b, h: (b, cb(A16_MK) + h)),
            pl.BlockSpec((S, LANES), lambda b, h: (b, cb(A16_MV) + h)),
            pl.BlockSpec((S, LANES), lambda b, h: (b, cb(A16_MO) + h)),
            pl.BlockSpec((None, nch, 4, L), lambda b, h: (h, b, 0, 0)),
            pl.BlockSpec((None, 4, 1), lambda b, h: (h, 0, 0)),
            pl.BlockSpec((None, 2, None, DK_M, DV_M), lambda b, h: (b, 0, h, 0, 0)),
            pl.BlockSpec((None, 2, None, 1, DK_M), lambda b, h: (b, 0, h, 0, 0)),
            pl.BlockSpec((None, 2, None, 1, LANES), lambda b, h: (b, 0, h, 0, 0)),
            pl.BlockSpec((1, LANES), lambda b, h: (0, h)),
        ],
        out_specs=[
            pl.BlockSpec((S, LANES), lambda b, h: (b, h)),
            pl.BlockSpec((None, 2, None, DK_M, DV_M), lambda b, h: (b, 0, h, 0, 0)),
            pl.BlockSpec((None, 2, None, 1, DK_M), lambda b, h: (b, 0, h, 0, 0)),
            pl.BlockSpec((None, 2, None, 1, LANES), lambda b, h: (b, 0, h, 0, 0)),
        ],
        out_shape=[
            jax.ShapeDtypeStruct((B * S, BRANCH_W), BF16),
            jax.ShapeDtypeStruct((B, 2, H_M, DK_M, DV_M), F32),
            jax.ShapeDtypeStruct((B, 2, H_M, 1, DK_M), F32),
            jax.ShapeDtypeStruct((B, 2, H_M, 1, LANES), F32),
        ],
        scratch_shapes=[pltpu.VMEM((2, S, 2 * DV_M), F32), pltpu.VMEM((2, S, LANES), F32),
                        pltpu.VMEM((2, S, LANES), F32), pltpu.VMEM((2, nch, DK_M, 2 * DV_M), F32),
                        pltpu.VMEM((2, nch, 8, LANES), F32), pltpu.VMEM((2, S, DV_M), F32)],
        compiler_params=_cparams("parallel", "parallel"),
        name="mlstm",
    )(a16, a16, a16, a16, grow, brow, c0, n0, m0, hnorm)


def _gla_local(c, q2_ref, k2_ref, v_ref, la_ref, oa_ref, qt_ref, u_ref, dec_ref, L):
    nb = L // GLA_SUB
    ti = lax.broadcasted_iota(jnp.int32, (L, L), 0)
    si = lax.broadcasted_iota(jnp.int32, (L, L), 1)
    row_blk = lax.broadcasted_iota(jnp.int32, (L, LANES), 0) // GLA_SUB
    lo_half = lax.broadcasted_iota(jnp.int32, (L, LANES), 1) < DK_G
    eye = (lax.broadcasted_iota(jnp.int32, (DK_G, LANES), 0)
           == lax.broadcasted_iota(jnp.int32, (DK_G, LANES), 1))
    rows = pl.ds(pl.multiple_of(c * L, L), L)
    q2 = q2_ref[rows, :]
    k2 = k2_ref[rows, :]
    v = v_ref[rows, :]
    row = lax.broadcasted_iota(jnp.int32, (L, LANES), 0)
    for d in range(2):
        rev = d == 1
        mask = (si >= ti) if rev else (si <= ti)
        g2 = la_ref[d, rows, :]
        step = 1
        while step < L:
            if rev:
                g2 = g2 + jnp.where(row < L - step, pltpu.roll(g2, L - step, 0), 0.0)
            else:
                g2 = g2 + jnp.where(row >= step, pltpu.roll(g2, step, 0), 0.0)
            step *= 2
        qt_ref[d, rows, :] = (q2 * jnp.exp(g2))[:, :DK_G].astype(BF16)
        a_parts, b_parts = [], []
        for p in range(nb // 2):
            ia, ib = 2 * p, 2 * p + 1
            ra = ia * GLA_SUB + (GLA_SUB - 1 if rev else 0)
            rb = ib * GLA_SUB + (GLA_SUB - 1 if rev else 0)
            ref2 = jnp.where(lo_half, g2[ra:ra + 1, :], g2[rb:rb + 1, :])
            blk = jnp.where(lo_half, ia, ib)
            in_blk = row_blk == blk
            key_ok = (row_blk >= blk) if rev else (row_blk <= blk)
            a_parts.append(jnp.where(in_blk, q2 * jnp.exp(jnp.minimum(g2 - ref2, 0.0)), 0.0))
            b_parts.append(
                jnp.where(key_ok, k2 * jnp.exp(jnp.minimum(ref2 - g2, GLA_EXP_CLAMP)), 0.0))
        a_big = jnp.concatenate(a_parts, axis=1).astype(BF16)
        b_big = jnp.concatenate(b_parts, axis=1).astype(BF16)
        att = jnp.where(mask, _dot_nt(a_big, b_big), 0.0)
        oa_ref[d, rows, :] = _dot(att.astype(BF16), v)
        gl_row = 0 if rev else L - 1
        glast = g2[gl_row:gl_row + 1, :]
        kd = (k2 * jnp.exp(glast - g2))[:, :DK_G]
        u_ref[d, c] = _dot_tn(kd.astype(BF16), v)
        glast_col = jnp.sum(jnp.where(eye, glast, 0.0), axis=1, keepdims=True)
        dec_ref[d, c] = jnp.broadcast_to(jnp.exp(glast_col), (DK_G, DV_G))


def _gla_kernel(qk_ref, v_ref, sm_ref, wup_ref, bup_ref, s0_ref, gr_ref, hn_ref,
                y_ref, s_out_ref, la_ref, q2_ref, k2_ref, oa_ref, oi_ref, qt_ref, u_ref, dec_ref,
                *, L, S):
    nch = S // L
    sm = sm_ref[...].astype(BF16)
    for d in range(2):
        la_ref[d] = _log_sigmoid(_dot(sm, wup_ref[d]) + bup_ref[d]) * (1.0 / GLA_TAU)
    qk = qk_ref[...].astype(F32)
    qk_sw = pltpu.roll(qk, DK_G, 1)
    lo_half = lax.broadcasted_iota(jnp.int32, qk.shape, 1) < DK_G
    q2_ref[...] = jnp.where(lo_half, qk, qk_sw) * (DK_G ** -0.5)
    k2_ref[...] = jnp.where(lo_half, qk_sw, qk)

    def local(ci, carry):
        _gla_local(ci, q2_ref, k2_ref, v_ref, la_ref, oa_ref, qt_ref, u_ref, dec_ref, L)
        return carry

    lax.fori_loop(0, nch, local, 0, unroll=4)

    def body(ci, carry):
        out = []
        for d, c in ((0, ci), (1, nch - 1 - ci)):
            rows = pl.ds(pl.multiple_of(c * L, L), L)
            st = carry[d]
            oi_ref[d, rows, :] = _dot(qt_ref[d, rows, :], st.astype(BF16))
            out.append(dec_ref[d, c] * st + u_ref[d, c])
        return tuple(out)

    st_f, st_b = lax.fori_loop(0, nch, body, (s0_ref[0], s0_ref[1]), unroll=2)
    s_out_ref[0] = st_f
    s_out_ref[1] = st_b

    og = (oa_ref[0] + oi_ref[0]) + (oa_ref[1] + oi_ref[1])
    y = og * lax.rsqrt(jnp.mean(og * og, axis=-1, keepdims=True) + EPS) * hn_ref[...]
    gr = gr_ref[...].astype(F32)
    y_ref[...] = (y * (gr * _sigmoid(gr))).astype(y_ref.dtype)


def _gla(a32, a16, wup, bup, s0, hnorm, B, S):
    L = min(GLA_CHUNK, S)
    nch = S // L
    cb = lambda off: off // LANES
    kern = functools.partial(_gla_kernel, L=L, S=S)
    return pl.pallas_call(
        kern,
        grid=(B, H_G),
        in_specs=[
            pl.BlockSpec((S, LANES), lambda b, h: (b, cb(A16_GQK) + h)),
            pl.BlockSpec((S, LANES), lambda b, h: (b, cb(A16_GV) + h)),
            pl.BlockSpec((S, LANES), lambda b, h: (b, cb(A32_SM))),
            pl.BlockSpec((None, 2, LANES, LANES), lambda b, h: (h, 0, 0, 0)),
            pl.BlockSpec((None, 2, 1, LANES), lambda b, h: (h, 0, 0, 0)),
            pl.BlockSpec((None, 2, None, DK_G, DV_G), lambda b, h: (b, 0, h, 0, 0)),
            pl.BlockSpec((S, LANES), lambda b, h: (b, cb(A16_GR) + h)),
            pl.BlockSpec((1, LANES), lambda b, h: (0, h)),
        ],
        out_specs=[
            pl.BlockSpec((S, LANES), lambda b, h: (b, h)),
            pl.BlockSpec((None, 2, None, DK_G, DV_G), lambda b, h: (b, 0, h, 0, 0)),
        ],
        out_shape=[
            jax.ShapeDtypeStruct((B * S, BRANCH_W), BF16),
            jax.ShapeDtypeStruct((B, 2, H_G, DK_G, DV_G), F32),
        ],
        scratch_shapes=[pltpu.VMEM((2, S, LANES), F32), pltpu.VMEM((S, LANES), F32),
                        pltpu.VMEM((S, LANES), F32), pltpu.VMEM((2, S, DV_G), F32),
                        pltpu.VMEM((2, S, DV_G), F32), pltpu.VMEM((2, S, DK_G), BF16),
                        pltpu.VMEM((2, nch, DK_G, DV_G), F32), pltpu.VMEM((2, nch, DK_G, DV_G), F32)],
        compiler_params=_cparams("parallel", "parallel"),
        name="gla",
    )(a16, a16, a32, wup, bup, s0, a16, hnorm)


def _rope(x, cos, sin_signed):
    lane = lax.broadcasted_iota(jnp.int32, x.shape, 1)
    first = (lane % DQK_D) < (DQK_D // 2)
    partner = jnp.where(first, pltpu.roll(x, LANES - DQK_D // 2, 1), pltpu.roll(x, DQK_D // 2, 1))
    return x * cos + partner * sin_signed


def _attn_kernel(*refs, S, P, TQ, lam_init, has_ctx):
    if has_ctx:
        (q_ref, k_ref, v_ref, ck_ref, cv_ref, cos_ref, sin_ref, lam_ref, hn_ref,
         y_ref, kk_ref, vv_ref) = refs
    else:
        q_ref, k_ref, v_ref, lam_ref, hn_ref, y_ref, kk_ref, vv_ref = refs
    qi = pl.program_id(2)

    @pl.when(qi == 0)
    def _():
        k = k_ref[...]
        if has_ctx:
            k = _rope(k, cos_ref[...], sin_ref[...])
            kk_ref[S:S + P, :] = ck_ref[...].astype(BF16)
            vv_ref[S:S + P, :] = cv_ref[...].astype(BF16)
        kk_ref[0:S, :] = k.astype(BF16)
        vv_ref[0:S, :] = v_ref[...].astype(BF16)

    q = q_ref[...]
    if has_ctx:
        r0 = pl.multiple_of(qi * TQ, TQ)
        q = _rope(q, cos_ref[pl.ds(r0, TQ), :], sin_ref[pl.ds(r0, TQ), :])
    q = q * (DQK_D ** -0.5 * math.log2(math.e))
    lane = lax.broadcasted_iota(jnp.int32, q.shape, 1)
    kk = kk_ref[...]
    vv = vv_ref[...]
    lv = lam_ref[...]
    lam = (jnp.exp(jnp.sum(lv[0:1, :] * lv[1:2, :], axis=-1, keepdims=True))
           - jnp.exp(jnp.sum(lv[2:3, :] * lv[3:4, :], axis=-1, keepdims=True)) + lam_init)
    es, ls = [], []
    for comp in range(2):
        sel = (lane < DQK_D) if comp == 0 else (lane >= DQK_D)
        s = _dot_nt(jnp.where(sel, q, 0.0).astype(BF16), kk)
        e = jnp.exp2(s - jnp.max(s, axis=-1, keepdims=True))
        es.append(e)
        ls.append(jnp.sum(e, axis=-1, keepdims=True))
    w = es[0] - es[1] * (lam * ls[0] / ls[1])
    o = _dot(w.astype(BF16), vv) * (1.0 / ls[0])
    y = o * lax.rsqrt(jnp.mean(o * o, axis=-1, keepdims=True) + EPS) * hn_ref[...]
    y_ref[...] = (y * (1.0 - lam_init)).astype(y_ref.dtype)


def _attn(a32, lamv, hnorm, B, S, lam_init, ctx=None):
    TQ = min(256, S)
    nq = S // TQ
    has_ctx = ctx is not None
    P = ctx[0].shape[2] if has_ctx else 0
    cb = lambda off: off // LANES
    kern = functools.partial(_attn_kernel, S=S, P=P, TQ=TQ, lam_init=lam_init, has_ctx=has_ctx)
    in_specs = [
        pl.BlockSpec((TQ, LANES), lambda b, h, i: (b * nq + i, cb(A32_DQ) + h)),
        pl.BlockSpec((S, LANES), lambda b, h, i: (b, cb(A32_DK) + h)),
        pl.BlockSpec((S, LANES), lambda b, h, i: (b, cb(A32_DV) + h)),
    ]
    args = [a32, a32, a32]
    if has_ctx:
        ck, cv, layer, cos, sin = ctx
        in_specs += [
            pl.BlockSpec((None, None, P, LANES), lambda b, h, i: (b, layer, 0, h)),
            pl.BlockSpec((None, None, P, LANES), lambda b, h, i: (b, layer, 0, h)),
            pl.BlockSpec((S, LANES), lambda b, h, i: (0, 0)),
            pl.BlockSpec((S, LANES), lambda b, h, i: (0, 0)),
        ]
        args += [ck, cv, cos, sin]
    in_specs += [
        pl.BlockSpec((4, DQK_D), lambda b, h, i: (0, 0)),
        pl.BlockSpec((1, LANES), lambda b, h, i: (0, h)),
    ]
    args += [lamv, hnorm]
    return pl.pallas_call(
        kern,
        grid=(B, H_D, nq),
        in_specs=in_specs,
        out_specs=pl.BlockSpec((TQ, LANES), lambda b, h, i: (b * nq + i, h)),
        out_shape=jax.ShapeDtypeStruct((B * S, BRANCH_W), BF16),
        scratch_shapes=[pltpu.VMEM((S + P, LANES), BF16), pltpu.VMEM((S + P, LANES), BF16)],
        compiler_params=_cparams("parallel", "parallel", "arbitrary"),
        name="diff_attn",
    )(*args)


def _conv_kernel(ca_ref, cb_ref, w_ref, g_ref, b_ref, y_ref, pad_ref, cv_ref, *, S):
    ca = ca_ref[...].astype(F32)
    cbv = cb_ref[...].astype(F32)
    zeros = jnp.zeros((CONV_PAD, BRANCH_W), F32)
    pad_ref[0:CONV_PAD, :] = zeros
    pad_ref[CONV_PAD + S:2 * CONV_PAD + S, :] = zeros
    pad_ref[CONV_PAD:CONV_PAD + S, :] = ca * _sigmoid(cbv)
    off = CONV_PAD - CONV_W // 2

    win_rows = CONV_ROWS + 2 * CONV_PAD

    def body(i, carry):
        base = pl.multiple_of(i * CONV_ROWS, CONV_ROWS)
        for lb in range(BRANCH_W // LANES):
            cols = slice(lb * LANES, (lb + 1) * LANES)
            win = pad_ref[pl.ds(base, win_rows), cols]
            acc = jnp.zeros((CONV_ROWS, LANES), F32)
            for r in range(8):
                rolled = win if r == 0 else pltpu.roll(win, win_rows - r, 0)
                for a in range(2 * CONV_PAD // 8):
                    j = 8 * a + r - off
                    if 0 <= j < CONV_W:
                        acc = acc + rolled[8 * a:8 * a + CONV_ROWS, :] * w_ref[j:j + 1, cols]
            cv_ref[:, cols] = acc
        acc = cv_ref[...]
        mu = jnp.mean(acc, axis=-1, keepdims=True)
        xc = acc - mu
        yn = xc * lax.rsqrt(jnp.mean(xc * xc, axis=-1, keepdims=True) + EPS) * g_ref[...] + b_ref[...]
        y_ref[pl.ds(base, CONV_ROWS), :] = (yn * _sigmoid(yn)).astype(y_ref.dtype)
        return carry

    lax.fori_loop(0, S // CONV_ROWS, body, 0)


def _conv(a16, w_dw, ln_g, ln_b, B, S):
    cb = lambda off: off // BRANCH_W
    kern = functools.partial(_conv_kernel, S=S)
    return pl.pallas_call(
        kern,
        grid=(B,),
        in_specs=[
            pl.BlockSpec((S, BRANCH_W), lambda b: (b, cb(A16_CA))),
            pl.BlockSpec((S, BRANCH_W), lambda b: (b, cb(A16_CB))),
            pl.BlockSpec((CONV_W + 1, BRANCH_W), lambda b: (0, 0)),
            pl.BlockSpec((1, BRANCH_W), lambda b: (0, 0)),
            pl.BlockSpec((1, BRANCH_W), lambda b: (0, 0)),
        ],
        out_specs=pl.BlockSpec((S, BRANCH_W), lambda b: (b, 0)),
        out_shape=jax.ShapeDtypeStruct((B * S, BRANCH_W), BF16),
        scratch_shapes=[pltpu.VMEM((S + 2 * CONV_PAD, BRANCH_W), F32),
                        pltpu.VMEM((CONV_ROWS, BRANCH_W), F32)],
        compiler_params=_cparams("parallel"),
        name="glu_conv_ln",
    )(a16, a16, w_dw, ln_g, ln_b)


def _merge_kernel(x_ref, mod_ref, ym_ref, yd_ref, yg_ref, yc_ref, g0_ref, g1_ref, g2_ref, g3_ref,
                  wb_ref, wo_ref, n2_ref, wr_ref, x1_ref, h2_ref, route_ref):
    ys = (ym_ref, yd_ref, yg_ref, yc_ref)
    gs = (g0_ref, g1_ref, g2_ref, g3_ref)
    merged = None
    for nbr in range(N_BRANCH):
        br = _dot(ys[nbr][...], wb_ref[nbr])
        term = _sigmoid(gs[nbr][...].astype(F32)) * br
        merged = term if merged is None else merged + term
    out = _dot(merged.astype(BF16), wo_ref[...])
    x1 = x_ref[...] + mod_ref[2:3, :] * out
    x1_ref[...] = x1
    y = x1 * lax.rsqrt(jnp.mean(x1 * x1, axis=-1, keepdims=True) + EPS) * n2_ref[...]
    h2 = y * (1.0 + mod_ref[4:5, :]) + mod_ref[3:4, :]
    h2_ref[...] = h2.reshape(h2_ref.shape)
    wr = wr_ref[...]
    h_hi, w_hi = h2.astype(BF16), wr.astype(BF16)
    h_lo = (h2 - h_hi.astype(F32)).astype(BF16)
    w_lo = (wr - w_hi.astype(F32)).astype(BF16)
    logits = _dot(h_hi, w_hi) + (_dot(h_hi, w_lo) + _dot(h_lo, w_hi))
    lane = lax.broadcasted_iota(jnp.int32, logits.shape, 1)
    neg = -jnp.inf
    big = jnp.int32(LANES)
    is_g = lane < N_GROUPS
    gl = jnp.where(is_g, logits, neg)
    gmax = jnp.max(gl, axis=-1, keepdims=True)
    gidx = jnp.min(jnp.where(is_g & (gl == gmax), lane, big), axis=-1, keepdims=True)
    g_p = 1.0 / jnp.sum(jnp.where(is_g, jnp.exp(gl - gmax), 0.0), axis=-1, keepdims=True)
    e_lane = lane - N_GROUPS
    in_grp = (e_lane >= 0) & (e_lane < N_EXPERTS) & ((e_lane // EXPERTS_PER_GROUP) == gidx)
    el = jnp.where(in_grp, logits, neg)
    v1 = jnp.max(el, axis=-1, keepdims=True)
    i1 = jnp.min(jnp.where(in_grp & (el == v1), lane, big), axis=-1, keepdims=True)
    el2 = jnp.where(lane == i1, neg, el)
    v2 = jnp.max(el2, axis=-1, keepdims=True)
    i2 = jnp.min(jnp.where(in_grp & (lane != i1) & (el2 == v2), lane, big), axis=-1, keepdims=True)
    e2 = jnp.exp(v2 - v1)
    w1 = g_p / (1.0 + e2)
    w2 = g_p * e2 / (1.0 + e2)
    id1 = (i1 - N_GROUPS).astype(F32)
    id2 = (i2 - N_GROUPS).astype(F32)
    route_ref[...] = jnp.where(lane == 0, id1, jnp.where(lane == 1, id2,
                               jnp.where(lane == 2, w1, jnp.where(lane == 3, w2, 0.0))))


def _merge(x2d, mod, a16, ym, yd, yg, yc, wb, wo, n2, wr, rows_per_mod, tm):
    T = x2d.shape[0]
    gcb = A16_GATE // D_MODEL
    row = lambda i: (i, 0)
    return pl.pallas_call(
        _merge_kernel,
        grid=(T // tm,),
        in_specs=[
            pl.BlockSpec((tm, D_MODEL), row),
            pl.BlockSpec((None, N_MOD, D_MODEL), lambda i: ((i * tm) // rows_per_mod, 0, 0)),
            pl.BlockSpec((tm, BRANCH_W), row),
            pl.BlockSpec((tm, BRANCH_W), row),
            pl.BlockSpec((tm, BRANCH_W), row),
            pl.BlockSpec((tm, BRANCH_W), row),
            pl.BlockSpec((tm, D_MODEL), lambda i: (i, gcb + 0)),
            pl.BlockSpec((tm, D_MODEL), lambda i: (i, gcb + 1)),
            pl.BlockSpec((tm, D_MODEL), lambda i: (i, gcb + 2)),
            pl.BlockSpec((tm, D_MODEL), lambda i: (i, gcb + 3)),
            pl.BlockSpec((N_BRANCH, BRANCH_W, D_MODEL), lambda i: (0, 0, 0)),
            pl.BlockSpec((D_MODEL, D_MODEL), lambda i: (0, 0)),
            pl.BlockSpec((1, D_MODEL), lambda i: (0, 0)),
            pl.BlockSpec((D_MODEL, LANES), lambda i: (0, 0)),
        ],
        out_specs=[
            pl.BlockSpec((tm, D_MODEL), row),
            pl.BlockSpec((tm, TOK_SUB, LANES), lambda i: (i, 0, 0)),
            pl.BlockSpec((tm, LANES), row),
        ],
        out_shape=[
            jax.ShapeDtypeStruct((T, D_MODEL), F32),
            jax.ShapeDtypeStruct((T, TOK_SUB, LANES), F32),
            jax.ShapeDtypeStruct((T, LANES), F32),
        ],
        compiler_params=_cparams("parallel"),
        name="merge_outproj_route",
    )(x2d, mod, ym, yd, yg, yc, a16, a16, a16, a16, wb, wo, n2, wr)


def _gather_rows(idx_ref, src_hbm, dst, sem, n):
    def body(j, carry):
        for u in range(ROW_DMA_UNROLL):
            r = j * ROW_DMA_UNROLL + u
            pltpu.make_async_copy(src_hbm.at[idx_ref[0, r]], dst.at[r], sem).start(priority=u % 2)
        return carry

    lax.fori_loop(0, n // ROW_DMA_UNROLL, body, 0)


def _scatter_rows(idx_ref, src, dst_hbm, sem, n):
    def body(j, carry):
        for u in range(ROW_DMA_UNROLL):
            r = j * ROW_DMA_UNROLL + u
            pltpu.make_async_copy(src.at[r], dst_hbm.at[idx_ref[0, r]], sem).start(priority=u % 2)
        return carry

    lax.fori_loop(0, n // ROW_DMA_UNROLL, body, 0)


def _wait_rows(buf, sem):
    pltpu.make_async_copy(buf, buf, sem).wait()


def _moe_dispatch_kernel(d0_ref, d1_ref, h_ref, xg_in, xg_out, sem, *, tm):
    del xg_in
    _scatter_rows(d0_ref, h_ref, xg_out, sem.at[0], tm)
    _scatter_rows(d1_ref, h_ref, xg_out, sem.at[1], tm)
    _wait_rows(h_ref, sem.at[0])
    _wait_rows(h_ref, sem.at[1])


def _moe_dispatch(h3, dest, n_rows, tm):
    T = h3.shape[0]
    kern = functools.partial(_moe_dispatch_kernel, tm=tm)
    return pl.pallas_call(
        kern,
        grid=(T // tm,),
        in_specs=[
            pl.BlockSpec((None, None, 1, tm), lambda i: (0, i, 0, 0), memory_space=pltpu.SMEM),
            pl.BlockSpec((None, None, 1, tm), lambda i: (1, i, 0, 0), memory_space=pltpu.SMEM),
            pl.BlockSpec((tm, TOK_SUB, LANES), lambda i: (i, 0, 0)),
            pl.BlockSpec(memory_space=pl.ANY),
        ],
        out_specs=pl.BlockSpec(memory_space=pl.ANY),
        out_shape=jax.ShapeDtypeStruct((n_rows, TOK_SUB, LANES), F32),
        input_output_aliases={3: 0},
        scratch_shapes=[pltpu.SemaphoreType.DMA((2,))],
        compiler_params=_cparams("arbitrary"),
        name="moe_dispatch",
    )(dest, dest, h3, jnp.zeros((n_rows, TOK_SUB, LANES), F32))


def _moe_ffn_kernel(te_ref, x_ref, w1_ref, w3_ref, w2_ref, o_ref):
    del te_ref
    x = x_ref[...].reshape(MOE_TILE, D_MODEL).astype(BF16)
    a = _dot(x, w1_ref[...].astype(BF16))
    b = _dot(x, w3_ref[...].astype(BF16))
    s = (a * _sigmoid(a)) * b
    y = _dot(s.astype(BF16), w2_ref[...].astype(BF16))
    o_ref[...] = y.reshape(o_ref.shape)


def _moe_ffn(xg, tile_e, w1, w3, w2):
    ntiles = xg.shape[0] // MOE_TILE
    tile = pl.BlockSpec((MOE_TILE, TOK_SUB, LANES), lambda i, te: (i, 0, 0))
    return pl.pallas_call(
        _moe_ffn_kernel,
        grid_spec=pltpu.PrefetchScalarGridSpec(
            num_scalar_prefetch=1,
            grid=(ntiles,),
            in_specs=[
                tile,
                pl.BlockSpec((None, D_MODEL, D_EXPERT), lambda i, te: (te[i], 0, 0)),
                pl.BlockSpec((None, D_MODEL, D_EXPERT), lambda i, te: (te[i], 0, 0)),
                pl.BlockSpec((None, D_EXPERT, D_MODEL), lambda i, te: (te[i], 0, 0)),
            ],
            out_specs=tile,
        ),
        out_shape=jax.ShapeDtypeStruct(xg.shape, F32),
        compiler_params=_cparams("parallel"),
        name="moe_grouped_experts",
    )(tile_e, xg, w1, w3, w2)


def _moe_combine_kernel(d0_ref, d1_ref, y_hbm, route_ref, x1_ref, mod_ref, fn_ref, o_ref,
                        ga, gb, sem, *, tm, final_norm):
    _gather_rows(d0_ref, y_hbm, ga, sem.at[0], tm)
    _gather_rows(d1_ref, y_hbm, gb, sem.at[1], tm)
    rt = route_ref[...]
    _wait_rows(ga, sem.at[0])
    _wait_rows(gb, sem.at[1])
    y = rt[:, 2:3] * ga[...].reshape(tm, D_MODEL) + rt[:, 3:4] * gb[...].reshape(tm, D_MODEL)
    x2 = x1_ref[...] + mod_ref[5:6, :] * y
    if final_norm:
        x2 = x2 * lax.rsqrt(jnp.mean(x2 * x2, axis=-1, keepdims=True) + EPS) * fn_ref[...]
    o_ref[...] = x2


def _moe_combine(yg, dest, route, x1, mod, fn, rows_per_mod, tm, final_norm):
    T = x1.shape[0]
    kern = functools.partial(_moe_combine_kernel, tm=tm, final_norm=final_norm)
    return pl.pallas_call(
        kern,
        grid=(T // tm,),
        in_specs=[
            pl.BlockSpec((None, None, 1, tm), lambda i: (0, i, 0, 0), memory_space=pltpu.SMEM),
            pl.BlockSpec((None, None, 1, tm), lambda i: (1, i, 0, 0), memory_space=pltpu.SMEM),
            pl.BlockSpec(memory_space=pl.ANY),
            pl.BlockSpec((tm, LANES), lambda i: (i, 0)),
            pl.BlockSpec((tm, D_MODEL), lambda i: (i, 0)),
            pl.BlockSpec((None, N_MOD, D_MODEL), lambda i: ((i * tm) // rows_per_mod, 0, 0)),
            pl.BlockSpec((1, D_MODEL), lambda i: (0, 0)),
        ],
        out_specs=pl.BlockSpec((tm, D_MODEL), lambda i: (i, 0)),
        out_shape=jax.ShapeDtypeStruct((T, D_MODEL), F32),
        scratch_shapes=[pltpu.VMEM((tm, TOK_SUB, LANES), F32), pltpu.VMEM((tm, TOK_SUB, LANES), F32),
                        pltpu.SemaphoreType.DMA((2,))],
        compiler_params=_cparams("arbitrary"),
        name="moe_combine",
    )(dest, dest, yg, route, x1, mod, fn)


def _route_plan(route, T):
    ntiles = (2 * T + N_EXPERTS * (MOE_TILE - 1) + MOE_TILE - 1) // MOE_TILE
    ef = route[:, 0:2].astype(jnp.int32).reshape(-1)
    oh = (ef[:, None] == jnp.arange(N_EXPERTS, dtype=jnp.int32)[None, :]).astype(jnp.int32)
    csum = jnp.cumsum(oh, axis=0)
    rank = jnp.sum((csum - oh) * oh, axis=1)
    counts = csum[-1]
    padded = ((counts + MOE_TILE - 1) // MOE_TILE) * MOE_TILE
    seg_end = jnp.cumsum(padded)
    dest = jnp.sum(oh * (seg_end - padded)[None, :], axis=1) + rank
    tile_row = jnp.arange(ntiles, dtype=jnp.int32) * MOE_TILE
    tile_e = jnp.minimum(jnp.sum((tile_row[:, None] >= seg_end[None, :]).astype(jnp.int32), axis=1),
                         N_EXPERTS - 1)
    return jnp.transpose(dest.reshape(T, 2)), tile_e, ntiles * MOE_TILE


def _split_w_in(w):
    sizes = (H_M * DK_M, H_M * DK_M, H_M * DV_M, H_M * DV_M, 2 * H_M, 2 * H_M,
             H_D * 2 * DQK_D, H_D * 2 * DQK_D, H_D * DV_D,
             H_G * DK_G, H_G * DK_G, H_G * DV_G, 2 * GATE_RANK, H_G * DV_G,
             BRANCH_W, BRANCH_W, N_BRANCH * D_MODEL)
    outs, acc = [], 0
    for s in sizes:
        outs.append(w[:, acc:acc + s])
        acc += s
    return outs


def _pack_layer_params(p):
    (m_q, m_k, m_v, m_o, m_i, m_f, d_q, d_k, d_v, g_q, g_k, g_v, g_a, g_r, c_a, c_b, gate) = \
        _split_w_in(p['w_in'])
    gqk = jnp.concatenate([g_q.reshape(D_MODEL, H_G, DK_G), g_k.reshape(D_MODEL, H_G, DK_G)],
                          axis=2).reshape(D_MODEL, 2 * H_G * DK_G)
    small = jnp.concatenate(
        [m_i, m_f, g_a, jnp.zeros((D_MODEL, LANES - 4 * H_M - 2 * GATE_RANK), F32)], axis=1)
    w16 = jnp.concatenate([m_q, m_k, m_v, m_o, gate, g_r, c_a, c_b, gqk, g_v], axis=1).astype(BF16)
    w32 = jnp.concatenate([d_q, d_k, d_v, small], axis=1).astype(BF16)
    bi = p['b_m_i'].reshape(2, H_M)
    bf = p['b_m_f'].reshape(2, H_M)
    bcol = jnp.stack([bi[0], bi[1], bf[0], bf[1]], axis=-1)
    wup = p['w_gla_up'].reshape(2, GATE_RANK, H_G, DK_G)
    wup_pad = jnp.zeros((H_G, 2, LANES, LANES), F32)
    bup = p['b_gla_gate'].reshape(2, H_G, DK_G)
    for d in range(2):
        blk = jnp.transpose(wup[d], (1, 0, 2))
        blk = jnp.concatenate([blk, blk], axis=-1)
        r0 = SM_GA + d * GATE_RANK
        wup_pad = wup_pad.at[:, d, r0:r0 + GATE_RANK, :].set(blk)
    bup2 = jnp.transpose(jnp.concatenate([bup, bup], axis=-1), (1, 0, 2))[:, :, None, :]
    wr = jnp.concatenate([p['w_group_router'], p['w_expert_router'],
                          jnp.zeros((D_MODEL, LANES - N_GROUPS - N_EXPERTS), F32)], axis=1)
    return dict(
        w16=w16, w32=w32, brow=bcol.reshape(H_M, 4, 1),
        wup=wup_pad.astype(BF16), bup=bup2,
        wdw=jnp.concatenate([p['w_dw'], jnp.zeros((1, BRANCH_W), F32)], axis=0),
        ln_g=p['conv_ln_g'].reshape(1, BRANCH_W), ln_b=p['conv_ln_b'].reshape(1, BRANCH_W),
        hn_m=p['hnorm_m'].reshape(1, BRANCH_W), hn_d=p['hnorm_d'].reshape(1, BRANCH_W),
        hn_g=p['hnorm_g'].reshape(1, BRANCH_W),
        lamv=jnp.stack([p['lam_q1'], p['lam_k1'], p['lam_q2'], p['lam_k2']], axis=0),
        wb=p['w_branch'].astype(BF16), wo=p['w_out'].astype(BF16),
        n1=p['norm1'].reshape(1, D_MODEL), n2=p['norm2'].reshape(1, D_MODEL), wr=wr,
        w1=p['w_e1'].reshape(N_EXPERTS, D_MODEL, D_EXPERT),
        w3=p['w_e3'].reshape(N_EXPERTS, D_MODEL, D_EXPERT),
        w2=p['w_e2'].reshape(N_EXPERTS, D_EXPERT, D_MODEL),
    )


def _rope_tables(S):
    rows = S // GRID_W
    r, col = jnp.meshgrid(jnp.arange(rows, dtype=F32), jnp.arange(GRID_W, dtype=F32), indexing='ij')
    r, col = r.reshape(-1), col.reshape(-1)
    n_freq = DQK_D // 4
    inv = ROPE_BASE ** (-jnp.arange(n_freq, dtype=F32) / n_freq)
    ang = jnp.concatenate([r[:, None] * inv, col[:, None] * inv], axis=-1)
    cos, sin = jnp.cos(ang), jnp.sin(ang)
    cos_t = jnp.tile(cos, (1, LANES // (DQK_D // 2)))
    sin_t = jnp.tile(jnp.concatenate([-sin, sin], axis=-1), (1, LANES // DQK_D))
    return cos_t, sin_t


def _pick_tile(T, cap):
    t = min(T, cap)
    while T % t:
        t //= 2
    return t


def _layer(x2d, mod, pk, B, S, lam_init, ctx, final_norm, fn):
    T = B * S
    rows_per_mod = T // mod.shape[0]
    tm = _pick_tile(rows_per_mod, 1024)
    a16 = _inproj(x2d, mod, pk['n1'], pk['w16'], BF16, rows_per_mod, tm, N_A16 // 4)
    a32 = _inproj(x2d, mod, pk['n1'], pk['w32'], F32, rows_per_mod, tm, N_A32)

    L = min(MLSTM_CHUNK, S)
    sm = a32[:, A32_SM:A32_SM + 4 * H_M]
    grow = jnp.transpose(sm.reshape(T // L, L, 4, H_M), (3, 0, 2, 1))
    if ctx is None:
        c0 = jnp.zeros((B, 2, H_M, DK_M, DV_M), F32)
        n0 = jnp.zeros((B, 2, H_M, 1, DK_M), F32)
        m0 = jnp.zeros((B, 2, H_M, 1, LANES), F32)
        s0 = jnp.zeros((B, 2, H_G, DK_G, DV_G), F32)
        attn_ctx = None
    else:
        c0 = ctx['C']
        n0 = ctx['n'][:, :, :, None, :]
        m0 = jnp.broadcast_to(ctx['m'][:, :, :, None, None], (B, 2, H_M, 1, LANES))
        s0 = ctx['S']
        attn_ctx = (ctx['k'], ctx['v'], ctx['layer'], ctx['cos'], ctx['sin'])
    ym, c_f, n_f, m_f = _mlstm(a16, grow, pk['brow'], c0, n0, m0, pk['hn_m'], B, S)
    yd = _attn(a32, pk['lamv'], pk['hn_d'], B, S, lam_init, attn_ctx)
    yg, s_f = _gla(a32, a16, pk['wup'], pk['bup'], s0, pk['hn_g'], B, S)
    yc = _conv(a16, pk['wdw'], pk['ln_g'], pk['ln_b'], B, S)
    x1, h3, route = _merge(x2d, mod, a16, ym, yd, yg, yc, pk['wb'], pk['wo'], pk['n2'], pk['wr'],
                           rows_per_mod, _pick_tile(rows_per_mod, 512))
    dest, tile_e, n_rows = _route_plan(route, T)
    tmd = _pick_tile(T, 512)
    xg = _moe_dispatch(h3, dest.reshape(2, T // tmd, 1, tmd), n_rows, tmd)
    y_grouped = _moe_ffn(xg, tile_e, pk['w1'], pk['w3'], pk['w2'])
    tmc = _pick_tile(rows_per_mod, 256)
    x2 = _moe_combine(y_grouped, dest.reshape(2, T // tmc, 1, tmc), route, x1, mod, fn,
                      rows_per_mod, tmc, final_norm)
    state = None
    if ctx is None:
        state = (a32[:, A32_DK:A32_DK + H_D * 2 * DQK_D].reshape(B, S, H_D, 2 * DQK_D),
                 a32[:, A32_DV:A32_DV + H_D * DV_D].reshape(B, S, H_D, DV_D),
                 c_f, n_f[:, :, :, 0, :], m_f[:, :, :, 0, 0], s_f)
    return x2, state


def kernel(x_prompt, x_sample, c, cache_diff_k, cache_diff_v, state_mlstm_C, state_mlstm_n, state_mlstm_m, state_gla_S, c_ctx, w_mod, b_mod, norm1, w_in, b_m_i, b_m_f, lam_q1, lam_k1, lam_q2, lam_k2, w_gla_up, b_gla_gate, w_dw, conv_ln_g, conv_ln_b, hnorm_m, hnorm_d, hnorm_g, w_branch, w_out, norm2, w_group_router, w_expert_router, w_e1, w_e3, w_e2, final_norm):
    Bp, Sp, _ = x_prompt.shape
    Bs, Ss, _ = x_sample.shape
    P = cache_diff_k.shape[2]
    n_cond = 8 * ((1 + Bs + 7) // 8)
    cond = jnp.concatenate([c_ctx[None, :], c, jnp.zeros((n_cond - 1 - Bs, D_MODEL), F32)], axis=0)
    mod_all = _modulation(cond, w_mod, b_mod).reshape(DEPTH, n_cond, N_MOD, D_MODEL)
    cos_t, sin_t = _rope_tables(Ss)
    ck4 = cache_diff_k.reshape(Bs, DEPTH, P, H_D * 2 * DQK_D)
    cv4 = cache_diff_v.reshape(Bs, DEPTH, P, H_D * DV_D)
    fn = final_norm.reshape(1, D_MODEL)
    yp = x_prompt.reshape(Bp * Sp, D_MODEL)
    ys = x_sample.reshape(Bs * Ss, D_MODEL)
    states = []
    for l in range(DEPTH):
        p = {'w_in': w_in[l], 'b_m_i': b_m_i[l], 'b_m_f': b_m_f[l], 'lam_q1': lam_q1[l],
             'lam_k1': lam_k1[l], 'lam_q2': lam_q2[l], 'lam_k2': lam_k2[l],
             'w_gla_up': w_gla_up[l], 'b_gla_gate': b_gla_gate[l], 'w_dw': w_dw[l],
             'conv_ln_g': conv_ln_g[l], 'conv_ln_b': conv_ln_b[l], 'hnorm_m': hnorm_m[l],
             'hnorm_d': hnorm_d[l], 'hnorm_g': hnorm_g[l], 'w_branch': w_branch[l],
             'w_out': w_out[l], 'norm1': norm1[l], 'norm2': norm2[l],
             'w_group_router': w_group_router[l], 'w_expert_router': w_expert_router[l],
             'w_e1': w_e1[l], 'w_e3': w_e3[l], 'w_e2': w_e2[l]}
        pk = _pack_layer_params(p)
        lam_init = 0.8 - 0.6 * math.exp(-0.3 * l)
        last = l == DEPTH - 1
        yp, st = _layer(yp, mod_all[l, 0:1], pk, Bp, Sp, lam_init, None, last, fn)
        states.append(st)
        ctx = {'k': ck4, 'v': cv4, 'layer': l, 'cos': cos_t, 'sin': sin_t,
               'C': state_mlstm_C[:, l], 'n': state_mlstm_n[:, l], 'm': state_mlstm_m[:, l],
               'S': state_gla_S[:, l]}
        ys, _ = _layer(ys, mod_all[l, 1:1 + Bs], pk, Bs, Ss, lam_init, ctx, last, fn)
    stack = lambda i: jnp.stack([s[i] for s in states], axis=1)
    return (yp.reshape(Bp, Sp, D_MODEL), ys.reshape(Bs, Ss, D_MODEL),
            stack(0), stack(1), stack(2), stack(3), stack(4), stack(5))
```

```python
import functools
import math

import jax
import jax.numpy as jnp
from jax import lax
from jax.experimental import pallas as pl
from jax.experimental.pallas import tpu as pltpu

F32 = jnp.float32
BF16 = jnp.bfloat16

D_MODEL = 1024
DEPTH = 2
GRID_W = 64
BRANCH_W = 512
N_BRANCH = 4
H_M, DK_M, DV_M = 4, 128, 128
H_D, DQK_D, DV_D = 4, 64, 128
H_G, DK_G, DV_G = 4, 64, 128
GATE_RANK = 16
GLA_TAU = 16.0
CONV_W = 31
N_GROUPS, EXPERTS_PER_GROUP, D_EXPERT = 4, 4, 512
N_EXPERTS = N_GROUPS * EXPERTS_PER_GROUP
ROPE_BASE = 10000.0
EPS = 1e-6
N_MOD = 6

LANES = 128
VMEM_LIMIT = 48 * 1024 * 1024

A16_MQ, A16_MK, A16_MV, A16_MO = 0, 512, 1024, 1536
A16_GATE, A16_GR, A16_CA, A16_CB = 2048, 6144, 6656, 7168
A16_GQK, A16_GV = 7680, 8192
N_A16 = 8704
A32_DQ, A32_DK, A32_DV, A32_SM = 0, 512, 1024, 1536
N_A32 = 1664
SM_MI, SM_MF, SM_GA = 0, 8, 16

MLSTM_CHUNK = 128
GLA_CHUNK = 64
GLA_SUB = 16
GLA_EXP_CLAMP = 80.0
CONV_ROWS = 64
CONV_PAD = 16
TOK_SUB = D_MODEL // LANES
MOE_TILE = 256
ROW_DMA_UNROLL = 8


def _cparams(*sem):
    return pltpu.CompilerParams(dimension_semantics=sem, vmem_limit_bytes=VMEM_LIMIT)


def _log_sigmoid(x):
    return jnp.minimum(x, 0.0) - jnp.log1p(jnp.exp(-jnp.abs(x)))


def _sigmoid(x):
    return 1.0 / (1.0 + jnp.exp(-x))


def _dot(a, b):
    return jnp.dot(a, b, preferred_element_type=F32)


def _dot_nt(a, b):
    return lax.dot_general(a, b, (((1,), (1,)), ((), ())), preferred_element_type=F32)


def _dot_tn(a, b):
    return lax.dot_general(a, b, (((0,), (0,)), ((), ())), preferred_element_type=F32)


def _mod_kernel(c_ref, w_ref, b_ref, o_ref):
    c = c_ref[...]
    a = (c * _sigmoid(c)).astype(BF16)
    o_ref[...] = _dot(a, w_ref[...].astype(BF16)) + b_ref[...]


def _modulation(cond, w_mod, b_mod):
    R = cond.shape[0]
    tn = 512
    nmod = N_MOD * D_MODEL
    return pl.pallas_call(
        _mod_kernel,
        grid=(DEPTH, nmod // tn),
        in_specs=[
            pl.BlockSpec((R, D_MODEL), lambda l, j: (0, 0)),
            pl.BlockSpec((None, D_MODEL, tn), lambda l, j: (l, 0, j)),
            pl.BlockSpec((None, 1, tn), lambda l, j: (l, 0, j)),
        ],
        out_specs=pl.BlockSpec((None, R, tn), lambda l, j: (l, 0, j)),
        out_shape=jax.ShapeDtypeStruct((DEPTH, R, nmod), F32),
        compiler_params=_cparams("parallel", "parallel"),
        name="adaln_mod",
    )(cond, w_mod, b_mod.reshape(DEPTH, 1, nmod))


def _inproj_kernel(x_ref, mod_ref, g_ref, w_ref, o_ref, h_ref):
    @pl.when(pl.program_id(1) == 0)
    def _():
        x = x_ref[...]
        y = x * lax.rsqrt(jnp.mean(x * x, axis=-1, keepdims=True) + EPS) * g_ref[...]
        h_ref[...] = (y * (1.0 + mod_ref[1:2, :]) + mod_ref[0:1, :]).astype(BF16)

    o_ref[...] = _dot(h_ref[...], w_ref[...]).astype(o_ref.dtype)


def _inproj(x2d, mod, g, w, out_dtype, rows_per_mod, tm, tn):
    T = x2d.shape[0]
    N = w.shape[1]
    return pl.pallas_call(
        _inproj_kernel,
        grid=(T // tm, N // tn),
        in_specs=[
            pl.BlockSpec((tm, D_MODEL), lambda i, j: (i, 0)),
            pl.BlockSpec((None, N_MOD, D_MODEL), lambda i, j: ((i * tm) // rows_per_mod, 0, 0)),
            pl.BlockSpec((1, D_MODEL), lambda i, j: (0, 0)),
            pl.BlockSpec((D_MODEL, tn), lambda i, j: (0, j)),
        ],
        out_specs=pl.BlockSpec((tm, tn), lambda i, j: (i, j)),
        out_shape=jax.ShapeDtypeStruct((T, N), out_dtype),
        scratch_shapes=[pltpu.VMEM((tm, D_MODEL), BF16)],
        compiler_params=_cparams("parallel", "arbitrary"),
        name="norm_inproj",
    )(x2d, mod, g, w)


def _mlstm_local(c, q_ref, k_ref, v_ref, gate_ref, pr_ref, bb_ref, mb_ref, kv_ref, rp_ref, L):
    scale = DK_M ** -0.5
    ti = lax.broadcasted_iota(jnp.int32, (L, L), 0)
    si = lax.broadcasted_iota(jnp.int32, (L, L), 1)
    sub = lax.broadcasted_iota(jnp.int32, (8, LANES), 0)
    rows = pl.ds(pl.multiple_of(c * L, L), L)
    q = q_ref[rows, :]
    v_ext = jnp.concatenate([v_ref[rows, :], jnp.ones((L, LANES), BF16)], axis=1)
    k_t = k_ref[rows, :].astype(F32).T
    qk = _dot(q, k_t.astype(BF16)) * scale
    for d in range(2):
        rev = d == 1
        mask = (si >= ti) if rev else (si <= ti)
        i_row = gate_ref[d, 0, pl.ds(c, 1), :]
        f_row = gate_ref[d, 1, pl.ds(c, 1), :]
        b_row = gate_ref[d, 2, pl.ds(c, 1), :]
        b_col = jnp.sum(jnp.where(mask, f_row, 0.0), axis=1, keepdims=True)
        log_d = jnp.where(mask, b_col + (i_row - b_row), -jnp.inf)
        m_loc = jnp.max(log_d, axis=1, keepdims=True)
        smat = qk * jnp.exp(log_d - m_loc)
        pr_ref[d, rows, :] = _dot(smat.astype(BF16), v_ext)
        bb_ref[d, rows, :] = jnp.broadcast_to(b_col, (L, LANES))
        mb_ref[d, rows, :] = jnp.broadcast_to(m_loc, (L, LANES))
        b_last = jnp.sum(f_row, axis=1, keepdims=True)
        ls_row = b_last - b_row + i_row
        m2 = jnp.max(ls_row, axis=1, keepdims=True)
        kw_t = (k_t * jnp.exp(ls_row - m2)).astype(BF16)
        kv_ref[d, c] = scale * _dot(kw_t, v_ext)
        rp_ref[d, c] = jnp.where(sub == 0, b_last, m2)


def _mlstm_carry(c, d, carry, q_ref, pr_ref, bb_ref, mb_ref, kv_ref, rp_ref, h_ref, L):
    cn, m = carry
    two = lambda x: jnp.concatenate([x, x], axis=1)
    rows = pl.ds(pl.multiple_of(c * L, L), L)
    bb = bb_ref[d, rows, :]
    mb = mb_ref[d, rows, :]
    m_t = jnp.maximum(bb + m, mb)
    a_int = jnp.exp(bb + m - m_t)
    e_loc = jnp.exp(mb - m_t)
    nd = two(a_int) * _dot(q_ref[rows, :], cn.astype(BF16)) + two(e_loc) * pr_ref[d, rows, :]
    h_ref[d, rows, :] = nd[:, :DV_M] / jnp.maximum(jnp.abs(nd[:, DV_M:]), jnp.exp(-m_t))
    rp = rp_ref[d, c]
    b_last, m2 = rp[0:1, :], rp[1:2, :]
    m_new = jnp.maximum(b_last + m, m2)
    a_c = jnp.exp(b_last + m - m_new)
    e2 = jnp.exp(m2 - m_new)
    return two(a_c) * cn + two(e2) * kv_ref[d, c], m_new


def _mlstm_gate_rows(gr_ref, br_ref, gate_ref, L):
    ui = lax.broadcasted_iota(jnp.int32, (L, L), 0)
    si = lax.broadcasted_iota(jnp.int32, (L, L), 1)
    for d in range(2):
        src = ((ui >= si) if d == 1 else (ui <= si)).astype(BF16)
        f = _log_sigmoid(gr_ref[2 + d] + br_ref[2 + d])
        f_hi = f.astype(BF16)
        f_r1 = f - f_hi.astype(F32)
        f_mid = f_r1.astype(BF16)
        f_lo = (f_r1 - f_mid.astype(F32)).astype(BF16)
        gate_ref[d, 0] = gr_ref[d] + br_ref[d]
        gate_ref[d, 1] = f
        gate_ref[d, 2] = _dot(f_hi, src) + _dot(f_mid, src) + _dot(f_lo, src)


def _mlstm_kernel(q_ref, k_ref, v_ref, og_ref, gr_ref, br_ref, c0_ref, n0_ref, m0_ref, hn_ref,
                  y_ref, c_out_ref, n_out_ref, m_out_ref,
                  gate_ref, pr_ref, bb_ref, mb_ref, kv_ref, rp_ref, h_ref, *, L, S):
    nch = S // L
    _mlstm_gate_rows(gr_ref, br_ref, gate_ref, L)

    def local(ci, carry):
        _mlstm_local(ci, q_ref, k_ref, v_ref, gate_ref, pr_ref, bb_ref, mb_ref, kv_ref, rp_ref, L)
        return carry

    lax.fori_loop(0, nch, local, 0, unroll=2)
    step = functools.partial(_mlstm_carry, q_ref=q_ref, pr_ref=pr_ref, bb_ref=bb_ref, mb_ref=mb_ref,
                             kv_ref=kv_ref, rp_ref=rp_ref, h_ref=h_ref, L=L)

    def body(ci, carry):
        return step(ci, 0, carry[0]), step(nch - 1 - ci, 1, carry[1])

    def init(d):
        n_rep = jnp.broadcast_to(n0_ref[d], (DK_M, DK_M)).T
        return jnp.concatenate([c0_ref[d], n_rep], axis=1), m0_ref[d]

    fin = lax.fori_loop(0, nch, body, (init(0), init(1)), unroll=2)
    for d in range(2):
        cn, m = fin[d]
        c_out_ref[d] = cn[:, :DV_M]
        n_out_ref[d] = cn[:, DV_M:].T[0:1, :]
        m_out_ref[d] = m

    hm = h_ref[0] + h_ref[1]
    y = hm * lax.rsqrt(jnp.mean(hm * hm, axis=-1, keepdims=True) + EPS) * hn_ref[...]
    y_ref[...] = (y * _sigmoid(og_ref[...].astype(F32))).astype(y_ref.dtype)


def _mlstm(a16, grow, brow, c0, n0, m0, hnorm, B, S):
    L = min(MLSTM_CHUNK, S)
    nch = S // L
    cb = lambda off: off // LANES
    kern = functools.partial(_mlstm_kernel, L=L, S=S)
    return pl.pallas_call(
        kern,
        grid=(B, H_M),
        in_specs=[
            pl.BlockSpec((S, LANES), lambda b, h: (b, cb(A16_MQ) + h)),
            pl.BlockSpec((S, LANES), lambda b, h: (b, cb(A16_MK) + h)),
            pl.BlockSpec((S, LANES), lambda b, h: (b, cb(A16_MV) + h)),
            pl.BlockSpec((S, LANES), lambda b, h: (b, cb(A16_MO) + h)),
            pl.BlockSpec((None, 4, None, nch, L), lambda b, h: (h, 0, b, 0, 0)),
            pl.BlockSpec((None, 4, 1, 1), lambda b, h: (h, 0, 0, 0)),
            pl.BlockSpec((None, 2, None, DK_M, DV_M), lambda b, h: (b, 0, h, 0, 0)),
            pl.BlockSpec((None, 2, None, 1, DK_M), lambda b, h: (b, 0, h, 0, 0)),
            pl.BlockSpec((None, 2, None, 1, LANES), lambda b, h: (b, 0, h, 0, 0)),
            pl.BlockSpec((1, LANES), lambda b, h: (0, h)),
        ],
        out_specs=[
            pl.BlockSpec((S, LANES), lambda b, h: (b, h)),
            pl.BlockSpec((None, 2, None, DK_M, DV_M), lambda b, h: (b, 0, h, 0, 0)),
            pl.BlockSpec((None, 2, None, 1, DK_M), lambda b, h: (b, 0, h, 0, 0)),
            pl.BlockSpec((None, 2, None, 1, LANES), lambda b, h: (b, 0, h, 0, 0)),
        ],
        out_shape=[
            jax.ShapeDtypeStruct((B * S, BRANCH_W), BF16),
            jax.ShapeDtypeStruct((B, 2, H_M, DK_M, DV_M), F32),
            jax.ShapeDtypeStruct((B, 2, H_M, 1, DK_M), F32),
            jax.ShapeDtypeStruct((B, 2, H_M, 1, LANES), F32),
        ],
        scratch_shapes=[pltpu.VMEM((2, 3, nch, L), F32),
                        pltpu.VMEM((2, S, 2 * DV_M), F32), pltpu.VMEM((2, S, LANES), F32),
                        pltpu.VMEM((2, S, LANES), F32), pltpu.VMEM((2, nch, DK_M, 2 * DV_M), F32),
                        pltpu.VMEM((2, nch, 8, LANES), F32), pltpu.VMEM((2, S, DV_M), F32)],
        compiler_params=_cparams("parallel", "parallel"),
        name="mlstm",
    )(a16, a16, a16, a16, grow, brow, c0, n0, m0, hnorm)


def _gla_local(c, q2_ref, k2_ref, v_ref, la_ref, oa_ref, qt_ref, u_ref, dec_ref, L):
    nb = L // GLA_SUB
    ti = lax.broadcasted_iota(jnp.int32, (L, L), 0)
    si = lax.broadcasted_iota(jnp.int32, (L, L), 1)
    row_blk = lax.broadcasted_iota(jnp.int32, (L, LANES), 0) // GLA_SUB
    lo_half = lax.broadcasted_iota(jnp.int32, (L, LANES), 1) < DK_G
    eye = (lax.broadcasted_iota(jnp.int32, (DK_G, LANES), 0)
           == lax.broadcasted_iota(jnp.int32, (DK_G, LANES), 1))
    rows = pl.ds(pl.multiple_of(c * L, L), L)
    q2 = q2_ref[rows, :]
    k2 = k2_ref[rows, :]
    v = v_ref[rows, :]
    row = lax.broadcasted_iota(jnp.int32, (L, LANES), 0)
    for d in range(2):
        rev = d == 1
        mask = (si >= ti) if rev else (si <= ti)
        g2 = la_ref[d, rows, :]
        step = 1
        while step < L:
            if rev:
                g2 = g2 + jnp.where(row < L - step, pltpu.roll(g2, L - step, 0), 0.0)
            else:
                g2 = g2 + jnp.where(row >= step, pltpu.roll(g2, step, 0), 0.0)
            step *= 2
        qt_ref[d, rows, :] = (q2 * jnp.exp(g2))[:, :DK_G].astype(BF16)
        a_parts, b_parts = [], []
        for p in range(nb // 2):
            ia, ib = 2 * p, 2 * p + 1
            ra = ia * GLA_SUB + (GLA_SUB - 1 if rev else 0)
            rb = ib * GLA_SUB + (GLA_SUB - 1 if rev else 0)
            ref2 = jnp.where(lo_half, g2[ra:ra + 1, :], g2[rb:rb + 1, :])
            blk = jnp.where(lo_half, ia, ib)
            in_blk = row_blk == blk
            key_ok = (row_blk >= blk) if rev else (row_blk <= blk)
            a_parts.append(jnp.where(in_blk, q2 * jnp.exp(jnp.minimum(g2 - ref2, 0.0)), 0.0))
            b_parts.append(
                jnp.where(key_ok, k2 * jnp.exp(jnp.minimum(ref2 - g2, GLA_EXP_CLAMP)), 0.0))
        a_big = jnp.concatenate(a_parts, axis=1).astype(BF16)
        b_big = jnp.concatenate(b_parts, axis=1).astype(BF16)
        att = jnp.where(mask, _dot_nt(a_big, b_big), 0.0)
        oa_ref[d, rows, :] = _dot(att.astype(BF16), v)
        gl_row = 0 if rev else L - 1
        glast = g2[gl_row:gl_row + 1, :]
        kd = (k2 * jnp.exp(glast - g2))[:, :DK_G]
        u_ref[d, c] = _dot_tn(kd.astype(BF16), v)
        glast_col = jnp.sum(jnp.where(eye, glast, 0.0), axis=1, keepdims=True)
        dec_ref[d, c] = jnp.broadcast_to(jnp.exp(glast_col), (DK_G, DV_G))


def _gla_kernel(qk_ref, v_ref, sm_ref, wup_ref, bup_ref, s0_ref, gr_ref, hn_ref,
                y_ref, s_out_ref, la_ref, q2_ref, k2_ref, oa_ref, oi_ref, qt_ref, u_ref, dec_ref,
                *, L, S):
    nch = S // L
    sm = sm_ref[...].astype(BF16)
    for d in range(2):
        la_ref[d] = _log_sigmoid(_dot(sm, wup_ref[d]) + bup_ref[d]) * (1.0 / GLA_TAU)
    qk = qk_ref[...].astype(F32)
    qk_sw = pltpu.roll(qk, DK_G, 1)
    lo_half = lax.broadcasted_iota(jnp.int32, qk.shape, 1) < DK_G
    q2_ref[...] = jnp.where(lo_half, qk, qk_sw) * (DK_G ** -0.5)
    k2_ref[...] = jnp.where(lo_half, qk_sw, qk)

    def local(ci, carry):
        _gla_local(ci, q2_ref, k2_ref, v_ref, la_ref, oa_ref, qt_ref, u_ref, dec_ref, L)
        return carry

    lax.fori_loop(0, nch, local, 0, unroll=4)

    def body(ci, carry):
        out = []
        for d, c in ((0, ci), (1, nch - 1 - ci)):
            rows = pl.ds(pl.multiple_of(c * L, L), L)
            st = carry[d]
            oi_ref[d, rows, :] = _dot(qt_ref[d, rows, :], st.astype(BF16))
            out.append(dec_ref[d, c] * st + u_ref[d, c])
        return tuple(out)

    st_f, st_b = lax.fori_loop(0, nch, body, (s0_ref[0], s0_ref[1]), unroll=2)
    s_out_ref[0] = st_f
    s_out_ref[1] = st_b

    og = (oa_ref[0] + oi_ref[0]) + (oa_ref[1] + oi_ref[1])
    y = og * lax.rsqrt(jnp.mean(og * og, axis=-1, keepdims=True) + EPS) * hn_ref[...]
    gr = gr_ref[...].astype(F32)
    y_ref[...] = (y * (gr * _sigmoid(gr))).astype(y_ref.dtype)


def _gla(a32, a16, wup, bup, s0, hnorm, B, S):
    L = min(GLA_CHUNK, S)
    nch = S // L
    cb = lambda off: off // LANES
    kern = functools.partial(_gla_kernel, L=L, S=S)
    return pl.pallas_call(
        kern,
        grid=(B, H_G),
        in_specs=[
            pl.BlockSpec((S, LANES), lambda b, h: (b, cb(A16_GQK) + h)),
            pl.BlockSpec((S, LANES), lambda b, h: (b, cb(A16_GV) + h)),
            pl.BlockSpec((S, LANES), lambda b, h: (b, cb(A32_SM))),
            pl.BlockSpec((None, 2, LANES, LANES), lambda b, h: (h, 0, 0, 0)),
            pl.BlockSpec((None, 2, 1, LANES), lambda b, h: (h, 0, 0, 0)),
            pl.BlockSpec((None, 2, None, DK_G, DV_G), lambda b, h: (b, 0, h, 0, 0)),
            pl.BlockSpec((S, LANES), lambda b, h: (b, cb(A16_GR) + h)),
            pl.BlockSpec((1, LANES), lambda b, h: (0, h)),
        ],
        out_specs=[
            pl.BlockSpec((S, LANES), lambda b, h: (b, h)),
            pl.BlockSpec((None, 2, None, DK_G, DV_G), lambda b, h: (b, 0, h, 0, 0)),
        ],
        out_shape=[
            jax.ShapeDtypeStruct((B * S, BRANCH_W), BF16),
            jax.ShapeDtypeStruct((B, 2, H_G, DK_G, DV_G), F32),
        ],
        scratch_shapes=[pltpu.VMEM((2, S, LANES), F32), pltpu.VMEM((S, LANES), F32),
                        pltpu.VMEM((S, LANES), F32), pltpu.VMEM((2, S, DV_G), F32),
                        pltpu.VMEM((2, S, DV_G), F32), pltpu.VMEM((2, S, DK_G), BF16),
                        pltpu.VMEM((2, nch, DK_G, DV_G), F32), pltpu.VMEM((2, nch, DK_G, DV_G), F32)],
        compiler_params=_cparams("parallel", "parallel"),
        name="gla",
    )(a16, a16, a32, wup, bup, s0, a16, hnorm)


def _rope(x, cos, sin_signed):
    lane = lax.broadcasted_iota(jnp.int32, x.shape, 1)
    first = (lane % DQK_D) < (DQK_D // 2)
    partner = jnp.where(first, pltpu.roll(x, LANES - DQK_D // 2, 1), pltpu.roll(x, DQK_D // 2, 1))
    return x * cos + partner * sin_signed


def _attn_kernel(*refs, S, P, TQ, lam_init, has_ctx):
    if has_ctx:
        (q_ref, k_ref, v_ref, ck_ref, cv_ref, cos_ref, sin_ref, lam_ref, hn_ref,
         y_ref, kk_ref, vv_ref) = refs
    else:
        q_ref, k_ref, v_ref, lam_ref, hn_ref, y_ref, kk_ref, vv_ref = refs
    qi = pl.program_id(2)

    @pl.when(qi == 0)
    def _():
        k = k_ref[...]
        if has_ctx:
            k = _rope(k, cos_ref[...], sin_ref[...])
            kk_ref[S:S + P, :] = ck_ref[...].astype(BF16)
            vv_ref[S:S + P, :] = cv_ref[...].astype(BF16)
        kk_ref[0:S, :] = k.astype(BF16)
        vv_ref[0:S, :] = v_ref[...].astype(BF16)

    q = q_ref[...]
    if has_ctx:
        r0 = pl.multiple_of(qi * TQ, TQ)
        q = _rope(q, cos_ref[pl.ds(r0, TQ), :], sin_ref[pl.ds(r0, TQ), :])
    q = q * (DQK_D ** -0.5 * math.log2(math.e))
    lane = lax.broadcasted_iota(jnp.int32, q.shape, 1)
    kk = kk_ref[...]
    vv = vv_ref[...]
    lv = lam_ref[...]
    lam = (jnp.exp(jnp.sum(lv[0:1, :] * lv[1:2, :], axis=-1, keepdims=True))
           - jnp.exp(jnp.sum(lv[2:3, :] * lv[3:4, :], axis=-1, keepdims=True)) + lam_init)
    es, ls = [], []
    for comp in range(2):
        sel = (lane < DQK_D) if comp == 0 else (lane >= DQK_D)
        s = _dot_nt(jnp.where(sel, q, 0.0).astype(BF16), kk)
        e = jnp.exp2(s - jnp.max(s, axis=-1, keepdims=True))
        es.append(e)
        ls.append(jnp.sum(e, axis=-1, keepdims=True))
    w = es[0] - es[1] * (lam * ls[0] / ls[1])
    o = _dot(w.astype(BF16), vv) * (1.0 / ls[0])
    y = o * lax.rsqrt(jnp.mean(o * o, axis=-1, keepdims=True) + EPS) * hn_ref[...]
    y_ref[...] = (y * (1.0 - lam_init)).astype(y_ref.dtype)


def _attn(a32, lamv, hnorm, B, S, lam_init, ctx=None):
    TQ = min(256, S)
    nq = S // TQ
    has_ctx = ctx is not None
    P = ctx[0].shape[2] if has_ctx else 0
    cb = lambda off: off // LANES
    kern = functools.partial(_attn_kernel, S=S, P=P, TQ=TQ, lam_init=lam_init, has_ctx=has_ctx)
    in_specs = [
        pl.BlockSpec((TQ, LANES), lambda b, h, i: (b * nq + i, cb(A32_DQ) + h)),
        pl.BlockSpec((S, LANES), lambda b, h, i: (b, cb(A32_DK) + h)),
        pl.BlockSpec((S, LANES), lambda b, h, i: (b, cb(A32_DV) + h)),
    ]
    args = [a32, a32, a32]
    if has_ctx:
        ck, cv, layer, cos, sin = ctx
        in_specs += [
            pl.BlockSpec((None, None, P, LANES), lambda b, h, i: (b, layer, 0, h)),
            pl.BlockSpec((None, None, P, LANES), lambda b, h, i: (b, layer, 0, h)),
            pl.BlockSpec((S, LANES), lambda b, h, i: (0, 0)),
            pl.BlockSpec((S, LANES), lambda b, h, i: (0, 0)),
        ]
        args += [ck, cv, cos, sin]
    in_specs += [
        pl.BlockSpec((4, DQK_D), lambda b, h, i: (0, 0)),
        pl.BlockSpec((1, LANES), lambda b, h, i: (0, h)),
    ]
    args += [lamv, hnorm]
    return pl.pallas_call(
        kern,
        grid=(B, H_D, nq),
        in_specs=in_specs,
        out_specs=pl.BlockSpec((TQ, LANES), lambda b, h, i: (b * nq + i, h)),
        out_shape=jax.ShapeDtypeStruct((B * S, BRANCH_W), BF16),
        scratch_shapes=[pltpu.VMEM((S + P, LANES), BF16), pltpu.VMEM((S + P, LANES), BF16)],
        compiler_params=_cparams("parallel", "parallel", "arbitrary"),
        name="diff_attn",
    )(*args)


def _conv_kernel(ca_ref, cb_ref, w_ref, g_ref, b_ref, y_ref, pad_ref, cv_ref, *, S):
    ca = ca_ref[...].astype(F32)
    cbv = cb_ref[...].astype(F32)
    zeros = jnp.zeros((CONV_PAD, BRANCH_W), F32)
    pad_ref[0:CONV_PAD, :] = zeros
    pad_ref[CONV_PAD + S:2 * CONV_PAD + S, :] = zeros
    pad_ref[CONV_PAD:CONV_PAD + S, :] = ca * _sigmoid(cbv)
    off = CONV_PAD - CONV_W // 2

    win_rows = CONV_ROWS + 2 * CONV_PAD

    def body(i, carry):
        base = pl.multiple_of(i * CONV_ROWS, CONV_ROWS)
        for lb in range(BRANCH_W // LANES):
            cols = slice(lb * LANES, (lb + 1) * LANES)
            win = pad_ref[pl.ds(base, win_rows), cols]
            acc = jnp.zeros((CONV_ROWS, LANES), F32)
            for r in range(8):
                rolled = win if r == 0 else pltpu.roll(win, win_rows - r, 0)
                for a in range(2 * CONV_PAD // 8):
                    j = 8 * a + r - off
                    if 0 <= j < CONV_W:
                        acc = acc + rolled[8 * a:8 * a + CONV_ROWS, :] * w_ref[j:j + 1, cols]
            cv_ref[:, cols] = acc
        acc = cv_ref[...]
        mu = jnp.mean(acc, axis=-1, keepdims=True)
        xc = acc - mu
        yn = xc * lax.rsqrt(jnp.mean(xc * xc, axis=-1, keepdims=True) + EPS) * g_ref[...] + b_ref[...]
        y_ref[pl.ds(base, CONV_ROWS), :] = (yn * _sigmoid(yn)).astype(y_ref.dtype)
        return carry

    lax.fori_loop(0, S // CONV_ROWS, body, 0)


def _conv(a16, w_dw, ln_g, ln_b, B, S):
    cb = lambda off: off // BRANCH_W
    kern = functools.partial(_conv_kernel, S=S)
    return pl.pallas_call(
        kern,
        grid=(B,),
        in_specs=[
            pl.BlockSpec((S, BRANCH_W), lambda b: (b, cb(A16_CA))),
            pl.BlockSpec((S, BRANCH_W), lambda b: (b, cb(A16_CB))),
            pl.BlockSpec((CONV_W + 1, BRANCH_W), lambda b: (0, 0)),
            pl.BlockSpec((1, BRANCH_W), lambda b: (0, 0)),
            pl.BlockSpec((1, BRANCH_W), lambda b: (0, 0)),
        ],
        out_specs=pl.BlockSpec((S, BRANCH_W), lambda b: (b, 0)),
        out_shape=jax.ShapeDtypeStruct((B * S, BRANCH_W), BF16),
        scratch_shapes=[pltpu.VMEM((S + 2 * CONV_PAD, BRANCH_W), F32),
                        pltpu.VMEM((CONV_ROWS, BRANCH_W), F32)],
        compiler_params=_cparams("parallel"),
        name="glu_conv_ln",
    )(a16, a16, w_dw, ln_g, ln_b)


def _merge_kernel(x_ref, mod_ref, ym_ref, yd_ref, yg_ref, yc_ref, g0_ref, g1_ref, g2_ref, g3_ref,
                  wb_ref, wo_ref, n2_ref, wr_ref, x1_ref, h2_ref, route_ref):
    ys = (ym_ref, yd_ref, yg_ref, yc_ref)
    gs = (g0_ref, g1_ref, g2_ref, g3_ref)
    merged = None
    for nbr in range(N_BRANCH):
        br = _dot(ys[nbr][...], wb_ref[nbr])
        term = _sigmoid(gs[nbr][...].astype(F32)) * br
        merged = term if merged is None else merged + term
    out = _dot(merged.astype(BF16), wo_ref[...])
    x1 = x_ref[...] + mod_ref[2:3, :] * out
    x1_ref[...] = x1
    y = x1 * lax.rsqrt(jnp.mean(x1 * x1, axis=-1, keepdims=True) + EPS) * n2_ref[...]
    h2 = y * (1.0 + mod_ref[4:5, :]) + mod_ref[3:4, :]
    h2_ref[...] = h2.reshape(h2_ref.shape)
    wr = wr_ref[...]
    h_hi, w_hi = h2.astype(BF16), wr.astype(BF16)
    h_lo = (h2 - h_hi.astype(F32)).astype(BF16)
    w_lo = (wr - w_hi.astype(F32)).astype(BF16)
    logits = _dot(h_hi, w_hi) + (_dot(h_hi, w_lo) + _dot(h_lo, w_hi))
    lane = lax.broadcasted_iota(jnp.int32, logits.shape, 1)
    neg = -jnp.inf
    big = jnp.int32(LANES)
    is_g = lane < N_GROUPS
    gl = jnp.where(is_g, logits, neg)
    gmax = jnp.max(gl, axis=-1, keepdims=True)
    gidx = jnp.min(jnp.where(is_g & (gl == gmax), lane, big), axis=-1, keepdims=True)
    g_p = 1.0 / jnp.sum(jnp.where(is_g, jnp.exp(gl - gmax), 0.0), axis=-1, keepdims=True)
    e_lane = lane - N_GROUPS
    in_grp = (e_lane >= 0) & (e_lane < N_EXPERTS) & ((e_lane // EXPERTS_PER_GROUP) == gidx)
    el = jnp.where(in_grp, logits, neg)
    v1 = jnp.max(el, axis=-1, keepdims=True)
    i1 = jnp.min(jnp.where(in_grp & (el == v1), lane, big), axis=-1, keepdims=True)
    el2 = jnp.where(lane == i1, neg, el)
    v2 = jnp.max(el2, axis=-1, keepdims=True)
    i2 = jnp.min(jnp.where(in_grp & (lane != i1) & (el2 == v2), lane, big), axis=-1, keepdims=True)
    e2 = jnp.exp(v2 - v1)
    w1 = g_p / (1.0 + e2)
    w2 = g_p * e2 / (1.0 + e2)
    id1 = (i1 - N_GROUPS).astype(F32)
    id2 = (i2 - N_GROUPS).astype(F32)
    route_ref[...] = jnp.where(lane == 0, id1, jnp.where(lane == 1, id2,
                               jnp.where(lane == 2, w1, jnp.where(lane == 3, w2, 0.0))))


def _merge(x2d, mod, a16, ym, yd, yg, yc, wb, wo, n2, wr, rows_per_mod, tm):
    T = x2d.shape[0]
    gcb = A16_GATE // D_MODEL
    row = lambda i: (i, 0)
    return pl.pallas_call(
        _merge_kernel,
        grid=(T // tm,),
        in_specs=[
            pl.BlockSpec((tm, D_MODEL), row),
            pl.BlockSpec((None, N_MOD, D_MODEL), lambda i: ((i * tm) // rows_per_mod, 0, 0)),
            pl.BlockSpec((tm, BRANCH_W), row),
            pl.BlockSpec((tm, BRANCH_W), row),
            pl.BlockSpec((tm, BRANCH_W), row),
            pl.BlockSpec((tm, BRANCH_W), row),
            pl.BlockSpec((tm, D_MODEL), lambda i: (i, gcb + 0)),
            pl.BlockSpec((tm, D_MODEL), lambda i: (i, gcb + 1)),
            pl.BlockSpec((tm, D_MODEL), lambda i: (i, gcb + 2)),
            pl.BlockSpec((tm, D_MODEL), lambda i: (i, gcb + 3)),
            pl.BlockSpec((N_BRANCH, BRANCH_W, D_MODEL), lambda i: (0, 0, 0)),
            pl.BlockSpec((D_MODEL, D_MODEL), lambda i: (0, 0)),
            pl.BlockSpec((1, D_MODEL), lambda i: (0, 0)),
            pl.BlockSpec((D_MODEL, LANES), lambda i: (0, 0)),
        ],
        out_specs=[
            pl.BlockSpec((tm, D_MODEL), row),
            pl.BlockSpec((tm, TOK_SUB, LANES), lambda i: (i, 0, 0)),
            pl.BlockSpec((tm, LANES), row),
        ],
        out_shape=[
            jax.ShapeDtypeStruct((T, D_MODEL), F32),
            jax.ShapeDtypeStruct((T, TOK_SUB, LANES), F32),
            jax.ShapeDtypeStruct((T, LANES), F32),
        ],
        compiler_params=_cparams("parallel"),
        name="merge_outproj_route",
    )(x2d, mod, ym, yd, yg, yc, a16, a16, a16, a16, wb, wo, n2, wr)


def _gather_rows(idx_ref, src_hbm, dst, sem, n):
    def body(j, carry):
        for u in range(ROW_DMA_UNROLL):
            r = j * ROW_DMA_UNROLL + u
            pltpu.make_async_copy(src_hbm.at[idx_ref[0, r]], dst.at[r], sem).start(priority=u % 2)
        return carry

    lax.fori_loop(0, n // ROW_DMA_UNROLL, body, 0)


def _scatter_rows(idx_ref, src, dst_hbm, sem, n):
    def body(j, carry):
        for u in range(ROW_DMA_UNROLL):
            r = j * ROW_DMA_UNROLL + u
            pltpu.make_async_copy(src.at[r], dst_hbm.at[idx_ref[0, r]], sem).start(priority=u % 2)
        return carry

    lax.fori_loop(0, n // ROW_DMA_UNROLL, body, 0)


def _wait_rows(buf, sem):
    pltpu.make_async_copy(buf, buf, sem).wait()


def _moe_dispatch_kernel(d0_ref, d1_ref, h_ref, xg_in, xg_out, sem, *, tm):
    del xg_in
    _scatter_rows(d0_ref, h_ref, xg_out, sem.at[0], tm)
    _scatter_rows(d1_ref, h_ref, xg_out, sem.at[1], tm)
    _wait_rows(h_ref, sem.at[0])
    _wait_rows(h_ref, sem.at[1])


def _moe_dispatch(h3, dest, n_rows, tm):
    T = h3.shape[0]
    kern = functools.partial(_moe_dispatch_kernel, tm=tm)
    return pl.pallas_call(
        kern,
        grid=(T // tm,),
        in_specs=[
            pl.BlockSpec((None, None, 1, tm), lambda i: (0, i, 0, 0), memory_space=pltpu.SMEM),
            pl.BlockSpec((None, None, 1, tm), lambda i: (1, i, 0, 0), memory_space=pltpu.SMEM),
            pl.BlockSpec((tm, TOK_SUB, LANES), lambda i: (i, 0, 0)),
            pl.BlockSpec(memory_space=pl.ANY),
        ],
        out_specs=pl.BlockSpec(memory_space=pl.ANY),
        out_shape=jax.ShapeDtypeStruct((n_rows, TOK_SUB, LANES), F32),
        input_output_aliases={3: 0},
        scratch_shapes=[pltpu.SemaphoreType.DMA((2,))],
        compiler_params=_cparams("arbitrary"),
        name="moe_dispatch",
    )(dest, dest, h3, jnp.zeros((n_rows, TOK_SUB, LANES), F32))


def _moe_ffn_kernel(te_ref, x_ref, w1_ref, w3_ref, w2_ref, o_ref):
    del te_ref
    x = x_ref[...].reshape(MOE_TILE, D_MODEL).astype(BF16)
    a = _dot(x, w1_ref[...].astype(BF16))
    b = _dot(x, w3_ref[...].astype(BF16))
    s = (a * _sigmoid(a)) * b
    y = _dot(s.astype(BF16), w2_ref[...].astype(BF16))
    o_ref[...] = y.reshape(o_ref.shape)


def _moe_ffn(xg, tile_e, w1, w3, w2):
    ntiles = xg.shape[0] // MOE_TILE
    tile = pl.BlockSpec((MOE_TILE, TOK_SUB, LANES), lambda i, te: (i, 0, 0))
    return pl.pallas_call(
        _moe_ffn_kernel,
        grid_spec=pltpu.PrefetchScalarGridSpec(
            num_scalar_prefetch=1,
            grid=(ntiles,),
            in_specs=[
                tile,
                pl.BlockSpec((None, D_MODEL, D_EXPERT), lambda i, te: (te[i], 0, 0)),
                pl.BlockSpec((None, D_MODEL, D_EXPERT), lambda i, te: (te[i], 0, 0)),
                pl.BlockSpec((None, D_EXPERT, D_MODEL), lambda i, te: (te[i], 0, 0)),
            ],
            out_specs=tile,
        ),
        out_shape=jax.ShapeDtypeStruct(xg.shape, F32),
        compiler_params=_cparams("parallel"),
        name="moe_grouped_experts",
    )(tile_e, xg, w1, w3, w2)


def _moe_combine_kernel(d0_ref, d1_ref, y_hbm, route_ref, x1_ref, mod_ref, fn_ref, o_ref,
                        ga, gb, sem, *, tm, final_norm):
    _gather_rows(d0_ref, y_hbm, ga, sem.at[0], tm)
    _gather_rows(d1_ref, y_hbm, gb, sem.at[1], tm)
    rt = route_ref[...]
    _wait_rows(ga, sem.at[0])
    _wait_rows(gb, sem.at[1])
    y = rt[:, 2:3] * ga[...].reshape(tm, D_MODEL) + rt[:, 3:4] * gb[...].reshape(tm, D_MODEL)
    x2 = x1_ref[...] + mod_ref[5:6, :] * y
    if final_norm:
        x2 = x2 * lax.rsqrt(jnp.mean(x2 * x2, axis=-1, keepdims=True) + EPS) * fn_ref[...]
    o_ref[...] = x2


def _moe_combine(yg, dest, route, x1, mod, fn, rows_per_mod, tm, final_norm):
    T = x1.shape[0]
    kern = functools.partial(_moe_combine_kernel, tm=tm, final_norm=final_norm)
    return pl.pallas_call(
        kern,
        grid=(T // tm,),
        in_specs=[
            pl.BlockSpec((None, None, 1, tm), lambda i: (0, i, 0, 0), memory_space=pltpu.SMEM),
            pl.BlockSpec((None, None, 1, tm), lambda i: (1, i, 0, 0), memory_space=pltpu.SMEM),
            pl.BlockSpec(memory_space=pl.ANY),
            pl.BlockSpec((tm, LANES), lambda i: (i, 0)),
            pl.BlockSpec((tm, D_MODEL), lambda i: (i, 0)),
            pl.BlockSpec((None, N_MOD, D_MODEL), lambda i: ((i * tm) // rows_per_mod, 0, 0)),
            pl.BlockSpec((1, D_MODEL), lambda i: (0, 0)),
        ],
        out_specs=pl.BlockSpec((tm, D_MODEL), lambda i: (i, 0)),
        out_shape=jax.ShapeDtypeStruct((T, D_MODEL), F32),
        scratch_shapes=[pltpu.VMEM((tm, TOK_SUB, LANES), F32), pltpu.VMEM((tm, TOK_SUB, LANES), F32),
                        pltpu.SemaphoreType.DMA((2,))],
        compiler_params=_cparams("arbitrary"),
        name="moe_combine",
    )(dest, dest, yg, route, x1, mod, fn)


def _route_plan(route, T):
    ntiles = (2 * T + N_EXPERTS * (MOE_TILE - 1) + MOE_TILE - 1) // MOE_TILE
    ef = route[:, 0:2].astype(jnp.int32).reshape(-1)
    oh = (ef[:, None] == jnp.arange(N_EXPERTS, dtype=jnp.int32)[None, :]).astype(jnp.int32)
    csum = jnp.cumsum(oh, axis=0)
    rank = jnp.sum((csum - oh) * oh, axis=1)
    counts = csum[-1]
    padded = ((counts + MOE_TILE - 1) // MOE_TILE) * MOE_TILE
    seg_end = jnp.cumsum(padded)
    dest = jnp.sum(oh * (seg_end - padded)[None, :], axis=1) + rank
    tile_row = jnp.arange(ntiles, dtype=jnp.int32) * MOE_TILE
    tile_e = jnp.minimum(jnp.sum((tile_row[:, None] >= seg_end[None, :]).astype(jnp.int32), axis=1),
                         N_EXPERTS - 1)
    return jnp.transpose(dest.reshape(T, 2)), tile_e, ntiles * MOE_TILE


def _split_w_in(w):
    sizes = (H_M * DK_M, H_M * DK_M, H_M * DV_M, H_M * DV_M, 2 * H_M, 2 * H_M,
             H_D * 2 * DQK_D, H_D * 2 * DQK_D, H_D * DV_D,
             H_G * DK_G, H_G * DK_G, H_G * DV_G, 2 * GATE_RANK, H_G * DV_G,
             BRANCH_W, BRANCH_W, N_BRANCH * D_MODEL)
    outs, acc = [], 0
    for s in sizes:
        outs.append(w[:, acc:acc + s])
        acc += s
    return outs


def _pack_layer_params(p):
    (m_q, m_k, m_v, m_o, m_i, m_f, d_q, d_k, d_v, g_q, g_k, g_v, g_a, g_r, c_a, c_b, gate) = \
        _split_w_in(p['w_in'])
    gqk = jnp.concatenate([g_q.reshape(D_MODEL, H_G, DK_G), g_k.reshape(D_MODEL, H_G, DK_G)],
                          axis=2).reshape(D_MODEL, 2 * H_G * DK_G)
    small = jnp.concatenate(
        [m_i, m_f, g_a, jnp.zeros((D_MODEL, LANES - 4 * H_M - 2 * GATE_RANK), F32)], axis=1)
    w16 = jnp.concatenate([m_q, m_k, m_v, m_o, gate, g_r, c_a, c_b, gqk, g_v], axis=1).astype(BF16)
    w32 = jnp.concatenate([d_q, d_k, d_v, small], axis=1).astype(BF16)
    bi = p['b_m_i'].reshape(2, H_M)
    bf = p['b_m_f'].reshape(2, H_M)
    bcol = jnp.stack([bi[0], bi[1], bf[0], bf[1]], axis=-1)
    wup = p['w_gla_up'].reshape(2, GATE_RANK, H_G, DK_G)
    wup_pad = jnp.zeros((H_G, 2, LANES, LANES), F32)
    bup = p['b_gla_gate'].reshape(2, H_G, DK_G)
    for d in range(2):
        blk = jnp.transpose(wup[d], (1, 0, 2))
        blk = jnp.concatenate([blk, blk], axis=-1)
        r0 = SM_GA + d * GATE_RANK
        wup_pad = wup_pad.at[:, d, r0:r0 + GATE_RANK, :].set(blk)
    bup2 = jnp.transpose(jnp.concatenate([bup, bup], axis=-1), (1, 0, 2))[:, :, None, :]
    wr = jnp.concatenate([p['w_group_router'], p['w_expert_router'],
                          jnp.zeros((D_MODEL, LANES - N_GROUPS - N_EXPERTS), F32)], axis=1)
    return dict(
        w16=w16, w32=w32, brow=bcol.reshape(H_M, 4, 1, 1),
        wup=wup_pad.astype(BF16), bup=bup2,
        wdw=jnp.concatenate([p['w_dw'], jnp.zeros((1, BRANCH_W), F32)], axis=0),
        ln_g=p['conv_ln_g'].reshape(1, BRANCH_W), ln_b=p['conv_ln_b'].reshape(1, BRANCH_W),
        hn_m=p['hnorm_m'].reshape(1, BRANCH_W), hn_d=p['hnorm_d'].reshape(1, BRANCH_W),
        hn_g=p['hnorm_g'].reshape(1, BRANCH_W),
        lamv=jnp.stack([p['lam_q1'], p['lam_k1'], p['lam_q2'], p['lam_k2']], axis=0),
        wb=p['w_branch'].astype(BF16), wo=p['w_out'].astype(BF16),
        n1=p['norm1'].reshape(1, D_MODEL), n2=p['norm2'].reshape(1, D_MODEL), wr=wr,
    )


def _rope_tables(S):
    rows = S // GRID_W
    r, col = jnp.meshgrid(jnp.arange(rows, dtype=F32), jnp.arange(GRID_W, dtype=F32), indexing='ij')
    r, col = r.reshape(-1), col.reshape(-1)
    n_freq = DQK_D // 4
    inv = ROPE_BASE ** (-jnp.arange(n_freq, dtype=F32) / n_freq)
    ang = jnp.concatenate([r[:, None] * inv, col[:, None] * inv], axis=-1)
    cos, sin = jnp.cos(ang), jnp.sin(ang)
    cos_t = jnp.tile(cos, (1, LANES // (DQK_D // 2)))
    sin_t = jnp.tile(jnp.concatenate([-sin, sin], axis=-1), (1, LANES // DQK_D))
    return cos_t, sin_t


def _pick_tile(T, cap):
    t = min(T, cap)
    while T % t:
        t //= 2
    return t


def _layer(x2d, mod, pk, B, S, lam_init, ctx, final_norm, fn):
    T = B * S
    rows_per_mod = T // mod.shape[0]
    tm = _pick_tile(rows_per_mod, 1024)
    a16 = _inproj(x2d, mod, pk['n1'], pk['w16'], BF16, rows_per_mod, tm, N_A16 // 4)
    a32 = _inproj(x2d, mod, pk['n1'], pk['w32'], F32, rows_per_mod, tm, N_A32)

    L = min(MLSTM_CHUNK, S)
    sm = a32[:, A32_SM:A32_SM + 4 * H_M]
    grow = jnp.transpose(sm.reshape(B, S // L, L, 4, H_M), (4, 3, 0, 1, 2))
    if ctx is None:
        c0 = jnp.zeros((B, 2, H_M, DK_M, DV_M), F32)
        n0 = jnp.zeros((B, 2, H_M, 1, DK_M), F32)
        m0 = jnp.zeros((B, 2, H_M, 1, LANES), F32)
        s0 = jnp.zeros((B, 2, H_G, DK_G, DV_G), F32)
        attn_ctx = None
    else:
        c0 = ctx['C']
        n0 = ctx['n'][:, :, :, None, :]
        m0 = jnp.broadcast_to(ctx['m'][:, :, :, None, None], (B, 2, H_M, 1, LANES))
        s0 = ctx['S']
        attn_ctx = (ctx['k'], ctx['v'], ctx['layer'], ctx['cos'], ctx['sin'])
    ym, c_f, n_f, m_f = _mlstm(a16, grow, pk['brow'], c0, n0, m0, pk['hn_m'], B, S)
    yd = _attn(a32, pk['lamv'], pk['hn_d'], B, S, lam_init, attn_ctx)
    yg, s_f = _gla(a32, a16, pk['wup'], pk['bup'], s0, pk['hn_g'], B, S)
    yc = _conv(a16, pk['wdw'], pk['ln_g'], pk['ln_b'], B, S)
    x1, h3, route = _merge(x2d, mod, a16, ym, yd, yg, yc, pk['wb'], pk['wo'], pk['n2'], pk['wr'],
                           rows_per_mod, _pick_tile(rows_per_mod, 512))
    dest, tile_e, n_rows = _route_plan(route, T)
    tmd = _pick_tile(T, 512)
    xg = _moe_dispatch(h3, dest.reshape(2, T // tmd, 1, tmd), n_rows, tmd)
    y_grouped = _moe_ffn(xg, tile_e + pk['expert_base'], pk['w1'], pk['w3'], pk['w2'])
    tmc = _pick_tile(rows_per_mod, 256)
    x2 = _moe_combine(y_grouped, dest.reshape(2, T // tmc, 1, tmc), route, x1, mod, fn,
                      rows_per_mod, tmc, final_norm)
    state = None
    if ctx is None:
        state = (a32[:, A32_DK:A32_DK + H_D * 2 * DQK_D].reshape(B, S, H_D, 2 * DQK_D),
                 a32[:, A32_DV:A32_DV + H_D * DV_D].reshape(B, S, H_D, DV_D),
                 c_f, n_f[:, :, :, 0, :], m_f[:, :, :, 0, 0], s_f)
    return x2, state


def kernel(x_prompt, x_sample, c, cache_diff_k, cache_diff_v, state_mlstm_C, state_mlstm_n, state_mlstm_m, state_gla_S, c_ctx, w_mod, b_mod, norm1, w_in, b_m_i, b_m_f, lam_q1, lam_k1, lam_q2, lam_k2, w_gla_up, b_gla_gate, w_dw, conv_ln_g, conv_ln_b, hnorm_m, hnorm_d, hnorm_g, w_branch, w_out, norm2, w_group_router, w_expert_router, w_e1, w_e3, w_e2, final_norm):
    Bp, Sp, _ = x_prompt.shape
    Bs, Ss, _ = x_sample.shape
    P = cache_diff_k.shape[2]
    n_cond = 8 * ((1 + Bs + 7) // 8)
    cond = jnp.concatenate([c_ctx[None, :], c, jnp.zeros((n_cond - 1 - Bs, D_MODEL), F32)], axis=0)
    mod_all = _modulation(cond, w_mod, b_mod).reshape(DEPTH, n_cond, N_MOD, D_MODEL)
    cos_t, sin_t = _rope_tables(Ss)
    ck4 = cache_diff_k.reshape(Bs, DEPTH, P, H_D * 2 * DQK_D)
    cv4 = cache_diff_v.reshape(Bs, DEPTH, P, H_D * DV_D)
    fn = final_norm.reshape(1, D_MODEL)
    yp = x_prompt.reshape(Bp * Sp, D_MODEL)
    ys = x_sample.reshape(Bs * Ss, D_MODEL)
    states = []
    for l in range(DEPTH):
        p = {'w_in': w_in[l], 'b_m_i': b_m_i[l], 'b_m_f': b_m_f[l], 'lam_q1': lam_q1[l],
             'lam_k1': lam_k1[l], 'lam_q2': lam_q2[l], 'lam_k2': lam_k2[l],
             'w_gla_up': w_gla_up[l], 'b_gla_gate': b_gla_gate[l], 'w_dw': w_dw[l],
             'conv_ln_g': conv_ln_g[l], 'conv_ln_b': conv_ln_b[l], 'hnorm_m': hnorm_m[l],
             'hnorm_d': hnorm_d[l], 'hnorm_g': hnorm_g[l], 'w_branch': w_branch[l],
             'w_out': w_out[l], 'norm1': norm1[l], 'norm2': norm2[l],
             'w_group_router': w_group_router[l], 'w_expert_router': w_expert_router[l]}
        pk = _pack_layer_params(p)
        pk.update(w1=w_e1.reshape(DEPTH * N_EXPERTS, D_MODEL, D_EXPERT),
                  w3=w_e3.reshape(DEPTH * N_EXPERTS, D_MODEL, D_EXPERT),
                  w2=w_e2.reshape(DEPTH * N_EXPERTS, D_EXPERT, D_MODEL), expert_base=l * N_EXPERTS)
        lam_init = 0.8 - 0.6 * math.exp(-0.3 * l)
        last = l == DEPTH - 1
        yp, st = _layer(yp, mod_all[l, 0:1], pk, Bp, Sp, lam_init, None, last, fn)
        states.append(st)
        ctx = {'k': ck4, 'v': cv4, 'layer': l, 'cos': cos_t, 'sin': sin_t,
               'C': state_mlstm_C[:, l], 'n': state_mlstm_n[:, l], 'm': state_mlstm_m[:, l],
               'S': state_gla_S[:, l]}
        ys, _ = _layer(ys, mod_all[l, 1:1 + Bs], pk, Bs, Ss, lam_init, ctx, last, fn)
    stack = lambda i: jnp.stack([s[i] for s in states], axis=1)
    return (yp.reshape(Bp, Sp, D_MODEL), ys.reshape(Bs, Ss, D_MODEL),
            stack(0), stack(1), stack(2), stack(3), stack(4), stack(5))
```

```python
import functools
import math

import jax
import jax.numpy as jnp
from jax import lax
from jax.experimental import pallas as pl
from jax.experimental.pallas import tpu as pltpu

F32 = jnp.float32
BF16 = jnp.bfloat16

D_MODEL = 1024
DEPTH = 2
GRID_W = 64
BRANCH_W = 512
N_BRANCH = 4
H_M, DK_M, DV_M = 4, 128, 128
H_D, DQK_D, DV_D = 4, 64, 128
H_G, DK_G, DV_G = 4, 64, 128
GATE_RANK = 16
GLA_TAU = 16.0
CONV_W = 31
N_GROUPS, EXPERTS_PER_GROUP, D_EXPERT = 4, 4, 512
N_EXPERTS = N_GROUPS * EXPERTS_PER_GROUP
ROPE_BASE = 10000.0
EPS = 1e-6
N_MOD = 6

LANES = 128
VMEM_LIMIT = 48 * 1024 * 1024

A16_MQ, A16_MK, A16_MV, A16_MO = 0, 512, 1024, 1536
A16_GATE, A16_GR, A16_CA, A16_CB = 2048, 6144, 6656, 7168
A16_GQK, A16_GV = 7680, 8192
N_A16 = 8704
A32_DQ, A32_DK, A32_DV, A32_SM = 0, 512, 1024, 1536
N_A32 = 1664
SM_MI, SM_MF, SM_GA = 0, 8, 16

MLSTM_CHUNK = 128
GLA_CHUNK = 64
GLA_SUB = 16
GLA_EXP_CLAMP = 80.0
CONV_ROWS = 64
CONV_PAD = 16
TOK_SUB = D_MODEL // LANES
ROW_DMA_UNROLL = 8


def _cparams(*sem):
    return pltpu.CompilerParams(dimension_semantics=sem, vmem_limit_bytes=VMEM_LIMIT)


def _log_sigmoid(x):
    return jnp.minimum(x, 0.0) - jnp.log1p(jnp.exp(-jnp.abs(x)))


def _sigmoid(x):
    return 1.0 / (1.0 + jnp.exp(-x))


def _dot(a, b):
    return jnp.dot(a, b, preferred_element_type=F32)


def _dot_nt(a, b):
    return lax.dot_general(a, b, (((1,), (1,)), ((), ())), preferred_element_type=F32)


def _dot_tn(a, b):
    return lax.dot_general(a, b, (((0,), (0,)), ((), ())), preferred_element_type=F32)


def _mod_kernel(c_ref, w_ref, b_ref, o_ref):
    c = c_ref[...]
    a = (c * _sigmoid(c)).astype(BF16)
    o_ref[...] = _dot(a, w_ref[...].astype(BF16)) + b_ref[...]


def _modulation(cond, w_mod, b_mod):
    R = cond.shape[0]
    tn = 512
    nmod = N_MOD * D_MODEL
    return pl.pallas_call(
        _mod_kernel,
        grid=(DEPTH, nmod // tn),
        in_specs=[
            pl.BlockSpec((R, D_MODEL), lambda l, j: (0, 0)),
            pl.BlockSpec((None, D_MODEL, tn), lambda l, j: (l, 0, j)),
            pl.BlockSpec((None, 1, tn), lambda l, j: (l, 0, j)),
        ],
        out_specs=pl.BlockSpec((None, R, tn), lambda l, j: (l, 0, j)),
        out_shape=jax.ShapeDtypeStruct((DEPTH, R, nmod), F32),
        compiler_params=_cparams("parallel", "parallel"),
        name="adaln_mod",
    )(cond, w_mod, b_mod.reshape(DEPTH, 1, nmod))


def _inproj_kernel(x_ref, mod_ref, g_ref, w_ref, o_ref, h_ref):
    @pl.when(pl.program_id(1) == 0)
    def _():
        x = x_ref[...]
        y = x * lax.rsqrt(jnp.mean(x * x, axis=-1, keepdims=True) + EPS) * g_ref[...]
        h_ref[...] = (y * (1.0 + mod_ref[1:2, :]) + mod_ref[0:1, :]).astype(BF16)

    o_ref[...] = _dot(h_ref[...], w_ref[...]).astype(o_ref.dtype)


def _inproj(x2d, mod, g, w, out_dtype, rows_per_mod, tm, tn):
    T = x2d.shape[0]
    N = w.shape[1]
    return pl.pallas_call(
        _inproj_kernel,
        grid=(T // tm, N // tn),
        in_specs=[
            pl.BlockSpec((tm, D_MODEL), lambda i, j: (i, 0)),
            pl.BlockSpec((None, N_MOD, D_MODEL), lambda i, j: ((i * tm) // rows_per_mod, 0, 0)),
            pl.BlockSpec((1, D_MODEL), lambda i, j: (0, 0)),
            pl.BlockSpec((D_MODEL, tn), lambda i, j: (0, j)),
        ],
        out_specs=pl.BlockSpec((tm, tn), lambda i, j: (i, j)),
        out_shape=jax.ShapeDtypeStruct((T, N), out_dtype),
        scratch_shapes=[pltpu.VMEM((tm, D_MODEL), BF16)],
        compiler_params=_cparams("parallel", "arbitrary"),
        name="norm_inproj",
    )(x2d, mod, g, w)


def _mlstm_local(c, q_ref, k_ref, v_ref, gate_ref, pr_ref, bb_ref, mb_ref, kv_ref, rp_ref, L):
    scale = DK_M ** -0.5
    ti = lax.broadcasted_iota(jnp.int32, (L, L), 0)
    si = lax.broadcasted_iota(jnp.int32, (L, L), 1)
    sub = lax.broadcasted_iota(jnp.int32, (8, LANES), 0)
    rows = pl.ds(pl.multiple_of(c * L, L), L)
    q = q_ref[rows, :]
    v_ext = jnp.concatenate([v_ref[rows, :], jnp.ones((L, LANES), BF16)], axis=1)
    k_t = k_ref[rows, :].astype(F32).T
    qk = _dot(q, k_t.astype(BF16)) * scale
    for d in range(2):
        rev = d == 1
        mask = (si >= ti) if rev else (si <= ti)
        i_row = gate_ref[d, 0, pl.ds(c, 1), :]
        f_row = gate_ref[d, 1, pl.ds(c, 1), :]
        b_row = gate_ref[d, 2, pl.ds(c, 1), :]
        b_col = jnp.sum(jnp.where(mask, f_row, 0.0), axis=1, keepdims=True)
        log_d = jnp.where(mask, b_col + (i_row - b_row), -jnp.inf)
        m_loc = jnp.max(log_d, axis=1, keepdims=True)
        smat = qk * jnp.exp(log_d - m_loc)
        pr_ref[d, rows, :] = _dot(smat.astype(BF16), v_ext)
        bb_ref[d, rows, :] = jnp.broadcast_to(b_col, (L, LANES))
        mb_ref[d, rows, :] = jnp.broadcast_to(m_loc, (L, LANES))
        b_last = jnp.sum(f_row, axis=1, keepdims=True)
        ls_row = b_last - b_row + i_row
        m2 = jnp.max(ls_row, axis=1, keepdims=True)
        kw_t = (k_t * jnp.exp(ls_row - m2)).astype(BF16)
        kv_ref[d, c] = scale * _dot(kw_t, v_ext)
        rp_ref[d, c] = jnp.where(sub == 0, b_last, m2)


def _mlstm_carry(c, d, carry, q_ref, pr_ref, bb_ref, mb_ref, kv_ref, rp_ref, h_ref, L):
    cn, m = carry
    two = lambda x: jnp.concatenate([x, x], axis=1)
    rows = pl.ds(pl.multiple_of(c * L, L), L)
    bb = bb_ref[d, rows, :]
    mb = mb_ref[d, rows, :]
    m_t = jnp.maximum(bb + m, mb)
    a_int = jnp.exp(bb + m - m_t)
    e_loc = jnp.exp(mb - m_t)
    nd = two(a_int) * _dot(q_ref[rows, :], cn.astype(BF16)) + two(e_loc) * pr_ref[d, rows, :]
    h_ref[d, rows, :] = nd[:, :DV_M] / jnp.maximum(jnp.abs(nd[:, DV_M:]), jnp.exp(-m_t))
    rp = rp_ref[d, c]
    b_last, m2 = rp[0:1, :], rp[1:2, :]
    m_new = jnp.maximum(b_last + m, m2)
    a_c = jnp.exp(b_last + m - m_new)
    e2 = jnp.exp(m2 - m_new)
    return two(a_c) * cn + two(e2) * kv_ref[d, c], m_new


def _mlstm_gate_rows(gr_ref, br_ref, gate_ref, L):
    ui = lax.broadcasted_iota(jnp.int32, (L, L), 0)
    si = lax.broadcasted_iota(jnp.int32, (L, L), 1)
    for d in range(2):
        src = ((ui >= si) if d == 1 else (ui <= si)).astype(BF16)
        f = _log_sigmoid(gr_ref[2 + d] + br_ref[2 + d])
        f_hi = f.astype(BF16)
        f_r1 = f - f_hi.astype(F32)
        f_mid = f_r1.astype(BF16)
        f_lo = (f_r1 - f_mid.astype(F32)).astype(BF16)
        gate_ref[d, 0] = gr_ref[d] + br_ref[d]
        gate_ref[d, 1] = f
        gate_ref[d, 2] = _dot(f_hi, src) + _dot(f_mid, src) + _dot(f_lo, src)


def _mlstm_kernel(q_ref, k_ref, v_ref, og_ref, gr_ref, br_ref, c0_ref, n0_ref, m0_ref, hn_ref,
                  y_ref, c_out_ref, n_out_ref, m_out_ref,
                  gate_ref, pr_ref, bb_ref, mb_ref, kv_ref, rp_ref, h_ref, *, L, S):
    nch = S // L
    _mlstm_gate_rows(gr_ref, br_ref, gate_ref, L)

    def local(ci, carry):
        _mlstm_local(ci, q_ref, k_ref, v_ref, gate_ref, pr_ref, bb_ref, mb_ref, kv_ref, rp_ref, L)
        return carry

    lax.fori_loop(0, nch, local, 0, unroll=2)
    step = functools.partial(_mlstm_carry, q_ref=q_ref, pr_ref=pr_ref, bb_ref=bb_ref, mb_ref=mb_ref,
                             kv_ref=kv_ref, rp_ref=rp_ref, h_ref=h_ref, L=L)

    def body(ci, carry):
        return step(ci, 0, carry[0]), step(nch - 1 - ci, 1, carry[1])

    def init(d):
        n_rep = jnp.broadcast_to(n0_ref[d], (DK_M, DK_M)).T
        return jnp.concatenate([c0_ref[d], n_rep], axis=1), m0_ref[d]

    fin = lax.fori_loop(0, nch, body, (init(0), init(1)), unroll=2)
    for d in range(2):
        cn, m = fin[d]
        c_out_ref[d] = cn[:, :DV_M]
        n_out_ref[d] = cn[:, DV_M:].T[0:1, :]
        m_out_ref[d] = m

    hm = h_ref[0] + h_ref[1]
    y = hm * lax.rsqrt(jnp.mean(hm * hm, axis=-1, keepdims=True) + EPS) * hn_ref[...]
    y_ref[...] = (y * _sigmoid(og_ref[...].astype(F32))).astype(y_ref.dtype)


def _mlstm(a16, grow, brow, c0, n0, m0, hnorm, B, S):
    L = min(MLSTM_CHUNK, S)
    nch = S // L
    cb = lambda off: off // LANES
    kern = functools.partial(_mlstm_kernel, L=L, S=S)
    return pl.pallas_call(
        kern,
        grid=(B, H_M),
        in_specs=[
            pl.BlockSpec((S, LANES), lambda b, h: (b, cb(A16_MQ) + h)),
            pl.BlockSpec((S, LANES), lambda b, h: (b, cb(A16_MK) + h)),
            pl.BlockSpec((S, LANES), lambda b, h: (b, cb(A16_MV) + h)),
            pl.BlockSpec((S, LANES), lambda b, h: (b, cb(A16_MO) + h)),
            pl.BlockSpec((None, 4, None, nch, L), lambda b, h: (h, 0, b, 0, 0)),
            pl.BlockSpec((None, 4, 1, 1), lambda b, h: (h, 0, 0, 0)),
            pl.BlockSpec((None, 2, None, DK_M, DV_M), lambda b, h: (b, 0, h, 0, 0)),
            pl.BlockSpec((None, 2, None, 1, DK_M), lambda b, h: (b, 0, h, 0, 0)),
            pl.BlockSpec((None, 2, None, 1, LANES), lambda b, h: (b, 0, h, 0, 0)),
            pl.BlockSpec((1, LANES), lambda b, h: (0, h)),
        ],
        out_specs=[
            pl.BlockSpec((S, LANES), lambda b, h: (b, h)),
            pl.BlockSpec((None, 2, None, DK_M, DV_M), lambda b, h: (b, 0, h, 0, 0)),
            pl.BlockSpec((None, 2, None, 1, DK_M), lambda b, h: (b, 0, h, 0, 0)),
            pl.BlockSpec((None, 2, None, 1, LANES), lambda b, h: (b, 0, h, 0, 0)),
        ],
        out_shape=[
            jax.ShapeDtypeStruct((B * S, BRANCH_W), BF16),
            jax.ShapeDtypeStruct((B, 2, H_M, DK_M, DV_M), F32),
            jax.ShapeDtypeStruct((B, 2, H_M, 1, DK_M), F32),
            jax.ShapeDtypeStruct((B, 2, H_M, 1, LANES), F32),
        ],
        scratch_shapes=[pltpu.VMEM((2, 3, nch, L), F32),
                        pltpu.VMEM((2, S, 2 * DV_M), F32), pltpu.VMEM((2, S, LANES), F32),
                        pltpu.VMEM((2, S, LANES), F32), pltpu.VMEM((2, nch, DK_M, 2 * DV_M), F32),
                        pltpu.VMEM((2, nch, 8, LANES), F32), pltpu.VMEM((2, S, DV_M), F32)],
        compiler_params=_cparams("parallel", "parallel"),
        name="mlstm",
    )(a16, a16, a16, a16, grow, brow, c0, n0, m0, hnorm)


def _gla_local(c, q2_ref, k2_ref, v_ref, la_ref, oa_ref, qt_ref, u_ref, dec_ref, L):
    nb = L // GLA_SUB
    ti = lax.broadcasted_iota(jnp.int32, (L, L), 0)
    si = lax.broadcasted_iota(jnp.int32, (L, L), 1)
    row_blk = lax.broadcasted_iota(jnp.int32, (L, LANES), 0) // GLA_SUB
    lo_half = lax.broadcasted_iota(jnp.int32, (L, LANES), 1) < DK_G
    eye = (lax.broadcasted_iota(jnp.int32, (DK_G, LANES), 0)
           == lax.broadcasted_iota(jnp.int32, (DK_G, LANES), 1))
    rows = pl.ds(pl.multiple_of(c * L, L), L)
    q2 = q2_ref[rows, :]
    k2 = k2_ref[rows, :]
    v = v_ref[rows, :]
    row = lax.broadcasted_iota(jnp.int32, (L, LANES), 0)
    for d in range(2):
        rev = d == 1
        mask = (si >= ti) if rev else (si <= ti)
        g2 = la_ref[d, rows, :]
        step = 1
        while step < L:
            if rev:
                g2 = g2 + jnp.where(row < L - step, pltpu.roll(g2, L - step, 0), 0.0)
            else:
                g2 = g2 + jnp.where(row >= step, pltpu.roll(g2, step, 0), 0.0)
            step *= 2
        qt_ref[d, rows, :] = (q2 * jnp.exp(g2))[:, :DK_G].astype(BF16)
        a_parts, b_parts = [], []
        for p in range(nb // 2):
            ia, ib = 2 * p, 2 * p + 1
            ra = ia * GLA_SUB + (GLA_SUB - 1 if rev else 0)
            rb = ib * GLA_SUB + (GLA_SUB - 1 if rev else 0)
            ref2 = jnp.where(lo_half, g2[ra:ra + 1, :], g2[rb:rb + 1, :])
            blk = jnp.where(lo_half, ia, ib)
            in_blk = row_blk == blk
            key_ok = (row_blk >= blk) if rev else (row_blk <= blk)
            a_parts.append(jnp.where(in_blk, q2 * jnp.exp(jnp.minimum(g2 - ref2, 0.0)), 0.0))
            b_parts.append(
                jnp.where(key_ok, k2 * jnp.exp(jnp.minimum(ref2 - g2, GLA_EXP_CLAMP)), 0.0))
        a_big = jnp.concatenate(a_parts, axis=1).astype(BF16)
        b_big = jnp.concatenate(b_parts, axis=1).astype(BF16)
        att = jnp.where(mask, _dot_nt(a_big, b_big), 0.0)
        oa_ref[d, rows, :] = _dot(att.astype(BF16), v)
        gl_row = 0 if rev else L - 1
        glast = g2[gl_row:gl_row + 1, :]
        kd = (k2 * jnp.exp(glast - g2))[:, :DK_G]
        u_ref[d, c] = _dot_tn(kd.astype(BF16), v)
        glast_col = jnp.sum(jnp.where(eye, glast, 0.0), axis=1, keepdims=True)
        dec_ref[d, c] = jnp.broadcast_to(jnp.exp(glast_col), (DK_G, DV_G))


def _gla_kernel(qk_ref, v_ref, sm_ref, wup_ref, bup_ref, s0_ref, gr_ref, hn_ref,
                y_ref, s_out_ref, la_ref, q2_ref, k2_ref, oa_ref, oi_ref, qt_ref, u_ref, dec_ref,
                *, L, S):
    nch = S // L
    sm = sm_ref[...].astype(BF16)
    for d in range(2):
        la_ref[d] = _log_sigmoid(_dot(sm, wup_ref[d]) + bup_ref[d]) * (1.0 / GLA_TAU)
    qk = qk_ref[...].astype(F32)
    qk_sw = pltpu.roll(qk, DK_G, 1)
    lo_half = lax.broadcasted_iota(jnp.int32, qk.shape, 1) < DK_G
    q2_ref[...] = jnp.where(lo_half, qk, qk_sw) * (DK_G ** -0.5)
    k2_ref[...] = jnp.where(lo_half, qk_sw, qk)

    def local(ci, carry):
        _gla_local(ci, q2_ref, k2_ref, v_ref, la_ref, oa_ref, qt_ref, u_ref, dec_ref, L)
        return carry

    lax.fori_loop(0, nch, local, 0, unroll=4)

    def body(ci, carry):
        out = []
        for d, c in ((0, ci), (1, nch - 1 - ci)):
            rows = pl.ds(pl.multiple_of(c * L, L), L)
            st = carry[d]
            oi_ref[d, rows, :] = _dot(qt_ref[d, rows, :], st.astype(BF16))
            out.append(dec_ref[d, c] * st + u_ref[d, c])
        return tuple(out)

    st_f, st_b = lax.fori_loop(0, nch, body, (s0_ref[0], s0_ref[1]), unroll=2)
    s_out_ref[0] = st_f
    s_out_ref[1] = st_b

    og = (oa_ref[0] + oi_ref[0]) + (oa_ref[1] + oi_ref[1])
    y = og * lax.rsqrt(jnp.mean(og * og, axis=-1, keepdims=True) + EPS) * hn_ref[...]
    gr = gr_ref[...].astype(F32)
    y_ref[...] = (y * (gr * _sigmoid(gr))).astype(y_ref.dtype)


def _gla(a32, a16, wup, bup, s0, hnorm, B, S):
    L = min(GLA_CHUNK, S)
    nch = S // L
    cb = lambda off: off // LANES
    kern = functools.partial(_gla_kernel, L=L, S=S)
    return pl.pallas_call(
        kern,
        grid=(B, H_G),
        in_specs=[
            pl.BlockSpec((S, LANES), lambda b, h: (b, cb(A16_GQK) + h)),
            pl.BlockSpec((S, LANES), lambda b, h: (b, cb(A16_GV) + h)),
            pl.BlockSpec((S, LANES), lambda b, h: (b, cb(A32_SM))),
            pl.BlockSpec((None, 2, LANES, LANES), lambda b, h: (h, 0, 0, 0)),
            pl.BlockSpec((None, 2, 1, LANES), lambda b, h: (h, 0, 0, 0)),
            pl.BlockSpec((None, 2, None, DK_G, DV_G), lambda b, h: (b, 0, h, 0, 0)),
            pl.BlockSpec((S, LANES), lambda b, h: (b, cb(A16_GR) + h)),
            pl.BlockSpec((1, LANES), lambda b, h: (0, h)),
        ],
        out_specs=[
            pl.BlockSpec((S, LANES), lambda b, h: (b, h)),
            pl.BlockSpec((None, 2, None, DK_G, DV_G), lambda b, h: (b, 0, h, 0, 0)),
        ],
        out_shape=[
            jax.ShapeDtypeStruct((B * S, BRANCH_W), BF16),
            jax.ShapeDtypeStruct((B, 2, H_G, DK_G, DV_G), F32),
        ],
        scratch_shapes=[pltpu.VMEM((2, S, LANES), F32), pltpu.VMEM((S, LANES), F32),
                        pltpu.VMEM((S, LANES), F32), pltpu.VMEM((2, S, DV_G), F32),
                        pltpu.VMEM((2, S, DV_G), F32), pltpu.VMEM((2, S, DK_G), BF16),
                        pltpu.VMEM((2, nch, DK_G, DV_G), F32), pltpu.VMEM((2, nch, DK_G, DV_G), F32)],
        compiler_params=_cparams("parallel", "parallel"),
        name="gla",
    )(a16, a16, a32, wup, bup, s0, a16, hnorm)


def _rope(x, cos, sin_signed):
    lane = lax.broadcasted_iota(jnp.int32, x.shape, 1)
    first = (lane % DQK_D) < (DQK_D // 2)
    partner = jnp.where(first, pltpu.roll(x, LANES - DQK_D // 2, 1), pltpu.roll(x, DQK_D // 2, 1))
    return x * cos + partner * sin_signed


def _attn_kernel(*refs, S, P, TQ, lam_init, has_ctx):
    if has_ctx:
        (q_ref, k_ref, v_ref, ck_ref, cv_ref, cos_ref, sin_ref, lam_ref, hn_ref,
         y_ref, kk_ref, vv_ref) = refs
    else:
        q_ref, k_ref, v_ref, lam_ref, hn_ref, y_ref, kk_ref, vv_ref = refs
    qi = pl.program_id(2)

    @pl.when(qi == 0)
    def _():
        k = k_ref[...]
        if has_ctx:
            k = _rope(k, cos_ref[...], sin_ref[...])
            kk_ref[S:S + P, :] = ck_ref[...].astype(BF16)
            vv_ref[S:S + P, :] = cv_ref[...].astype(BF16)
        kk_ref[0:S, :] = k.astype(BF16)
        vv_ref[0:S, :] = v_ref[...].astype(BF16)

    q = q_ref[...]
    if has_ctx:
        r0 = pl.multiple_of(qi * TQ, TQ)
        q = _rope(q, cos_ref[pl.ds(r0, TQ), :], sin_ref[pl.ds(r0, TQ), :])
    q = q * (DQK_D ** -0.5 * math.log2(math.e))
    lane = lax.broadcasted_iota(jnp.int32, q.shape, 1)
    kk = kk_ref[...]
    vv = vv_ref[...]
    lv = lam_ref[...]
    lam = (jnp.exp(jnp.sum(lv[0:1, :] * lv[1:2, :], axis=-1, keepdims=True))
           - jnp.exp(jnp.sum(lv[2:3, :] * lv[3:4, :], axis=-1, keepdims=True)) + lam_init)
    es, ls = [], []
    for comp in range(2):
        sel = (lane < DQK_D) if comp == 0 else (lane >= DQK_D)
        s = _dot_nt(jnp.where(sel, q, 0.0).astype(BF16), kk)
        e = jnp.exp2(s - jnp.max(s, axis=-1, keepdims=True))
        es.append(e)
        ls.append(jnp.sum(e, axis=-1, keepdims=True))
    w = es[0] - es[1] * (lam * ls[0] / ls[1])
    o = _dot(w.astype(BF16), vv) * (1.0 / ls[0])
    y = o * lax.rsqrt(jnp.mean(o * o, axis=-1, keepdims=True) + EPS) * hn_ref[...]
    y_ref[...] = (y * (1.0 - lam_init)).astype(y_ref.dtype)


def _attn(a32, lamv, hnorm, B, S, lam_init, ctx=None):
    TQ = min(256, S)
    nq = S // TQ
    has_ctx = ctx is not None
    P = ctx[0].shape[2] if has_ctx else 0
    cb = lambda off: off // LANES
    kern = functools.partial(_attn_kernel, S=S, P=P, TQ=TQ, lam_init=lam_init, has_ctx=has_ctx)
    in_specs = [
        pl.BlockSpec((TQ, LANES), lambda b, h, i: (b * nq + i, cb(A32_DQ) + h)),
        pl.BlockSpec((S, LANES), lambda b, h, i: (b, cb(A32_DK) + h)),
        pl.BlockSpec((S, LANES), lambda b, h, i: (b, cb(A32_DV) + h)),
    ]
    args = [a32, a32, a32]
    if has_ctx:
        ck, cv, layer, cos, sin = ctx
        in_specs += [
            pl.BlockSpec((None, None, P, LANES), lambda b, h, i: (b, layer, 0, h)),
            pl.BlockSpec((None, None, P, LANES), lambda b, h, i: (b, layer, 0, h)),
            pl.BlockSpec((S, LANES), lambda b, h, i: (0, 0)),
            pl.BlockSpec((S, LANES), lambda b, h, i: (0, 0)),
        ]
        args += [ck, cv, cos, sin]
    in_specs += [
        pl.BlockSpec((4, DQK_D), lambda b, h, i: (0, 0)),
        pl.BlockSpec((1, LANES), lambda b, h, i: (0, h)),
    ]
    args += [lamv, hnorm]
    return pl.pallas_call(
        kern,
        grid=(B, H_D, nq),
        in_specs=in_specs,
        out_specs=pl.BlockSpec((TQ, LANES), lambda b, h, i: (b * nq + i, h)),
        out_shape=jax.ShapeDtypeStruct((B * S, BRANCH_W), BF16),
        scratch_shapes=[pltpu.VMEM((S + P, LANES), BF16), pltpu.VMEM((S + P, LANES), BF16)],
        compiler_params=_cparams("parallel", "parallel", "arbitrary"),
        name="diff_attn",
    )(*args)


def _conv_kernel(ca_ref, cb_ref, w_ref, g_ref, b_ref, y_ref, pad_ref, cv_ref, *, S):
    ca = ca_ref[...].astype(F32)
    cbv = cb_ref[...].astype(F32)
    zeros = jnp.zeros((CONV_PAD, BRANCH_W), F32)
    pad_ref[0:CONV_PAD, :] = zeros
    pad_ref[CONV_PAD + S:2 * CONV_PAD + S, :] = zeros
    pad_ref[CONV_PAD:CONV_PAD + S, :] = ca * _sigmoid(cbv)
    off = CONV_PAD - CONV_W // 2

    win_rows = CONV_ROWS + 2 * CONV_PAD

    def body(i, carry):
        base = pl.multiple_of(i * CONV_ROWS, CONV_ROWS)
        for lb in range(BRANCH_W // LANES):
            cols = slice(lb * LANES, (lb + 1) * LANES)
            win = pad_ref[pl.ds(base, win_rows), cols]
            acc = jnp.zeros((CONV_ROWS, LANES), F32)
            for r in range(8):
                rolled = win if r == 0 else pltpu.roll(win, win_rows - r, 0)
                for a in range(2 * CONV_PAD // 8):
                    j = 8 * a + r - off
                    if 0 <= j < CONV_W:
                        acc = acc + rolled[8 * a:8 * a + CONV_ROWS, :] * w_ref[j:j + 1, cols]
            cv_ref[:, cols] = acc
        acc = cv_ref[...]
        mu = jnp.mean(acc, axis=-1, keepdims=True)
        xc = acc - mu
        yn = xc * lax.rsqrt(jnp.mean(xc * xc, axis=-1, keepdims=True) + EPS) * g_ref[...] + b_ref[...]
        y_ref[pl.ds(base, CONV_ROWS), :] = (yn * _sigmoid(yn)).astype(y_ref.dtype)
        return carry

    lax.fori_loop(0, S // CONV_ROWS, body, 0)


def _conv(a16, w_dw, ln_g, ln_b, B, S):
    cb = lambda off: off // BRANCH_W
    kern = functools.partial(_conv_kernel, S=S)
    return pl.pallas_call(
        kern,
        grid=(B,),
        in_specs=[
            pl.BlockSpec((S, BRANCH_W), lambda b: (b, cb(A16_CA))),
            pl.BlockSpec((S, BRANCH_W), lambda b: (b, cb(A16_CB))),
            pl.BlockSpec((CONV_W + 1, BRANCH_W), lambda b: (0, 0)),
            pl.BlockSpec((1, BRANCH_W), lambda b: (0, 0)),
            pl.BlockSpec((1, BRANCH_W), lambda b: (0, 0)),
        ],
        out_specs=pl.BlockSpec((S, BRANCH_W), lambda b: (b, 0)),
        out_shape=jax.ShapeDtypeStruct((B * S, BRANCH_W), BF16),
        scratch_shapes=[pltpu.VMEM((S + 2 * CONV_PAD, BRANCH_W), F32),
                        pltpu.VMEM((CONV_ROWS, BRANCH_W), F32)],
        compiler_params=_cparams("parallel"),
        name="glu_conv_ln",
    )(a16, a16, w_dw, ln_g, ln_b)


def _merge_kernel(x_ref, mod_ref, ym_ref, yd_ref, yg_ref, yc_ref, g0_ref, g1_ref, g2_ref, g3_ref,
                  wb_ref, wo_ref, n2_ref, wr_ref, x1_ref, h2_ref, route_ref):
    ys = (ym_ref, yd_ref, yg_ref, yc_ref)
    gs = (g0_ref, g1_ref, g2_ref, g3_ref)
    merged = None
    for nbr in range(N_BRANCH):
        br = _dot(ys[nbr][...], wb_ref[nbr])
        term = _sigmoid(gs[nbr][...].astype(F32)) * br
        merged = term if merged is None else merged + term
    out = _dot(merged.astype(BF16), wo_ref[...])
    x1 = x_ref[...] + mod_ref[2:3, :] * out
    x1_ref[...] = x1
    y = x1 * lax.rsqrt(jnp.mean(x1 * x1, axis=-1, keepdims=True) + EPS) * n2_ref[...]
    h2 = y * (1.0 + mod_ref[4:5, :]) + mod_ref[3:4, :]
    h2_ref[...] = h2.reshape(h2_ref.shape)
    wr = wr_ref[...]
    h_hi, w_hi = h2.astype(BF16), wr.astype(BF16)
    h_lo = (h2 - h_hi.astype(F32)).astype(BF16)
    w_lo = (wr - w_hi.astype(F32)).astype(BF16)
    logits = _dot(h_hi, w_hi) + (_dot(h_hi, w_lo) + _dot(h_lo, w_hi))
    lane = lax.broadcasted_iota(jnp.int32, logits.shape, 1)
    neg = -jnp.inf
    big = jnp.int32(LANES)
    is_g = lane < N_GROUPS
    gl = jnp.where(is_g, logits, neg)
    gmax = jnp.max(gl, axis=-1, keepdims=True)
    gidx = jnp.min(jnp.where(is_g & (gl == gmax), lane, big), axis=-1, keepdims=True)
    g_p = 1.0 / jnp.sum(jnp.where(is_g, jnp.exp(gl - gmax), 0.0), axis=-1, keepdims=True)
    e_lane = lane - N_GROUPS
    in_grp = (e_lane >= 0) & (e_lane < N_EXPERTS) & ((e_lane // EXPERTS_PER_GROUP) == gidx)
    el = jnp.where(in_grp, logits, neg)
    v1 = jnp.max(el, axis=-1, keepdims=True)
    i1 = jnp.min(jnp.where(in_grp & (el == v1), lane, big), axis=-1, keepdims=True)
    el2 = jnp.where(lane == i1, neg, el)
    v2 = jnp.max(el2, axis=-1, keepdims=True)
    i2 = jnp.min(jnp.where(in_grp & (lane != i1) & (el2 == v2), lane, big), axis=-1, keepdims=True)
    e2 = jnp.exp(v2 - v1)
    w1 = g_p / (1.0 + e2)
    w2 = g_p * e2 / (1.0 + e2)
    id1 = (i1 - N_GROUPS).astype(F32)
    id2 = (i2 - N_GROUPS).astype(F32)
    route_ref[...] = jnp.where(lane == 0, id1, jnp.where(lane == 1, id2,
                               jnp.where(lane == 2, w1, jnp.where(lane == 3, w2, 0.0))))


def _merge(x2d, mod, a16, ym, yd, yg, yc, wb, wo, n2, wr, rows_per_mod, tm):
    T = x2d.shape[0]
    gcb = A16_GATE // D_MODEL
    row = lambda i: (i, 0)
    return pl.pallas_call(
        _merge_kernel,
        grid=(T // tm,),
        in_specs=[
            pl.BlockSpec((tm, D_MODEL), row),
            pl.BlockSpec((None, N_MOD, D_MODEL), lambda i: ((i * tm) // rows_per_mod, 0, 0)),
            pl.BlockSpec((tm, BRANCH_W), row),
            pl.BlockSpec((tm, BRANCH_W), row),
            pl.BlockSpec((tm, BRANCH_W), row),
            pl.BlockSpec((tm, BRANCH_W), row),
            pl.BlockSpec((tm, D_MODEL), lambda i: (i, gcb + 0)),
            pl.BlockSpec((tm, D_MODEL), lambda i: (i, gcb + 1)),
            pl.BlockSpec((tm, D_MODEL), lambda i: (i, gcb + 2)),
            pl.BlockSpec((tm, D_MODEL), lambda i: (i, gcb + 3)),
            pl.BlockSpec((N_BRANCH, BRANCH_W, D_MODEL), lambda i: (0, 0, 0)),
            pl.BlockSpec((D_MODEL, D_MODEL), lambda i: (0, 0)),
            pl.BlockSpec((1, D_MODEL), lambda i: (0, 0)),
            pl.BlockSpec((D_MODEL, LANES), lambda i: (0, 0)),
        ],
        out_specs=[
            pl.BlockSpec((tm, D_MODEL), row),
            pl.BlockSpec((tm, TOK_SUB, LANES), lambda i: (i, 0, 0)),
            pl.BlockSpec((tm, LANES), row),
        ],
        out_shape=[
            jax.ShapeDtypeStruct((T, D_MODEL), F32),
            jax.ShapeDtypeStruct((T, TOK_SUB, LANES), F32),
            jax.ShapeDtypeStruct((T, LANES), F32),
        ],
        compiler_params=_cparams("parallel"),
        name="merge_outproj_route",
    )(x2d, mod, ym, yd, yg, yc, a16, a16, a16, a16, wb, wo, n2, wr)


def _gather_rows(idx_ref, src_hbm, dst, sem, n):
    def body(j, carry):
        for u in range(ROW_DMA_UNROLL):
            r = j * ROW_DMA_UNROLL + u
            pltpu.make_async_copy(src_hbm.at[idx_ref[0, r]], dst.at[r], sem).start(priority=u % 2)
        return carry

    lax.fori_loop(0, n // ROW_DMA_UNROLL, body, 0)


def _scatter_rows(idx_ref, src, dst_hbm, sem, n):
    def body(j, carry):
        for u in range(ROW_DMA_UNROLL):
            r = j * ROW_DMA_UNROLL + u
            pltpu.make_async_copy(src.at[r], dst_hbm.at[idx_ref[0, r]], sem).start(priority=u % 2)
        return carry

    lax.fori_loop(0, n // ROW_DMA_UNROLL, body, 0)


def _wait_rows(buf, sem):
    pltpu.make_async_copy(buf, buf, sem).wait()


def _moe_dispatch_kernel(d0_ref, d1_ref, h_ref, xg_in, xg_out, sem, *, tm):
    del xg_in
    _scatter_rows(d0_ref, h_ref, xg_out, sem.at[0], tm)
    _scatter_rows(d1_ref, h_ref, xg_out, sem.at[1], tm)
    _wait_rows(h_ref, sem.at[0])
    _wait_rows(h_ref, sem.at[1])


def _moe_dispatch(h3, dest, xg_init, tm):
    T = h3.shape[0]
    kern = functools.partial(_moe_dispatch_kernel, tm=tm)
    return pl.pallas_call(
        kern,
        grid=(T // tm,),
        in_specs=[
            pl.BlockSpec((None, None, 1, tm), lambda i: (0, i, 0, 0), memory_space=pltpu.SMEM),
            pl.BlockSpec((None, None, 1, tm), lambda i: (1, i, 0, 0), memory_space=pltpu.SMEM),
            pl.BlockSpec((tm, TOK_SUB, LANES), lambda i: (i, 0, 0)),
            pl.BlockSpec(memory_space=pl.ANY),
        ],
        out_specs=pl.BlockSpec(memory_space=pl.ANY),
        out_shape=jax.ShapeDtypeStruct(xg_init.shape, F32),
        input_output_aliases={3: 0},
        scratch_shapes=[pltpu.SemaphoreType.DMA((2,))],
        compiler_params=_cparams("arbitrary"),
        name="moe_dispatch",
    )(dest, dest, h3, xg_init)


def _moe_ffn_kernel(te_ref, x_ref, w1_ref, w3_ref, w2_ref, o_ref):
    del te_ref
    x = x_ref[...].reshape(x_ref.shape[0], D_MODEL).astype(BF16)
    a = _dot(x, w1_ref[...].astype(BF16))
    b = _dot(x, w3_ref[...].astype(BF16))
    s = (a * _sigmoid(a)) * b
    y = _dot(s.astype(BF16), w2_ref[...].astype(BF16))
    o_ref[...] = y.reshape(o_ref.shape)


def _moe_ffn(xg, tile_e, w1, w3, w2, rows):
    ntiles = xg.shape[0] // rows
    tile = pl.BlockSpec((rows, TOK_SUB, LANES), lambda i, te: (i, 0, 0))
    return pl.pallas_call(
        _moe_ffn_kernel,
        grid_spec=pltpu.PrefetchScalarGridSpec(
            num_scalar_prefetch=1,
            grid=(ntiles,),
            in_specs=[
                tile,
                pl.BlockSpec((None, D_MODEL, D_EXPERT), lambda i, te: (te[i], 0, 0)),
                pl.BlockSpec((None, D_MODEL, D_EXPERT), lambda i, te: (te[i], 0, 0)),
                pl.BlockSpec((None, D_EXPERT, D_MODEL), lambda i, te: (te[i], 0, 0)),
            ],
            out_specs=tile,
        ),
        out_shape=jax.ShapeDtypeStruct(xg.shape, F32),
        compiler_params=_cparams("parallel"),
        name="moe_grouped_experts",
    )(tile_e, xg, w1, w3, w2)


def _moe_combine_kernel(d0_ref, d1_ref, y_hbm, route_ref, x1_ref, mod_ref, fn_ref, o_ref,
                        ga, gb, sem, *, tm, final_norm):
    _gather_rows(d0_ref, y_hbm, ga, sem.at[0], tm)
    _gather_rows(d1_ref, y_hbm, gb, sem.at[1], tm)
    rt = route_ref[...]
    _wait_rows(ga, sem.at[0])
    _wait_rows(gb, sem.at[1])
    y = rt[:, 2:3] * ga[...].reshape(tm, D_MODEL) + rt[:, 3:4] * gb[...].reshape(tm, D_MODEL)
    x2 = x1_ref[...] + mod_ref[5:6, :] * y
    if final_norm:
        x2 = x2 * lax.rsqrt(jnp.mean(x2 * x2, axis=-1, keepdims=True) + EPS) * fn_ref[...]
    o_ref[...] = x2


def _moe_combine(yg, dest, route, x1, mod, fn, rows_per_mod, tm, final_norm):
    T = x1.shape[0]
    kern = functools.partial(_moe_combine_kernel, tm=tm, final_norm=final_norm)
    return pl.pallas_call(
        kern,
        grid=(T // tm,),
        in_specs=[
            pl.BlockSpec((None, None, 1, tm), lambda i: (0, i, 0, 0), memory_space=pltpu.SMEM),
            pl.BlockSpec((None, None, 1, tm), lambda i: (1, i, 0, 0), memory_space=pltpu.SMEM),
            pl.BlockSpec(memory_space=pl.ANY),
            pl.BlockSpec((tm, LANES), lambda i: (i, 0)),
            pl.BlockSpec((tm, D_MODEL), lambda i: (i, 0)),
            pl.BlockSpec((None, N_MOD, D_MODEL), lambda i: ((i * tm) // rows_per_mod, 0, 0)),
            pl.BlockSpec((1, D_MODEL), lambda i: (0, 0)),
        ],
        out_specs=pl.BlockSpec((tm, D_MODEL), lambda i: (i, 0)),
        out_shape=jax.ShapeDtypeStruct((T, D_MODEL), F32),
        scratch_shapes=[pltpu.VMEM((tm, TOK_SUB, LANES), F32), pltpu.VMEM((tm, TOK_SUB, LANES), F32),
                        pltpu.SemaphoreType.DMA((2,))],
        compiler_params=_cparams("arbitrary"),
        name="moe_combine",
    )(dest, dest, yg, route, x1, mod, fn)


def _moe_rows(T):
    rows = 512 if T >= 8192 else 256
    ntiles = (2 * T + N_EXPERTS * (rows - 1) + rows - 1) // rows
    return rows, ntiles * rows


def _route_plan(route, T):
    tile, n_rows = _moe_rows(T)
    ef = route[:, 0:2].astype(jnp.int32).reshape(-1)
    oh = (ef[:, None] == jnp.arange(N_EXPERTS, dtype=jnp.int32)[None, :]).astype(jnp.int32)
    csum = jnp.cumsum(oh, axis=0)
    rank = jnp.sum((csum - oh) * oh, axis=1)
    counts = csum[-1]
    padded = ((counts + tile - 1) // tile) * tile
    seg_end = jnp.cumsum(padded)
    dest = jnp.sum(oh * (seg_end - padded)[None, :], axis=1) + rank
    tile_row = jnp.arange(n_rows // tile, dtype=jnp.int32) * tile
    tile_e = jnp.minimum(jnp.sum((tile_row[:, None] >= seg_end[None, :]).astype(jnp.int32), axis=1),
                         N_EXPERTS - 1)
    return jnp.transpose(dest.reshape(T, 2)), tile_e


def _split_w_in(w):
    sizes = (H_M * DK_M, H_M * DK_M, H_M * DV_M, H_M * DV_M, 2 * H_M, 2 * H_M,
             H_D * 2 * DQK_D, H_D * 2 * DQK_D, H_D * DV_D,
             H_G * DK_G, H_G * DK_G, H_G * DV_G, 2 * GATE_RANK, H_G * DV_G,
             BRANCH_W, BRANCH_W, N_BRANCH * D_MODEL)
    outs, acc = [], 0
    for s in sizes:
        outs.append(w[:, acc:acc + s])
        acc += s
    return outs


def _pack_layer_params(p):
    (m_q, m_k, m_v, m_o, m_i, m_f, d_q, d_k, d_v, g_q, g_k, g_v, g_a, g_r, c_a, c_b, gate) = \
        _split_w_in(p['w_in'])
    gqk = jnp.concatenate([g_q.reshape(D_MODEL, H_G, DK_G), g_k.reshape(D_MODEL, H_G, DK_G)],
                          axis=2).reshape(D_MODEL, 2 * H_G * DK_G)
    small = jnp.concatenate(
        [m_i, m_f, g_a, jnp.zeros((D_MODEL, LANES - 4 * H_M - 2 * GATE_RANK), F32)], axis=1)
    w16 = jnp.concatenate([m_q, m_k, m_v, m_o, gate, g_r, c_a, c_b, gqk, g_v], axis=1).astype(BF16)
    w32 = jnp.concatenate([d_q, d_k, d_v, small], axis=1).astype(BF16)
    bi = p['b_m_i'].reshape(2, H_M)
    bf = p['b_m_f'].reshape(2, H_M)
    bcol = jnp.stack([bi[0], bi[1], bf[0], bf[1]], axis=-1)
    wup = p['w_gla_up'].reshape(2, GATE_RANK, H_G, DK_G)
    wup_pad = jnp.zeros((H_G, 2, LANES, LANES), F32)
    bup = p['b_gla_gate'].reshape(2, H_G, DK_G)
    for d in range(2):
        blk = jnp.transpose(wup[d], (1, 0, 2))
        blk = jnp.concatenate([blk, blk], axis=-1)
        r0 = SM_GA + d * GATE_RANK
        wup_pad = wup_pad.at[:, d, r0:r0 + GATE_RANK, :].set(blk)
    bup2 = jnp.transpose(jnp.concatenate([bup, bup], axis=-1), (1, 0, 2))[:, :, None, :]
    wr = jnp.concatenate([p['w_group_router'], p['w_expert_router'],
                          jnp.zeros((D_MODEL, LANES - N_GROUPS - N_EXPERTS), F32)], axis=1)
    return dict(
        w16=w16, w32=w32, brow=bcol.reshape(H_M, 4, 1, 1),
        wup=wup_pad.astype(BF16), bup=bup2,
        wdw=jnp.concatenate([p['w_dw'], jnp.zeros((1, BRANCH_W), F32)], axis=0),
        ln_g=p['conv_ln_g'].reshape(1, BRANCH_W), ln_b=p['conv_ln_b'].reshape(1, BRANCH_W),
        hn_m=p['hnorm_m'].reshape(1, BRANCH_W), hn_d=p['hnorm_d'].reshape(1, BRANCH_W),
        hn_g=p['hnorm_g'].reshape(1, BRANCH_W),
        lamv=jnp.stack([p['lam_q1'], p['lam_k1'], p['lam_q2'], p['lam_k2']], axis=0),
        wb=p['w_branch'].astype(BF16), wo=p['w_out'].astype(BF16),
        n1=p['norm1'].reshape(1, D_MODEL), n2=p['norm2'].reshape(1, D_MODEL), wr=wr,
    )


def _rope_tables(S):
    rows = S // GRID_W
    r, col = jnp.meshgrid(jnp.arange(rows, dtype=F32), jnp.arange(GRID_W, dtype=F32), indexing='ij')
    r, col = r.reshape(-1), col.reshape(-1)
    n_freq = DQK_D // 4
    inv = ROPE_BASE ** (-jnp.arange(n_freq, dtype=F32) / n_freq)
    ang = jnp.concatenate([r[:, None] * inv, col[:, None] * inv], axis=-1)
    cos, sin = jnp.cos(ang), jnp.sin(ang)
    cos_t = jnp.tile(cos, (1, LANES // (DQK_D // 2)))
    sin_t = jnp.tile(jnp.concatenate([-sin, sin], axis=-1), (1, LANES // DQK_D))
    return cos_t, sin_t


def _pick_tile(T, cap):
    t = min(T, cap)
    while T % t:
        t //= 2
    return t


def _layer(x2d, mod, pk, B, S, lam_init, ctx, final_norm, fn, xg):
    T = B * S
    rows_per_mod = T // mod.shape[0]
    tm = _pick_tile(rows_per_mod, 1024)
    a16 = _inproj(x2d, mod, pk['n1'], pk['w16'], BF16, rows_per_mod, tm, N_A16 // 4)
    a32 = _inproj(x2d, mod, pk['n1'], pk['w32'], F32, rows_per_mod, tm, N_A32)

    L = min(MLSTM_CHUNK, S)
    sm = a32[:, A32_SM:A32_SM + 4 * H_M]
    grow = jnp.transpose(sm.reshape(B, S // L, L, 4, H_M), (4, 3, 0, 1, 2))
    if ctx is None:
        c0 = jnp.zeros((B, 2, H_M, DK_M, DV_M), F32)
        n0 = jnp.zeros((B, 2, H_M, 1, DK_M), F32)
        m0 = jnp.zeros((B, 2, H_M, 1, LANES), F32)
        s0 = jnp.zeros((B, 2, H_G, DK_G, DV_G), F32)
        attn_ctx = None
    else:
        c0 = ctx['C']
        n0 = ctx['n'][:, :, :, None, :]
        m0 = jnp.broadcast_to(ctx['m'][:, :, :, None, None], (B, 2, H_M, 1, LANES))
        s0 = ctx['S']
        attn_ctx = (ctx['k'], ctx['v'], ctx['layer'], ctx['cos'], ctx['sin'])
    ym, c_f, n_f, m_f = _mlstm(a16, grow, pk['brow'], c0, n0, m0, pk['hn_m'], B, S)
    yd = _attn(a32, pk['lamv'], pk['hn_d'], B, S, lam_init, attn_ctx)
    yg, s_f = _gla(a32, a16, pk['wup'], pk['bup'], s0, pk['hn_g'], B, S)
    yc = _conv(a16, pk['wdw'], pk['ln_g'], pk['ln_b'], B, S)
    x1, h3, route = _merge(x2d, mod, a16, ym, yd, yg, yc, pk['wb'], pk['wo'], pk['n2'], pk['wr'],
                           rows_per_mod, _pick_tile(rows_per_mod, 512))
    dest, tile_e = _route_plan(route, T)
    tmd = _pick_tile(T, 512)
    moe_rows, n_rows = _moe_rows(T)
    if xg is None:
        xg = jnp.zeros((n_rows, TOK_SUB, LANES), F32)
    xg = _moe_dispatch(h3, dest.reshape(2, T // tmd, 1, tmd), xg, tmd)
    y_grouped = _moe_ffn(xg, tile_e + pk['expert_base'], pk['w1'], pk['w3'], pk['w2'], moe_rows)
    tmc = _pick_tile(rows_per_mod, 256)
    x2 = _moe_combine(y_grouped, dest.reshape(2, T // tmc, 1, tmc), route, x1, mod, fn,
                      rows_per_mod, tmc, final_norm)
    state = None
    if ctx is None:
        state = (a32[:, A32_DK:A32_DK + H_D * 2 * DQK_D].reshape(B, S, H_D, 2 * DQK_D),
                 a32[:, A32_DV:A32_DV + H_D * DV_D].reshape(B, S, H_D, DV_D),
                 c_f, n_f[:, :, :, 0, :], m_f[:, :, :, 0, 0], s_f)
    return x2, state, xg


def kernel(x_prompt, x_sample, c, cache_diff_k, cache_diff_v, state_mlstm_C, state_mlstm_n, state_mlstm_m, state_gla_S, c_ctx, w_mod, b_mod, norm1, w_in, b_m_i, b_m_f, lam_q1, lam_k1, lam_q2, lam_k2, w_gla_up, b_gla_gate, w_dw, conv_ln_g, conv_ln_b, hnorm_m, hnorm_d, hnorm_g, w_branch, w_out, norm2, w_group_router, w_expert_router, w_e1, w_e3, w_e2, final_norm):
    Bp, Sp, _ = x_prompt.shape
    Bs, Ss, _ = x_sample.shape
    P = cache_diff_k.shape[2]
    n_cond = 8 * ((1 + Bs + 7) // 8)
    cond = jnp.concatenate([c_ctx[None, :], c, jnp.zeros((n_cond - 1 - Bs, D_MODEL), F32)], axis=0)
    mod_all = _modulation(cond, w_mod, b_mod).reshape(DEPTH, n_cond, N_MOD, D_MODEL)
    cos_t, sin_t = _rope_tables(Ss)
    ck4 = cache_diff_k.reshape(Bs, DEPTH, P, H_D * 2 * DQK_D)
    cv4 = cache_diff_v.reshape(Bs, DEPTH, P, H_D * DV_D)
    fn = final_norm.reshape(1, D_MODEL)
    yp = x_prompt.reshape(Bp * Sp, D_MODEL)
    ys = x_sample.reshape(Bs * Ss, D_MODEL)
    states = []
    xg_p = xg_s = None
    for l in range(DEPTH):
        p = {'w_in': w_in[l], 'b_m_i': b_m_i[l], 'b_m_f': b_m_f[l], 'lam_q1': lam_q1[l],
             'lam_k1': lam_k1[l], 'lam_q2': lam_q2[l], 'lam_k2': lam_k2[l],
             'w_gla_up': w_gla_up[l], 'b_gla_gate': b_gla_gate[l], 'w_dw': w_dw[l],
             'conv_ln_g': conv_ln_g[l], 'conv_ln_b': conv_ln_b[l], 'hnorm_m': hnorm_m[l],
             'hnorm_d': hnorm_d[l], 'hnorm_g': hnorm_g[l], 'w_branch': w_branch[l],
             'w_out': w_out[l], 'norm1': norm1[l], 'norm2': norm2[l],
             'w_group_router': w_group_router[l], 'w_expert_router': w_expert_router[l]}
        pk = _pack_layer_params(p)
        pk.update(w1=w_e1.reshape(DEPTH * N_EXPERTS, D_MODEL, D_EXPERT),
                  w3=w_e3.reshape(DEPTH * N_EXPERTS, D_MODEL, D_EXPERT),
                  w2=w_e2.reshape(DEPTH * N_EXPERTS, D_EXPERT, D_MODEL), expert_base=l * N_EXPERTS)
        lam_init = 0.8 - 0.6 * math.exp(-0.3 * l)
        last = l == DEPTH - 1
        yp, st, xg_p = _layer(yp, mod_all[l, 0:1], pk, Bp, Sp, lam_init, None, last, fn, xg_p)
        states.append(st)
        ctx = {'k': ck4, 'v': cv4, 'layer': l, 'cos': cos_t, 'sin': sin_t,
               'C': state_mlstm_C[:, l], 'n': state_mlstm_n[:, l], 'm': state_mlstm_m[:, l],
               'S': state_gla_S[:, l]}
        ys, _, xg_s = _layer(ys, mod_all[l, 1:1 + Bs], pk, Bs, Ss, lam_init, ctx, last, fn, xg_s)
    stack = lambda i: jnp.stack([s[i] for s in states], axis=1)
    return (yp.reshape(Bp, Sp, D_MODEL), ys.reshape(Bs, Ss, D_MODEL),
            stack(0), stack(1), stack(2), stack(3), stack(4), stack(5))
```

```python
import functools
import math

import jax
import jax.numpy as jnp
from jax import lax
from jax.experimental import pallas as pl
from jax.experimental.pallas import tpu as pltpu

F32 = jnp.float32
BF16 = jnp.bfloat16

D_MODEL = 1024
DEPTH = 2
GRID_W = 64
BRANCH_W = 512
N_BRANCH = 4
H_M, DK_M, DV_M = 4, 128, 128
H_D, DQK_D, DV_D = 4, 64, 128
H_G, DK_G, DV_G = 4, 64, 128
GATE_RANK = 16
GLA_TAU = 16.0
CONV_W = 31
N_GROUPS, EXPERTS_PER_GROUP, D_EXPERT = 4, 4, 512
N_EXPERTS = N_GROUPS * EXPERTS_PER_GROUP
ROPE_BASE = 10000.0
EPS = 1e-6
N_MOD = 6

LANES = 128
VMEM_LIMIT = 48 * 1024 * 1024

A16_MQ, A16_MK, A16_MV, A16_MO = 0, 512, 1024, 1536
A16_GATE, A16_GR, A16_CA, A16_CB = 2048, 6144, 6656, 7168
A16_GQK, A16_GV, A16_DQ = 7680, 8192, 8704
N_A16 = 9216
A32_DK, A32_DV, A32_SM = 0, 512, 1024
N_A32 = 1152
SM_MI, SM_MF, SM_GA = 0, 8, 16

MLSTM_CHUNK = 128
GLA_CHUNK = 64
GLA_SUB = 16
GLA_EXP_CLAMP = 80.0
CONV_ROWS = 64
CONV_PAD = 16
TOK_SUB = D_MODEL // LANES
ROW_DMA_UNROLL = 8


def _cparams(*sem):
    return pltpu.CompilerParams(dimension_semantics=sem, vmem_limit_bytes=VMEM_LIMIT)


def _log_sigmoid(x):
    return jnp.minimum(x, 0.0) - jnp.log1p(jnp.exp(-jnp.abs(x)))


def _sigmoid(x):
    return 0.5 * jnp.tanh(0.5 * x) + 0.5


def _dot(a, b):
    return jnp.dot(a, b, preferred_element_type=F32)


def _dot_nt(a, b):
    return lax.dot_general(a, b, (((1,), (1,)), ((), ())), preferred_element_type=F32)


def _dot_tn(a, b):
    return lax.dot_general(a, b, (((0,), (0,)), ((), ())), preferred_element_type=F32)


def _mod_kernel(c_ref, w_ref, b_ref, o_ref):
    c = c_ref[...]
    a = (c * _sigmoid(c)).astype(BF16)
    o_ref[...] = _dot(a, w_ref[...].astype(BF16)) + b_ref[...]


def _modulation(cond, w_mod, b_mod):
    R = cond.shape[0]
    tn = 512
    nmod = N_MOD * D_MODEL
    return pl.pallas_call(
        _mod_kernel,
        grid=(DEPTH, nmod // tn),
        in_specs=[
            pl.BlockSpec((R, D_MODEL), lambda l, j: (0, 0)),
            pl.BlockSpec((None, D_MODEL, tn), lambda l, j: (l, 0, j)),
            pl.BlockSpec((None, 1, tn), lambda l, j: (l, 0, j)),
        ],
        out_specs=pl.BlockSpec((None, R, tn), lambda l, j: (l, 0, j)),
        out_shape=jax.ShapeDtypeStruct((DEPTH, R, nmod), F32),
        compiler_params=_cparams("parallel", "parallel"),
        name="adaln_mod",
    )(cond, w_mod, b_mod.reshape(DEPTH, 1, nmod))


def _inproj_kernel(x_ref, mod_ref, g_ref, w_ref, o_ref, h_ref):
    @pl.when(pl.program_id(1) == 0)
    def _():
        x = x_ref[...]
        y = x * lax.rsqrt(jnp.mean(x * x, axis=-1, keepdims=True) + EPS) * g_ref[...]
        h_ref[...] = (y * (1.0 + mod_ref[1:2, :]) + mod_ref[0:1, :]).astype(BF16)

    o_ref[...] = _dot(h_ref[...], w_ref[...]).astype(o_ref.dtype)


def _inproj(x2d, mod, g, w, out_dtype, rows_per_mod, tm, tn):
    T = x2d.shape[0]
    N = w.shape[1]
    return pl.pallas_call(
        _inproj_kernel,
        grid=(T // tm, N // tn),
        in_specs=[
            pl.BlockSpec((tm, D_MODEL), lambda i, j: (i, 0)),
            pl.BlockSpec((None, N_MOD, D_MODEL), lambda i, j: ((i * tm) // rows_per_mod, 0, 0)),
            pl.BlockSpec((1, D_MODEL), lambda i, j: (0, 0)),
            pl.BlockSpec((D_MODEL, tn), lambda i, j: (0, j)),
        ],
        out_specs=pl.BlockSpec((tm, tn), lambda i, j: (i, j)),
        out_shape=jax.ShapeDtypeStruct((T, N), out_dtype),
        scratch_shapes=[pltpu.VMEM((tm, D_MODEL), BF16)],
        compiler_params=_cparams("parallel", "arbitrary"),
        name="norm_inproj",
    )(x2d, mod, g, w)


def _mlstm_local(c, q_ref, k_ref, v_ref, gate_ref, pr_ref, bb_ref, mb_ref, kv_ref, rp_ref, L):
    scale = DK_M ** -0.5
    ti = lax.broadcasted_iota(jnp.int32, (L, L), 0)
    si = lax.broadcasted_iota(jnp.int32, (L, L), 1)
    sub = lax.broadcasted_iota(jnp.int32, (8, LANES), 0)
    rows = pl.ds(pl.multiple_of(c * L, L), L)
    q = q_ref[rows, :]
    v_ext = jnp.concatenate([v_ref[rows, :], jnp.ones((L, LANES), BF16)], axis=1)
    k_t = k_ref[rows, :].astype(F32).T
    qk = _dot(q, k_t.astype(BF16)) * scale
    for d in range(2):
        rev = d == 1
        mask = (si >= ti) if rev else (si <= ti)
        i_row = gate_ref[d, 0, pl.ds(c, 1), :]
        f_row = gate_ref[d, 1, pl.ds(c, 1), :]
        b_row = gate_ref[d, 2, pl.ds(c, 1), :]
        b_col = jnp.sum(jnp.where(mask, f_row, 0.0), axis=1, keepdims=True)
        log_d = jnp.where(mask, b_col + (i_row - b_row), -jnp.inf)
        m_loc = jnp.max(log_d, axis=1, keepdims=True)
        smat = qk * jnp.exp(log_d - m_loc)
        pr_ref[d, rows, :] = _dot(smat.astype(BF16), v_ext)
        bb_ref[d, rows, :] = jnp.broadcast_to(b_col, (L, LANES))
        mb_ref[d, rows, :] = jnp.broadcast_to(m_loc, (L, LANES))
        b_last = jnp.sum(f_row, axis=1, keepdims=True)
        ls_row = b_last - b_row + i_row
        m2 = jnp.max(ls_row, axis=1, keepdims=True)
        kw_t = (k_t * jnp.exp(ls_row - m2)).astype(BF16)
        kv_ref[d, c] = scale * _dot(kw_t, v_ext)
        rp_ref[d, c] = jnp.where(sub == 0, b_last, m2)


def _mlstm_carry(c, d, carry, q_ref, pr_ref, bb_ref, mb_ref, kv_ref, rp_ref, h_ref, L):
    cn, m = carry
    two = lambda x: jnp.concatenate([x, x], axis=1)
    rows = pl.ds(pl.multiple_of(c * L, L), L)
    bb = bb_ref[d, rows, :]
    mb = mb_ref[d, rows, :]
    m_t = jnp.maximum(bb + m, mb)
    a_int = jnp.exp(bb + m - m_t)
    e_loc = jnp.exp(mb - m_t)
    nd = two(a_int) * _dot(q_ref[rows, :], cn.astype(BF16)) + two(e_loc) * pr_ref[d, rows, :]
    h_ref[d, rows, :] = nd[:, :DV_M] / jnp.maximum(jnp.abs(nd[:, DV_M:]), jnp.exp(-m_t))
    rp = rp_ref[d, c]
    b_last, m2 = rp[0:1, :], rp[1:2, :]
    m_new = jnp.maximum(b_last + m, m2)
    a_c = jnp.exp(b_last + m - m_new)
    e2 = jnp.exp(m2 - m_new)
    return two(a_c) * cn + two(e2) * kv_ref[d, c], m_new


def _mlstm_gate_rows(gr_ref, br_ref, gate_ref, L):
    ui = lax.broadcasted_iota(jnp.int32, (L, L), 0)
    si = lax.broadcasted_iota(jnp.int32, (L, L), 1)
    for d in range(2):
        src = ((ui >= si) if d == 1 else (ui <= si)).astype(BF16)
        f = _log_sigmoid(gr_ref[2 + d] + br_ref[2 + d])
        f_hi = f.astype(BF16)
        f_r1 = f - f_hi.astype(F32)
        f_mid = f_r1.astype(BF16)
        f_lo = (f_r1 - f_mid.astype(F32)).astype(BF16)
        gate_ref[d, 0] = gr_ref[d] + br_ref[d]
        gate_ref[d, 1] = f
        gate_ref[d, 2] = _dot(f_hi, src) + _dot(f_mid, src) + _dot(f_lo, src)


def _mlstm_kernel(q_ref, k_ref, v_ref, og_ref, gr_ref, br_ref, c0_ref, n0_ref, m0_ref, hn_ref,
                  y_ref, c_out_ref, n_out_ref, m_out_ref,
                  gate_ref, pr_ref, bb_ref, mb_ref, kv_ref, rp_ref, h_ref, *, L, S):
    nch = S // L
    _mlstm_gate_rows(gr_ref, br_ref, gate_ref, L)

    def local(ci, carry):
        _mlstm_local(ci, q_ref, k_ref, v_ref, gate_ref, pr_ref, bb_ref, mb_ref, kv_ref, rp_ref, L)
        return carry

    lax.fori_loop(0, nch, local, 0, unroll=2)
    step = functools.partial(_mlstm_carry, q_ref=q_ref, pr_ref=pr_ref, bb_ref=bb_ref, mb_ref=mb_ref,
                             kv_ref=kv_ref, rp_ref=rp_ref, h_ref=h_ref, L=L)

    def body(ci, carry):
        return step(ci, 0, carry[0]), step(nch - 1 - ci, 1, carry[1])

    def init(d):
        n_rep = jnp.broadcast_to(n0_ref[d], (DK_M, DK_M)).T
        return jnp.concatenate([c0_ref[d], n_rep], axis=1), m0_ref[d]

    fin = lax.fori_loop(0, nch, body, (init(0), init(1)), unroll=2)
    for d in range(2):
        cn, m = fin[d]
        c_out_ref[d] = cn[:, :DV_M]
        n_out_ref[d] = cn[:, DV_M:].T[0:1, :]
        m_out_ref[d] = m

    hm = h_ref[0] + h_ref[1]
    y = hm * lax.rsqrt(jnp.mean(hm * hm, axis=-1, keepdims=True) + EPS) * hn_ref[...]
    y_ref[...] = (y * _sigmoid(og_ref[...].astype(F32))).astype(y_ref.dtype)


def _mlstm(a16, grow, brow, c0, n0, m0, hnorm, B, S):
    L = min(MLSTM_CHUNK, S)
    nch = S // L
    cb = lambda off: off // LANES
    kern = functools.partial(_mlstm_kernel, L=L, S=S)
    return pl.pallas_call(
        kern,
        grid=(B, H_M),
        in_specs=[
            pl.BlockSpec((S, LANES), lambda b, h: (b, cb(A16_MQ) + h)),
            pl.BlockSpec((S, LANES), lambda b, h: (b, cb(A16_MK) + h)),
            pl.BlockSpec((S, LANES), lambda b, h: (b, cb(A16_MV) + h)),
            pl.BlockSpec((S, LANES), lambda b, h: (b, cb(A16_MO) + h)),
            pl.BlockSpec((None, 4, None, nch, L), lambda b, h: (h, 0, b, 0, 0)),
            pl.BlockSpec((None, 4, 1, 1), lambda b, h: (h, 0, 0, 0)),
            pl.BlockSpec((None, 2, None, DK_M, DV_M), lambda b, h: (b, 0, h, 0, 0)),
            pl.BlockSpec((None, 2, None, 1, DK_M), lambda b, h: (b, 0, h, 0, 0)),
            pl.BlockSpec((None, 2, None, 1, LANES), lambda b, h: (b, 0, h, 0, 0)),
            pl.BlockSpec((1, LANES), lambda b, h: (0, h)),
        ],
        out_specs=[
            pl.BlockSpec((S, LANES), lambda b, h: (b, h)),
            pl.BlockSpec((None, 2, None, DK_M, DV_M), lambda b, h: (b, 0, h, 0, 0)),
            pl.BlockSpec((None, 2, None, 1, DK_M), lambda b, h: (b, 0, h, 0, 0)),
            pl.BlockSpec((None, 2, None, 1, LANES), lambda b, h: (b, 0, h, 0, 0)),
        ],
        out_shape=[
            jax.ShapeDtypeStruct((B * S, BRANCH_W), BF16),
            jax.ShapeDtypeStruct((B, 2, H_M, DK_M, DV_M), F32),
            jax.ShapeDtypeStruct((B, 2, H_M, 1, DK_M), F32),
            jax.ShapeDtypeStruct((B, 2, H_M, 1, LANES), F32),
        ],
        scratch_shapes=[pltpu.VMEM((2, 3, nch, L), F32),
                        pltpu.VMEM((2, S, 2 * DV_M), F32), pltpu.VMEM((2, S, LANES), F32),
                        pltpu.VMEM((2, S, LANES), F32), pltpu.VMEM((2, nch, DK_M, 2 * DV_M), F32),
                        pltpu.VMEM((2, nch, 8, LANES), F32), pltpu.VMEM((2, S, DV_M), F32)],
        compiler_params=_cparams("parallel", "parallel"),
        name="mlstm",
    )(a16, a16, a16, a16, grow, brow, c0, n0, m0, hnorm)


def _gla_local(c, q2_ref, k2_ref, v_ref, la_ref, oa_ref, qt_ref, u_ref, dec_ref, L):
    nb = L // GLA_SUB
    ti = lax.broadcasted_iota(jnp.int32, (L, L), 0)
    si = lax.broadcasted_iota(jnp.int32, (L, L), 1)
    row_blk = lax.broadcasted_iota(jnp.int32, (L, LANES), 0) // GLA_SUB
    lo_half = lax.broadcasted_iota(jnp.int32, (L, LANES), 1) < DK_G
    eye = (lax.broadcasted_iota(jnp.int32, (DK_G, LANES), 0)
           == lax.broadcasted_iota(jnp.int32, (DK_G, LANES), 1))
    rows = pl.ds(pl.multiple_of(c * L, L), L)
    q2 = q2_ref[rows, :]
    k2 = k2_ref[rows, :]
    v = v_ref[rows, :]
    row = lax.broadcasted_iota(jnp.int32, (L, LANES), 0)
    for d in range(2):
        rev = d == 1
        mask = (si >= ti) if rev else (si <= ti)
        g2 = la_ref[d, rows, :]
        step = 1
        while step < L:
            if rev:
                g2 = g2 + jnp.where(row < L - step, pltpu.roll(g2, L - step, 0), 0.0)
            else:
                g2 = g2 + jnp.where(row >= step, pltpu.roll(g2, step, 0), 0.0)
            step *= 2
        qt_ref[d, rows, :] = (q2 * jnp.exp(g2))[:, :DK_G].astype(BF16)
        a_parts, b_parts = [], []
        for p in range(nb // 2):
            ia, ib = 2 * p, 2 * p + 1
            ra = ia * GLA_SUB + (GLA_SUB - 1 if rev else 0)
            rb = ib * GLA_SUB + (GLA_SUB - 1 if rev else 0)
            ref2 = jnp.where(lo_half, g2[ra:ra + 1, :], g2[rb:rb + 1, :])
            blk = jnp.where(lo_half, ia, ib)
            in_blk = row_blk == blk
            key_ok = (row_blk >= blk) if rev else (row_blk <= blk)
            a_parts.append(jnp.where(in_blk, q2 * jnp.exp(jnp.minimum(g2 - ref2, 0.0)), 0.0))
            b_parts.append(
                jnp.where(key_ok, k2 * jnp.exp(jnp.minimum(ref2 - g2, GLA_EXP_CLAMP)), 0.0))
        a_big = jnp.concatenate(a_parts, axis=1).astype(BF16)
        b_big = jnp.concatenate(b_parts, axis=1).astype(BF16)
        att = jnp.where(mask, _dot_nt(a_big, b_big), 0.0)
        oa_ref[d, rows, :] = _dot(att.astype(BF16), v)
        gl_row = 0 if rev else L - 1
        glast = g2[gl_row:gl_row + 1, :]
        kd = (k2 * jnp.exp(glast - g2))[:, :DK_G]
        u_ref[d, c] = _dot_tn(kd.astype(BF16), v)
        glast_col = jnp.sum(jnp.where(eye, glast, 0.0), axis=1, keepdims=True)
        dec_ref[d, c] = jnp.broadcast_to(jnp.exp(glast_col), (DK_G, DV_G))


def _gla_kernel(qk_ref, v_ref, sm_ref, wup_ref, bup_ref, s0_ref, gr_ref, hn_ref,
                y_ref, s_out_ref, la_ref, q2_ref, k2_ref, oa_ref, oi_ref, qt_ref, u_ref, dec_ref,
                *, L, S):
    nch = S // L
    sm = sm_ref[...].astype(BF16)
    for d in range(2):
        la_ref[d] = _log_sigmoid(_dot(sm, wup_ref[d]) + bup_ref[d]) * (1.0 / GLA_TAU)
    qk = qk_ref[...].astype(F32)
    qk_sw = pltpu.roll(qk, DK_G, 1)
    lo_half = lax.broadcasted_iota(jnp.int32, qk.shape, 1) < DK_G
    q2_ref[...] = jnp.where(lo_half, qk, qk_sw) * (DK_G ** -0.5)
    k2_ref[...] = jnp.where(lo_half, qk_sw, qk)

    def local(ci, carry):
        _gla_local(ci, q2_ref, k2_ref, v_ref, la_ref, oa_ref, qt_ref, u_ref, dec_ref, L)
        return carry

    lax.fori_loop(0, nch, local, 0, unroll=4)

    def body(ci, carry):
        out = []
        for d, c in ((0, ci), (1, nch - 1 - ci)):
            rows = pl.ds(pl.multiple_of(c * L, L), L)
            st = carry[d]
            oi_ref[d, rows, :] = _dot(qt_ref[d, rows, :], st.astype(BF16))
            out.append(dec_ref[d, c] * st + u_ref[d, c])
        return tuple(out)

    st_f, st_b = lax.fori_loop(0, nch, body, (s0_ref[0], s0_ref[1]), unroll=2)
    s_out_ref[0] = st_f
    s_out_ref[1] = st_b

    og = (oa_ref[0] + oi_ref[0]) + (oa_ref[1] + oi_ref[1])
    y = og * lax.rsqrt(jnp.mean(og * og, axis=-1, keepdims=True) + EPS) * hn_ref[...]
    gr = gr_ref[...].astype(F32)
    y_ref[...] = (y * (gr * _sigmoid(gr))).astype(y_ref.dtype)


def _gla(a32, a16, wup, bup, s0, hnorm, B, S):
    L = min(GLA_CHUNK, S)
    nch = S // L
    cb = lambda off: off // LANES
    kern = functools.partial(_gla_kernel, L=L, S=S)
    return pl.pallas_call(
        kern,
        grid=(B, H_G),
        in_specs=[
            pl.BlockSpec((S, LANES), lambda b, h: (b, cb(A16_GQK) + h)),
            pl.BlockSpec((S, LANES), lambda b, h: (b, cb(A16_GV) + h)),
            pl.BlockSpec((S, LANES), lambda b, h: (b, cb(A32_SM))),
            pl.BlockSpec((None, 2, LANES, LANES), lambda b, h: (h, 0, 0, 0)),
            pl.BlockSpec((None, 2, 1, LANES), lambda b, h: (h, 0, 0, 0)),
            pl.BlockSpec((None, 2, None, DK_G, DV_G), lambda b, h: (b, 0, h, 0, 0)),
            pl.BlockSpec((S, LANES), lambda b, h: (b, cb(A16_GR) + h)),
            pl.BlockSpec((1, LANES), lambda b, h: (0, h)),
        ],
        out_specs=[
            pl.BlockSpec((S, LANES), lambda b, h: (b, h)),
            pl.BlockSpec((None, 2, None, DK_G, DV_G), lambda b, h: (b, 0, h, 0, 0)),
        ],
        out_shape=[
            jax.ShapeDtypeStruct((B * S, BRANCH_W), BF16),
            jax.ShapeDtypeStruct((B, 2, H_G, DK_G, DV_G), F32),
        ],
        scratch_shapes=[pltpu.VMEM((2, S, LANES), F32), pltpu.VMEM((S, LANES), F32),
                        pltpu.VMEM((S, LANES), F32), pltpu.VMEM((2, S, DV_G), F32),
                        pltpu.VMEM((2, S, DV_G), F32), pltpu.VMEM((2, S, DK_G), BF16),
                        pltpu.VMEM((2, nch, DK_G, DV_G), F32), pltpu.VMEM((2, nch, DK_G, DV_G), F32)],
        compiler_params=_cparams("parallel", "parallel"),
        name="gla",
    )(a16, a16, a32, wup, bup, s0, a16, hnorm)


def _rope(x, cos, sin_signed):
    lane = lax.broadcasted_iota(jnp.int32, x.shape, 1)
    first = (lane % DQK_D) < (DQK_D // 2)
    partner = jnp.where(first, pltpu.roll(x, LANES - DQK_D // 2, 1), pltpu.roll(x, DQK_D // 2, 1))
    return x * cos + partner * sin_signed


def _attn_kernel(*refs, S, P, TQ, lam_init, has_ctx):
    if has_ctx:
        (q_ref, k_ref, v_ref, ck_ref, cv_ref, cos_ref, sin_ref, lam_ref, hn_ref,
         y_ref, kk_ref, vv_ref) = refs
    else:
        q_ref, k_ref, v_ref, lam_ref, hn_ref, y_ref, kk_ref, vv_ref = refs
    qi = pl.program_id(2)

    @pl.when(qi == 0)
    def _():
        k = k_ref[...]
        if has_ctx:
            k = _rope(k, cos_ref[...], sin_ref[...])
            kk_ref[S:S + P, :] = ck_ref[...].astype(BF16)
            vv_ref[S:S + P, :] = cv_ref[...].astype(BF16)
        kk_ref[0:S, :] = k.astype(BF16)
        vv_ref[0:S, :] = v_ref[...].astype(BF16)

    kk = kk_ref[...]
    vv = vv_ref[...]
    lv = lam_ref[...]
    lam = (jnp.exp(jnp.sum(lv[0:1, :] * lv[1:2, :], axis=-1, keepdims=True))
           - jnp.exp(jnp.sum(lv[2:3, :] * lv[3:4, :], axis=-1, keepdims=True)) + lam_init)
    q = q_ref[...].astype(F32)
    if has_ctx:
        r0 = pl.multiple_of(qi * TQ, TQ)
        q = _rope(q, cos_ref[pl.ds(r0, TQ), :], sin_ref[pl.ds(r0, TQ), :])
    q = q * (DQK_D ** -0.5 * math.log2(math.e))
    lane = lax.broadcasted_iota(jnp.int32, q.shape, 1)
    es, ls = [], []
    for comp in range(2):
        sel = (lane < DQK_D) if comp == 0 else (lane >= DQK_D)
        s = _dot_nt(jnp.where(sel, q, 0.0).astype(BF16), kk)
        e = jnp.exp2(s - jnp.max(s, axis=-1, keepdims=True))
        es.append(e)
        ls.append(jnp.sum(e, axis=-1, keepdims=True))
    w = es[0] - es[1] * (lam * ls[0] / ls[1])
    o = _dot(w.astype(BF16), vv) * (1.0 / ls[0])
    y = o * lax.rsqrt(jnp.mean(o * o, axis=-1, keepdims=True) + EPS) * hn_ref[...]
    y_ref[...] = (y * (1.0 - lam_init)).astype(y_ref.dtype)


def _attn(a16, a32, lamv, hnorm, B, S, lam_init, ctx=None):
    TQ = min(256, S)
    nq = S // TQ
    has_ctx = ctx is not None
    P = ctx[0].shape[2] if has_ctx else 0
    cb = lambda off: off // LANES
    kern = functools.partial(_attn_kernel, S=S, P=P, TQ=TQ, lam_init=lam_init, has_ctx=has_ctx)
    in_specs = [
        pl.BlockSpec((TQ, LANES), lambda b, h, i: (b * nq + i, cb(A16_DQ) + h)),
        pl.BlockSpec((S, LANES), lambda b, h, i: (b, cb(A32_DK) + h)),
        pl.BlockSpec((S, LANES), lambda b, h, i: (b, cb(A32_DV) + h)),
    ]
    args = [a16, a32, a32]
    if has_ctx:
        ck, cv, layer, cos, sin = ctx
        in_specs += [
            pl.BlockSpec((None, None, P, LANES), lambda b, h, i: (b, layer, 0, h)),
            pl.BlockSpec((None, None, P, LANES), lambda b, h, i: (b, layer, 0, h)),
            pl.BlockSpec((S, LANES), lambda b, h, i: (0, 0)),
            pl.BlockSpec((S, LANES), lambda b, h, i: (0, 0)),
        ]
        args += [ck, cv, cos, sin]
    in_specs += [
        pl.BlockSpec((4, DQK_D), lambda b, h, i: (0, 0)),
        pl.BlockSpec((1, LANES), lambda b, h, i: (0, h)),
    ]
    args += [lamv, hnorm]
    return pl.pallas_call(
        kern,
        grid=(B, H_D, nq),
        in_specs=in_specs,
        out_specs=pl.BlockSpec((TQ, LANES), lambda b, h, i: (b * nq + i, h)),
        out_shape=jax.ShapeDtypeStruct((B * S, BRANCH_W), BF16),
        scratch_shapes=[pltpu.VMEM((S + P, LANES), BF16), pltpu.VMEM((S + P, LANES), BF16)],
        compiler_params=_cparams("parallel", "parallel", "arbitrary"),
        name="diff_attn",
    )(*args)


def _conv_kernel(ca_ref, cb_ref, w_ref, g_ref, b_ref, y_ref, pad_ref, cv_ref, *, S):
    ca = ca_ref[...].astype(F32)
    cbv = cb_ref[...].astype(F32)
    zeros = jnp.zeros((CONV_PAD, BRANCH_W), F32)
    pad_ref[0:CONV_PAD, :] = zeros
    pad_ref[CONV_PAD + S:2 * CONV_PAD + S, :] = zeros
    pad_ref[CONV_PAD:CONV_PAD + S, :] = ca * _sigmoid(cbv)
    off = CONV_PAD - CONV_W // 2

    win_rows = CONV_ROWS + 2 * CONV_PAD

    def body(i, carry):
        base = pl.multiple_of(i * CONV_ROWS, CONV_ROWS)
        for lb in range(BRANCH_W // LANES):
            cols = slice(lb * LANES, (lb + 1) * LANES)
            win = pad_ref[pl.ds(base, win_rows), cols]
            acc = jnp.zeros((CONV_ROWS, LANES), F32)
            for r in range(8):
                rolled = win if r == 0 else pltpu.roll(win, win_rows - r, 0)
                for a in range(2 * CONV_PAD // 8):
                    j = 8 * a + r - off
                    if 0 <= j < CONV_W:
                        acc = acc + rolled[8 * a:8 * a + CONV_ROWS, :] * w_ref[j:j + 1, cols]
            cv_ref[:, cols] = acc
        acc = cv_ref[...]
        mu = jnp.mean(acc, axis=-1, keepdims=True)
        xc = acc - mu
        yn = xc * lax.rsqrt(jnp.mean(xc * xc, axis=-1, keepdims=True) + EPS) * g_ref[...] + b_ref[...]
        y_ref[pl.ds(base, CONV_ROWS), :] = (yn * _sigmoid(yn)).astype(y_ref.dtype)
        return carry

    lax.fori_loop(0, S // CONV_ROWS, body, 0)


def _conv(a16, w_dw, ln_g, ln_b, B, S):
    cb = lambda off: off // BRANCH_W
    kern = functools.partial(_conv_kernel, S=S)
    return pl.pallas_call(
        kern,
        grid=(B,),
        in_specs=[
            pl.BlockSpec((S, BRANCH_W), lambda b: (b, cb(A16_CA))),
            pl.BlockSpec((S, BRANCH_W), lambda b: (b, cb(A16_CB))),
            pl.BlockSpec((CONV_W + 1, BRANCH_W), lambda b: (0, 0)),
            pl.BlockSpec((1, BRANCH_W), lambda b: (0, 0)),
            pl.BlockSpec((1, BRANCH_W), lambda b: (0, 0)),
        ],
        out_specs=pl.BlockSpec((S, BRANCH_W), lambda b: (b, 0)),
        out_shape=jax.ShapeDtypeStruct((B * S, BRANCH_W), BF16),
        scratch_shapes=[pltpu.VMEM((S + 2 * CONV_PAD, BRANCH_W), F32),
                        pltpu.VMEM((CONV_ROWS, BRANCH_W), F32)],
        compiler_params=_cparams("parallel"),
        name="glu_conv_ln",
    )(a16, a16, w_dw, ln_g, ln_b)


def _merge_kernel(x_ref, mod_ref, ym_ref, yd_ref, yg_ref, yc_ref, g0_ref, g1_ref, g2_ref, g3_ref,
                  wb_ref, wo_ref, n2_ref, wr_ref, x1_ref, h2_ref, route_ref):
    ys = (ym_ref, yd_ref, yg_ref, yc_ref)
    gs = (g0_ref, g1_ref, g2_ref, g3_ref)
    merged = None
    for nbr in range(N_BRANCH):
        br = _dot(ys[nbr][...], wb_ref[nbr])
        term = _sigmoid(gs[nbr][...].astype(F32)) * br
        merged = term if merged is None else merged + term
    out = _dot(merged.astype(BF16), wo_ref[...])
    x1 = x_ref[...] + mod_ref[2:3, :] * out
    x1_ref[...] = x1
    y = x1 * lax.rsqrt(jnp.mean(x1 * x1, axis=-1, keepdims=True) + EPS) * n2_ref[...]
    h2 = y * (1.0 + mod_ref[4:5, :]) + mod_ref[3:4, :]
    h2_ref[...] = h2.reshape(h2_ref.shape)
    wr = wr_ref[...]
    h_hi, w_hi = h2.astype(BF16), wr.astype(BF16)
    h_lo = (h2 - h_hi.astype(F32)).astype(BF16)
    w_lo = (wr - w_hi.astype(F32)).astype(BF16)
    logits = _dot(h_hi, w_hi) + (_dot(h_hi, w_lo) + _dot(h_lo, w_hi))
    lane = lax.broadcasted_iota(jnp.int32, logits.shape, 1)
    neg = -jnp.inf
    big = jnp.int32(LANES)
    is_g = lane < N_GROUPS
    gl = jnp.where(is_g, logits, neg)
    gmax = jnp.max(gl, axis=-1, keepdims=True)
    gidx = jnp.min(jnp.where(is_g & (gl == gmax), lane, big), axis=-1, keepdims=True)
    g_p = 1.0 / jnp.sum(jnp.where(is_g, jnp.exp(gl - gmax), 0.0), axis=-1, keepdims=True)
    e_lane = lane - N_GROUPS
    in_grp = (e_lane >= 0) & (e_lane < N_EXPERTS) & ((e_lane // EXPERTS_PER_GROUP) == gidx)
    el = jnp.where(in_grp, logits, neg)
    v1 = jnp.max(el, axis=-1, keepdims=True)
    i1 = jnp.min(jnp.where(in_grp & (el == v1), lane, big), axis=-1, keepdims=True)
    el2 = jnp.where(lane == i1, neg, el)
    v2 = jnp.max(el2, axis=-1, keepdims=True)
    i2 = jnp.min(jnp.where(in_grp & (lane != i1) & (el2 == v2), lane, big), axis=-1, keepdims=True)
    e2 = jnp.exp(v2 - v1)
    w1 = g_p / (1.0 + e2)
    w2 = g_p * e2 / (1.0 + e2)
    id1 = (i1 - N_GROUPS).astype(F32)
    id2 = (i2 - N_GROUPS).astype(F32)
    route_ref[...] = jnp.where(lane == 0, id1, jnp.where(lane == 1, id2,
                               jnp.where(lane == 2, w1, jnp.where(lane == 3, w2, 0.0))))


def _merge(x2d, mod, a16, ym, yd, yg, yc, wb, wo, n2, wr, rows_per_mod, tm):
    T = x2d.shape[0]
    gcb = A16_GATE // D_MODEL
    row = lambda i: (i, 0)
    return pl.pallas_call(
        _merge_kernel,
        grid=(T // tm,),
        in_specs=[
            pl.BlockSpec((tm, D_MODEL), row),
            pl.BlockSpec((None, N_MOD, D_MODEL), lambda i: ((i * tm) // rows_per_mod, 0, 0)),
            pl.BlockSpec((tm, BRANCH_W), row),
            pl.BlockSpec((tm, BRANCH_W), row),
            pl.BlockSpec((tm, BRANCH_W), row),
            pl.BlockSpec((tm, BRANCH_W), row),
            pl.BlockSpec((tm, D_MODEL), lambda i: (i, gcb + 0)),
            pl.BlockSpec((tm, D_MODEL), lambda i: (i, gcb + 1)),
            pl.BlockSpec((tm, D_MODEL), lambda i: (i, gcb + 2)),
            pl.BlockSpec((tm, D_MODEL), lambda i: (i, gcb + 3)),
            pl.BlockSpec((N_BRANCH, BRANCH_W, D_MODEL), lambda i: (0, 0, 0)),
            pl.BlockSpec((D_MODEL, D_MODEL), lambda i: (0, 0)),
            pl.BlockSpec((1, D_MODEL), lambda i: (0, 0)),
            pl.BlockSpec((D_MODEL, LANES), lambda i: (0, 0)),
        ],
        out_specs=[
            pl.BlockSpec((tm, D_MODEL), row),
            pl.BlockSpec((tm, TOK_SUB, LANES), lambda i: (i, 0, 0)),
            pl.BlockSpec((tm, LANES), row),
        ],
        out_shape=[
            jax.ShapeDtypeStruct((T, D_MODEL), F32),
            jax.ShapeDtypeStruct((T, TOK_SUB, LANES), F32),
            jax.ShapeDtypeStruct((T, LANES), F32),
        ],
        compiler_params=_cparams("parallel"),
        name="merge_outproj_route",
    )(x2d, mod, ym, yd, yg, yc, a16, a16, a16, a16, wb, wo, n2, wr)


def _gather_rows(idx_ref, src_hbm, dst, sem, n):
    def body(j, carry):
        for u in range(ROW_DMA_UNROLL):
            r = j * ROW_DMA_UNROLL + u
            pltpu.make_async_copy(src_hbm.at[idx_ref[0, r]], dst.at[r], sem).start(priority=u % 2)
        return carry

    lax.fori_loop(0, n // ROW_DMA_UNROLL, body, 0)


def _scatter_rows(idx_ref, src, dst_hbm, sem, n):
    def body(j, carry):
        for u in range(ROW_DMA_UNROLL):
            r = j * ROW_DMA_UNROLL + u
            pltpu.make_async_copy(src.at[r], dst_hbm.at[idx_ref[0, r]], sem).start(priority=u % 2)
        return carry

    lax.fori_loop(0, n // ROW_DMA_UNROLL, body, 0)


def _wait_rows(buf, sem):
    pltpu.make_async_copy(buf, buf, sem).wait()


def _moe_dispatch_kernel(d0_ref, d1_ref, h_ref, xg_in, xg_out, sem, *, tm):
    del xg_in
    _scatter_rows(d0_ref, h_ref, xg_out, sem.at[0], tm)
    _scatter_rows(d1_ref, h_ref, xg_out, sem.at[1], tm)
    _wait_rows(h_ref, sem.at[0])
    _wait_rows(h_ref, sem.at[1])


def _moe_dispatch(h3, dest, xg_init, tm):
    T = h3.shape[0]
    kern = functools.partial(_moe_dispatch_kernel, tm=tm)
    return pl.pallas_call(
        kern,
        grid=(T // tm,),
        in_specs=[
            pl.BlockSpec((None, None, 1, tm), lambda i: (0, i, 0, 0), memory_space=pltpu.SMEM),
            pl.BlockSpec((None, None, 1, tm), lambda i: (1, i, 0, 0), memory_space=pltpu.SMEM),
            pl.BlockSpec((tm, TOK_SUB, LANES), lambda i: (i, 0, 0)),
            pl.BlockSpec(memory_space=pl.ANY),
        ],
        out_specs=pl.BlockSpec(memory_space=pl.ANY),
        out_shape=jax.ShapeDtypeStruct(xg_init.shape, F32),
        input_output_aliases={3: 0},
        scratch_shapes=[pltpu.SemaphoreType.DMA((2,))],
        compiler_params=_cparams("arbitrary"),
        name="moe_dispatch",
    )(dest, dest, h3, xg_init)


def _moe_ffn_kernel(te_ref, x_ref, w1_ref, w3_ref, w2_ref, o_ref):
    del te_ref
    x = x_ref[...].reshape(x_ref.shape[0], D_MODEL).astype(BF16)
    a = _dot(x, w1_ref[...].astype(BF16))
    b = _dot(x, w3_ref[...].astype(BF16))
    s = (a * _sigmoid(a)) * b
    y = _dot(s.astype(BF16), w2_ref[...].astype(BF16))
    o_ref[...] = y.reshape(o_ref.shape)


def _moe_ffn(xg, tile_e, w1, w3, w2, rows):
    ntiles = xg.shape[0] // rows
    tile = pl.BlockSpec((rows, TOK_SUB, LANES), lambda i, te: (i, 0, 0))
    return pl.pallas_call(
        _moe_ffn_kernel,
        grid_spec=pltpu.PrefetchScalarGridSpec(
            num_scalar_prefetch=1,
            grid=(ntiles,),
            in_specs=[
                tile,
                pl.BlockSpec((None, D_MODEL, D_EXPERT), lambda i, te: (te[i], 0, 0)),
                pl.BlockSpec((None, D_MODEL, D_EXPERT), lambda i, te: (te[i], 0, 0)),
                pl.BlockSpec((None, D_EXPERT, D_MODEL), lambda i, te: (te[i], 0, 0)),
            ],
            out_specs=tile,
        ),
        out_shape=jax.ShapeDtypeStruct(xg.shape, F32),
        compiler_params=_cparams("parallel"),
        name="moe_grouped_experts",
    )(tile_e, xg, w1, w3, w2)


def _moe_combine_kernel(d0_ref, d1_ref, y_hbm, route_ref, x1_ref, mod_ref, fn_ref, o_ref,
                        ga, gb, sem, *, tm, final_norm):
    _gather_rows(d0_ref, y_hbm, ga, sem.at[0], tm)
    _gather_rows(d1_ref, y_hbm, gb, sem.at[1], tm)
    rt = route_ref[...]
    _wait_rows(ga, sem.at[0])
    _wait_rows(gb, sem.at[1])
    y = rt[:, 2:3] * ga[...].reshape(tm, D_MODEL) + rt[:, 3:4] * gb[...].reshape(tm, D_MODEL)
    x2 = x1_ref[...] + mod_ref[5:6, :] * y
    if final_norm:
        x2 = x2 * lax.rsqrt(jnp.mean(x2 * x2, axis=-1, keepdims=True) + EPS) * fn_ref[...]
    o_ref[...] = x2


def _moe_combine(yg, dest, route, x1, mod, fn, rows_per_mod, tm, final_norm):
    T = x1.shape[0]
    kern = functools.partial(_moe_combine_kernel, tm=tm, final_norm=final_norm)
    return pl.pallas_call(
        kern,
        grid=(T // tm,),
        in_specs=[
            pl.BlockSpec((None, None, 1, tm), lambda i: (0, i, 0, 0), memory_space=pltpu.SMEM),
            pl.BlockSpec((None, None, 1, tm), lambda i: (1, i, 0, 0), memory_space=pltpu.SMEM),
            pl.BlockSpec(memory_space=pl.ANY),
            pl.BlockSpec((tm, LANES), lambda i: (i, 0)),
            pl.BlockSpec((tm, D_MODEL), lambda i: (i, 0)),
            pl.BlockSpec((None, N_MOD, D_MODEL), lambda i: ((i * tm) // rows_per_mod, 0, 0)),
            pl.BlockSpec((1, D_MODEL), lambda i: (0, 0)),
        ],
        out_specs=pl.BlockSpec((tm, D_MODEL), lambda i: (i, 0)),
        out_shape=jax.ShapeDtypeStruct((T, D_MODEL), F32),
        scratch_shapes=[pltpu.VMEM((tm, TOK_SUB, LANES), F32), pltpu.VMEM((tm, TOK_SUB, LANES), F32),
                        pltpu.SemaphoreType.DMA((2,))],
        compiler_params=_cparams("arbitrary"),
        name="moe_combine",
    )(dest, dest, yg, route, x1, mod, fn)


def _moe_rows(T):
    rows = 512 if T >= 8192 else 256
    ntiles = (2 * T + N_EXPERTS * (rows - 1) + rows - 1) // rows
    return rows, ntiles * rows


def _route_plan(route, T):
    tile, n_rows = _moe_rows(T)
    ef = route[:, 0:2].astype(jnp.int32).reshape(-1)
    oh = (ef[:, None] == jnp.arange(N_EXPERTS, dtype=jnp.int32)[None, :]).astype(jnp.int32)
    csum = jnp.cumsum(oh, axis=0)
    rank = jnp.sum((csum - oh) * oh, axis=1)
    counts = csum[-1]
    padded = ((counts + tile - 1) // tile) * tile
    seg_end = jnp.cumsum(padded)
    dest = jnp.sum(oh * (seg_end - padded)[None, :], axis=1) + rank
    tile_row = jnp.arange(n_rows // tile, dtype=jnp.int32) * tile
    tile_e = jnp.minimum(jnp.sum((tile_row[:, None] >= seg_end[None, :]).astype(jnp.int32), axis=1),
                         N_EXPERTS - 1)
    return jnp.transpose(dest.reshape(T, 2)), tile_e


def _split_w_in(w):
    sizes = (H_M * DK_M, H_M * DK_M, H_M * DV_M, H_M * DV_M, 2 * H_M, 2 * H_M,
             H_D * 2 * DQK_D, H_D * 2 * DQK_D, H_D * DV_D,
             H_G * DK_G, H_G * DK_G, H_G * DV_G, 2 * GATE_RANK, H_G * DV_G,
             BRANCH_W, BRANCH_W, N_BRANCH * D_MODEL)
    outs, acc = [], 0
    for s in sizes:
        outs.append(w[:, acc:acc + s])
        acc += s
    return outs


def _pack_layer_params(p):
    (m_q, m_k, m_v, m_o, m_i, m_f, d_q, d_k, d_v, g_q, g_k, g_v, g_a, g_r, c_a, c_b, gate) = \
        _split_w_in(p['w_in'])
    gqk = jnp.concatenate([g_q.reshape(D_MODEL, H_G, DK_G), g_k.reshape(D_MODEL, H_G, DK_G)],
                          axis=2).reshape(D_MODEL, 2 * H_G * DK_G)
    small = jnp.concatenate(
        [m_i, m_f, g_a, jnp.zeros((D_MODEL, LANES - 4 * H_M - 2 * GATE_RANK), F32)], axis=1)
    w16 = jnp.concatenate([m_q, m_k, m_v, m_o, gate, g_r, c_a, c_b, gqk, g_v, d_q],
                          axis=1).astype(BF16)
    w32 = jnp.concatenate([d_k, d_v, small], axis=1).astype(BF16)
    bi = p['b_m_i'].reshape(2, H_M)
    bf = p['b_m_f'].reshape(2, H_M)
    bcol = jnp.stack([bi[0], bi[1], bf[0], bf[1]], axis=-1)
    wup = p['w_gla_up'].reshape(2, GATE_RANK, H_G, DK_G)
    wup_pad = jnp.zeros((H_G, 2, LANES, LANES), F32)
    bup = p['b_gla_gate'].reshape(2, H_G, DK_G)
    for d in range(2):
        blk = jnp.transpose(wup[d], (1, 0, 2))
        blk = jnp.concatenate([blk, blk], axis=-1)
        r0 = SM_GA + d * GATE_RANK
        wup_pad = wup_pad.at[:, d, r0:r0 + GATE_RANK, :].set(blk)
    bup2 = jnp.transpose(jnp.concatenate([bup, bup], axis=-1), (1, 0, 2))[:, :, None, :]
    wr = jnp.concatenate([p['w_group_router'], p['w_expert_router'],
                          jnp.zeros((D_MODEL, LANES - N_GROUPS - N_EXPERTS), F32)], axis=1)
    return dict(
        w16=w16, w32=w32, brow=bcol.reshape(H_M, 4, 1, 1),
        wup=wup_pad.astype(BF16), bup=bup2,
        wdw=jnp.concatenate([p['w_dw'], jnp.zeros((1, BRANCH_W), F32)], axis=0),
        ln_g=p['conv_ln_g'].reshape(1, BRANCH_W), ln_b=p['conv_ln_b'].reshape(1, BRANCH_W),
        hn_m=p['hnorm_m'].reshape(1, BRANCH_W), hn_d=p['hnorm_d'].reshape(1, BRANCH_W),
        hn_g=p['hnorm_g'].reshape(1, BRANCH_W),
        lamv=jnp.stack([p['lam_q1'], p['lam_k1'], p['lam_q2'], p['lam_k2']], axis=0),
        wb=p['w_branch'].astype(BF16), wo=p['w_out'].astype(BF16),
        n1=p['norm1'].reshape(1, D_MODEL), n2=p['norm2'].reshape(1, D_MODEL), wr=wr,
    )


def _rope_tables(S):
    rows = S // GRID_W
    r, col = jnp.meshgrid(jnp.arange(rows, dtype=F32), jnp.arange(GRID_W, dtype=F32), indexing='ij')
    r, col = r.reshape(-1), col.reshape(-1)
    n_freq = DQK_D // 4
    inv = ROPE_BASE ** (-jnp.arange(n_freq, dtype=F32) / n_freq)
    ang = jnp.concatenate([r[:, None] * inv, col[:, None] * inv], axis=-1)
    cos, sin = jnp.cos(ang), jnp.sin(ang)
    cos_t = jnp.tile(cos, (1, LANES // (DQK_D // 2)))
    sin_t = jnp.tile(jnp.concatenate([-sin, sin], axis=-1), (1, LANES // DQK_D))
    return cos_t, sin_t


def _pick_tile(T, cap):
    t = min(T, cap)
    while T % t:
        t //= 2
    return t


def _layer(x2d, mod, pk, B, S, lam_init, ctx, final_norm, fn, xg):
    T = B * S
    rows_per_mod = T // mod.shape[0]
    tm = _pick_tile(rows_per_mod, 1024)
    a16 = _inproj(x2d, mod, pk['n1'], pk['w16'], BF16, rows_per_mod, tm, N_A16 // 4)
    a32 = _inproj(x2d, mod, pk['n1'], pk['w32'], F32, rows_per_mod, tm, N_A32)

    L = min(MLSTM_CHUNK, S)
    sm = a32[:, A32_SM:A32_SM + 4 * H_M]
    grow = jnp.transpose(sm.reshape(B, S // L, L, 4, H_M), (4, 3, 0, 1, 2))
    if ctx is None:
        c0 = jnp.zeros((B, 2, H_M, DK_M, DV_M), F32)
        n0 = jnp.zeros((B, 2, H_M, 1, DK_M), F32)
        m0 = jnp.zeros((B, 2, H_M, 1, LANES), F32)
        s0 = jnp.zeros((B, 2, H_G, DK_G, DV_G), F32)
        attn_ctx = None
    else:
        c0 = ctx['C']
        n0 = ctx['n'][:, :, :, None, :]
        m0 = jnp.broadcast_to(ctx['m'][:, :, :, None, None], (B, 2, H_M, 1, LANES))
        s0 = ctx['S']
        attn_ctx = (ctx['k'], ctx['v'], ctx['layer'], ctx['cos'], ctx['sin'])
    ym, c_f, n_f, m_f = _mlstm(a16, grow, pk['brow'], c0, n0, m0, pk['hn_m'], B, S)
    yd = _attn(a16, a32, pk['lamv'], pk['hn_d'], B, S, lam_init, attn_ctx)
    yg, s_f = _gla(a32, a16, pk['wup'], pk['bup'], s0, pk['hn_g'], B, S)
    yc = _conv(a16, pk['wdw'], pk['ln_g'], pk['ln_b'], B, S)
    x1, h3, route = _merge(x2d, mod, a16, ym, yd, yg, yc, pk['wb'], pk['wo'], pk['n2'], pk['wr'],
                           rows_per_mod, _pick_tile(rows_per_mod, 512))
    dest, tile_e = _route_plan(route, T)
    tmd = _pick_tile(T, 512)
    moe_rows, n_rows = _moe_rows(T)
    if xg is None:
        xg = jnp.zeros((n_rows, TOK_SUB, LANES), F32)
    xg = _moe_dispatch(h3, dest.reshape(2, T // tmd, 1, tmd), xg, tmd)
    y_grouped = _moe_ffn(xg, tile_e + pk['expert_base'], pk['w1'], pk['w3'], pk['w2'], moe_rows)
    tmc = _pick_tile(rows_per_mod, 256)
    x2 = _moe_combine(y_grouped, dest.reshape(2, T // tmc, 1, tmc), route, x1, mod, fn,
                      rows_per_mod, tmc, final_norm)
    state = None
    if ctx is None:
        state = (a32[:, A32_DK:A32_DK + H_D * 2 * DQK_D].reshape(B, S, H_D, 2 * DQK_D),
                 a32[:, A32_DV:A32_DV + H_D * DV_D].reshape(B, S, H_D, DV_D),
                 c_f, n_f[:, :, :, 0, :], m_f[:, :, :, 0, 0], s_f)
    return x2, state, xg


def kernel(x_prompt, x_sample, c, cache_diff_k, cache_diff_v, state_mlstm_C, state_mlstm_n, state_mlstm_m, state_gla_S, c_ctx, w_mod, b_mod, norm1, w_in, b_m_i, b_m_f, lam_q1, lam_k1, lam_q2, lam_k2, w_gla_up, b_gla_gate, w_dw, conv_ln_g, conv_ln_b, hnorm_m, hnorm_d, hnorm_g, w_branch, w_out, norm2, w_group_router, w_expert_router, w_e1, w_e3, w_e2, final_norm):
    Bp, Sp, _ = x_prompt.shape
    Bs, Ss, _ = x_sample.shape
    P = cache_diff_k.shape[2]
    n_cond = 8 * ((1 + Bs + 7) // 8)
    cond = jnp.concatenate([c_ctx[None, :], c, jnp.zeros((n_cond - 1 - Bs, D_MODEL), F32)], axis=0)
    mod_all = _modulation(cond, w_mod, b_mod).reshape(DEPTH, n_cond, N_MOD, D_MODEL)
    cos_t, sin_t = _rope_tables(Ss)
    ck4 = cache_diff_k.reshape(Bs, DEPTH, P, H_D * 2 * DQK_D)
    cv4 = cache_diff_v.reshape(Bs, DEPTH, P, H_D * DV_D)
    fn = final_norm.reshape(1, D_MODEL)
    yp = x_prompt.reshape(Bp * Sp, D_MODEL)
    ys = x_sample.reshape(Bs * Ss, D_MODEL)
    states = []
    xg_p = xg_s = None
    for l in range(DEPTH):
        p = {'w_in': w_in[l], 'b_m_i': b_m_i[l], 'b_m_f': b_m_f[l], 'lam_q1': lam_q1[l],
             'lam_k1': lam_k1[l], 'lam_q2': lam_q2[l], 'lam_k2': lam_k2[l],
             'w_gla_up': w_gla_up[l], 'b_gla_gate': b_gla_gate[l], 'w_dw': w_dw[l],
             'conv_ln_g': conv_ln_g[l], 'conv_ln_b': conv_ln_b[l], 'hnorm_m': hnorm_m[l],
             'hnorm_d': hnorm_d[l], 'hnorm_g': hnorm_g[l], 'w_branch': w_branch[l],
             'w_out': w_out[l], 'norm1': norm1[l], 'norm2': norm2[l],
             'w_group_router': w_group_router[l], 'w_expert_router': w_expert_router[l]}
        pk = _pack_layer_params(p)
        pk.update(w1=w_e1.reshape(DEPTH * N_EXPERTS, D_MODEL, D_EXPERT),
                  w3=w_e3.reshape(DEPTH * N_EXPERTS, D_MODEL, D_EXPERT),
                  w2=w_e2.reshape(DEPTH * N_EXPERTS, D_EXPERT, D_MODEL), expert_base=l * N_EXPERTS)
        lam_init = 0.8 - 0.6 * math.exp(-0.3 * l)
        last = l == DEPTH - 1
        yp, st, xg_p = _layer(yp, mod_all[l, 0:1], pk, Bp, Sp, lam_init, None, last, fn, xg_p)
        states.append(st)
        ctx = {'k': ck4, 'v': cv4, 'layer': l, 'cos': cos_t, 'sin': sin_t,
               'C': state_mlstm_C[:, l], 'n': state_mlstm_n[:, l], 'm': state_mlstm_m[:, l],
               'S': state_gla_S[:, l]}
        ys, _, xg_s = _layer(ys, mod_all[l, 1:1 + Bs], pk, Bs, Ss, lam_init, ctx, last, fn, xg_s)
    stack = lambda i: jnp.stack([s[i] for s in states], axis=1)
    return (yp.reshape(Bp, Sp, D_MODEL), ys.reshape(Bs, Ss, D_MODEL),
            stack(0), stack(1), stack(2), stack(3), stack(4), stack(5))
```

```python
import functools
import math

import jax
import jax.numpy as jnp
from jax import lax
from jax.experimental import pallas as pl
from jax.experimental.pallas import tpu as pltpu

F32 = jnp.float32
BF16 = jnp.bfloat16

D_MODEL = 1024
DEPTH = 2
GRID_W = 64
BRANCH_W = 512
N_BRANCH = 4
H_M, DK_M, DV_M = 4, 128, 128
H_D, DQK_D, DV_D = 4, 64, 128
H_G, DK_G, DV_G = 4, 64, 128
GATE_RANK = 16
GLA_TAU = 16.0
CONV_W = 31
N_GROUPS, EXPERTS_PER_GROUP, D_EXPERT = 4, 4, 512
N_EXPERTS = N_GROUPS * EXPERTS_PER_GROUP
ROPE_BASE = 10000.0
EPS = 1e-6
N_MOD = 6

LANES = 128
VMEM_LIMIT = 48 * 1024 * 1024

A16_MQ, A16_MK, A16_MV, A16_MO = 0, 512, 1024, 1536
A16_GATE, A16_GR, A16_CA, A16_CB = 2048, 6144, 6656, 7168
A16_GQK, A16_GV, A16_DQ = 7680, 8192, 8704
N_A16 = 9216
A32_DK, A32_DV, A32_SM = 0, 512, 1024
N_A32 = 1152
SM_MI, SM_MF, SM_GA = 0, 8, 16

MLSTM_CHUNK = 128
GLA_CHUNK = 64
GLA_SUB = 16
GLA_EXP_CLAMP = 80.0
CONV_ROWS = 128
CONV_PAD = 16
TOK_SUB = D_MODEL // LANES
ROW_DMA_UNROLL = 8


def _cparams(*sem):
    return pltpu.CompilerParams(dimension_semantics=sem, vmem_limit_bytes=VMEM_LIMIT)


def _log_sigmoid(x):
    return jnp.minimum(x, 0.0) - jnp.log1p(jnp.exp(-jnp.abs(x)))


def _sigmoid(x):
    return 0.5 * jnp.tanh(0.5 * x) + 0.5


def _dot(a, b):
    return jnp.dot(a, b, preferred_element_type=F32)


def _dot_nt(a, b):
    return lax.dot_general(a, b, (((1,), (1,)), ((), ())), preferred_element_type=F32)


def _dot_tn(a, b):
    return lax.dot_general(a, b, (((0,), (0,)), ((), ())), preferred_element_type=F32)


def _mod_kernel(c_ref, w_ref, b_ref, o_ref):
    c = c_ref[...]
    a = (c * _sigmoid(c)).astype(BF16)
    o_ref[...] = _dot(a, w_ref[...].astype(BF16)) + b_ref[...]


def _modulation(cond, w_mod, b_mod):
    R = cond.shape[0]
    tn = 512
    nmod = N_MOD * D_MODEL
    return pl.pallas_call(
        _mod_kernel,
        grid=(DEPTH, nmod // tn),
        in_specs=[
            pl.BlockSpec((R, D_MODEL), lambda l, j: (0, 0)),
            pl.BlockSpec((None, D_MODEL, tn), lambda l, j: (l, 0, j)),
            pl.BlockSpec((None, 1, tn), lambda l, j: (l, 0, j)),
        ],
        out_specs=pl.BlockSpec((None, R, tn), lambda l, j: (l, 0, j)),
        out_shape=jax.ShapeDtypeStruct((DEPTH, R, nmod), F32),
        compiler_params=_cparams("parallel", "parallel"),
        name="adaln_mod",
    )(cond, w_mod, b_mod.reshape(DEPTH, 1, nmod))


def _inproj_kernel(x_ref, mod_ref, g_ref, w_ref, o_ref, h_ref):
    @pl.when(pl.program_id(1) == 0)
    def _():
        x = x_ref[...]
        y = x * lax.rsqrt(jnp.mean(x * x, axis=-1, keepdims=True) + EPS) * g_ref[...]
        h_ref[...] = (y * (1.0 + mod_ref[1:2, :]) + mod_ref[0:1, :]).astype(BF16)

    o_ref[...] = _dot(h_ref[...], w_ref[...]).astype(o_ref.dtype)


def _inproj(x2d, mod, g, w, out_dtype, rows_per_mod, tm, tn):
    T = x2d.shape[0]
    N = w.shape[1]
    return pl.pallas_call(
        _inproj_kernel,
        grid=(T // tm, N // tn),
        in_specs=[
            pl.BlockSpec((tm, D_MODEL), lambda i, j: (i, 0)),
            pl.BlockSpec((None, N_MOD, D_MODEL), lambda i, j: ((i * tm) // rows_per_mod, 0, 0)),
            pl.BlockSpec((1, D_MODEL), lambda i, j: (0, 0)),
            pl.BlockSpec((D_MODEL, tn), lambda i, j: (0, j)),
        ],
        out_specs=pl.BlockSpec((tm, tn), lambda i, j: (i, j)),
        out_shape=jax.ShapeDtypeStruct((T, N), out_dtype),
        scratch_shapes=[pltpu.VMEM((tm, D_MODEL), BF16)],
        compiler_params=_cparams("parallel", "arbitrary"),
        name="norm_inproj",
    )(x2d, mod, g, w)


def _mlstm_local(c, q_ref, k_ref, v_ref, gate_ref, pr_ref, bb_ref, mb_ref, kv_ref, rp_ref, L):
    scale = DK_M ** -0.5
    ti = lax.broadcasted_iota(jnp.int32, (L, L), 0)
    si = lax.broadcasted_iota(jnp.int32, (L, L), 1)
    sub = lax.broadcasted_iota(jnp.int32, (8, LANES), 0)
    rows = pl.ds(pl.multiple_of(c * L, L), L)
    q = q_ref[rows, :]
    v_ext = jnp.concatenate([v_ref[rows, :], jnp.ones((L, LANES), BF16)], axis=1)
    k_t = k_ref[rows, :].astype(F32).T
    qk = _dot(q, k_t.astype(BF16)) * scale
    for d in range(2):
        rev = d == 1
        mask = (si >= ti) if rev else (si <= ti)
        i_row = gate_ref[d, 0, pl.ds(c, 1), :]
        f_row = gate_ref[d, 1, pl.ds(c, 1), :]
        b_row = gate_ref[d, 2, pl.ds(c, 1), :]
        b_col = jnp.sum(jnp.where(mask, f_row, 0.0), axis=1, keepdims=True)
        log_d = jnp.where(mask, b_col + (i_row - b_row), -jnp.inf)
        m_loc = jnp.max(log_d, axis=1, keepdims=True)
        smat = qk * jnp.exp(log_d - m_loc)
        pr_ref[d, rows, :] = _dot(smat.astype(BF16), v_ext)
        bb_ref[d, rows, :] = jnp.broadcast_to(b_col, (L, LANES))
        mb_ref[d, rows, :] = jnp.broadcast_to(m_loc, (L, LANES))
        b_last = jnp.sum(f_row, axis=1, keepdims=True)
        ls_row = b_last - b_row + i_row
        m2 = jnp.max(ls_row, axis=1, keepdims=True)
        kw_t = (k_t * jnp.exp(ls_row - m2)).astype(BF16)
        kv_ref[d, c] = scale * _dot(kw_t, v_ext)
        rp_ref[d, c] = jnp.where(sub == 0, b_last, m2)


def _mlstm_carry(c, d, carry, q_ref, pr_ref, bb_ref, mb_ref, kv_ref, rp_ref, h_ref, L):
    cn, m = carry
    two = lambda x: jnp.concatenate([x, x], axis=1)
    rows = pl.ds(pl.multiple_of(c * L, L), L)
    bb = bb_ref[d, rows, :]
    mb = mb_ref[d, rows, :]
    m_t = jnp.maximum(bb + m, mb)
    a_int = jnp.exp(bb + m - m_t)
    e_loc = jnp.exp(mb - m_t)
    nd = two(a_int) * _dot(q_ref[rows, :], cn.astype(BF16)) + two(e_loc) * pr_ref[d, rows, :]
    h_ref[d, rows, :] = nd[:, :DV_M] / jnp.maximum(jnp.abs(nd[:, DV_M:]), jnp.exp(-m_t))
    rp = rp_ref[d, c]
    b_last, m2 = rp[0:1, :], rp[1:2, :]
    m_new = jnp.maximum(b_last + m, m2)
    a_c = jnp.exp(b_last + m - m_new)
    e2 = jnp.exp(m2 - m_new)
    return two(a_c) * cn + two(e2) * kv_ref[d, c], m_new


def _mlstm_gate_rows(gr_ref, br_ref, gate_ref, L):
    ui = lax.broadcasted_iota(jnp.int32, (L, L), 0)
    si = lax.broadcasted_iota(jnp.int32, (L, L), 1)
    for d in range(2):
        src = ((ui >= si) if d == 1 else (ui <= si)).astype(BF16)
        f = _log_sigmoid(gr_ref[2 + d] + br_ref[2 + d])
        f_hi = f.astype(BF16)
        f_r1 = f - f_hi.astype(F32)
        f_mid = f_r1.astype(BF16)
        f_lo = (f_r1 - f_mid.astype(F32)).astype(BF16)
        gate_ref[d, 0] = gr_ref[d] + br_ref[d]
        gate_ref[d, 1] = f
        gate_ref[d, 2] = _dot(f_hi, src) + _dot(f_mid, src) + _dot(f_lo, src)


def _mlstm_kernel(q_ref, k_ref, v_ref, og_ref, gr_ref, br_ref, c0_ref, n0_ref, m0_ref, hn_ref,
                  y_ref, c_out_ref, n_out_ref, m_out_ref,
                  gate_ref, pr_ref, bb_ref, mb_ref, kv_ref, rp_ref, h_ref, *, L, S):
    nch = S // L
    _mlstm_gate_rows(gr_ref, br_ref, gate_ref, L)

    def local(ci, carry):
        _mlstm_local(ci, q_ref, k_ref, v_ref, gate_ref, pr_ref, bb_ref, mb_ref, kv_ref, rp_ref, L)
        return carry

    lax.fori_loop(0, nch, local, 0, unroll=4)
    step = functools.partial(_mlstm_carry, q_ref=q_ref, pr_ref=pr_ref, bb_ref=bb_ref, mb_ref=mb_ref,
                             kv_ref=kv_ref, rp_ref=rp_ref, h_ref=h_ref, L=L)

    def body(ci, carry):
        return step(ci, 0, carry[0]), step(nch - 1 - ci, 1, carry[1])

    def init(d):
        n_rep = jnp.broadcast_to(n0_ref[d], (DK_M, DK_M)).T
        return jnp.concatenate([c0_ref[d], n_rep], axis=1), m0_ref[d]

    fin = lax.fori_loop(0, nch, body, (init(0), init(1)), unroll=2)
    for d in range(2):
        cn, m = fin[d]
        c_out_ref[d] = cn[:, :DV_M]
        n_out_ref[d] = cn[:, DV_M:].T[0:1, :]
        m_out_ref[d] = m

    hm = h_ref[0] + h_ref[1]
    y = hm * lax.rsqrt(jnp.mean(hm * hm, axis=-1, keepdims=True) + EPS) * hn_ref[...]
    y_ref[...] = (y * _sigmoid(og_ref[...].astype(F32))).astype(y_ref.dtype)


def _mlstm(a16, grow, brow, c0, n0, m0, hnorm, B, S):
    L = min(MLSTM_CHUNK, S)
    nch = S // L
    cb = lambda off: off // LANES
    kern = functools.partial(_mlstm_kernel, L=L, S=S)
    return pl.pallas_call(
        kern,
        grid=(B, H_M),
        in_specs=[
            pl.BlockSpec((S, LANES), lambda b, h: (b, cb(A16_MQ) + h)),
            pl.BlockSpec((S, LANES), lambda b, h: (b, cb(A16_MK) + h)),
            pl.BlockSpec((S, LANES), lambda b, h: (b, cb(A16_MV) + h)),
            pl.BlockSpec((S, LANES), lambda b, h: (b, cb(A16_MO) + h)),
            pl.BlockSpec((None, 4, None, nch, L), lambda b, h: (h, 0, b, 0, 0)),
            pl.BlockSpec((None, 4, 1, 1), lambda b, h: (h, 0, 0, 0)),
            pl.BlockSpec((None, 2, None, DK_M, DV_M), lambda b, h: (b, 0, h, 0, 0)),
            pl.BlockSpec((None, 2, None, 1, DK_M), lambda b, h: (b, 0, h, 0, 0)),
            pl.BlockSpec((None, 2, None, 1, LANES), lambda b, h: (b, 0, h, 0, 0)),
            pl.BlockSpec((1, LANES), lambda b, h: (0, h)),
        ],
        out_specs=[
            pl.BlockSpec((S, LANES), lambda b, h: (b, h)),
            pl.BlockSpec((None, 2, None, DK_M, DV_M), lambda b, h: (b, 0, h, 0, 0)),
            pl.BlockSpec((None, 2, None, 1, DK_M), lambda b, h: (b, 0, h, 0, 0)),
            pl.BlockSpec((None, 2, None, 1, LANES), lambda b, h: (b, 0, h, 0, 0)),
        ],
        out_shape=[
            jax.ShapeDtypeStruct((B * S, BRANCH_W), BF16),
            jax.ShapeDtypeStruct((B, 2, H_M, DK_M, DV_M), F32),
            jax.ShapeDtypeStruct((B, 2, H_M, 1, DK_M), F32),
            jax.ShapeDtypeStruct((B, 2, H_M, 1, LANES), F32),
        ],
        scratch_shapes=[pltpu.VMEM((2, 3, nch, L), F32),
                        pltpu.VMEM((2, S, 2 * DV_M), F32), pltpu.VMEM((2, S, LANES), F32),
                        pltpu.VMEM((2, S, LANES), F32), pltpu.VMEM((2, nch, DK_M, 2 * DV_M), F32),
                        pltpu.VMEM((2, nch, 8, LANES), F32), pltpu.VMEM((2, S, DV_M), F32)],
        compiler_params=_cparams("parallel", "parallel"),
        name="mlstm",
    )(a16, a16, a16, a16, grow, brow, c0, n0, m0, hnorm)


def _gla_local(c, q2_ref, k2_ref, v_ref, la_ref, oa_ref, qt_ref, u_ref, dec_ref, L):
    nb = L // GLA_SUB
    ti = lax.broadcasted_iota(jnp.int32, (L, L), 0)
    si = lax.broadcasted_iota(jnp.int32, (L, L), 1)
    row_blk = lax.broadcasted_iota(jnp.int32, (L, LANES), 0) // GLA_SUB
    lo_half = lax.broadcasted_iota(jnp.int32, (L, LANES), 1) < DK_G
    eye = (lax.broadcasted_iota(jnp.int32, (DK_G, LANES), 0)
           == lax.broadcasted_iota(jnp.int32, (DK_G, LANES), 1))
    rows = pl.ds(pl.multiple_of(c * L, L), L)
    q2 = q2_ref[rows, :]
    k2 = k2_ref[rows, :]
    v = v_ref[rows, :]
    row = lax.broadcasted_iota(jnp.int32, (L, LANES), 0)
    for d in range(2):
        rev = d == 1
        mask = (si >= ti) if rev else (si <= ti)
        g2 = la_ref[d, rows, :]
        step = 1
        while step < L:
            if rev:
                g2 = g2 + jnp.where(row < L - step, pltpu.roll(g2, L - step, 0), 0.0)
            else:
                g2 = g2 + jnp.where(row >= step, pltpu.roll(g2, step, 0), 0.0)
            step *= 2
        qt_ref[d, rows, :] = (q2 * jnp.exp(g2))[:, :DK_G].astype(BF16)
        a_parts, b_parts = [], []
        for p in range(nb // 2):
            ia, ib = 2 * p, 2 * p + 1
            ra = ia * GLA_SUB + (GLA_SUB - 1 if rev else 0)
            rb = ib * GLA_SUB + (GLA_SUB - 1 if rev else 0)
            ref2 = jnp.where(lo_half, g2[ra:ra + 1, :], g2[rb:rb + 1, :])
            blk = jnp.where(lo_half, ia, ib)
            in_blk = row_blk == blk
            key_ok = (row_blk >= blk) if rev else (row_blk <= blk)
            a_parts.append(jnp.where(in_blk, q2 * jnp.exp(jnp.minimum(g2 - ref2, 0.0)), 0.0))
            b_parts.append(
                jnp.where(key_ok, k2 * jnp.exp(jnp.minimum(ref2 - g2, GLA_EXP_CLAMP)), 0.0))
        a_big = jnp.concatenate(a_parts, axis=1).astype(BF16)
        b_big = jnp.concatenate(b_parts, axis=1).astype(BF16)
        att = jnp.where(mask, _dot_nt(a_big, b_big), 0.0)
        oa_ref[d, rows, :] = _dot(att.astype(BF16), v)
        gl_row = 0 if rev else L - 1
        glast = g2[gl_row:gl_row + 1, :]
        kd = (k2 * jnp.exp(glast - g2))[:, :DK_G]
        u_ref[d, c] = _dot_tn(kd.astype(BF16), v)
        glast_col = jnp.sum(jnp.where(eye, glast, 0.0), axis=1, keepdims=True)
        dec_ref[d, c] = jnp.broadcast_to(jnp.exp(glast_col), (DK_G, DV_G))


def _gla_kernel(qk_ref, v_ref, sm_ref, wup_ref, bup_ref, s0_ref, gr_ref, hn_ref,
                y_ref, s_out_ref, la_ref, q2_ref, k2_ref, oa_ref, oi_ref, qt_ref, u_ref, dec_ref,
                *, L, S):
    nch = S // L
    sm = sm_ref[...].astype(BF16)
    for d in range(2):
        la_ref[d] = _log_sigmoid(_dot(sm, wup_ref[d]) + bup_ref[d]) * (1.0 / GLA_TAU)
    qk = qk_ref[...].astype(F32)
    qk_sw = pltpu.roll(qk, DK_G, 1)
    lo_half = lax.broadcasted_iota(jnp.int32, qk.shape, 1) < DK_G
    q2_ref[...] = jnp.where(lo_half, qk, qk_sw) * (DK_G ** -0.5)
    k2_ref[...] = jnp.where(lo_half, qk_sw, qk)

    def local(ci, carry):
        _gla_local(ci, q2_ref, k2_ref, v_ref, la_ref, oa_ref, qt_ref, u_ref, dec_ref, L)
        return carry

    lax.fori_loop(0, nch, local, 0, unroll=4)

    def body(ci, carry):
        out = []
        for d, c in ((0, ci), (1, nch - 1 - ci)):
            rows = pl.ds(pl.multiple_of(c * L, L), L)
            st = carry[d]
            oi_ref[d, rows, :] = _dot(qt_ref[d, rows, :], st.astype(BF16))
            out.append(dec_ref[d, c] * st + u_ref[d, c])
        return tuple(out)

    st_f, st_b = lax.fori_loop(0, nch, body, (s0_ref[0], s0_ref[1]), unroll=4)
    s_out_ref[0] = st_f
    s_out_ref[1] = st_b

    og = (oa_ref[0] + oi_ref[0]) + (oa_ref[1] + oi_ref[1])
    y = og * lax.rsqrt(jnp.mean(og * og, axis=-1, keepdims=True) + EPS) * hn_ref[...]
    gr = gr_ref[...].astype(F32)
    y_ref[...] = (y * (gr * _sigmoid(gr))).astype(y_ref.dtype)


def _gla(a32, a16, wup, bup, s0, hnorm, B, S):
    L = min(GLA_CHUNK, S)
    nch = S // L
    cb = lambda off: off // LANES
    kern = functools.partial(_gla_kernel, L=L, S=S)
    return pl.pallas_call(
        kern,
        grid=(B, H_G),
        in_specs=[
            pl.BlockSpec((S, LANES), lambda b, h: (b, cb(A16_GQK) + h)),
            pl.BlockSpec((S, LANES), lambda b, h: (b, cb(A16_GV) + h)),
            pl.BlockSpec((S, LANES), lambda b, h: (b, cb(A32_SM))),
            pl.BlockSpec((None, 2, LANES, LANES), lambda b, h: (h, 0, 0, 0)),
            pl.BlockSpec((None, 2, 1, LANES), lambda b, h: (h, 0, 0, 0)),
            pl.BlockSpec((None, 2, None, DK_G, DV_G), lambda b, h: (b, 0, h, 0, 0)),
            pl.BlockSpec((S, LANES), lambda b, h: (b, cb(A16_GR) + h)),
            pl.BlockSpec((1, LANES), lambda b, h: (0, h)),
        ],
        out_specs=[
            pl.BlockSpec((S, LANES), lambda b, h: (b, h)),
            pl.BlockSpec((None, 2, None, DK_G, DV_G), lambda b, h: (b, 0, h, 0, 0)),
        ],
        out_shape=[
            jax.ShapeDtypeStruct((B * S, BRANCH_W), BF16),
            jax.ShapeDtypeStruct((B, 2, H_G, DK_G, DV_G), F32),
        ],
        scratch_shapes=[pltpu.VMEM((2, S, LANES), F32), pltpu.VMEM((S, LANES), F32),
                        pltpu.VMEM((S, LANES), F32), pltpu.VMEM((2, S, DV_G), F32),
                        pltpu.VMEM((2, S, DV_G), F32), pltpu.VMEM((2, S, DK_G), BF16),
                        pltpu.VMEM((2, nch, DK_G, DV_G), F32), pltpu.VMEM((2, nch, DK_G, DV_G), F32)],
        compiler_params=_cparams("parallel", "parallel"),
        name="gla",
    )(a16, a16, a32, wup, bup, s0, a16, hnorm)


def _rope(x, cos, sin_signed):
    lane = lax.broadcasted_iota(jnp.int32, x.shape, 1)
    first = (lane % DQK_D) < (DQK_D // 2)
    partner = jnp.where(first, pltpu.roll(x, LANES - DQK_D // 2, 1), pltpu.roll(x, DQK_D // 2, 1))
    return x * cos + partner * sin_signed


def _attn_kernel(*refs, S, P, TQ, lam_init, has_ctx):
    if has_ctx:
        (q_ref, k_ref, v_ref, ck_ref, cv_ref, cos_ref, sin_ref, lam_ref, hn_ref,
         y_ref, kk_ref, vv_ref) = refs
    else:
        q_ref, k_ref, v_ref, lam_ref, hn_ref, y_ref, kk_ref, vv_ref = refs
    qi = pl.program_id(2)

    @pl.when(qi == 0)
    def _():
        k = k_ref[...]
        if has_ctx:
            k = _rope(k, cos_ref[...], sin_ref[...])
            kk_ref[S:S + P, :] = ck_ref[...].astype(BF16)
            vv_ref[S:S + P, :] = cv_ref[...].astype(BF16)
        kk_ref[0:S, :] = k.astype(BF16)
        vv_ref[0:S, :] = v_ref[...].astype(BF16)

    kk = kk_ref[...]
    vv = vv_ref[...]
    lv = lam_ref[...]
    lam = (jnp.exp(jnp.sum(lv[0:1, :] * lv[1:2, :], axis=-1, keepdims=True))
           - jnp.exp(jnp.sum(lv[2:3, :] * lv[3:4, :], axis=-1, keepdims=True)) + lam_init)
    q = q_ref[...].astype(F32)
    if has_ctx:
        r0 = pl.multiple_of(qi * TQ, TQ)
        q = _rope(q, cos_ref[pl.ds(r0, TQ), :], sin_ref[pl.ds(r0, TQ), :])
    q = q * (DQK_D ** -0.5 * math.log2(math.e))
    lane = lax.broadcasted_iota(jnp.int32, q.shape, 1)
    es, ls = [], []
    for comp in range(2):
        sel = (lane < DQK_D) if comp == 0 else (lane >= DQK_D)
        s = _dot_nt(jnp.where(sel, q, 0.0).astype(BF16), kk)
        e = jnp.exp2(s - jnp.max(s, axis=-1, keepdims=True))
        es.append(e)
        ls.append(jnp.sum(e, axis=-1, keepdims=True))
    w = es[0] - es[1] * (lam * ls[0] / ls[1])
    o = _dot(w.astype(BF16), vv) * (1.0 / ls[0])
    y = o * lax.rsqrt(jnp.mean(o * o, axis=-1, keepdims=True) + EPS) * hn_ref[...]
    y_ref[...] = (y * (1.0 - lam_init)).astype(y_ref.dtype)


def _attn(a16, a32, lamv, hnorm, B, S, lam_init, ctx=None):
    TQ = min(256, S)
    nq = S // TQ
    has_ctx = ctx is not None
    P = ctx[0].shape[2] if has_ctx else 0
    cb = lambda off: off // LANES
    kern = functools.partial(_attn_kernel, S=S, P=P, TQ=TQ, lam_init=lam_init, has_ctx=has_ctx)
    in_specs = [
        pl.BlockSpec((TQ, LANES), lambda b, h, i: (b * nq + i, cb(A16_DQ) + h)),
        pl.BlockSpec((S, LANES), lambda b, h, i: (b, cb(A32_DK) + h)),
        pl.BlockSpec((S, LANES), lambda b, h, i: (b, cb(A32_DV) + h)),
    ]
    args = [a16, a32, a32]
    if has_ctx:
        ck, cv, layer, cos, sin = ctx
        in_specs += [
            pl.BlockSpec((None, None, P, LANES), lambda b, h, i: (b, layer, 0, h)),
            pl.BlockSpec((None, None, P, LANES), lambda b, h, i: (b, layer, 0, h)),
            pl.BlockSpec((S, LANES), lambda b, h, i: (0, 0)),
            pl.BlockSpec((S, LANES), lambda b, h, i: (0, 0)),
        ]
        args += [ck, cv, cos, sin]
    in_specs += [
        pl.BlockSpec((4, DQK_D), lambda b, h, i: (0, 0)),
        pl.BlockSpec((1, LANES), lambda b, h, i: (0, h)),
    ]
    args += [lamv, hnorm]
    return pl.pallas_call(
        kern,
        grid=(B, H_D, nq),
        in_specs=in_specs,
        out_specs=pl.BlockSpec((TQ, LANES), lambda b, h, i: (b * nq + i, h)),
        out_shape=jax.ShapeDtypeStruct((B * S, BRANCH_W), BF16),
        scratch_shapes=[pltpu.VMEM((S + P, LANES), BF16), pltpu.VMEM((S + P, LANES), BF16)],
        compiler_params=_cparams("parallel", "parallel", "arbitrary"),
        name="diff_attn",
    )(*args)


def _conv_kernel(ca_ref, cb_ref, w_ref, g_ref, b_ref, y_ref, pad_ref, cv_ref, *, S):
    ca = ca_ref[...].astype(F32)
    cbv = cb_ref[...].astype(F32)
    zeros = jnp.zeros((CONV_PAD, BRANCH_W), F32)
    pad_ref[0:CONV_PAD, :] = zeros
    pad_ref[CONV_PAD + S:2 * CONV_PAD + S, :] = zeros
    pad_ref[CONV_PAD:CONV_PAD + S, :] = ca * _sigmoid(cbv)
    off = CONV_PAD - CONV_W // 2

    win_rows = CONV_ROWS + 2 * CONV_PAD

    def body(i, carry):
        base = pl.multiple_of(i * CONV_ROWS, CONV_ROWS)
        for lb in range(BRANCH_W // LANES):
            cols = slice(lb * LANES, (lb + 1) * LANES)
            win = pad_ref[pl.ds(base, win_rows), cols]
            acc = jnp.zeros((CONV_ROWS, LANES), F32)
            for r in range(8):
                rolled = win if r == 0 else pltpu.roll(win, win_rows - r, 0)
                for a in range(2 * CONV_PAD // 8):
                    j = 8 * a + r - off
                    if 0 <= j < CONV_W:
                        acc = acc + rolled[8 * a:8 * a + CONV_ROWS, :] * w_ref[j:j + 1, cols]
            cv_ref[:, cols] = acc
        acc = cv_ref[...]
        mu = jnp.mean(acc, axis=-1, keepdims=True)
        xc = acc - mu
        yn = xc * lax.rsqrt(jnp.mean(xc * xc, axis=-1, keepdims=True) + EPS) * g_ref[...] + b_ref[...]
        y_ref[pl.ds(base, CONV_ROWS), :] = (yn * _sigmoid(yn)).astype(y_ref.dtype)
        return carry

    lax.fori_loop(0, S // CONV_ROWS, body, 0)


def _conv(a16, w_dw, ln_g, ln_b, B, S):
    cb = lambda off: off // BRANCH_W
    kern = functools.partial(_conv_kernel, S=S)
    return pl.pallas_call(
        kern,
        grid=(B,),
        in_specs=[
            pl.BlockSpec((S, BRANCH_W), lambda b: (b, cb(A16_CA))),
            pl.BlockSpec((S, BRANCH_W), lambda b: (b, cb(A16_CB))),
            pl.BlockSpec((CONV_W + 1, BRANCH_W), lambda b: (0, 0)),
            pl.BlockSpec((1, BRANCH_W), lambda b: (0, 0)),
            pl.BlockSpec((1, BRANCH_W), lambda b: (0, 0)),
        ],
        out_specs=pl.BlockSpec((S, BRANCH_W), lambda b: (b, 0)),
        out_shape=jax.ShapeDtypeStruct((B * S, BRANCH_W), BF16),
        scratch_shapes=[pltpu.VMEM((S + 2 * CONV_PAD, BRANCH_W), F32),
                        pltpu.VMEM((CONV_ROWS, BRANCH_W), F32)],
        compiler_params=_cparams("parallel"),
        name="glu_conv_ln",
    )(a16, a16, w_dw, ln_g, ln_b)


def _merge_kernel(x_ref, mod_ref, ym_ref, yd_ref, yg_ref, yc_ref, g0_ref, g1_ref, g2_ref, g3_ref,
                  wb_ref, wo_ref, n2_ref, wr_ref, x1_ref, h2_ref, route_ref):
    ys = (ym_ref, yd_ref, yg_ref, yc_ref)
    gs = (g0_ref, g1_ref, g2_ref, g3_ref)
    merged = None
    for nbr in range(N_BRANCH):
        br = _dot(ys[nbr][...], wb_ref[nbr])
        term = _sigmoid(gs[nbr][...].astype(F32)) * br
        merged = term if merged is None else merged + term
    out = _dot(merged.astype(BF16), wo_ref[...])
    x1 = x_ref[...] + mod_ref[2:3, :] * out
    x1_ref[...] = x1
    y = x1 * lax.rsqrt(jnp.mean(x1 * x1, axis=-1, keepdims=True) + EPS) * n2_ref[...]
    h2 = y * (1.0 + mod_ref[4:5, :]) + mod_ref[3:4, :]
    h2_ref[...] = h2.reshape(h2_ref.shape)
    wr = wr_ref[...]
    h_hi, w_hi = h2.astype(BF16), wr.astype(BF16)
    h_lo = (h2 - h_hi.astype(F32)).astype(BF16)
    w_lo = (wr - w_hi.astype(F32)).astype(BF16)
    logits = _dot(h_hi, w_hi) + (_dot(h_hi, w_lo) + _dot(h_lo, w_hi))
    lane = lax.broadcasted_iota(jnp.int32, logits.shape, 1)
    neg = -jnp.inf
    big = jnp.int32(LANES)
    is_g = lane < N_GROUPS
    gl = jnp.where(is_g, logits, neg)
    gmax = jnp.max(gl, axis=-1, keepdims=True)
    gidx = jnp.min(jnp.where(is_g & (gl == gmax), lane, big), axis=-1, keepdims=True)
    g_p = 1.0 / jnp.sum(jnp.where(is_g, jnp.exp(gl - gmax), 0.0), axis=-1, keepdims=True)
    e_lane = lane - N_GROUPS
    in_grp = (e_lane >= 0) & (e_lane < N_EXPERTS) & ((e_lane // EXPERTS_PER_GROUP) == gidx)
    el = jnp.where(in_grp, logits, neg)
    v1 = jnp.max(el, axis=-1, keepdims=True)
    i1 = jnp.min(jnp.where(in_grp & (el == v1), lane, big), axis=-1, keepdims=True)
    el2 = jnp.where(lane == i1, neg, el)
    v2 = jnp.max(el2, axis=-1, keepdims=True)
    i2 = jnp.min(jnp.where(in_grp & (lane != i1) & (el2 == v2), lane, big), axis=-1, keepdims=True)
    e2 = jnp.exp(v2 - v1)
    w1 = g_p / (1.0 + e2)
    w2 = g_p * e2 / (1.0 + e2)
    id1 = (i1 - N_GROUPS).astype(F32)
    id2 = (i2 - N_GROUPS).astype(F32)
    route_ref[...] = jnp.where(lane == 0, id1, jnp.where(lane == 1, id2,
                               jnp.where(lane == 2, w1, jnp.where(lane == 3, w2, 0.0))))


def _merge(x2d, mod, a16, ym, yd, yg, yc, wb, wo, n2, wr, rows_per_mod, tm):
    T = x2d.shape[0]
    gcb = A16_GATE // D_MODEL
    row = lambda i: (i, 0)
    return pl.pallas_call(
        _merge_kernel,
        grid=(T // tm,),
        in_specs=[
            pl.BlockSpec((tm, D_MODEL), row),
            pl.BlockSpec((None, N_MOD, D_MODEL), lambda i: ((i * tm) // rows_per_mod, 0, 0)),
            pl.BlockSpec((tm, BRANCH_W), row),
            pl.BlockSpec((tm, BRANCH_W), row),
            pl.BlockSpec((tm, BRANCH_W), row),
            pl.BlockSpec((tm, BRANCH_W), row),
            pl.BlockSpec((tm, D_MODEL), lambda i: (i, gcb + 0)),
            pl.BlockSpec((tm, D_MODEL), lambda i: (i, gcb + 1)),
            pl.BlockSpec((tm, D_MODEL), lambda i: (i, gcb + 2)),
            pl.BlockSpec((tm, D_MODEL), lambda i: (i, gcb + 3)),
            pl.BlockSpec((N_BRANCH, BRANCH_W, D_MODEL), lambda i: (0, 0, 0)),
            pl.BlockSpec((D_MODEL, D_MODEL), lambda i: (0, 0)),
            pl.BlockSpec((1, D_MODEL), lambda i: (0, 0)),
            pl.BlockSpec((D_MODEL, LANES), lambda i: (0, 0)),
        ],
        out_specs=[
            pl.BlockSpec((tm, D_MODEL), row),
            pl.BlockSpec((tm, TOK_SUB, LANES), lambda i: (i, 0, 0)),
            pl.BlockSpec((tm, LANES), row),
        ],
        out_shape=[
            jax.ShapeDtypeStruct((T, D_MODEL), F32),
            jax.ShapeDtypeStruct((T, TOK_SUB, LANES), F32),
            jax.ShapeDtypeStruct((T, LANES), F32),
        ],
        compiler_params=_cparams("parallel"),
        name="merge_outproj_route",
    )(x2d, mod, ym, yd, yg, yc, a16, a16, a16, a16, wb, wo, n2, wr)


def _gather_rows(idx_ref, src_hbm, dst, sem, n):
    def body(j, carry):
        for u in range(ROW_DMA_UNROLL):
            r = j * ROW_DMA_UNROLL + u
            pltpu.make_async_copy(src_hbm.at[idx_ref[0, r]], dst.at[r], sem).start(priority=u % 2)
        return carry

    lax.fori_loop(0, n // ROW_DMA_UNROLL, body, 0)


def _scatter_rows(idx_ref, src, dst_hbm, sem, n):
    def body(j, carry):
        for u in range(ROW_DMA_UNROLL):
            r = j * ROW_DMA_UNROLL + u
            pltpu.make_async_copy(src.at[r], dst_hbm.at[idx_ref[0, r]], sem).start(priority=u % 2)
        return carry

    lax.fori_loop(0, n // ROW_DMA_UNROLL, body, 0)


def _wait_rows(buf, sem):
    pltpu.make_async_copy(buf, buf, sem).wait()


def _moe_dispatch_kernel(d0_ref, d1_ref, h_ref, xg_in, xg_out, sem, *, tm):
    del xg_in
    _scatter_rows(d0_ref, h_ref, xg_out, sem.at[0], tm)
    _scatter_rows(d1_ref, h_ref, xg_out, sem.at[1], tm)
    _wait_rows(h_ref, sem.at[0])
    _wait_rows(h_ref, sem.at[1])


def _moe_dispatch(h3, dest, xg_init, tm):
    T = h3.shape[0]
    kern = functools.partial(_moe_dispatch_kernel, tm=tm)
    return pl.pallas_call(
        kern,
        grid=(T // tm,),
        in_specs=[
            pl.BlockSpec((None, None, 1, tm), lambda i: (0, i, 0, 0), memory_space=pltpu.SMEM),
            pl.BlockSpec((None, None, 1, tm), lambda i: (1, i, 0, 0), memory_space=pltpu.SMEM),
            pl.BlockSpec((tm, TOK_SUB, LANES), lambda i: (i, 0, 0)),
            pl.BlockSpec(memory_space=pl.ANY),
        ],
        out_specs=pl.BlockSpec(memory_space=pl.ANY),
        out_shape=jax.ShapeDtypeStruct(xg_init.shape, F32),
        input_output_aliases={3: 0},
        scratch_shapes=[pltpu.SemaphoreType.DMA((2,))],
        compiler_params=_cparams("arbitrary"),
        name="moe_dispatch",
    )(dest, dest, h3, xg_init)


def _moe_ffn_kernel(te_ref, x_ref, w1_ref, w3_ref, w2_ref, o_ref):
    del te_ref
    x = x_ref[...].reshape(x_ref.shape[0], D_MODEL).astype(BF16)
    a = _dot(x, w1_ref[...].astype(BF16))
    b = _dot(x, w3_ref[...].astype(BF16))
    s = (a * _sigmoid(a)) * b
    y = _dot(s.astype(BF16), w2_ref[...].astype(BF16))
    o_ref[...] = y.reshape(o_ref.shape)


def _moe_ffn(xg, tile_e, w1, w3, w2, rows):
    ntiles = xg.shape[0] // rows
    tile = pl.BlockSpec((rows, TOK_SUB, LANES), lambda i, te: (i, 0, 0))
    return pl.pallas_call(
        _moe_ffn_kernel,
        grid_spec=pltpu.PrefetchScalarGridSpec(
            num_scalar_prefetch=1,
            grid=(ntiles,),
            in_specs=[
                tile,
                pl.BlockSpec((None, D_MODEL, D_EXPERT), lambda i, te: (te[i], 0, 0)),
                pl.BlockSpec((None, D_MODEL, D_EXPERT), lambda i, te: (te[i], 0, 0)),
                pl.BlockSpec((None, D_EXPERT, D_MODEL), lambda i, te: (te[i], 0, 0)),
            ],
            out_specs=tile,
        ),
        out_shape=jax.ShapeDtypeStruct(xg.shape, F32),
        compiler_params=_cparams("parallel"),
        name="moe_grouped_experts",
    )(tile_e, xg, w1, w3, w2)


def _moe_combine_kernel(d0_ref, d1_ref, y_hbm, route_ref, x1_ref, mod_ref, fn_ref, o_ref,
                        ga, gb, sem, *, tm, final_norm):
    _gather_rows(d0_ref, y_hbm, ga, sem.at[0], tm)
    _gather_rows(d1_ref, y_hbm, gb, sem.at[1], tm)
    rt = route_ref[...]
    _wait_rows(ga, sem.at[0])
    _wait_rows(gb, sem.at[1])
    y = rt[:, 2:3] * ga[...].reshape(tm, D_MODEL) + rt[:, 3:4] * gb[...].reshape(tm, D_MODEL)
    x2 = x1_ref[...] + mod_ref[5:6, :] * y
    if final_norm:
        x2 = x2 * lax.rsqrt(jnp.mean(x2 * x2, axis=-1, keepdims=True) + EPS) * fn_ref[...]
    o_ref[...] = x2


def _moe_combine(yg, dest, route, x1, mod, fn, rows_per_mod, tm, final_norm):
    T = x1.shape[0]
    kern = functools.partial(_moe_combine_kernel, tm=tm, final_norm=final_norm)
    return pl.pallas_call(
        kern,
        grid=(T // tm,),
        in_specs=[
            pl.BlockSpec((None, None, 1, tm), lambda i: (0, i, 0, 0), memory_space=pltpu.SMEM),
            pl.BlockSpec((None, None, 1, tm), lambda i: (1, i, 0, 0), memory_space=pltpu.SMEM),
            pl.BlockSpec(memory_space=pl.ANY),
            pl.BlockSpec((tm, LANES), lambda i: (i, 0)),
            pl.BlockSpec((tm, D_MODEL), lambda i: (i, 0)),
            pl.BlockSpec((None, N_MOD, D_MODEL), lambda i: ((i * tm) // rows_per_mod, 0, 0)),
            pl.BlockSpec((1, D_MODEL), lambda i: (0, 0)),
        ],
        out_specs=pl.BlockSpec((tm, D_MODEL), lambda i: (i, 0)),
        out_shape=jax.ShapeDtypeStruct((T, D_MODEL), F32),
        scratch_shapes=[pltpu.VMEM((tm, TOK_SUB, LANES), F32), pltpu.VMEM((tm, TOK_SUB, LANES), F32),
                        pltpu.SemaphoreType.DMA((2,))],
        compiler_params=_cparams("arbitrary"),
        name="moe_combine",
    )(dest, dest, yg, route, x1, mod, fn)


def _moe_rows(T):
    rows = 512 if T >= 8192 else 256
    ntiles = (2 * T + N_EXPERTS * (rows - 1) + rows - 1) // rows
    return rows, ntiles * rows


def _route_plan(route, T):
    tile, n_rows = _moe_rows(T)
    ef = route[:, 0:2].astype(jnp.int32).reshape(-1)
    oh = (ef[:, None] == jnp.arange(N_EXPERTS, dtype=jnp.int32)[None, :]).astype(jnp.int32)
    csum = jnp.cumsum(oh, axis=0)
    rank = jnp.sum((csum - oh) * oh, axis=1)
    counts = csum[-1]
    padded = ((counts + tile - 1) // tile) * tile
    seg_end = jnp.cumsum(padded)
    dest = jnp.sum(oh * (seg_end - padded)[None, :], axis=1) + rank
    tile_row = jnp.arange(n_rows // tile, dtype=jnp.int32) * tile
    tile_e = jnp.minimum(jnp.sum((tile_row[:, None] >= seg_end[None, :]).astype(jnp.int32), axis=1),
                         N_EXPERTS - 1)
    return jnp.transpose(dest.reshape(T, 2)), tile_e


def _split_w_in(w):
    sizes = (H_M * DK_M, H_M * DK_M, H_M * DV_M, H_M * DV_M, 2 * H_M, 2 * H_M,
             H_D * 2 * DQK_D, H_D * 2 * DQK_D, H_D * DV_D,
             H_G * DK_G, H_G * DK_G, H_G * DV_G, 2 * GATE_RANK, H_G * DV_G,
             BRANCH_W, BRANCH_W, N_BRANCH * D_MODEL)
    outs, acc = [], 0
    for s in sizes:
        outs.append(w[:, acc:acc + s])
        acc += s
    return outs


def _pack_layer_params(p):
    (m_q, m_k, m_v, m_o, m_i, m_f, d_q, d_k, d_v, g_q, g_k, g_v, g_a, g_r, c_a, c_b, gate) = \
        _split_w_in(p['w_in'])
    gqk = jnp.concatenate([g_q.reshape(D_MODEL, H_G, DK_G), g_k.reshape(D_MODEL, H_G, DK_G)],
                          axis=2).reshape(D_MODEL, 2 * H_G * DK_G)
    small = jnp.concatenate(
        [m_i, m_f, g_a, jnp.zeros((D_MODEL, LANES - 4 * H_M - 2 * GATE_RANK), F32)], axis=1)
    w16 = jnp.concatenate([m_q, m_k, m_v, m_o, gate, g_r, c_a, c_b, gqk, g_v, d_q],
                          axis=1).astype(BF16)
    w32 = jnp.concatenate([d_k, d_v, small], axis=1).astype(BF16)
    bi = p['b_m_i'].reshape(2, H_M)
    bf = p['b_m_f'].reshape(2, H_M)
    bcol = jnp.stack([bi[0], bi[1], bf[0], bf[1]], axis=-1)
    wup = p['w_gla_up'].reshape(2, GATE_RANK, H_G, DK_G)
    wup_pad = jnp.zeros((H_G, 2, LANES, LANES), F32)
    bup = p['b_gla_gate'].reshape(2, H_G, DK_G)
    for d in range(2):
        blk = jnp.transpose(wup[d], (1, 0, 2))
        blk = jnp.concatenate([blk, blk], axis=-1)
        r0 = SM_GA + d * GATE_RANK
        wup_pad = wup_pad.at[:, d, r0:r0 + GATE_RANK, :].set(blk)
    bup2 = jnp.transpose(jnp.concatenate([bup, bup], axis=-1), (1, 0, 2))[:, :, None, :]
    wr = jnp.concatenate([p['w_group_router'], p['w_expert_router'],
                          jnp.zeros((D_MODEL, LANES - N_GROUPS - N_EXPERTS), F32)], axis=1)
    return dict(
        w16=w16, w32=w32, brow=bcol.reshape(H_M, 4, 1, 1),
        wup=wup_pad.astype(BF16), bup=bup2,
        wdw=jnp.concatenate([p['w_dw'], jnp.zeros((1, BRANCH_W), F32)], axis=0),
        ln_g=p['conv_ln_g'].reshape(1, BRANCH_W), ln_b=p['conv_ln_b'].reshape(1, BRANCH_W),
        hn_m=p['hnorm_m'].reshape(1, BRANCH_W), hn_d=p['hnorm_d'].reshape(1, BRANCH_W),
        hn_g=p['hnorm_g'].reshape(1, BRANCH_W),
        lamv=jnp.stack([p['lam_q1'], p['lam_k1'], p['lam_q2'], p['lam_k2']], axis=0),
        wb=p['w_branch'].astype(BF16), wo=p['w_out'].astype(BF16),
        n1=p['norm1'].reshape(1, D_MODEL), n2=p['norm2'].reshape(1, D_MODEL), wr=wr,
    )


def _rope_tables(S):
    rows = S // GRID_W
    r, col = jnp.meshgrid(jnp.arange(rows, dtype=F32), jnp.arange(GRID_W, dtype=F32), indexing='ij')
    r, col = r.reshape(-1), col.reshape(-1)
    n_freq = DQK_D // 4
    inv = ROPE_BASE ** (-jnp.arange(n_freq, dtype=F32) / n_freq)
    ang = jnp.concatenate([r[:, None] * inv, col[:, None] * inv], axis=-1)
    cos, sin = jnp.cos(ang), jnp.sin(ang)
    cos_t = jnp.tile(cos, (1, LANES // (DQK_D // 2)))
    sin_t = jnp.tile(jnp.concatenate([-sin, sin], axis=-1), (1, LANES // DQK_D))
    return cos_t, sin_t


def _pick_tile(T, cap):
    t = min(T, cap)
    while T % t:
        t //= 2
    return t


def _layer(x2d, mod, pk, B, S, lam_init, ctx, final_norm, fn, xg):
    T = B * S
    rows_per_mod = T // mod.shape[0]
    tm = _pick_tile(rows_per_mod, 1024)
    a16 = _inproj(x2d, mod, pk['n1'], pk['w16'], BF16, rows_per_mod, tm, N_A16 // 4)
    a32 = _inproj(x2d, mod, pk['n1'], pk['w32'], F32, rows_per_mod, tm, N_A32)

    L = min(MLSTM_CHUNK, S)
    sm = a32[:, A32_SM:A32_SM + 4 * H_M]
    grow = jnp.transpose(sm.reshape(B, S // L, L, 4, H_M), (4, 3, 0, 1, 2))
    if ctx is None:
        c0 = jnp.zeros((B, 2, H_M, DK_M, DV_M), F32)
        n0 = jnp.zeros((B, 2, H_M, 1, DK_M), F32)
        m0 = jnp.zeros((B, 2, H_M, 1, LANES), F32)
        s0 = jnp.zeros((B, 2, H_G, DK_G, DV_G), F32)
        attn_ctx = None
    else:
        c0 = ctx['C']
        n0 = ctx['n'][:, :, :, None, :]
        m0 = jnp.broadcast_to(ctx['m'][:, :, :, None, None], (B, 2, H_M, 1, LANES))
        s0 = ctx['S']
        attn_ctx = (ctx['k'], ctx['v'], ctx['layer'], ctx['cos'], ctx['sin'])
    ym, c_f, n_f, m_f = _mlstm(a16, grow, pk['brow'], c0, n0, m0, pk['hn_m'], B, S)
    yd = _attn(a16, a32, pk['lamv'], pk['hn_d'], B, S, lam_init, attn_ctx)
    yg, s_f = _gla(a32, a16, pk['wup'], pk['bup'], s0, pk['hn_g'], B, S)
    yc = _conv(a16, pk['wdw'], pk['ln_g'], pk['ln_b'], B, S)
    x1, h3, route = _merge(x2d, mod, a16, ym, yd, yg, yc, pk['wb'], pk['wo'], pk['n2'], pk['wr'],
                           rows_per_mod, _pick_tile(rows_per_mod, 512))
    dest, tile_e = _route_plan(route, T)
    tmd = _pick_tile(T, 512)
    moe_rows, n_rows = _moe_rows(T)
    if xg is None:
        xg = jnp.zeros((n_rows, TOK_SUB, LANES), F32)
    xg = _moe_dispatch(h3, dest.reshape(2, T // tmd, 1, tmd), xg, tmd)
    y_grouped = _moe_ffn(xg, tile_e + pk['expert_base'], pk['w1'], pk['w3'], pk['w2'], moe_rows)
    tmc = _pick_tile(rows_per_mod, 256)
    x2 = _moe_combine(y_grouped, dest.reshape(2, T // tmc, 1, tmc), route, x1, mod, fn,
                      rows_per_mod, tmc, final_norm)
    state = None
    if ctx is None:
        state = (a32[:, A32_DK:A32_DK + H_D * 2 * DQK_D].reshape(B, S, H_D, 2 * DQK_D),
                 a32[:, A32_DV:A32_DV + H_D * DV_D].reshape(B, S, H_D, DV_D),
                 c_f, n_f[:, :, :, 0, :], m_f[:, :, :, 0, 0], s_f)
    return x2, state, xg


def kernel(x_prompt, x_sample, c, cache_diff_k, cache_diff_v, state_mlstm_C, state_mlstm_n, state_mlstm_m, state_gla_S, c_ctx, w_mod, b_mod, norm1, w_in, b_m_i, b_m_f, lam_q1, lam_k1, lam_q2, lam_k2, w_gla_up, b_gla_gate, w_dw, conv_ln_g, conv_ln_b, hnorm_m, hnorm_d, hnorm_g, w_branch, w_out, norm2, w_group_router, w_expert_router, w_e1, w_e3, w_e2, final_norm):
    Bp, Sp, _ = x_prompt.shape
    Bs, Ss, _ = x_sample.shape
    P = cache_diff_k.shape[2]
    n_cond = 8 * ((1 + Bs + 7) // 8)
    cond = jnp.concatenate([c_ctx[None, :], c, jnp.zeros((n_cond - 1 - Bs, D_MODEL), F32)], axis=0)
    mod_all = _modulation(cond, w_mod, b_mod).reshape(DEPTH, n_cond, N_MOD, D_MODEL)
    cos_t, sin_t = _rope_tables(Ss)
    ck4 = cache_diff_k.reshape(Bs, DEPTH, P, H_D * 2 * DQK_D)
    cv4 = cache_diff_v.reshape(Bs, DEPTH, P, H_D * DV_D)
    fn = final_norm.reshape(1, D_MODEL)
    yp = x_prompt.reshape(Bp * Sp, D_MODEL)
    ys = x_sample.reshape(Bs * Ss, D_MODEL)
    states = []
    xg_p = xg_s = None
    for l in range(DEPTH):
        p = {'w_in': w_in[l], 'b_m_i': b_m_i[l], 'b_m_f': b_m_f[l], 'lam_q1': lam_q1[l],
             'lam_k1': lam_k1[l], 'lam_q2': lam_q2[l], 'lam_k2': lam_k2[l],
             'w_gla_up': w_gla_up[l], 'b_gla_gate': b_gla_gate[l], 'w_dw': w_dw[l],
             'conv_ln_g': conv_ln_g[l], 'conv_ln_b': conv_ln_b[l], 'hnorm_m': hnorm_m[l],
             'hnorm_d': hnorm_d[l], 'hnorm_g': hnorm_g[l], 'w_branch': w_branch[l],
             'w_out': w_out[l], 'norm1': norm1[l], 'norm2': norm2[l],
             'w_group_router': w_group_router[l], 'w_expert_router': w_expert_router[l]}
        pk = _pack_layer_params(p)
        pk.update(w1=w_e1.reshape(DEPTH * N_EXPERTS, D_MODEL, D_EXPERT),
                  w3=w_e3.reshape(DEPTH * N_EXPERTS, D_MODEL, D_EXPERT),
                  w2=w_e2.reshape(DEPTH * N_EXPERTS, D_EXPERT, D_MODEL), expert_base=l * N_EXPERTS)
        lam_init = 0.8 - 0.6 * math.exp(-0.3 * l)
        last = l == DEPTH - 1
        yp, st, xg_p = _layer(yp, mod_all[l, 0:1], pk, Bp, Sp, lam_init, None, last, fn, xg_p)
        states.append(st)
        ctx = {'k': ck4, 'v': cv4, 'layer': l, 'cos': cos_t, 'sin': sin_t,
               'C': state_mlstm_C[:, l], 'n': state_mlstm_n[:, l], 'm': state_mlstm_m[:, l],
               'S': state_gla_S[:, l]}
        ys, _, xg_s = _layer(ys, mod_all[l, 1:1 + Bs], pk, Bs, Ss, lam_init, ctx, last, fn, xg_s)
    stack = lambda i: jnp.stack([s[i] for s in states], axis=1)
    return (yp.reshape(Bp, Sp, D_MODEL), ys.reshape(Bs, Ss, D_MODEL),
            stack(0), stack(1), stack(2), stack(3), stack(4), stack(5))
```

```python
import functools
import math

import jax
import jax.numpy as jnp
from jax import lax
from jax.experimental import pallas as pl
from jax.experimental.pallas import tpu as pltpu

F32 = jnp.float32
BF16 = jnp.bfloat16

D_MODEL = 1024
DEPTH = 2
GRID_W = 64
BRANCH_W = 512
N_BRANCH = 4
H_M, DK_M, DV_M = 4, 128, 128
H_D, DQK_D, DV_D = 4, 64, 128
H_G, DK_G, DV_G = 4, 64, 128
GATE_RANK = 16
GLA_TAU = 16.0
CONV_W = 31
N_GROUPS, EXPERTS_PER_GROUP, D_EXPERT = 4, 4, 512
N_EXPERTS = N_GROUPS * EXPERTS_PER_GROUP
ROPE_BASE = 10000.0
EPS = 1e-6
N_MOD = 6

LANES = 128
VMEM_LIMIT = 48 * 1024 * 1024

A16_MQ, A16_MK, A16_MV, A16_MO = 0, 512, 1024, 1536
A16_GATE, A16_GR, A16_CA, A16_CB = 2048, 6144, 6656, 7168
A16_GQK, A16_GV, A16_DQ = 7680, 8192, 8704
N_A16 = 9216
A32_DK, A32_DV, A32_SM = 0, 512, 1024
N_A32 = 1152
SM_MI, SM_MF, SM_GA = 0, 8, 16

MLSTM_CHUNK = 128
GLA_CHUNK = 64
GLA_SUB = 16
GLA_EXP_CLAMP = 80.0
CONV_ROWS = 128
CONV_PAD = 16
TOK_SUB = D_MODEL // LANES
ROW_DMA_UNROLL = 8


def _cparams(*sem):
    return pltpu.CompilerParams(dimension_semantics=sem, vmem_limit_bytes=VMEM_LIMIT)


def _log_sigmoid(x):
    return jnp.minimum(x, 0.0) - jnp.log1p(jnp.exp(-jnp.abs(x)))


def _sigmoid(x):
    return 0.5 * jnp.tanh(0.5 * x) + 0.5


def _dot(a, b):
    return jnp.dot(a, b, preferred_element_type=F32)


def _dot_nt(a, b):
    return lax.dot_general(a, b, (((1,), (1,)), ((), ())), preferred_element_type=F32)


def _dot_tn(a, b):
    return lax.dot_general(a, b, (((0,), (0,)), ((), ())), preferred_element_type=F32)


def _mod_kernel(c_ref, w_ref, b_ref, o_ref):
    c = c_ref[...]
    a = (c * _sigmoid(c)).astype(BF16)
    o_ref[...] = _dot(a, w_ref[...].astype(BF16)) + b_ref[...]


def _modulation(cond, w_mod, b_mod):
    R = cond.shape[0]
    tn = 512
    nmod = N_MOD * D_MODEL
    return pl.pallas_call(
        _mod_kernel,
        grid=(DEPTH, nmod // tn),
        in_specs=[
            pl.BlockSpec((R, D_MODEL), lambda l, j: (0, 0)),
            pl.BlockSpec((None, D_MODEL, tn), lambda l, j: (l, 0, j)),
            pl.BlockSpec((None, 1, tn), lambda l, j: (l, 0, j)),
        ],
        out_specs=pl.BlockSpec((None, R, tn), lambda l, j: (l, 0, j)),
        out_shape=jax.ShapeDtypeStruct((DEPTH, R, nmod), F32),
        compiler_params=_cparams("parallel", "parallel"),
        name="adaln_mod",
    )(cond, w_mod, b_mod.reshape(DEPTH, 1, nmod))


def _inproj_kernel(x_ref, mod_ref, g_ref, w_ref, o_ref, h_ref):
    @pl.when(pl.program_id(1) == 0)
    def _():
        x = x_ref[...]
        y = x * lax.rsqrt(jnp.mean(x * x, axis=-1, keepdims=True) + EPS) * g_ref[...]
        h_ref[...] = (y * (1.0 + mod_ref[1:2, :]) + mod_ref[0:1, :]).astype(BF16)

    o_ref[...] = _dot(h_ref[...], w_ref[...]).astype(o_ref.dtype)


def _inproj(x2d, mod, g, w, out_dtype, rows_per_mod, tm, tn):
    T = x2d.shape[0]
    N = w.shape[1]
    return pl.pallas_call(
        _inproj_kernel,
        grid=(T // tm, N // tn),
        in_specs=[
            pl.BlockSpec((tm, D_MODEL), lambda i, j: (i, 0)),
            pl.BlockSpec((None, N_MOD, D_MODEL), lambda i, j: ((i * tm) // rows_per_mod, 0, 0)),
            pl.BlockSpec((1, D_MODEL), lambda i, j: (0, 0)),
            pl.BlockSpec((D_MODEL, tn), lambda i, j: (0, j)),
        ],
        out_specs=pl.BlockSpec((tm, tn), lambda i, j: (i, j)),
        out_shape=jax.ShapeDtypeStruct((T, N), out_dtype),
        scratch_shapes=[pltpu.VMEM((tm, D_MODEL), BF16)],
        compiler_params=_cparams("parallel", "arbitrary"),
        name="norm_inproj",
    )(x2d, mod, g, w)


def _mlstm_local(c, q_ref, k_ref, v_ref, gate_ref, pr_ref, bb_ref, mb_ref, kv_ref, rp_ref, L):
    scale = DK_M ** -0.5
    ti = lax.broadcasted_iota(jnp.int32, (L, L), 0)
    si = lax.broadcasted_iota(jnp.int32, (L, L), 1)
    sub = lax.broadcasted_iota(jnp.int32, (8, LANES), 0)
    rows = pl.ds(pl.multiple_of(c * L, L), L)
    q = q_ref[rows, :]
    v_ext = jnp.concatenate([v_ref[rows, :], jnp.ones((L, LANES), BF16)], axis=1)
    k_t = k_ref[rows, :].astype(F32).T
    qk = _dot(q, k_t.astype(BF16)) * scale
    for d in range(2):
        rev = d == 1
        mask = (si >= ti) if rev else (si <= ti)
        i_row = gate_ref[d, 0, pl.ds(c, 1), :]
        f_row = gate_ref[d, 1, pl.ds(c, 1), :]
        b_row = gate_ref[d, 2, pl.ds(c, 1), :]
        b_col = jnp.sum(jnp.where(mask, f_row, 0.0), axis=1, keepdims=True)
        log_d = jnp.where(mask, b_col + (i_row - b_row), -jnp.inf)
        m_loc = jnp.max(log_d, axis=1, keepdims=True)
        smat = qk * jnp.exp(log_d - m_loc)
        pr_ref[d, rows, :] = _dot(smat.astype(BF16), v_ext)
        bb_ref[d, rows, :] = jnp.broadcast_to(b_col, (L, LANES))
        mb_ref[d, rows, :] = jnp.broadcast_to(m_loc, (L, LANES))
        b_last = jnp.sum(f_row, axis=1, keepdims=True)
        ls_row = b_last - b_row + i_row
        m2 = jnp.max(ls_row, axis=1, keepdims=True)
        kw_t = (k_t * jnp.exp(ls_row - m2)).astype(BF16)
        kv_ref[d, c] = scale * _dot(kw_t, v_ext)
        rp_ref[d, c] = jnp.where(sub == 0, b_last, m2)


def _mlstm_carry(c, d, carry, q_ref, pr_ref, bb_ref, mb_ref, kv_ref, rp_ref, h_ref, L):
    cn, m = carry
    two = lambda x: jnp.concatenate([x, x], axis=1)
    rows = pl.ds(pl.multiple_of(c * L, L), L)
    bb = bb_ref[d, rows, :]
    mb = mb_ref[d, rows, :]
    m_t = jnp.maximum(bb + m, mb)
    a_int = jnp.exp(bb + m - m_t)
    e_loc = jnp.exp(mb - m_t)
    nd = two(a_int) * _dot(q_ref[rows, :], cn.astype(BF16)) + two(e_loc) * pr_ref[d, rows, :]
    h_ref[d, rows, :] = nd[:, :DV_M] / jnp.maximum(jnp.abs(nd[:, DV_M:]), jnp.exp(-m_t))
    rp = rp_ref[d, c]
    b_last, m2 = rp[0:1, :], rp[1:2, :]
    m_new = jnp.maximum(b_last + m, m2)
    a_c = jnp.exp(b_last + m - m_new)
    e2 = jnp.exp(m2 - m_new)
    return two(a_c) * cn + two(e2) * kv_ref[d, c], m_new


def _mlstm_gate_rows(gr_ref, br_ref, gate_ref, L):
    ui = lax.broadcasted_iota(jnp.int32, (L, L), 0)
    si = lax.broadcasted_iota(jnp.int32, (L, L), 1)
    for d in range(2):
        src = ((ui >= si) if d == 1 else (ui <= si)).astype(BF16)
        f = _log_sigmoid(gr_ref[2 + d] + br_ref[2 + d])
        f_hi = f.astype(BF16)
        f_r1 = f - f_hi.astype(F32)
        f_mid = f_r1.astype(BF16)
        f_lo = (f_r1 - f_mid.astype(F32)).astype(BF16)
        gate_ref[d, 0] = gr_ref[d] + br_ref[d]
        gate_ref[d, 1] = f
        gate_ref[d, 2] = _dot(f_hi, src) + _dot(f_mid, src) + _dot(f_lo, src)


def _mlstm_kernel(q_ref, k_ref, v_ref, og_ref, gr_ref, br_ref, c0_ref, n0_ref, m0_ref, hn_ref,
                  y_ref, c_out_ref, n_out_ref, m_out_ref,
                  gate_ref, pr_ref, bb_ref, mb_ref, kv_ref, rp_ref, h_ref, *, L, S):
    nch = S // L
    _mlstm_gate_rows(gr_ref, br_ref, gate_ref, L)

    def local(ci, carry):
        _mlstm_local(ci, q_ref, k_ref, v_ref, gate_ref, pr_ref, bb_ref, mb_ref, kv_ref, rp_ref, L)
        return carry

    lax.fori_loop(0, nch, local, 0, unroll=4)
    step = functools.partial(_mlstm_carry, q_ref=q_ref, pr_ref=pr_ref, bb_ref=bb_ref, mb_ref=mb_ref,
                             kv_ref=kv_ref, rp_ref=rp_ref, h_ref=h_ref, L=L)

    def body(ci, carry):
        return step(ci, 0, carry[0]), step(nch - 1 - ci, 1, carry[1])

    def init(d):
        n_rep = jnp.broadcast_to(n0_ref[d], (DK_M, DK_M)).T
        return jnp.concatenate([c0_ref[d], n_rep], axis=1), m0_ref[d]

    fin = lax.fori_loop(0, nch, body, (init(0), init(1)), unroll=2)
    for d in range(2):
        cn, m = fin[d]
        c_out_ref[d] = cn[:, :DV_M]
        n_out_ref[d] = cn[:, DV_M:].T[0:1, :]
        m_out_ref[d] = m

    hm = h_ref[0] + h_ref[1]
    y = hm * lax.rsqrt(jnp.mean(hm * hm, axis=-1, keepdims=True) + EPS) * hn_ref[...]
    y_ref[...] = (y * _sigmoid(og_ref[...].astype(F32))).astype(y_ref.dtype)


def _mlstm(a16, grow, brow, c0, n0, m0, hnorm, B, S):
    L = min(MLSTM_CHUNK, S)
    nch = S // L
    cb = lambda off: off // LANES
    kern = functools.partial(_mlstm_kernel, L=L, S=S)
    return pl.pallas_call(
        kern,
        grid=(B, H_M),
        in_specs=[
            pl.BlockSpec((S, LANES), lambda b, h: (b, cb(A16_MQ) + h)),
            pl.BlockSpec((S, LANES), lambda b, h: (b, cb(A16_MK) + h)),
            pl.BlockSpec((S, LANES), lambda b, h: (b, cb(A16_MV) + h)),
            pl.BlockSpec((S, LANES), lambda b, h: (b, cb(A16_MO) + h)),
            pl.BlockSpec((None, 4, None, nch, L), lambda b, h: (h, 0, b, 0, 0)),
            pl.BlockSpec((None, 4, 1, 1), lambda b, h: (h, 0, 0, 0)),
            pl.BlockSpec((None, 2, None, DK_M, DV_M), lambda b, h: (b, 0, h, 0, 0)),
            pl.BlockSpec((None, 2, None, 1, DK_M), lambda b, h: (b, 0, h, 0, 0)),
            pl.BlockSpec((None, 2, None, 1, LANES), lambda b, h: (b, 0, h, 0, 0)),
            pl.BlockSpec((1, LANES), lambda b, h: (0, h)),
        ],
        out_specs=[
            pl.BlockSpec((S, LANES), lambda b, h: (b, h)),
            pl.BlockSpec((None, 2, None, DK_M, DV_M), lambda b, h: (b, 0, h, 0, 0)),
            pl.BlockSpec((None, 2, None, 1, DK_M), lambda b, h: (b, 0, h, 0, 0)),
            pl.BlockSpec((None, 2, None, 1, LANES), lambda b, h: (b, 0, h, 0, 0)),
        ],
        out_shape=[
            jax.ShapeDtypeStruct((B * S, BRANCH_W), BF16),
            jax.ShapeDtypeStruct((B, 2, H_M, DK_M, DV_M), F32),
            jax.ShapeDtypeStruct((B, 2, H_M, 1, DK_M), F32),
            jax.ShapeDtypeStruct((B, 2, H_M, 1, LANES), F32),
        ],
        scratch_shapes=[pltpu.VMEM((2, 3, nch, L), F32),
                        pltpu.VMEM((2, S, 2 * DV_M), F32), pltpu.VMEM((2, S, LANES), F32),
                        pltpu.VMEM((2, S, LANES), F32), pltpu.VMEM((2, nch, DK_M, 2 * DV_M), F32),
                        pltpu.VMEM((2, nch, 8, LANES), F32), pltpu.VMEM((2, S, DV_M), F32)],
        compiler_params=_cparams("parallel", "parallel"),
        name="mlstm",
    )(a16, a16, a16, a16, grow, brow, c0, n0, m0, hnorm)


def _gla_local(c, q2_ref, k2_ref, v_ref, la_ref, oa_ref, qt_ref, u_ref, dec_ref, L):
    nb = L // GLA_SUB
    ti = lax.broadcasted_iota(jnp.int32, (L, L), 0)
    si = lax.broadcasted_iota(jnp.int32, (L, L), 1)
    row_blk = lax.broadcasted_iota(jnp.int32, (L, LANES), 0) // GLA_SUB
    lo_half = lax.broadcasted_iota(jnp.int32, (L, LANES), 1) < DK_G
    eye = (lax.broadcasted_iota(jnp.int32, (DK_G, LANES), 0)
           == lax.broadcasted_iota(jnp.int32, (DK_G, LANES), 1))
    rows = pl.ds(pl.multiple_of(c * L, L), L)
    q2 = q2_ref[rows, :]
    k2 = k2_ref[rows, :]
    v = v_ref[rows, :]
    row = lax.broadcasted_iota(jnp.int32, (L, LANES), 0)
    for d in range(2):
        rev = d == 1
        mask = (si >= ti) if rev else (si <= ti)
        g2 = la_ref[d, rows, :]
        step = 1
        while step < L:
            if rev:
                g2 = g2 + jnp.where(row < L - step, pltpu.roll(g2, L - step, 0), 0.0)
            else:
                g2 = g2 + jnp.where(row >= step, pltpu.roll(g2, step, 0), 0.0)
            step *= 2
        qt_ref[d, rows, :] = (q2 * jnp.exp(g2))[:, :DK_G].astype(BF16)
        a_parts, b_parts = [], []
        for p in range(nb // 2):
            ia, ib = 2 * p, 2 * p + 1
            ra = ia * GLA_SUB + (GLA_SUB - 1 if rev else 0)
            rb = ib * GLA_SUB + (GLA_SUB - 1 if rev else 0)
            ref2 = jnp.where(lo_half, g2[ra:ra + 1, :], g2[rb:rb + 1, :])
            blk = jnp.where(lo_half, ia, ib)
            in_blk = row_blk == blk
            key_ok = (row_blk >= blk) if rev else (row_blk <= blk)
            a_parts.append(jnp.where(in_blk, q2 * jnp.exp(jnp.minimum(g2 - ref2, 0.0)), 0.0))
            b_parts.append(
                jnp.where(key_ok, k2 * jnp.exp(jnp.minimum(ref2 - g2, GLA_EXP_CLAMP)), 0.0))
        a_big = jnp.concatenate(a_parts, axis=1).astype(BF16)
        b_big = jnp.concatenate(b_parts, axis=1).astype(BF16)
        att = jnp.where(mask, _dot_nt(a_big, b_big), 0.0)
        oa_ref[d, rows, :] = _dot(att.astype(BF16), v)
        gl_row = 0 if rev else L - 1
        glast = g2[gl_row:gl_row + 1, :]
        kd = (k2 * jnp.exp(glast - g2))[:, :DK_G]
        u_ref[d, c] = _dot_tn(kd.astype(BF16), v)
        glast_col = jnp.sum(jnp.where(eye, glast, 0.0), axis=1, keepdims=True)
        dec_ref[d, c] = jnp.broadcast_to(jnp.exp(glast_col), (DK_G, DV_G))


def _gla_kernel(qk_ref, v_ref, sm_ref, wup_ref, bup_ref, s0_ref, gr_ref, hn_ref,
                y_ref, s_out_ref, la_ref, q2_ref, k2_ref, oa_ref, oi_ref, qt_ref, u_ref, dec_ref,
                *, L, S):
    nch = S // L
    sm = sm_ref[...].astype(BF16)
    for d in range(2):
        la_ref[d] = _log_sigmoid(_dot(sm, wup_ref[d]) + bup_ref[d]) * (1.0 / GLA_TAU)
    qk = qk_ref[...].astype(F32)
    qk_sw = pltpu.roll(qk, DK_G, 1)
    lo_half = lax.broadcasted_iota(jnp.int32, qk.shape, 1) < DK_G
    q2_ref[...] = jnp.where(lo_half, qk, qk_sw) * (DK_G ** -0.5)
    k2_ref[...] = jnp.where(lo_half, qk_sw, qk)

    def local(ci, carry):
        _gla_local(ci, q2_ref, k2_ref, v_ref, la_ref, oa_ref, qt_ref, u_ref, dec_ref, L)
        return carry

    lax.fori_loop(0, nch, local, 0, unroll=8)

    def body(ci, carry):
        out = []
        for d, c in ((0, ci), (1, nch - 1 - ci)):
            rows = pl.ds(pl.multiple_of(c * L, L), L)
            st = carry[d]
            oi_ref[d, rows, :] = _dot(qt_ref[d, rows, :], st.astype(BF16))
            out.append(dec_ref[d, c] * st + u_ref[d, c])
        return tuple(out)

    st_f, st_b = lax.fori_loop(0, nch, body, (s0_ref[0], s0_ref[1]), unroll=4)
    s_out_ref[0] = st_f
    s_out_ref[1] = st_b

    og = (oa_ref[0] + oi_ref[0]) + (oa_ref[1] + oi_ref[1])
    y = og * lax.rsqrt(jnp.mean(og * og, axis=-1, keepdims=True) + EPS) * hn_ref[...]
    gr = gr_ref[...].astype(F32)
    y_ref[...] = (y * (gr * _sigmoid(gr))).astype(y_ref.dtype)


def _gla(a32, a16, wup, bup, s0, hnorm, B, S):
    L = min(GLA_CHUNK, S)
    nch = S // L
    cb = lambda off: off // LANES
    kern = functools.partial(_gla_kernel, L=L, S=S)
    return pl.pallas_call(
        kern,
        grid=(B, H_G),
        in_specs=[
            pl.BlockSpec((S, LANES), lambda b, h: (b, cb(A16_GQK) + h)),
            pl.BlockSpec((S, LANES), lambda b, h: (b, cb(A16_GV) + h)),
            pl.BlockSpec((S, LANES), lambda b, h: (b, cb(A32_SM))),
            pl.BlockSpec((None, 2, LANES, LANES), lambda b, h: (h, 0, 0, 0)),
            pl.BlockSpec((None, 2, 1, LANES), lambda b, h: (h, 0, 0, 0)),
            pl.BlockSpec((None, 2, None, DK_G, DV_G), lambda b, h: (b, 0, h, 0, 0)),
            pl.BlockSpec((S, LANES), lambda b, h: (b, cb(A16_GR) + h)),
            pl.BlockSpec((1, LANES), lambda b, h: (0, h)),
        ],
        out_specs=[
            pl.BlockSpec((S, LANES), lambda b, h: (b, h)),
            pl.BlockSpec((None, 2, None, DK_G, DV_G), lambda b, h: (b, 0, h, 0, 0)),
        ],
        out_shape=[
            jax.ShapeDtypeStruct((B * S, BRANCH_W), BF16),
            jax.ShapeDtypeStruct((B, 2, H_G, DK_G, DV_G), F32),
        ],
        scratch_shapes=[pltpu.VMEM((2, S, LANES), F32), pltpu.VMEM((S, LANES), F32),
                        pltpu.VMEM((S, LANES), F32), pltpu.VMEM((2, S, DV_G), F32),
                        pltpu.VMEM((2, S, DV_G), F32), pltpu.VMEM((2, S, DK_G), BF16),
                        pltpu.VMEM((2, nch, DK_G, DV_G), F32), pltpu.VMEM((2, nch, DK_G, DV_G), F32)],
        compiler_params=_cparams("parallel", "parallel"),
        name="gla",
    )(a16, a16, a32, wup, bup, s0, a16, hnorm)


def _rope(x, cos, sin_signed):
    lane = lax.broadcasted_iota(jnp.int32, x.shape, 1)
    first = (lane % DQK_D) < (DQK_D // 2)
    partner = jnp.where(first, pltpu.roll(x, LANES - DQK_D // 2, 1), pltpu.roll(x, DQK_D // 2, 1))
    return x * cos + partner * sin_signed


def _attn_kernel(*refs, S, P, TQ, lam_init, has_ctx):
    if has_ctx:
        (q_ref, k_ref, v_ref, ck_ref, cv_ref, cos_ref, sin_ref, lam_ref, hn_ref,
         y_ref, kk_ref, vv_ref) = refs
    else:
        q_ref, k_ref, v_ref, lam_ref, hn_ref, y_ref, kk_ref, vv_ref = refs
    qi = pl.program_id(2)

    @pl.when(qi == 0)
    def _():
        k = k_ref[...]
        if has_ctx:
            k = _rope(k, cos_ref[...], sin_ref[...])
            kk_ref[S:S + P, :] = ck_ref[...].astype(BF16)
            vv_ref[S:S + P, :] = cv_ref[...].astype(BF16)
        kk_ref[0:S, :] = k.astype(BF16)
        vv_ref[0:S, :] = v_ref[...].astype(BF16)

    kk = kk_ref[...]
    vv = vv_ref[...]
    lv = lam_ref[...]
    lam = (jnp.exp(jnp.sum(lv[0:1, :] * lv[1:2, :], axis=-1, keepdims=True))
           - jnp.exp(jnp.sum(lv[2:3, :] * lv[3:4, :], axis=-1, keepdims=True)) + lam_init)
    q = q_ref[...].astype(F32)
    if has_ctx:
        r0 = pl.multiple_of(qi * TQ, TQ)
        q = _rope(q, cos_ref[pl.ds(r0, TQ), :], sin_ref[pl.ds(r0, TQ), :])
    q = q * (DQK_D ** -0.5 * math.log2(math.e))
    lane = lax.broadcasted_iota(jnp.int32, q.shape, 1)
    es, ls = [], []
    for comp in range(2):
        sel = (lane < DQK_D) if comp == 0 else (lane >= DQK_D)
        s = _dot_nt(jnp.where(sel, q, 0.0).astype(BF16), kk)
        e = jnp.exp2(s - jnp.max(s, axis=-1, keepdims=True))
        es.append(e)
        ls.append(jnp.sum(e, axis=-1, keepdims=True))
    w = es[0] - es[1] * (lam * ls[0] / ls[1])
    o = _dot(w.astype(BF16), vv) * (1.0 / ls[0])
    y = o * lax.rsqrt(jnp.mean(o * o, axis=-1, keepdims=True) + EPS) * hn_ref[...]
    y_ref[...] = (y * (1.0 - lam_init)).astype(y_ref.dtype)


def _attn(a16, a32, lamv, hnorm, B, S, lam_init, ctx=None):
    TQ = min(256, S)
    nq = S // TQ
    has_ctx = ctx is not None
    P = ctx[0].shape[2] if has_ctx else 0
    cb = lambda off: off // LANES
    kern = functools.partial(_attn_kernel, S=S, P=P, TQ=TQ, lam_init=lam_init, has_ctx=has_ctx)
    in_specs = [
        pl.BlockSpec((TQ, LANES), lambda b, h, i: (b * nq + i, cb(A16_DQ) + h)),
        pl.BlockSpec((S, LANES), lambda b, h, i: (b, cb(A32_DK) + h)),
        pl.BlockSpec((S, LANES), lambda b, h, i: (b, cb(A32_DV) + h)),
    ]
    args = [a16, a32, a32]
    if has_ctx:
        ck, cv, layer, cos, sin = ctx
        in_specs += [
            pl.BlockSpec((None, None, P, LANES), lambda b, h, i: (b, layer, 0, h)),
            pl.BlockSpec((None, None, P, LANES), lambda b, h, i: (b, layer, 0, h)),
            pl.BlockSpec((S, LANES), lambda b, h, i: (0, 0)),
            pl.BlockSpec((S, LANES), lambda b, h, i: (0, 0)),
        ]
        args += [ck, cv, cos, sin]
    in_specs += [
        pl.BlockSpec((4, DQK_D), lambda b, h, i: (0, 0)),
        pl.BlockSpec((1, LANES), lambda b, h, i: (0, h)),
    ]
    args += [lamv, hnorm]
    return pl.pallas_call(
        kern,
        grid=(B, H_D, nq),
        in_specs=in_specs,
        out_specs=pl.BlockSpec((TQ, LANES), lambda b, h, i: (b * nq + i, h)),
        out_shape=jax.ShapeDtypeStruct((B * S, BRANCH_W), BF16),
        scratch_shapes=[pltpu.VMEM((S + P, LANES), BF16), pltpu.VMEM((S + P, LANES), BF16)],
        compiler_params=_cparams("parallel", "parallel", "arbitrary"),
        name="diff_attn",
    )(*args)


def _conv_kernel(ca_ref, cb_ref, w_ref, g_ref, b_ref, y_ref, pad_ref, cv_ref, *, S):
    ca = ca_ref[...].astype(F32)
    cbv = cb_ref[...].astype(F32)
    zeros = jnp.zeros((CONV_PAD, BRANCH_W), F32)
    pad_ref[0:CONV_PAD, :] = zeros
    pad_ref[CONV_PAD + S:2 * CONV_PAD + S, :] = zeros
    pad_ref[CONV_PAD:CONV_PAD + S, :] = ca * _sigmoid(cbv)
    off = CONV_PAD - CONV_W // 2

    win_rows = CONV_ROWS + 2 * CONV_PAD

    def body(i, carry):
        base = pl.multiple_of(i * CONV_ROWS, CONV_ROWS)
        for lb in range(BRANCH_W // LANES):
            cols = slice(lb * LANES, (lb + 1) * LANES)
            win = pad_ref[pl.ds(base, win_rows), cols]
            acc = jnp.zeros((CONV_ROWS, LANES), F32)
            for r in range(8):
                rolled = win if r == 0 else pltpu.roll(win, win_rows - r, 0)
                for a in range(2 * CONV_PAD // 8):
                    j = 8 * a + r - off
                    if 0 <= j < CONV_W:
                        acc = acc + rolled[8 * a:8 * a + CONV_ROWS, :] * w_ref[j:j + 1, cols]
            cv_ref[:, cols] = acc
        acc = cv_ref[...]
        mu = jnp.mean(acc, axis=-1, keepdims=True)
        xc = acc - mu
        yn = xc * lax.rsqrt(jnp.mean(xc * xc, axis=-1, keepdims=True) + EPS) * g_ref[...] + b_ref[...]
        y_ref[pl.ds(base, CONV_ROWS), :] = (yn * _sigmoid(yn)).astype(y_ref.dtype)
        return carry

    lax.fori_loop(0, S // CONV_ROWS, body, 0)


def _conv(a16, w_dw, ln_g, ln_b, B, S):
    cb = lambda off: off // BRANCH_W
    kern = functools.partial(_conv_kernel, S=S)
    return pl.pallas_call(
        kern,
        grid=(B,),
        in_specs=[
            pl.BlockSpec((S, BRANCH_W), lambda b: (b, cb(A16_CA))),
            pl.BlockSpec((S, BRANCH_W), lambda b: (b, cb(A16_CB))),
            pl.BlockSpec((CONV_W + 1, BRANCH_W), lambda b: (0, 0)),
            pl.BlockSpec((1, BRANCH_W), lambda b: (0, 0)),
            pl.BlockSpec((1, BRANCH_W), lambda b: (0, 0)),
        ],
        out_specs=pl.BlockSpec((S, BRANCH_W), lambda b: (b, 0)),
        out_shape=jax.ShapeDtypeStruct((B * S, BRANCH_W), BF16),
        scratch_shapes=[pltpu.VMEM((S + 2 * CONV_PAD, BRANCH_W), F32),
                        pltpu.VMEM((CONV_ROWS, BRANCH_W), F32)],
        compiler_params=_cparams("parallel"),
        name="glu_conv_ln",
    )(a16, a16, w_dw, ln_g, ln_b)


def _merge_kernel(x_ref, mod_ref, ym_ref, yd_ref, yg_ref, yc_ref, g0_ref, g1_ref, g2_ref, g3_ref,
                  wb_ref, wo_ref, n2_ref, wr_ref, x1_ref, h2_ref, route_ref):
    ys = (ym_ref, yd_ref, yg_ref, yc_ref)
    gs = (g0_ref, g1_ref, g2_ref, g3_ref)
    merged = None
    for nbr in range(N_BRANCH):
        br = _dot(ys[nbr][...], wb_ref[nbr])
        term = _sigmoid(gs[nbr][...].astype(F32)) * br
        merged = term if merged is None else merged + term
    out = _dot(merged.astype(BF16), wo_ref[...])
    x1 = x_ref[...] + mod_ref[2:3, :] * out
    x1_ref[...] = x1
    y = x1 * lax.rsqrt(jnp.mean(x1 * x1, axis=-1, keepdims=True) + EPS) * n2_ref[...]
    h2 = y * (1.0 + mod_ref[4:5, :]) + mod_ref[3:4, :]
    h2_ref[...] = h2.reshape(h2_ref.shape)
    wr = wr_ref[...]
    h_hi, w_hi = h2.astype(BF16), wr.astype(BF16)
    h_lo = (h2 - h_hi.astype(F32)).astype(BF16)
    w_lo = (wr - w_hi.astype(F32)).astype(BF16)
    logits = _dot(h_hi, w_hi) + (_dot(h_hi, w_lo) + _dot(h_lo, w_hi))
    lane = lax.broadcasted_iota(jnp.int32, logits.shape, 1)
    neg = -jnp.inf
    big = jnp.int32(LANES)
    is_g = lane < N_GROUPS
    gl = jnp.where(is_g, logits, neg)
    gmax = jnp.max(gl, axis=-1, keepdims=True)
    gidx = jnp.min(jnp.where(is_g & (gl == gmax), lane, big), axis=-1, keepdims=True)
    g_p = 1.0 / jnp.sum(jnp.where(is_g, jnp.exp(gl - gmax), 0.0), axis=-1, keepdims=True)
    e_lane = lane - N_GROUPS
    in_grp = (e_lane >= 0) & (e_lane < N_EXPERTS) & ((e_lane // EXPERTS_PER_GROUP) == gidx)
    el = jnp.where(in_grp, logits, neg)
    v1 = jnp.max(el, axis=-1, keepdims=True)
    i1 = jnp.min(jnp.where(in_grp & (el == v1), lane, big), axis=-1, keepdims=True)
    el2 = jnp.where(lane == i1, neg, el)
    v2 = jnp.max(el2, axis=-1, keepdims=True)
    i2 = jnp.min(jnp.where(in_grp & (lane != i1) & (el2 == v2), lane, big), axis=-1, keepdims=True)
    e2 = jnp.exp(v2 - v1)
    w1 = g_p / (1.0 + e2)
    w2 = g_p * e2 / (1.0 + e2)
    id1 = (i1 - N_GROUPS).astype(F32)
    id2 = (i2 - N_GROUPS).astype(F32)
    route_ref[...] = jnp.where(lane == 0, id1, jnp.where(lane == 1, id2,
                               jnp.where(lane == 2, w1, jnp.where(lane == 3, w2, 0.0))))


def _merge(x2d, mod, a16, ym, yd, yg, yc, wb, wo, n2, wr, rows_per_mod, tm):
    T = x2d.shape[0]
    gcb = A16_GATE // D_MODEL
    row = lambda i: (i, 0)
    return pl.pallas_call(
        _merge_kernel,
        grid=(T // tm,),
        in_specs=[
            pl.BlockSpec((tm, D_MODEL), row),
            pl.BlockSpec((None, N_MOD, D_MODEL), lambda i: ((i * tm) // rows_per_mod, 0, 0)),
            pl.BlockSpec((tm, BRANCH_W), row),
            pl.BlockSpec((tm, BRANCH_W), row),
            pl.BlockSpec((tm, BRANCH_W), row),
            pl.BlockSpec((tm, BRANCH_W), row),
            pl.BlockSpec((tm, D_MODEL), lambda i: (i, gcb + 0)),
            pl.BlockSpec((tm, D_MODEL), lambda i: (i, gcb + 1)),
            pl.BlockSpec((tm, D_MODEL), lambda i: (i, gcb + 2)),
            pl.BlockSpec((tm, D_MODEL), lambda i: (i, gcb + 3)),
            pl.BlockSpec((N_BRANCH, BRANCH_W, D_MODEL), lambda i: (0, 0, 0)),
            pl.BlockSpec((D_MODEL, D_MODEL), lambda i: (0, 0)),
            pl.BlockSpec((1, D_MODEL), lambda i: (0, 0)),
            pl.BlockSpec((D_MODEL, LANES), lambda i: (0, 0)),
        ],
        out_specs=[
            pl.BlockSpec((tm, D_MODEL), row),
            pl.BlockSpec((tm, TOK_SUB, LANES), lambda i: (i, 0, 0)),
            pl.BlockSpec((tm, LANES), row),
        ],
        out_shape=[
            jax.ShapeDtypeStruct((T, D_MODEL), F32),
            jax.ShapeDtypeStruct((T, TOK_SUB, LANES), F32),
            jax.ShapeDtypeStruct((T, LANES), F32),
        ],
        compiler_params=_cparams("parallel"),
        name="merge_outproj_route",
    )(x2d, mod, ym, yd, yg, yc, a16, a16, a16, a16, wb, wo, n2, wr)


def _gather_rows(idx_ref, src_hbm, dst, sem, n):
    def body(j, carry):
        for u in range(ROW_DMA_UNROLL):
            r = j * ROW_DMA_UNROLL + u
            pltpu.make_async_copy(src_hbm.at[idx_ref[0, r]], dst.at[r], sem).start(priority=u % 2)
        return carry

    lax.fori_loop(0, n // ROW_DMA_UNROLL, body, 0)


def _scatter_rows(idx_ref, src, dst_hbm, sem, n):
    def body(j, carry):
        for u in range(ROW_DMA_UNROLL):
            r = j * ROW_DMA_UNROLL + u
            pltpu.make_async_copy(src.at[r], dst_hbm.at[idx_ref[0, r]], sem).start(priority=u % 2)
        return carry

    lax.fori_loop(0, n // ROW_DMA_UNROLL, body, 0)


def _wait_rows(buf, sem):
    pltpu.make_async_copy(buf, buf, sem).wait()


def _moe_dispatch_kernel(d0_ref, d1_ref, h_ref, xg_in, xg_out, sem, *, tm):
    del xg_in
    _scatter_rows(d0_ref, h_ref, xg_out, sem.at[0], tm)
    _scatter_rows(d1_ref, h_ref, xg_out, sem.at[1], tm)
    _wait_rows(h_ref, sem.at[0])
    _wait_rows(h_ref, sem.at[1])


def _moe_dispatch(h3, dest, xg_init, tm):
    T = h3.shape[0]
    kern = functools.partial(_moe_dispatch_kernel, tm=tm)
    return pl.pallas_call(
        kern,
        grid=(T // tm,),
        in_specs=[
            pl.BlockSpec((None, None, 1, tm), lambda i: (0, i, 0, 0), memory_space=pltpu.SMEM),
            pl.BlockSpec((None, None, 1, tm), lambda i: (1, i, 0, 0), memory_space=pltpu.SMEM),
            pl.BlockSpec((tm, TOK_SUB, LANES), lambda i: (i, 0, 0)),
            pl.BlockSpec(memory_space=pl.ANY),
        ],
        out_specs=pl.BlockSpec(memory_space=pl.ANY),
        out_shape=jax.ShapeDtypeStruct(xg_init.shape, F32),
        input_output_aliases={3: 0},
        scratch_shapes=[pltpu.SemaphoreType.DMA((2,))],
        compiler_params=_cparams("arbitrary"),
        name="moe_dispatch",
    )(dest, dest, h3, xg_init)


def _moe_ffn_kernel(te_ref, x_ref, w1_ref, w3_ref, w2_ref, o_ref):
    del te_ref
    x = x_ref[...].reshape(x_ref.shape[0], D_MODEL).astype(BF16)
    a = _dot(x, w1_ref[...].astype(BF16))
    b = _dot(x, w3_ref[...].astype(BF16))
    s = (a * _sigmoid(a)) * b
    y = _dot(s.astype(BF16), w2_ref[...].astype(BF16))
    o_ref[...] = y.reshape(o_ref.shape)


def _moe_ffn(xg, tile_e, w1, w3, w2, rows):
    ntiles = xg.shape[0] // rows
    tile = pl.BlockSpec((rows, TOK_SUB, LANES), lambda i, te: (i, 0, 0))
    return pl.pallas_call(
        _moe_ffn_kernel,
        grid_spec=pltpu.PrefetchScalarGridSpec(
            num_scalar_prefetch=1,
            grid=(ntiles,),
            in_specs=[
                tile,
                pl.BlockSpec((None, D_MODEL, D_EXPERT), lambda i, te: (te[i], 0, 0)),
                pl.BlockSpec((None, D_MODEL, D_EXPERT), lambda i, te: (te[i], 0, 0)),
                pl.BlockSpec((None, D_EXPERT, D_MODEL), lambda i, te: (te[i], 0, 0)),
            ],
            out_specs=tile,
        ),
        out_shape=jax.ShapeDtypeStruct(xg.shape, F32),
        compiler_params=_cparams("parallel"),
        name="moe_grouped_experts",
    )(tile_e, xg, w1, w3, w2)


def _moe_combine_kernel(d0_ref, d1_ref, y_hbm, route_ref, x1_ref, mod_ref, fn_ref, o_ref,
                        ga, gb, sem, *, tm, final_norm):
    _gather_rows(d0_ref, y_hbm, ga, sem.at[0], tm)
    _gather_rows(d1_ref, y_hbm, gb, sem.at[1], tm)
    rt = route_ref[...]
    _wait_rows(ga, sem.at[0])
    _wait_rows(gb, sem.at[1])
    y = rt[:, 2:3] * ga[...].reshape(tm, D_MODEL) + rt[:, 3:4] * gb[...].reshape(tm, D_MODEL)
    x2 = x1_ref[...] + mod_ref[5:6, :] * y
    if final_norm:
        x2 = x2 * lax.rsqrt(jnp.mean(x2 * x2, axis=-1, keepdims=True) + EPS) * fn_ref[...]
    o_ref[...] = x2


def _moe_combine(yg, dest, route, x1, mod, fn, rows_per_mod, tm, final_norm):
    T = x1.shape[0]
    kern = functools.partial(_moe_combine_kernel, tm=tm, final_norm=final_norm)
    return pl.pallas_call(
        kern,
        grid=(T // tm,),
        in_specs=[
            pl.BlockSpec((None, None, 1, tm), lambda i: (0, i, 0, 0), memory_space=pltpu.SMEM),
            pl.BlockSpec((None, None, 1, tm), lambda i: (1, i, 0, 0), memory_space=pltpu.SMEM),
            pl.BlockSpec(memory_space=pl.ANY),
            pl.BlockSpec((tm, LANES), lambda i: (i, 0)),
            pl.BlockSpec((tm, D_MODEL), lambda i: (i, 0)),
            pl.BlockSpec((None, N_MOD, D_MODEL), lambda i: ((i * tm) // rows_per_mod, 0, 0)),
            pl.BlockSpec((1, D_MODEL), lambda i: (0, 0)),
        ],
        out_specs=pl.BlockSpec((tm, D_MODEL), lambda i: (i, 0)),
        out_shape=jax.ShapeDtypeStruct((T, D_MODEL), F32),
        scratch_shapes=[pltpu.VMEM((tm, TOK_SUB, LANES), F32), pltpu.VMEM((tm, TOK_SUB, LANES), F32),
                        pltpu.SemaphoreType.DMA((2,))],
        compiler_params=_cparams("arbitrary"),
        name="moe_combine",
    )(dest, dest, yg, route, x1, mod, fn)


def _moe_rows(T):
    rows = 512 if T >= 4096 else 256
    ntiles = (2 * T + N_EXPERTS * (rows - 1) + rows - 1) // rows
    return rows, ntiles * rows


def _route_plan(route, T):
    tile, n_rows = _moe_rows(T)
    ef = route[:, 0:2].astype(jnp.int32).reshape(-1)
    oh = (ef[:, None] == jnp.arange(N_EXPERTS, dtype=jnp.int32)[None, :]).astype(jnp.int32)
    csum = jnp.cumsum(oh, axis=0)
    rank = jnp.sum((csum - oh) * oh, axis=1)
    counts = csum[-1]
    padded = ((counts + tile - 1) // tile) * tile
    seg_end = jnp.cumsum(padded)
    dest = jnp.sum(oh * (seg_end - padded)[None, :], axis=1) + rank
    tile_row = jnp.arange(n_rows // tile, dtype=jnp.int32) * tile
    tile_e = jnp.minimum(jnp.sum((tile_row[:, None] >= seg_end[None, :]).astype(jnp.int32), axis=1),
                         N_EXPERTS - 1)
    return jnp.transpose(dest.reshape(T, 2)), tile_e


def _split_w_in(w):
    sizes = (H_M * DK_M, H_M * DK_M, H_M * DV_M, H_M * DV_M, 2 * H_M, 2 * H_M,
             H_D * 2 * DQK_D, H_D * 2 * DQK_D, H_D * DV_D,
             H_G * DK_G, H_G * DK_G, H_G * DV_G, 2 * GATE_RANK, H_G * DV_G,
             BRANCH_W, BRANCH_W, N_BRANCH * D_MODEL)
    outs, acc = [], 0
    for s in sizes:
        outs.append(w[:, acc:acc + s])
        acc += s
    return outs


def _pack_layer_params(p):
    (m_q, m_k, m_v, m_o, m_i, m_f, d_q, d_k, d_v, g_q, g_k, g_v, g_a, g_r, c_a, c_b, gate) = \
        _split_w_in(p['w_in'])
    gqk = jnp.concatenate([g_q.reshape(D_MODEL, H_G, DK_G), g_k.reshape(D_MODEL, H_G, DK_G)],
                          axis=2).reshape(D_MODEL, 2 * H_G * DK_G)
    small = jnp.concatenate(
        [m_i, m_f, g_a, jnp.zeros((D_MODEL, LANES - 4 * H_M - 2 * GATE_RANK), F32)], axis=1)
    w16 = jnp.concatenate([m_q, m_k, m_v, m_o, gate, g_r, c_a, c_b, gqk, g_v, d_q],
                          axis=1).astype(BF16)
    w32 = jnp.concatenate([d_k, d_v, small], axis=1).astype(BF16)
    bi = p['b_m_i'].reshape(2, H_M)
    bf = p['b_m_f'].reshape(2, H_M)
    bcol = jnp.stack([bi[0], bi[1], bf[0], bf[1]], axis=-1)
    wup = p['w_gla_up'].reshape(2, GATE_RANK, H_G, DK_G)
    wup_pad = jnp.zeros((H_G, 2, LANES, LANES), F32)
    bup = p['b_gla_gate'].reshape(2, H_G, DK_G)
    for d in range(2):
        blk = jnp.transpose(wup[d], (1, 0, 2))
        blk = jnp.concatenate([blk, blk], axis=-1)
        r0 = SM_GA + d * GATE_RANK
        wup_pad = wup_pad.at[:, d, r0:r0 + GATE_RANK, :].set(blk)
    bup2 = jnp.transpose(jnp.concatenate([bup, bup], axis=-1), (1, 0, 2))[:, :, None, :]
    wr = jnp.concatenate([p['w_group_router'], p['w_expert_router'],
                          jnp.zeros((D_MODEL, LANES - N_GROUPS - N_EXPERTS), F32)], axis=1)
    return dict(
        w16=w16, w32=w32, brow=bcol.reshape(H_M, 4, 1, 1),
        wup=wup_pad.astype(BF16), bup=bup2,
        wdw=jnp.concatenate([p['w_dw'], jnp.zeros((1, BRANCH_W), F32)], axis=0),
        ln_g=p['conv_ln_g'].reshape(1, BRANCH_W), ln_b=p['conv_ln_b'].reshape(1, BRANCH_W),
        hn_m=p['hnorm_m'].reshape(1, BRANCH_W), hn_d=p['hnorm_d'].reshape(1, BRANCH_W),
        hn_g=p['hnorm_g'].reshape(1, BRANCH_W),
        lamv=jnp.stack([p['lam_q1'], p['lam_k1'], p['lam_q2'], p['lam_k2']], axis=0),
        wb=p['w_branch'].astype(BF16), wo=p['w_out'].astype(BF16),
        n1=p['norm1'].reshape(1, D_MODEL), n2=p['norm2'].reshape(1, D_MODEL), wr=wr,
    )


def _rope_tables(S):
    rows = S // GRID_W
    r, col = jnp.meshgrid(jnp.arange(rows, dtype=F32), jnp.arange(GRID_W, dtype=F32), indexing='ij')
    r, col = r.reshape(-1), col.reshape(-1)
    n_freq = DQK_D // 4
    inv = ROPE_BASE ** (-jnp.arange(n_freq, dtype=F32) / n_freq)
    ang = jnp.concatenate([r[:, None] * inv, col[:, None] * inv], axis=-1)
    cos, sin = jnp.cos(ang), jnp.sin(ang)
    cos_t = jnp.tile(cos, (1, LANES // (DQK_D // 2)))
    sin_t = jnp.tile(jnp.concatenate([-sin, sin], axis=-1), (1, LANES // DQK_D))
    return cos_t, sin_t


def _pick_tile(T, cap):
    t = min(T, cap)
    while T % t:
        t //= 2
    return t


def _layer(x2d, mod, pk, B, S, lam_init, ctx, final_norm, fn, xg):
    T = B * S
    rows_per_mod = T // mod.shape[0]
    tm = _pick_tile(rows_per_mod, 1024)
    a16 = _inproj(x2d, mod, pk['n1'], pk['w16'], BF16, rows_per_mod, tm, N_A16 // 4)
    a32 = _inproj(x2d, mod, pk['n1'], pk['w32'], F32, rows_per_mod, tm, N_A32)

    L = min(MLSTM_CHUNK, S)
    sm = a32[:, A32_SM:A32_SM + 4 * H_M]
    grow = jnp.transpose(sm.reshape(B, S // L, L, 4, H_M), (4, 3, 0, 1, 2))
    if ctx is None:
        c0 = jnp.zeros((B, 2, H_M, DK_M, DV_M), F32)
        n0 = jnp.zeros((B, 2, H_M, 1, DK_M), F32)
        m0 = jnp.zeros((B, 2, H_M, 1, LANES), F32)
        s0 = jnp.zeros((B, 2, H_G, DK_G, DV_G), F32)
        attn_ctx = None
    else:
        c0 = ctx['C']
        n0 = ctx['n'][:, :, :, None, :]
        m0 = jnp.broadcast_to(ctx['m'][:, :, :, None, None], (B, 2, H_M, 1, LANES))
        s0 = ctx['S']
        attn_ctx = (ctx['k'], ctx['v'], ctx['layer'], ctx['cos'], ctx['sin'])
    ym, c_f, n_f, m_f = _mlstm(a16, grow, pk['brow'], c0, n0, m0, pk['hn_m'], B, S)
    yd = _attn(a16, a32, pk['lamv'], pk['hn_d'], B, S, lam_init, attn_ctx)
    yg, s_f = _gla(a32, a16, pk['wup'], pk['bup'], s0, pk['hn_g'], B, S)
    yc = _conv(a16, pk['wdw'], pk['ln_g'], pk['ln_b'], B, S)
    x1, h3, route = _merge(x2d, mod, a16, ym, yd, yg, yc, pk['wb'], pk['wo'], pk['n2'], pk['wr'],
                           rows_per_mod, _pick_tile(rows_per_mod, 512))
    dest, tile_e = _route_plan(route, T)
    tmd = _pick_tile(T, 512)
    moe_rows, n_rows = _moe_rows(T)
    if xg is None:
        xg = jnp.zeros((n_rows, TOK_SUB, LANES), F32)
    xg = _moe_dispatch(h3, dest.reshape(2, T // tmd, 1, tmd), xg, tmd)
    y_grouped = _moe_ffn(xg, tile_e + pk['expert_base'], pk['w1'], pk['w3'], pk['w2'], moe_rows)
    tmc = _pick_tile(rows_per_mod, 256)
    x2 = _moe_combine(y_grouped, dest.reshape(2, T // tmc, 1, tmc), route, x1, mod, fn,
                      rows_per_mod, tmc, final_norm)
    state = None
    if ctx is None:
        state = (a32[:, A32_DK:A32_DK + H_D * 2 * DQK_D].reshape(B, S, H_D, 2 * DQK_D),
                 a32[:, A32_DV:A32_DV + H_D * DV_D].reshape(B, S, H_D, DV_D),
                 c_f, n_f[:, :, :, 0, :], m_f[:, :, :, 0, 0], s_f)
    return x2, state, xg


def kernel(x_prompt, x_sample, c, cache_diff_k, cache_diff_v, state_mlstm_C, state_mlstm_n, state_mlstm_m, state_gla_S, c_ctx, w_mod, b_mod, norm1, w_in, b_m_i, b_m_f, lam_q1, lam_k1, lam_q2, lam_k2, w_gla_up, b_gla_gate, w_dw, conv_ln_g, conv_ln_b, hnorm_m, hnorm_d, hnorm_g, w_branch, w_out, norm2, w_group_router, w_expert_router, w_e1, w_e3, w_e2, final_norm):
    Bp, Sp, _ = x_prompt.shape
    Bs, Ss, _ = x_sample.shape
    P = cache_diff_k.shape[2]
    n_cond = 8 * ((1 + Bs + 7) // 8)
    cond = jnp.concatenate([c_ctx[None, :], c, jnp.zeros((n_cond - 1 - Bs, D_MODEL), F32)], axis=0)
    mod_all = _modulation(cond, w_mod, b_mod).reshape(DEPTH, n_cond, N_MOD, D_MODEL)
    cos_t, sin_t = _rope_tables(Ss)
    ck4 = cache_diff_k.reshape(Bs, DEPTH, P, H_D * 2 * DQK_D)
    cv4 = cache_diff_v.reshape(Bs, DEPTH, P, H_D * DV_D)
    fn = final_norm.reshape(1, D_MODEL)
    yp = x_prompt.reshape(Bp * Sp, D_MODEL)
    ys = x_sample.reshape(Bs * Ss, D_MODEL)
    states = []
    xg_p = xg_s = None
    for l in range(DEPTH):
        p = {'w_in': w_in[l], 'b_m_i': b_m_i[l], 'b_m_f': b_m_f[l], 'lam_q1': lam_q1[l],
             'lam_k1': lam_k1[l], 'lam_q2': lam_q2[l], 'lam_k2': lam_k2[l],
             'w_gla_up': w_gla_up[l], 'b_gla_gate': b_gla_gate[l], 'w_dw': w_dw[l],
             'conv_ln_g': conv_ln_g[l], 'conv_ln_b': conv_ln_b[l], 'hnorm_m': hnorm_m[l],
             'hnorm_d': hnorm_d[l], 'hnorm_g': hnorm_g[l], 'w_branch': w_branch[l],
             'w_out': w_out[l], 'norm1': norm1[l], 'norm2': norm2[l],
             'w_group_router': w_group_router[l], 'w_expert_router': w_expert_router[l]}
        pk = _pack_layer_params(p)
        pk.update(w1=w_e1.reshape(DEPTH * N_EXPERTS, D_MODEL, D_EXPERT),
                  w3=w_e3.reshape(DEPTH * N_EXPERTS, D_MODEL, D_EXPERT),
                  w2=w_e2.reshape(DEPTH * N_EXPERTS, D_EXPERT, D_MODEL), expert_base=l * N_EXPERTS)
        lam_init = 0.8 - 0.6 * math.exp(-0.3 * l)
        last = l == DEPTH - 1
        yp, st, xg_p = _layer(yp, mod_all[l, 0:1], pk, Bp, Sp, lam_init, None, last, fn, xg_p)
        states.append(st)
        ctx = {'k': ck4, 'v': cv4, 'layer': l, 'cos': cos_t, 'sin': sin_t,
               'C': state_mlstm_C[:, l], 'n': state_mlstm_n[:, l], 'm': state_mlstm_m[:, l],
               'S': state_gla_S[:, l]}
        ys, _, xg_s = _layer(ys, mod_all[l, 1:1 + Bs], pk, Bs, Ss, lam_init, ctx, last, fn, xg_s)
    stack = lambda i: jnp.stack([s[i] for s in states], axis=1)
    return (yp.reshape(Bp, Sp, D_MODEL), ys.reshape(Bs, Ss, D_MODEL),
            stack(0), stack(1), stack(2), stack(3), stack(4), stack(5))
```

```python
import functools
import math

import jax
import jax.numpy as jnp
from jax import lax
from jax.experimental import pallas as pl
from jax.experimental.pallas import tpu as pltpu

F32 = jnp.float32
BF16 = jnp.bfloat16

D_MODEL = 1024
DEPTH = 2
GRID_W = 64
BRANCH_W = 512
N_BRANCH = 4
H_M, DK_M, DV_M = 4, 128, 128
H_D, DQK_D, DV_D = 4, 64, 128
H_G, DK_G, DV_G = 4, 64, 128
GATE_RANK = 16
GLA_TAU = 16.0
CONV_W = 31
N_GROUPS, EXPERTS_PER_GROUP, D_EXPERT = 4, 4, 512
N_EXPERTS = N_GROUPS * EXPERTS_PER_GROUP
ROPE_BASE = 10000.0
EPS = 1e-6
N_MOD = 6

LANES = 128
VMEM_LIMIT = 48 * 1024 * 1024

A16_MQ, A16_MK, A16_MV, A16_MO = 0, 512, 1024, 1536
A16_GATE, A16_GR, A16_CA, A16_CB = 2048, 6144, 6656, 7168
A16_GQK, A16_GV, A16_DQ = 7680, 8192, 8704
N_A16 = 9216
A32_DK, A32_DV, A32_SM = 0, 512, 1024
N_A32 = 1152
SM_MI, SM_MF, SM_GA = 0, 8, 16

MLSTM_CHUNK = 128
GLA_CHUNK = 64
GLA_SUB = 16
GLA_EXP_CLAMP = 80.0
CONV_ROWS = 128
CONV_PAD = 16
TOK_SUB = D_MODEL // LANES
ROW_DMA_UNROLL = 8


def _cparams(*sem):
    return pltpu.CompilerParams(dimension_semantics=sem, vmem_limit_bytes=VMEM_LIMIT)


def _log_sigmoid(x):
    return jnp.minimum(x, 0.0) - jnp.log1p(jnp.exp(-jnp.abs(x)))


def _sigmoid(x):
    return 0.5 * jnp.tanh(0.5 * x) + 0.5


def _dot(a, b):
    return jnp.dot(a, b, preferred_element_type=F32)


def _dot_nt(a, b):
    return lax.dot_general(a, b, (((1,), (1,)), ((), ())), preferred_element_type=F32)


def _dot_tn(a, b):
    return lax.dot_general(a, b, (((0,), (0,)), ((), ())), preferred_element_type=F32)


def _mod_kernel(c_ref, w_ref, b_ref, o_ref):
    c = c_ref[...]
    a = (c * _sigmoid(c)).astype(BF16)
    o_ref[...] = _dot(a, w_ref[...].astype(BF16)) + b_ref[...]


def _modulation(cond, w_mod, b_mod):
    R = cond.shape[0]
    tn = 512
    nmod = N_MOD * D_MODEL
    return pl.pallas_call(
        _mod_kernel,
        grid=(DEPTH, nmod // tn),
        in_specs=[
            pl.BlockSpec((R, D_MODEL), lambda l, j: (0, 0)),
            pl.BlockSpec((None, D_MODEL, tn), lambda l, j: (l, 0, j)),
            pl.BlockSpec((None, 1, tn), lambda l, j: (l, 0, j)),
        ],
        out_specs=pl.BlockSpec((None, R, tn), lambda l, j: (l, 0, j)),
        out_shape=jax.ShapeDtypeStruct((DEPTH, R, nmod), F32),
        compiler_params=_cparams("parallel", "parallel"),
        name="adaln_mod",
    )(cond, w_mod, b_mod.reshape(DEPTH, 1, nmod))


def _inproj_kernel(x_ref, mod_ref, g_ref, w_ref, o_ref, h_ref):
    @pl.when(pl.program_id(1) == 0)
    def _():
        x = x_ref[...]
        y = x * lax.rsqrt(jnp.mean(x * x, axis=-1, keepdims=True) + EPS) * g_ref[...]
        h_ref[...] = (y * (1.0 + mod_ref[1:2, :]) + mod_ref[0:1, :]).astype(BF16)

    o_ref[...] = _dot(h_ref[...], w_ref[...]).astype(o_ref.dtype)


def _inproj(x2d, mod, g, w, out_dtype, rows_per_mod, tm, tn):
    T = x2d.shape[0]
    N = w.shape[1]
    return pl.pallas_call(
        _inproj_kernel,
        grid=(T // tm, N // tn),
        in_specs=[
            pl.BlockSpec((tm, D_MODEL), lambda i, j: (i, 0)),
            pl.BlockSpec((None, N_MOD, D_MODEL), lambda i, j: ((i * tm) // rows_per_mod, 0, 0)),
            pl.BlockSpec((1, D_MODEL), lambda i, j: (0, 0)),
            pl.BlockSpec((D_MODEL, tn), lambda i, j: (0, j)),
        ],
        out_specs=pl.BlockSpec((tm, tn), lambda i, j: (i, j)),
        out_shape=jax.ShapeDtypeStruct((T, N), out_dtype),
        scratch_shapes=[pltpu.VMEM((tm, D_MODEL), BF16)],
        compiler_params=_cparams("parallel", "arbitrary"),
        name="norm_inproj",
    )(x2d, mod, g, w)


def _mlstm_local(c, q_ref, k_ref, v_ref, gate_ref, pr_ref, bb_ref, mb_ref, kv_ref, rp_ref, L):
    scale = DK_M ** -0.5
    ti = lax.broadcasted_iota(jnp.int32, (L, L), 0)
    si = lax.broadcasted_iota(jnp.int32, (L, L), 1)
    sub = lax.broadcasted_iota(jnp.int32, (8, LANES), 0)
    rows = pl.ds(pl.multiple_of(c * L, L), L)
    q = q_ref[rows, :]
    v_ext = jnp.concatenate([v_ref[rows, :], jnp.ones((L, LANES), BF16)], axis=1)
    k_t = k_ref[rows, :].astype(F32).T
    qk = _dot(q, k_t.astype(BF16)) * scale
    for d in range(2):
        rev = d == 1
        mask = (si >= ti) if rev else (si <= ti)
        i_row = gate_ref[d, 0, pl.ds(c, 1), :]
        f_row = gate_ref[d, 1, pl.ds(c, 1), :]
        b_row = gate_ref[d, 2, pl.ds(c, 1), :]
        b_col = jnp.sum(jnp.where(mask, f_row, 0.0), axis=1, keepdims=True)
        log_d = jnp.where(mask, b_col + (i_row - b_row), -jnp.inf)
        m_loc = jnp.max(log_d, axis=1, keepdims=True)
        smat = qk * jnp.exp(log_d - m_loc)
        pr_ref[d, rows, :] = _dot(smat.astype(BF16), v_ext)
        bb_ref[d, rows, :] = jnp.broadcast_to(b_col, (L, LANES))
        mb_ref[d, rows, :] = jnp.broadcast_to(m_loc, (L, LANES))
        b_last = jnp.sum(f_row, axis=1, keepdims=True)
        ls_row = b_last - b_row + i_row
        m2 = jnp.max(ls_row, axis=1, keepdims=True)
        kw_t = (k_t * jnp.exp(ls_row - m2)).astype(BF16)
        kv_ref[d, c] = scale * _dot(kw_t, v_ext)
        rp_ref[d, c] = jnp.where(sub == 0, b_last, m2)


def _mlstm_carry(c, d, carry, q_ref, pr_ref, bb_ref, mb_ref, kv_ref, rp_ref, h_ref, L):
    cn, m = carry
    two = lambda x: jnp.concatenate([x, x], axis=1)
    rows = pl.ds(pl.multiple_of(c * L, L), L)
    bb = bb_ref[d, rows, :]
    mb = mb_ref[d, rows, :]
    m_t = jnp.maximum(bb + m, mb)
    a_int = jnp.exp(bb + m - m_t)
    e_loc = jnp.exp(mb - m_t)
    nd = two(a_int) * _dot(q_ref[rows, :], cn.astype(BF16)) + two(e_loc) * pr_ref[d, rows, :]
    h_ref[d, rows, :] = nd[:, :DV_M] / jnp.maximum(jnp.abs(nd[:, DV_M:]), jnp.exp(-m_t))
    rp = rp_ref[d, c]
    b_last, m2 = rp[0:1, :], rp[1:2, :]
    m_new = jnp.maximum(b_last + m, m2)
    a_c = jnp.exp(b_last + m - m_new)
    e2 = jnp.exp(m2 - m_new)
    return two(a_c) * cn + two(e2) * kv_ref[d, c], m_new


def _mlstm_gate_rows(gr_ref, br_ref, gate_ref, L):
    ui = lax.broadcasted_iota(jnp.int32, (L, L), 0)
    si = lax.broadcasted_iota(jnp.int32, (L, L), 1)
    for d in range(2):
        src = ((ui >= si) if d == 1 else (ui <= si)).astype(BF16)
        f = _log_sigmoid(gr_ref[2 + d] + br_ref[2 + d])
        f_hi = f.astype(BF16)
        f_r1 = f - f_hi.astype(F32)
        f_mid = f_r1.astype(BF16)
        f_lo = (f_r1 - f_mid.astype(F32)).astype(BF16)
        gate_ref[d, 0] = gr_ref[d] + br_ref[d]
        gate_ref[d, 1] = f
        gate_ref[d, 2] = _dot(f_hi, src) + _dot(f_mid, src) + _dot(f_lo, src)


def _mlstm_kernel(q_ref, k_ref, v_ref, og_ref, gr_ref, br_ref, c0_ref, n0_ref, m0_ref, hn_ref,
                  y_ref, c_out_ref, n_out_ref, m_out_ref,
                  gate_ref, pr_ref, bb_ref, mb_ref, kv_ref, rp_ref, h_ref, *, L, S):
    nch = S // L
    _mlstm_gate_rows(gr_ref, br_ref, gate_ref, L)

    def local(ci, carry):
        _mlstm_local(ci, q_ref, k_ref, v_ref, gate_ref, pr_ref, bb_ref, mb_ref, kv_ref, rp_ref, L)
        return carry

    lax.fori_loop(0, nch, local, 0, unroll=4)
    step = functools.partial(_mlstm_carry, q_ref=q_ref, pr_ref=pr_ref, bb_ref=bb_ref, mb_ref=mb_ref,
                             kv_ref=kv_ref, rp_ref=rp_ref, h_ref=h_ref, L=L)

    def body(ci, carry):
        return step(ci, 0, carry[0]), step(nch - 1 - ci, 1, carry[1])

    def init(d):
        n_rep = jnp.broadcast_to(n0_ref[d], (DK_M, DK_M)).T
        return jnp.concatenate([c0_ref[d], n_rep], axis=1), m0_ref[d]

    fin = lax.fori_loop(0, nch, body, (init(0), init(1)), unroll=2)
    for d in range(2):
        cn, m = fin[d]
        c_out_ref[d] = cn[:, :DV_M]
        n_out_ref[d] = cn[:, DV_M:].T[0:1, :]
        m_out_ref[d] = m

    hm = h_ref[0] + h_ref[1]
    y = hm * lax.rsqrt(jnp.mean(hm * hm, axis=-1, keepdims=True) + EPS) * hn_ref[...]
    y_ref[...] = (y * _sigmoid(og_ref[...].astype(F32))).astype(y_ref.dtype)


def _mlstm(a16, grow, brow, c0, n0, m0, hnorm, B, S):
    L = min(MLSTM_CHUNK, S)
    nch = S // L
    cb = lambda off: off // LANES
    kern = functools.partial(_mlstm_kernel, L=L, S=S)
    return pl.pallas_call(
        kern,
        grid=(B, H_M),
        in_specs=[
            pl.BlockSpec((S, LANES), lambda b, h: (b, cb(A16_MQ) + h)),
            pl.BlockSpec((S, LANES), lambda b, h: (b, cb(A16_MK) + h)),
            pl.BlockSpec((S, LANES), lambda b, h: (b, cb(A16_MV) + h)),
            pl.BlockSpec((S, LANES), lambda b, h: (b, cb(A16_MO) + h)),
            pl.BlockSpec((None, 4, None, nch, L), lambda b, h: (h, 0, b, 0, 0)),
            pl.BlockSpec((None, 4, 1, 1), lambda b, h: (h, 0, 0, 0)),
            pl.BlockSpec((None, 2, None, DK_M, DV_M), lambda b, h: (b, 0, h, 0, 0)),
            pl.BlockSpec((None, 2, None, 1, DK_M), lambda b, h: (b, 0, h, 0, 0)),
            pl.BlockSpec((None, 2, None, 1, LANES), lambda b, h: (b, 0, h, 0, 0)),
            pl.BlockSpec((1, LANES), lambda b, h: (0, h)),
        ],
        out_specs=[
            pl.BlockSpec((S, LANES), lambda b, h: (b, h)),
            pl.BlockSpec((None, 2, None, DK_M, DV_M), lambda b, h: (b, 0, h, 0, 0)),
            pl.BlockSpec((None, 2, None, 1, DK_M), lambda b, h: (b, 0, h, 0, 0)),
            pl.BlockSpec((None, 2, None, 1, LANES), lambda b, h: (b, 0, h, 0, 0)),
        ],
        out_shape=[
            jax.ShapeDtypeStruct((B * S, BRANCH_W), BF16),
            jax.ShapeDtypeStruct((B, 2, H_M, DK_M, DV_M), F32),
            jax.ShapeDtypeStruct((B, 2, H_M, 1, DK_M), F32),
            jax.ShapeDtypeStruct((B, 2, H_M, 1, LANES), F32),
        ],
        scratch_shapes=[pltpu.VMEM((2, 3, nch, L), F32),
                        pltpu.VMEM((2, S, 2 * DV_M), F32), pltpu.VMEM((2, S, LANES), F32),
                        pltpu.VMEM((2, S, LANES), F32), pltpu.VMEM((2, nch, DK_M, 2 * DV_M), F32),
                        pltpu.VMEM((2, nch, 8, LANES), F32), pltpu.VMEM((2, S, DV_M), F32)],
        compiler_params=_cparams("parallel", "parallel"),
        name="mlstm",
    )(a16, a16, a16, a16, grow, brow, c0, n0, m0, hnorm)


def _gla_local(c, q2_ref, k2_ref, v_ref, la_ref, oa_ref, qt_ref, u_ref, dec_ref, L):
    nb = L // GLA_SUB
    ti = lax.broadcasted_iota(jnp.int32, (L, L), 0)
    si = lax.broadcasted_iota(jnp.int32, (L, L), 1)
    row_blk = lax.broadcasted_iota(jnp.int32, (L, LANES), 0) // GLA_SUB
    lo_half = lax.broadcasted_iota(jnp.int32, (L, LANES), 1) < DK_G
    eye = (lax.broadcasted_iota(jnp.int32, (DK_G, LANES), 0)
           == lax.broadcasted_iota(jnp.int32, (DK_G, LANES), 1))
    rows = pl.ds(pl.multiple_of(c * L, L), L)
    q2 = q2_ref[rows, :]
    k2 = k2_ref[rows, :]
    v = v_ref[rows, :]
    row = lax.broadcasted_iota(jnp.int32, (L, LANES), 0)
    for d in range(2):
        rev = d == 1
        mask = (si >= ti) if rev else (si <= ti)
        g2 = la_ref[d, rows, :]
        step = 1
        while step < L:
            if rev:
                g2 = g2 + jnp.where(row < L - step, pltpu.roll(g2, L - step, 0), 0.0)
            else:
                g2 = g2 + jnp.where(row >= step, pltpu.roll(g2, step, 0), 0.0)
            step *= 2
        qt_ref[d, rows, :] = (q2 * jnp.exp(g2))[:, :DK_G].astype(BF16)
        a_parts, b_parts = [], []
        for p in range(nb // 2):
            ia, ib = 2 * p, 2 * p + 1
            ra = ia * GLA_SUB + (GLA_SUB - 1 if rev else 0)
            rb = ib * GLA_SUB + (GLA_SUB - 1 if rev else 0)
            ref2 = jnp.where(lo_half, g2[ra:ra + 1, :], g2[rb:rb + 1, :])
            blk = jnp.where(lo_half, ia, ib)
            in_blk = row_blk == blk
            key_ok = (row_blk >= blk) if rev else (row_blk <= blk)
            a_parts.append(jnp.where(in_blk, q2 * jnp.exp(jnp.minimum(g2 - ref2, 0.0)), 0.0))
            b_parts.append(
                jnp.where(key_ok, k2 * jnp.exp(jnp.minimum(ref2 - g2, GLA_EXP_CLAMP)), 0.0))
        a_big = jnp.concatenate(a_parts, axis=1).astype(BF16)
        b_big = jnp.concatenate(b_parts, axis=1).astype(BF16)
        att = jnp.where(mask, _dot_nt(a_big, b_big), 0.0)
        oa_ref[d, rows, :] = _dot(att.astype(BF16), v)
        gl_row = 0 if rev else L - 1
        glast = g2[gl_row:gl_row + 1, :]
        kd = (k2 * jnp.exp(glast - g2))[:, :DK_G]
        u_ref[d, c] = _dot_tn(kd.astype(BF16), v)
        glast_col = jnp.sum(jnp.where(eye, glast, 0.0), axis=1, keepdims=True)
        dec_ref[d, c] = jnp.broadcast_to(jnp.exp(glast_col), (DK_G, DV_G))


def _gla_kernel(qk_ref, v_ref, sm_ref, wup_ref, bup_ref, s0_ref, gr_ref, hn_ref,
                y_ref, s_out_ref, la_ref, q2_ref, k2_ref, oa_ref, oi_ref, qt_ref, u_ref, dec_ref,
                *, L, S):
    nch = S // L
    sm = sm_ref[...].astype(BF16)
    for d in range(2):
        la_ref[d] = _log_sigmoid(_dot(sm, wup_ref[d]) + bup_ref[d]) * (1.0 / GLA_TAU)
    qk = qk_ref[...].astype(F32)
    qk_sw = pltpu.roll(qk, DK_G, 1)
    lo_half = lax.broadcasted_iota(jnp.int32, qk.shape, 1) < DK_G
    q2_ref[...] = jnp.where(lo_half, qk, qk_sw) * (DK_G ** -0.5)
    k2_ref[...] = jnp.where(lo_half, qk_sw, qk)

    def local(ci, carry):
        _gla_local(ci, q2_ref, k2_ref, v_ref, la_ref, oa_ref, qt_ref, u_ref, dec_ref, L)
        return carry

    lax.fori_loop(0, nch, local, 0, unroll=8)

    def body(ci, carry):
        out = []
        for d, c in ((0, ci), (1, nch - 1 - ci)):
            rows = pl.ds(pl.multiple_of(c * L, L), L)
            st = carry[d]
            oi_ref[d, rows, :] = _dot(qt_ref[d, rows, :], st.astype(BF16))
            out.append(dec_ref[d, c] * st + u_ref[d, c])
        return tuple(out)

    st_f, st_b = lax.fori_loop(0, nch, body, (s0_ref[0], s0_ref[1]), unroll=4)
    s_out_ref[0] = st_f
    s_out_ref[1] = st_b

    og = (oa_ref[0] + oi_ref[0]) + (oa_ref[1] + oi_ref[1])
    y = og * lax.rsqrt(jnp.mean(og * og, axis=-1, keepdims=True) + EPS) * hn_ref[...]
    gr = gr_ref[...].astype(F32)
    y_ref[...] = (y * (gr * _sigmoid(gr))).astype(y_ref.dtype)


def _gla(a32, a16, wup, bup, s0, hnorm, B, S):
    L = min(GLA_CHUNK, S)
    nch = S // L
    cb = lambda off: off // LANES
    kern = functools.partial(_gla_kernel, L=L, S=S)
    return pl.pallas_call(
        kern,
        grid=(B, H_G),
        in_specs=[
            pl.BlockSpec((S, LANES), lambda b, h: (b, cb(A16_GQK) + h)),
            pl.BlockSpec((S, LANES), lambda b, h: (b, cb(A16_GV) + h)),
            pl.BlockSpec((S, LANES), lambda b, h: (b, cb(A32_SM))),
            pl.BlockSpec((None, 2, LANES, LANES), lambda b, h: (h, 0, 0, 0)),
            pl.BlockSpec((None, 2, 1, LANES), lambda b, h: (h, 0, 0, 0)),
            pl.BlockSpec((None, 2, None, DK_G, DV_G), lambda b, h: (b, 0, h, 0, 0)),
            pl.BlockSpec((S, LANES), lambda b, h: (b, cb(A16_GR) + h)),
            pl.BlockSpec((1, LANES), lambda b, h: (0, h)),
        ],
        out_specs=[
            pl.BlockSpec((S, LANES), lambda b, h: (b, h)),
            pl.BlockSpec((None, 2, None, DK_G, DV_G), lambda b, h: (b, 0, h, 0, 0)),
        ],
        out_shape=[
            jax.ShapeDtypeStruct((B * S, BRANCH_W), BF16),
            jax.ShapeDtypeStruct((B, 2, H_G, DK_G, DV_G), F32),
        ],
        scratch_shapes=[pltpu.VMEM((2, S, LANES), F32), pltpu.VMEM((S, LANES), F32),
                        pltpu.VMEM((S, LANES), F32), pltpu.VMEM((2, S, DV_G), F32),
                        pltpu.VMEM((2, S, DV_G), F32), pltpu.VMEM((2, S, DK_G), BF16),
                        pltpu.VMEM((2, nch, DK_G, DV_G), F32), pltpu.VMEM((2, nch, DK_G, DV_G), F32)],
        compiler_params=_cparams("parallel", "parallel"),
        name="gla",
    )(a16, a16, a32, wup, bup, s0, a16, hnorm)


def _rope(x, cos, sin_signed):
    lane = lax.broadcasted_iota(jnp.int32, x.shape, 1)
    first = (lane % DQK_D) < (DQK_D // 2)
    partner = jnp.where(first, pltpu.roll(x, LANES - DQK_D // 2, 1), pltpu.roll(x, DQK_D // 2, 1))
    return x * cos + partner * sin_signed


def _attn_kernel(*refs, S, P, TQ, lam_init, has_ctx):
    if has_ctx:
        (q_ref, k_ref, v_ref, ck_ref, cv_ref, cos_ref, sin_ref, lam_ref, hn_ref,
         y_ref, kk_ref, vv_ref) = refs
    else:
        q_ref, k_ref, v_ref, lam_ref, hn_ref, y_ref, kk_ref, vv_ref = refs
    qi = pl.program_id(2)

    @pl.when(qi == 0)
    def _():
        k = k_ref[...]
        if has_ctx:
            k = _rope(k, cos_ref[...], sin_ref[...])
            kk_ref[S:S + P, :] = ck_ref[...].astype(BF16)
            vv_ref[S:S + P, :] = cv_ref[...].astype(BF16)
        kk_ref[0:S, :] = k.astype(BF16)
        vv_ref[0:S, :] = v_ref[...].astype(BF16)

    kk = kk_ref[...]
    vv = vv_ref[...]
    lv = lam_ref[...]
    lam = (jnp.exp(jnp.sum(lv[0:1, :] * lv[1:2, :], axis=-1, keepdims=True))
           - jnp.exp(jnp.sum(lv[2:3, :] * lv[3:4, :], axis=-1, keepdims=True)) + lam_init)
    q = q_ref[...].astype(F32)
    if has_ctx:
        r0 = pl.multiple_of(qi * TQ, TQ)
        q = _rope(q, cos_ref[pl.ds(r0, TQ), :], sin_ref[pl.ds(r0, TQ), :])
    q = q * (DQK_D ** -0.5 * math.log2(math.e))
    lane = lax.broadcasted_iota(jnp.int32, q.shape, 1)
    es, ls = [], []
    for comp in range(2):
        sel = (lane < DQK_D) if comp == 0 else (lane >= DQK_D)
        s = _dot_nt(jnp.where(sel, q, 0.0).astype(BF16), kk)
        e = jnp.exp2(s - jnp.max(s, axis=-1, keepdims=True))
        es.append(e)
        ls.append(jnp.sum(e, axis=-1, keepdims=True))
    w = es[0] - es[1] * (lam * ls[0] / ls[1])
    o = _dot(w.astype(BF16), vv) * (1.0 / ls[0])
    y = o * lax.rsqrt(jnp.mean(o * o, axis=-1, keepdims=True) + EPS) * hn_ref[...]
    y_ref[...] = (y * (1.0 - lam_init)).astype(y_ref.dtype)


def _attn(a16, a32, lamv, hnorm, B, S, lam_init, ctx=None):
    TQ = min(256, S)
    nq = S // TQ
    has_ctx = ctx is not None
    P = ctx[0].shape[2] if has_ctx else 0
    cb = lambda off: off // LANES
    kern = functools.partial(_attn_kernel, S=S, P=P, TQ=TQ, lam_init=lam_init, has_ctx=has_ctx)
    in_specs = [
        pl.BlockSpec((TQ, LANES), lambda b, h, i: (b * nq + i, cb(A16_DQ) + h)),
        pl.BlockSpec((S, LANES), lambda b, h, i: (b, cb(A32_DK) + h)),
        pl.BlockSpec((S, LANES), lambda b, h, i: (b, cb(A32_DV) + h)),
    ]
    args = [a16, a32, a32]
    if has_ctx:
        ck, cv, layer, cos, sin = ctx
        in_specs += [
            pl.BlockSpec((None, None, P, LANES), lambda b, h, i: (b, layer, 0, h)),
            pl.BlockSpec((None, None, P, LANES), lambda b, h, i: (b, layer, 0, h)),
            pl.BlockSpec((S, LANES), lambda b, h, i: (0, 0)),
            pl.BlockSpec((S, LANES), lambda b, h, i: (0, 0)),
        ]
        args += [ck, cv, cos, sin]
    in_specs += [
        pl.BlockSpec((4, DQK_D), lambda b, h, i: (0, 0)),
        pl.BlockSpec((1, LANES), lambda b, h, i: (0, h)),
    ]
    args += [lamv, hnorm]
    return pl.pallas_call(
        kern,
        grid=(B, H_D, nq),
        in_specs=in_specs,
        out_specs=pl.BlockSpec((TQ, LANES), lambda b, h, i: (b * nq + i, h)),
        out_shape=jax.ShapeDtypeStruct((B * S, BRANCH_W), BF16),
        scratch_shapes=[pltpu.VMEM((S + P, LANES), BF16), pltpu.VMEM((S + P, LANES), BF16)],
        compiler_params=_cparams("parallel", "parallel", "arbitrary"),
        name="diff_attn",
    )(*args)


def _conv_kernel(ca_ref, cb_ref, w_ref, g_ref, b_ref, y_ref, pad_ref, cv_ref, *, S):
    ca = ca_ref[...].astype(F32)
    cbv = cb_ref[...].astype(F32)
    zeros = jnp.zeros((CONV_PAD, BRANCH_W), F32)
    pad_ref[0:CONV_PAD, :] = zeros
    pad_ref[CONV_PAD + S:2 * CONV_PAD + S, :] = zeros
    pad_ref[CONV_PAD:CONV_PAD + S, :] = ca * _sigmoid(cbv)
    off = CONV_PAD - CONV_W // 2

    win_rows = CONV_ROWS + 2 * CONV_PAD

    def body(i, carry):
        base = pl.multiple_of(i * CONV_ROWS, CONV_ROWS)
        for lb in range(BRANCH_W // LANES):
            cols = slice(lb * LANES, (lb + 1) * LANES)
            win = pad_ref[pl.ds(base, win_rows), cols]
            acc = jnp.zeros((CONV_ROWS, LANES), F32)
            for r in range(8):
                rolled = win if r == 0 else pltpu.roll(win, win_rows - r, 0)
                for a in range(2 * CONV_PAD // 8):
                    j = 8 * a + r - off
                    if 0 <= j < CONV_W:
                        acc = acc + rolled[8 * a:8 * a + CONV_ROWS, :] * w_ref[j:j + 1, cols]
            cv_ref[:, cols] = acc
        acc = cv_ref[...]
        mu = jnp.mean(acc, axis=-1, keepdims=True)
        xc = acc - mu
        yn = xc * lax.rsqrt(jnp.mean(xc * xc, axis=-1, keepdims=True) + EPS) * g_ref[...] + b_ref[...]
        y_ref[pl.ds(base, CONV_ROWS), :] = (yn * _sigmoid(yn)).astype(y_ref.dtype)
        return carry

    lax.fori_loop(0, S // CONV_ROWS, body, 0)


def _conv(a16, w_dw, ln_g, ln_b, B, S):
    cb = lambda off: off // BRANCH_W
    kern = functools.partial(_conv_kernel, S=S)
    return pl.pallas_call(
        kern,
        grid=(B,),
        in_specs=[
            pl.BlockSpec((S, BRANCH_W), lambda b: (b, cb(A16_CA))),
            pl.BlockSpec((S, BRANCH_W), lambda b: (b, cb(A16_CB))),
            pl.BlockSpec((CONV_W + 1, BRANCH_W), lambda b: (0, 0)),
            pl.BlockSpec((1, BRANCH_W), lambda b: (0, 0)),
            pl.BlockSpec((1, BRANCH_W), lambda b: (0, 0)),
        ],
        out_specs=pl.BlockSpec((S, BRANCH_W), lambda b: (b, 0)),
        out_shape=jax.ShapeDtypeStruct((B * S, BRANCH_W), BF16),
        scratch_shapes=[pltpu.VMEM((S + 2 * CONV_PAD, BRANCH_W), F32),
                        pltpu.VMEM((CONV_ROWS, BRANCH_W), F32)],
        compiler_params=_cparams("parallel"),
        name="glu_conv_ln",
    )(a16, a16, w_dw, ln_g, ln_b)


def _merge_kernel(x_ref, mod_ref, ym_ref, yd_ref, yg_ref, yc_ref, g0_ref, g1_ref, g2_ref, g3_ref,
                  wb_ref, wo_ref, n2_ref, wr_ref, x1_ref, h2_ref, route_ref):
    ys = (ym_ref, yd_ref, yg_ref, yc_ref)
    gs = (g0_ref, g1_ref, g2_ref, g3_ref)
    merged = None
    for nbr in range(N_BRANCH):
        br = _dot(ys[nbr][...], wb_ref[nbr])
        term = _sigmoid(gs[nbr][...].astype(F32)) * br
        merged = term if merged is None else merged + term
    out = _dot(merged.astype(BF16), wo_ref[...])
    x1 = x_ref[...] + mod_ref[2:3, :] * out
    x1_ref[...] = x1
    y = x1 * lax.rsqrt(jnp.mean(x1 * x1, axis=-1, keepdims=True) + EPS) * n2_ref[...]
    h2 = y * (1.0 + mod_ref[4:5, :]) + mod_ref[3:4, :]
    h2_ref[...] = h2.reshape(h2_ref.shape)
    wr = wr_ref[...]
    h_hi, w_hi = h2.astype(BF16), wr.astype(BF16)
    h_lo = (h2 - h_hi.astype(F32)).astype(BF16)
    w_lo = (wr - w_hi.astype(F32)).astype(BF16)
    logits = _dot(h_hi, w_hi) + (_dot(h_hi, w_lo) + _dot(h_lo, w_hi))
    lane = lax.broadcasted_iota(jnp.int32, logits.shape, 1)
    neg = -jnp.inf
    big = jnp.int32(LANES)
    is_g = lane < N_GROUPS
    gl = jnp.where(is_g, logits, neg)
    gmax = jnp.max(gl, axis=-1, keepdims=True)
    gidx = jnp.min(jnp.where(is_g & (gl == gmax), lane, big), axis=-1, keepdims=True)
    g_p = 1.0 / jnp.sum(jnp.where(is_g, jnp.exp(gl - gmax), 0.0), axis=-1, keepdims=True)
    e_lane = lane - N_GROUPS
    in_grp = (e_lane >= 0) & (e_lane < N_EXPERTS) & ((e_lane // EXPERTS_PER_GROUP) == gidx)
    el = jnp.where(in_grp, logits, neg)
    v1 = jnp.max(el, axis=-1, keepdims=True)
    i1 = jnp.min(jnp.where(in_grp & (el == v1), lane, big), axis=-1, keepdims=True)
    el2 = jnp.where(lane == i1, neg, el)
    v2 = jnp.max(el2, axis=-1, keepdims=True)
    i2 = jnp.min(jnp.where(in_grp & (lane != i1) & (el2 == v2), lane, big), axis=-1, keepdims=True)
    e2 = jnp.exp(v2 - v1)
    w1 = g_p / (1.0 + e2)
    w2 = g_p * e2 / (1.0 + e2)
    id1 = (i1 - N_GROUPS).astype(F32)
    id2 = (i2 - N_GROUPS).astype(F32)
    route_ref[...] = jnp.where(lane == 0, id1, jnp.where(lane == 1, id2,
                               jnp.where(lane == 2, w1, jnp.where(lane == 3, w2, 0.0))))


def _merge(x2d, mod, a16, ym, yd, yg, yc, wb, wo, n2, wr, rows_per_mod, tm):
    T = x2d.shape[0]
    gcb = A16_GATE // D_MODEL
    row = lambda i: (i, 0)
    return pl.pallas_call(
        _merge_kernel,
        grid=(T // tm,),
        in_specs=[
            pl.BlockSpec((tm, D_MODEL), row),
            pl.BlockSpec((None, N_MOD, D_MODEL), lambda i: ((i * tm) // rows_per_mod, 0, 0)),
            pl.BlockSpec((tm, BRANCH_W), row),
            pl.BlockSpec((tm, BRANCH_W), row),
            pl.BlockSpec((tm, BRANCH_W), row),
            pl.BlockSpec((tm, BRANCH_W), row),
            pl.BlockSpec((tm, D_MODEL), lambda i: (i, gcb + 0)),
            pl.BlockSpec((tm, D_MODEL), lambda i: (i, gcb + 1)),
            pl.BlockSpec((tm, D_MODEL), lambda i: (i, gcb + 2)),
            pl.BlockSpec((tm, D_MODEL), lambda i: (i, gcb + 3)),
            pl.BlockSpec((N_BRANCH, BRANCH_W, D_MODEL), lambda i: (0, 0, 0)),
            pl.BlockSpec((D_MODEL, D_MODEL), lambda i: (0, 0)),
            pl.BlockSpec((1, D_MODEL), lambda i: (0, 0)),
            pl.BlockSpec((D_MODEL, LANES), lambda i: (0, 0)),
        ],
        out_specs=[
            pl.BlockSpec((tm, D_MODEL), row),
            pl.BlockSpec((tm, TOK_SUB, LANES), lambda i: (i, 0, 0)),
            pl.BlockSpec((tm, LANES), row),
        ],
        out_shape=[
            jax.ShapeDtypeStruct((T, D_MODEL), F32),
            jax.ShapeDtypeStruct((T, TOK_SUB, LANES), F32),
            jax.ShapeDtypeStruct((T, LANES), F32),
        ],
        compiler_params=_cparams("parallel"),
        name="merge_outproj_route",
    )(x2d, mod, ym, yd, yg, yc, a16, a16, a16, a16, wb, wo, n2, wr)


def _gather_rows(idx_ref, src_hbm, dst, sem, n):
    def body(j, carry):
        for u in range(ROW_DMA_UNROLL):
            r = j * ROW_DMA_UNROLL + u
            pltpu.make_async_copy(src_hbm.at[idx_ref[0, r]], dst.at[r], sem).start(priority=u % 2)
        return carry

    lax.fori_loop(0, n // ROW_DMA_UNROLL, body, 0)


def _scatter_rows(idx_ref, src, dst_hbm, sem, n):
    def body(j, carry):
        for u in range(ROW_DMA_UNROLL):
            r = j * ROW_DMA_UNROLL + u
            pltpu.make_async_copy(src.at[r], dst_hbm.at[idx_ref[0, r]], sem).start(priority=u % 2)
        return carry

    lax.fori_loop(0, n // ROW_DMA_UNROLL, body, 0)


def _wait_rows(buf, sem):
    pltpu.make_async_copy(buf, buf, sem).wait()


def _moe_dispatch_kernel(d0_ref, d1_ref, h_ref, xg_in, xg_out, sem, *, tm):
    del xg_in
    _scatter_rows(d0_ref, h_ref, xg_out, sem.at[0], tm)
    _scatter_rows(d1_ref, h_ref, xg_out, sem.at[1], tm)
    _wait_rows(h_ref, sem.at[0])
    _wait_rows(h_ref, sem.at[1])


def _moe_dispatch(h3, dest, xg_init, tm):
    T = h3.shape[0]
    kern = functools.partial(_moe_dispatch_kernel, tm=tm)
    return pl.pallas_call(
        kern,
        grid=(T // tm,),
        in_specs=[
            pl.BlockSpec((None, None, 1, tm), lambda i: (0, i, 0, 0), memory_space=pltpu.SMEM),
            pl.BlockSpec((None, None, 1, tm), lambda i: (1, i, 0, 0), memory_space=pltpu.SMEM),
            pl.BlockSpec((tm, TOK_SUB, LANES), lambda i: (i, 0, 0)),
            pl.BlockSpec(memory_space=pl.ANY),
        ],
        out_specs=pl.BlockSpec(memory_space=pl.ANY),
        out_shape=jax.ShapeDtypeStruct(xg_init.shape, F32),
        input_output_aliases={3: 0},
        scratch_shapes=[pltpu.SemaphoreType.DMA((2,))],
        compiler_params=_cparams("arbitrary"),
        name="moe_dispatch",
    )(dest, dest, h3, xg_init)


def _moe_ffn_kernel(te_ref, x_ref, w1_ref, w3_ref, w2_ref, o_ref):
    del te_ref
    x = x_ref[...].reshape(x_ref.shape[0], D_MODEL).astype(BF16)
    a = _dot(x, w1_ref[...].astype(BF16))
    b = _dot(x, w3_ref[...].astype(BF16))
    s = (a * _sigmoid(a)) * b
    y = _dot(s.astype(BF16), w2_ref[...].astype(BF16))
    o_ref[...] = y.reshape(o_ref.shape)


def _moe_ffn(xg, tile_e, w1, w3, w2, rows):
    ntiles = xg.shape[0] // rows
    tile = pl.BlockSpec((rows, TOK_SUB, LANES), lambda i, te: (i, 0, 0))
    return pl.pallas_call(
        _moe_ffn_kernel,
        grid_spec=pltpu.PrefetchScalarGridSpec(
            num_scalar_prefetch=1,
            grid=(ntiles,),
            in_specs=[
                tile,
                pl.BlockSpec((None, D_MODEL, D_EXPERT), lambda i, te: (te[i], 0, 0)),
                pl.BlockSpec((None, D_MODEL, D_EXPERT), lambda i, te: (te[i], 0, 0)),
                pl.BlockSpec((None, D_EXPERT, D_MODEL), lambda i, te: (te[i], 0, 0)),
            ],
            out_specs=tile,
        ),
        out_shape=jax.ShapeDtypeStruct(xg.shape, F32),
        compiler_params=_cparams("parallel"),
        name="moe_grouped_experts",
    )(tile_e, xg, w1, w3, w2)


def _moe_combine_kernel(d0_ref, d1_ref, y_hbm, route_ref, x1_ref, mod_ref, fn_ref, o_ref,
                        ga, gb, sem, *, tm, final_norm):
    _gather_rows(d0_ref, y_hbm, ga, sem.at[0], tm)
    _gather_rows(d1_ref, y_hbm, gb, sem.at[1], tm)
    rt = route_ref[...]
    _wait_rows(ga, sem.at[0])
    _wait_rows(gb, sem.at[1])
    y = rt[:, 2:3] * ga[...].reshape(tm, D_MODEL) + rt[:, 3:4] * gb[...].reshape(tm, D_MODEL)
    x2 = x1_ref[...] + mod_ref[5:6, :] * y
    if final_norm:
        x2 = x2 * lax.rsqrt(jnp.mean(x2 * x2, axis=-1, keepdims=True) + EPS) * fn_ref[...]
    o_ref[...] = x2


def _moe_combine(yg, dest, route, x1, mod, fn, rows_per_mod, tm, final_norm):
    T = x1.shape[0]
    kern = functools.partial(_moe_combine_kernel, tm=tm, final_norm=final_norm)
    return pl.pallas_call(
        kern,
        grid=(T // tm,),
        in_specs=[
            pl.BlockSpec((None, None, 1, tm), lambda i: (0, i, 0, 0), memory_space=pltpu.SMEM),
            pl.BlockSpec((None, None, 1, tm), lambda i: (1, i, 0, 0), memory_space=pltpu.SMEM),
            pl.BlockSpec(memory_space=pl.ANY),
            pl.BlockSpec((tm, LANES), lambda i: (i, 0)),
            pl.BlockSpec((tm, D_MODEL), lambda i: (i, 0)),
            pl.BlockSpec((None, N_MOD, D_MODEL), lambda i: ((i * tm) // rows_per_mod, 0, 0)),
            pl.BlockSpec((1, D_MODEL), lambda i: (0, 0)),
        ],
        out_specs=pl.BlockSpec((tm, D_MODEL), lambda i: (i, 0)),
        out_shape=jax.ShapeDtypeStruct((T, D_MODEL), F32),
        scratch_shapes=[pltpu.VMEM((tm, TOK_SUB, LANES), F32), pltpu.VMEM((tm, TOK_SUB, LANES), F32),
                        pltpu.SemaphoreType.DMA((2,))],
        compiler_params=_cparams("arbitrary"),
        name="moe_combine",
    )(dest, dest, yg, route, x1, mod, fn)


def _moe_rows(T):
    rows = 512 if T >= 4096 else 256
    ntiles = (2 * T + N_EXPERTS * (rows - 1) + rows - 1) // rows
    return rows, ntiles * rows


def _route_plan(route, T):
    tile, n_rows = _moe_rows(T)
    ef = route[:, 0:2].astype(jnp.int32).reshape(-1)
    oh = (ef[:, None] == jnp.arange(N_EXPERTS, dtype=jnp.int32)[None, :]).astype(jnp.int32)
    csum = jnp.cumsum(oh, axis=0)
    rank = jnp.sum((csum - oh) * oh, axis=1)
    counts = csum[-1]
    padded = ((counts + tile - 1) // tile) * tile
    seg_end = jnp.cumsum(padded)
    dest = jnp.sum(oh * (seg_end - padded)[None, :], axis=1) + rank
    tile_row = jnp.arange(n_rows // tile, dtype=jnp.int32) * tile
    tile_e = jnp.minimum(jnp.sum((tile_row[:, None] >= seg_end[None, :]).astype(jnp.int32), axis=1),
                         N_EXPERTS - 1)
    return jnp.transpose(dest.reshape(T, 2)), tile_e


def _split_w_in(w):
    sizes = (H_M * DK_M, H_M * DK_M, H_M * DV_M, H_M * DV_M, 2 * H_M, 2 * H_M,
             H_D * 2 * DQK_D, H_D * 2 * DQK_D, H_D * DV_D,
             H_G * DK_G, H_G * DK_G, H_G * DV_G, 2 * GATE_RANK, H_G * DV_G,
             BRANCH_W, BRANCH_W, N_BRANCH * D_MODEL)
    outs, acc = [], 0
    for s in sizes:
        outs.append(w[:, acc:acc + s])
        acc += s
    return outs


def _pack_layer_params(p):
    (m_q, m_k, m_v, m_o, m_i, m_f, d_q, d_k, d_v, g_q, g_k, g_v, g_a, g_r, c_a, c_b, gate) = \
        _split_w_in(p['w_in'])
    gqk = jnp.concatenate([g_q.reshape(D_MODEL, H_G, DK_G), g_k.reshape(D_MODEL, H_G, DK_G)],
                          axis=2).reshape(D_MODEL, 2 * H_G * DK_G)
    small = jnp.concatenate(
        [m_i, m_f, g_a, jnp.zeros((D_MODEL, LANES - 4 * H_M - 2 * GATE_RANK), F32)], axis=1)
    w16 = jnp.concatenate([m_q, m_k, m_v, m_o, gate, g_r, c_a, c_b, gqk, g_v, d_q],
                          axis=1).astype(BF16)
    w32 = jnp.concatenate([d_k, d_v, small], axis=1).astype(BF16)
    bi = p['b_m_i'].reshape(2, H_M)
    bf = p['b_m_f'].reshape(2, H_M)
    bcol = jnp.stack([bi[0], bi[1], bf[0], bf[1]], axis=-1)
    wup = p['w_gla_up'].reshape(2, GATE_RANK, H_G, DK_G)
    wup_pad = jnp.zeros((H_G, 2, LANES, LANES), F32)
    bup = p['b_gla_gate'].reshape(2, H_G, DK_G)
    for d in range(2):
        blk = jnp.transpose(wup[d], (1, 0, 2))
        blk = jnp.concatenate([blk, blk], axis=-1)
        r0 = SM_GA + d * GATE_RANK
        wup_pad = wup_pad.at[:, d, r0:r0 + GATE_RANK, :].set(blk)
    bup2 = jnp.transpose(jnp.concatenate([bup, bup], axis=-1), (1, 0, 2))[:, :, None, :]
    wr = jnp.concatenate([p['w_group_router'], p['w_expert_router'],
                          jnp.zeros((D_MODEL, LANES - N_GROUPS - N_EXPERTS), F32)], axis=1)
    return dict(
        w16=w16, w32=w32, brow=bcol.reshape(H_M, 4, 1, 1),
        wup=wup_pad.astype(BF16), bup=bup2,
        wdw=jnp.concatenate([p['w_dw'], jnp.zeros((1, BRANCH_W), F32)], axis=0),
        ln_g=p['conv_ln_g'].reshape(1, BRANCH_W), ln_b=p['conv_ln_b'].reshape(1, BRANCH_W),
        hn_m=p['hnorm_m'].reshape(1, BRANCH_W), hn_d=p['hnorm_d'].reshape(1, BRANCH_W),
        hn_g=p['hnorm_g'].reshape(1, BRANCH_W),
        lamv=jnp.stack([p['lam_q1'], p['lam_k1'], p['lam_q2'], p['lam_k2']], axis=0),
        wb=p['w_branch'].astype(BF16), wo=p['w_out'].astype(BF16),
        n1=p['norm1'].reshape(1, D_MODEL), n2=p['norm2'].reshape(1, D_MODEL), wr=wr,
    )


def _rope_tables(S):
    rows = S // GRID_W
    r, col = jnp.meshgrid(jnp.arange(rows, dtype=F32), jnp.arange(GRID_W, dtype=F32), indexing='ij')
    r, col = r.reshape(-1), col.reshape(-1)
    n_freq = DQK_D // 4
    inv = ROPE_BASE ** (-jnp.arange(n_freq, dtype=F32) / n_freq)
    ang = jnp.concatenate([r[:, None] * inv, col[:, None] * inv], axis=-1)
    cos, sin = jnp.cos(ang), jnp.sin(ang)
    cos_t = jnp.tile(cos, (1, LANES // (DQK_D // 2)))
    sin_t = jnp.tile(jnp.concatenate([-sin, sin], axis=-1), (1, LANES // DQK_D))
    return cos_t, sin_t


def _pick_tile(T, cap):
    t = min(T, cap)
    while T % t:
        t //= 2
    return t


def _layer(x2d, mod, pk, B, S, lam_init, ctx, final_norm, fn, xg):
    T = B * S
    rows_per_mod = T // mod.shape[0]
    tm = _pick_tile(rows_per_mod, 1024)
    a16 = _inproj(x2d, mod, pk['n1'], pk['w16'], BF16, rows_per_mod, tm, N_A16 // 4)
    a32 = _inproj(x2d, mod, pk['n1'], pk['w32'], F32, rows_per_mod, tm, N_A32)

    L = min(MLSTM_CHUNK, S)
    sm = a32[:, A32_SM:A32_SM + 4 * H_M]
    grow = jnp.transpose(sm.reshape(B, S // L, L, 4, H_M), (4, 3, 0, 1, 2))
    if ctx is None:
        c0 = jnp.zeros((B, 2, H_M, DK_M, DV_M), F32)
        n0 = jnp.zeros((B, 2, H_M, 1, DK_M), F32)
        m0 = jnp.zeros((B, 2, H_M, 1, LANES), F32)
        s0 = jnp.zeros((B, 2, H_G, DK_G, DV_G), F32)
        attn_ctx = None
    else:
        c0 = ctx['C']
        n0 = ctx['n'][:, :, :, None, :]
        m0 = jnp.broadcast_to(ctx['m'][:, :, :, None, None], (B, 2, H_M, 1, LANES))
        s0 = ctx['S']
        attn_ctx = (ctx['k'], ctx['v'], ctx['layer'], ctx['cos'], ctx['sin'])
    ym, c_f, n_f, m_f = _mlstm(a16, grow, pk['brow'], c0, n0, m0, pk['hn_m'], B, S)
    yd = _attn(a16, a32, pk['lamv'], pk['hn_d'], B, S, lam_init, attn_ctx)
    yg, s_f = _gla(a32, a16, pk['wup'], pk['bup'], s0, pk['hn_g'], B, S)
    yc = _conv(a16, pk['wdw'], pk['ln_g'], pk['ln_b'], B, S)
    x1, h3, route = _merge(x2d, mod, a16, ym, yd, yg, yc, pk['wb'], pk['wo'], pk['n2'], pk['wr'],
                           rows_per_mod, _pick_tile(rows_per_mod, 512))
    dest, tile_e = _route_plan(route, T)
    tmd = _pick_tile(T, 512)
    moe_rows, n_rows = _moe_rows(T)
    if xg is None:
        xg = jnp.zeros((n_rows, TOK_SUB, LANES), F32)
    xg = _moe_dispatch(h3, dest.reshape(2, T // tmd, 1, tmd), xg, tmd)
    y_grouped = _moe_ffn(xg, tile_e + pk['expert_base'], pk['w1'], pk['w3'], pk['w2'], moe_rows)
    tmc = _pick_tile(rows_per_mod, 512)
    x2 = _moe_combine(y_grouped, dest.reshape(2, T // tmc, 1, tmc), route, x1, mod, fn,
                      rows_per_mod, tmc, final_norm)
    state = None
    if ctx is None:
        state = (a32[:, A32_DK:A32_DK + H_D * 2 * DQK_D].reshape(B, S, H_D, 2 * DQK_D),
                 a32[:, A32_DV:A32_DV + H_D * DV_D].reshape(B, S, H_D, DV_D),
                 c_f, n_f[:, :, :, 0, :], m_f[:, :, :, 0, 0], s_f)
    return x2, state, xg


def kernel(x_prompt, x_sample, c, cache_diff_k, cache_diff_v, state_mlstm_C, state_mlstm_n, state_mlstm_m, state_gla_S, c_ctx, w_mod, b_mod, norm1, w_in, b_m_i, b_m_f, lam_q1, lam_k1, lam_q2, lam_k2, w_gla_up, b_gla_gate, w_dw, conv_ln_g, conv_ln_b, hnorm_m, hnorm_d, hnorm_g, w_branch, w_out, norm2, w_group_router, w_expert_router, w_e1, w_e3, w_e2, final_norm):
    Bp, Sp, _ = x_prompt.shape
    Bs, Ss, _ = x_sample.shape
    P = cache_diff_k.shape[2]
    n_cond = 8 * ((1 + Bs + 7) // 8)
    cond = jnp.concatenate([c_ctx[None, :], c, jnp.zeros((n_cond - 1 - Bs, D_MODEL), F32)], axis=0)
    mod_all = _modulation(cond, w_mod, b_mod).reshape(DEPTH, n_cond, N_MOD, D_MODEL)
    cos_t, sin_t = _rope_tables(Ss)
    ck4 = cache_diff_k.reshape(Bs, DEPTH, P, H_D * 2 * DQK_D)
    cv4 = cache_diff_v.reshape(Bs, DEPTH, P, H_D * DV_D)
    fn = final_norm.reshape(1, D_MODEL)
    yp = x_prompt.reshape(Bp * Sp, D_MODEL)
    ys = x_sample.reshape(Bs * Ss, D_MODEL)
    states = []
    xg_p = xg_s = None
    for l in range(DEPTH):
        p = {'w_in': w_in[l], 'b_m_i': b_m_i[l], 'b_m_f': b_m_f[l], 'lam_q1': lam_q1[l],
             'lam_k1': lam_k1[l], 'lam_q2': lam_q2[l], 'lam_k2': lam_k2[l],
             'w_gla_up': w_gla_up[l], 'b_gla_gate': b_gla_gate[l], 'w_dw': w_dw[l],
             'conv_ln_g': conv_ln_g[l], 'conv_ln_b': conv_ln_b[l], 'hnorm_m': hnorm_m[l],
             'hnorm_d': hnorm_d[l], 'hnorm_g': hnorm_g[l], 'w_branch': w_branch[l],
             'w_out': w_out[l], 'norm1': norm1[l], 'norm2': norm2[l],
             'w_group_router': w_group_router[l], 'w_expert_router': w_expert_router[l]}
        pk = _pack_layer_params(p)
        pk.update(w1=w_e1.reshape(DEPTH * N_EXPERTS, D_MODEL, D_EXPERT),
                  w3=w_e3.reshape(DEPTH * N_EXPERTS, D_MODEL, D_EXPERT),
                  w2=w_e2.reshape(DEPTH * N_EXPERTS, D_EXPERT, D_MODEL), expert_base=l * N_EXPERTS)
        lam_init = 0.8 - 0.6 * math.exp(-0.3 * l)
        last = l == DEPTH - 1
        yp, st, xg_p = _layer(yp, mod_all[l, 0:1], pk, Bp, Sp, lam_init, None, last, fn, xg_p)
        states.append(st)
        ctx = {'k': ck4, 'v': cv4, 'layer': l, 'cos': cos_t, 'sin': sin_t,
               'C': state_mlstm_C[:, l], 'n': state_mlstm_n[:, l], 'm': state_mlstm_m[:, l],
               'S': state_gla_S[:, l]}
        ys, _, xg_s = _layer(ys, mod_all[l, 1:1 + Bs], pk, Bs, Ss, lam_init, ctx, last, fn, xg_s)
    stack = lambda i: jnp.stack([s[i] for s in states], axis=1)
    return (yp.reshape(Bp, Sp, D_MODEL), ys.reshape(Bs, Ss, D_MODEL),
            stack(0), stack(1), stack(2), stack(3), stack(4), stack(5))
```

```python
import functools
import math

import jax
import jax.numpy as jnp
from jax import lax
from jax.experimental import pallas as pl
from jax.experimental.pallas import tpu as pltpu

F32 = jnp.float32
BF16 = jnp.bfloat16

D_MODEL = 1024
DEPTH = 2
GRID_W = 64
BRANCH_W = 512
N_BRANCH = 4
H_M, DK_M, DV_M = 4, 128, 128
H_D, DQK_D, DV_D = 4, 64, 128
H_G, DK_G, DV_G = 4, 64, 128
GATE_RANK = 16
GLA_TAU = 16.0
CONV_W = 31
N_GROUPS, EXPERTS_PER_GROUP, D_EXPERT = 4, 4, 512
N_EXPERTS = N_GROUPS * EXPERTS_PER_GROUP
ROPE_BASE = 10000.0
EPS = 1e-6
N_MOD = 6

LANES = 128
VMEM_LIMIT = 48 * 1024 * 1024

A16_MQ, A16_MK, A16_MV, A16_MO = 0, 512, 1024, 1536
A16_GATE, A16_GR, A16_CA, A16_CB = 2048, 6144, 6656, 7168
A16_GQK, A16_GV, A16_DQ = 7680, 8192, 8704
N_A16 = 9216
A32_DK, A32_DV, A32_SM = 0, 512, 1024
N_A32 = 1152
SM_MI, SM_MF, SM_GA = 0, 8, 16

MLSTM_CHUNK = 128
GLA_CHUNK = 64
GLA_SUB = 16
GLA_EXP_CLAMP = 80.0
CONV_ROWS = 128
CONV_PAD = 16
TOK_SUB = D_MODEL // LANES
ROW_DMA_UNROLL = 8


def _cparams(*sem):
    return pltpu.CompilerParams(dimension_semantics=sem, vmem_limit_bytes=VMEM_LIMIT)


def _log_sigmoid(x):
    return jnp.minimum(x, 0.0) - jnp.log1p(jnp.exp(-jnp.abs(x)))


def _sigmoid(x):
    return 0.5 * jnp.tanh(0.5 * x) + 0.5


def _dot(a, b):
    return jnp.dot(a, b, preferred_element_type=F32)


def _dot_nt(a, b):
    return lax.dot_general(a, b, (((1,), (1,)), ((), ())), preferred_element_type=F32)


def _dot_tn(a, b):
    return lax.dot_general(a, b, (((0,), (0,)), ((), ())), preferred_element_type=F32)


def _mod_kernel(c_ref, w_ref, b_ref, o_ref):
    c = c_ref[...]
    a = (c * _sigmoid(c)).astype(BF16)
    o_ref[...] = _dot(a, w_ref[...].astype(BF16)) + b_ref[...]


def _modulation(cond, w_mod, b_mod):
    R = cond.shape[0]
    tn = 512
    nmod = N_MOD * D_MODEL
    return pl.pallas_call(
        _mod_kernel,
        grid=(DEPTH, nmod // tn),
        in_specs=[
            pl.BlockSpec((R, D_MODEL), lambda l, j: (0, 0)),
            pl.BlockSpec((None, D_MODEL, tn), lambda l, j: (l, 0, j)),
            pl.BlockSpec((None, 1, tn), lambda l, j: (l, 0, j)),
        ],
        out_specs=pl.BlockSpec((None, R, tn), lambda l, j: (l, 0, j)),
        out_shape=jax.ShapeDtypeStruct((DEPTH, R, nmod), F32),
        compiler_params=_cparams("parallel", "parallel"),
        name="adaln_mod",
    )(cond, w_mod, b_mod.reshape(DEPTH, 1, nmod))


def _inproj_kernel(x_ref, mod_ref, g_ref, w_ref, o_ref, h_ref):
    @pl.when(pl.program_id(1) == 0)
    def _():
        x = x_ref[...]
        y = x * lax.rsqrt(jnp.mean(x * x, axis=-1, keepdims=True) + EPS) * g_ref[...]
        h_ref[...] = (y * (1.0 + mod_ref[1:2, :]) + mod_ref[0:1, :]).astype(BF16)

    o_ref[...] = _dot(h_ref[...], w_ref[...]).astype(o_ref.dtype)


def _inproj(x2d, mod, g, w, out_dtype, rows_per_mod, tm, tn):
    T = x2d.shape[0]
    N = w.shape[1]
    return pl.pallas_call(
        _inproj_kernel,
        grid=(T // tm, N // tn),
        in_specs=[
            pl.BlockSpec((tm, D_MODEL), lambda i, j: (i, 0)),
            pl.BlockSpec((None, N_MOD, D_MODEL), lambda i, j: ((i * tm) // rows_per_mod, 0, 0)),
            pl.BlockSpec((1, D_MODEL), lambda i, j: (0, 0)),
            pl.BlockSpec((D_MODEL, tn), lambda i, j: (0, j)),
        ],
        out_specs=pl.BlockSpec((tm, tn), lambda i, j: (i, j)),
        out_shape=jax.ShapeDtypeStruct((T, N), out_dtype),
        scratch_shapes=[pltpu.VMEM((tm, D_MODEL), BF16)],
        compiler_params=_cparams("parallel", "arbitrary"),
        name="norm_inproj",
    )(x2d, mod, g, w)


def _mlstm_local(c, q_ref, k_ref, v_ref, gate_ref, pr_ref, bb_ref, mb_ref, kv_ref, rp_ref, L):
    scale = DK_M ** -0.5
    ti = lax.broadcasted_iota(jnp.int32, (L, L), 0)
    si = lax.broadcasted_iota(jnp.int32, (L, L), 1)
    sub = lax.broadcasted_iota(jnp.int32, (8, LANES), 0)
    rows = pl.ds(pl.multiple_of(c * L, L), L)
    q = q_ref[rows, :]
    v_ext = jnp.concatenate([v_ref[rows, :], jnp.ones((L, LANES), BF16)], axis=1)
    k_t = k_ref[rows, :].astype(F32).T
    qk = _dot(q, k_t.astype(BF16)) * scale
    for d in range(2):
        rev = d == 1
        mask = (si >= ti) if rev else (si <= ti)
        i_row = gate_ref[d, 0, pl.ds(c, 1), :]
        f_row = gate_ref[d, 1, pl.ds(c, 1), :]
        b_row = gate_ref[d, 2, pl.ds(c, 1), :]
        b_col = jnp.sum(jnp.where(mask, f_row, 0.0), axis=1, keepdims=True)
        log_d = jnp.where(mask, b_col + (i_row - b_row), -jnp.inf)
        m_loc = jnp.max(log_d, axis=1, keepdims=True)
        smat = qk * jnp.exp(log_d - m_loc)
        pr_ref[d, rows, :] = _dot(smat.astype(BF16), v_ext)
        bb_ref[d, rows, :] = jnp.broadcast_to(b_col, (L, LANES))
        mb_ref[d, rows, :] = jnp.broadcast_to(m_loc, (L, LANES))
        b_last = jnp.sum(f_row, axis=1, keepdims=True)
        ls_row = b_last - b_row + i_row
        m2 = jnp.max(ls_row, axis=1, keepdims=True)
        kw_t = (k_t * jnp.exp(ls_row - m2)).astype(BF16)
        kv_ref[d, c] = scale * _dot(kw_t, v_ext)
        rp_ref[d, c] = jnp.where(sub == 0, b_last, m2)


def _mlstm_carry(c, d, carry, q_ref, pr_ref, bb_ref, mb_ref, kv_ref, rp_ref, h_ref, L):
    cn, m = carry
    two = lambda x: jnp.concatenate([x, x], axis=1)
    rows = pl.ds(pl.multiple_of(c * L, L), L)
    bb = bb_ref[d, rows, :]
    mb = mb_ref[d, rows, :]
    m_t = jnp.maximum(bb + m, mb)
    a_int = jnp.exp(bb + m - m_t)
    e_loc = jnp.exp(mb - m_t)
    nd = two(a_int) * _dot(q_ref[rows, :], cn.astype(BF16)) + two(e_loc) * pr_ref[d, rows, :]
    h_ref[d, rows, :] = nd[:, :DV_M] / jnp.maximum(jnp.abs(nd[:, DV_M:]), jnp.exp(-m_t))
    rp = rp_ref[d, c]
    b_last, m2 = rp[0:1, :], rp[1:2, :]
    m_new = jnp.maximum(b_last + m, m2)
    a_c = jnp.exp(b_last + m - m_new)
    e2 = jnp.exp(m2 - m_new)
    return two(a_c) * cn + two(e2) * kv_ref[d, c], m_new


def _mlstm_gate_rows(gr_ref, br_ref, gate_ref, L):
    ui = lax.broadcasted_iota(jnp.int32, (L, L), 0)
    si = lax.broadcasted_iota(jnp.int32, (L, L), 1)
    for d in range(2):
        src = ((ui >= si) if d == 1 else (ui <= si)).astype(BF16)
        f = _log_sigmoid(gr_ref[2 + d] + br_ref[2 + d])
        f_hi = f.astype(BF16)
        f_r1 = f - f_hi.astype(F32)
        f_mid = f_r1.astype(BF16)
        f_lo = (f_r1 - f_mid.astype(F32)).astype(BF16)
        gate_ref[d, 0] = gr_ref[d] + br_ref[d]
        gate_ref[d, 1] = f
        gate_ref[d, 2] = _dot(f_hi, src) + _dot(f_mid, src) + _dot(f_lo, src)


def _mlstm_kernel(q_ref, k_ref, v_ref, og_ref, gr_ref, br_ref, c0_ref, n0_ref, m0_ref, hn_ref,
                  y_ref, c_out_ref, n_out_ref, m_out_ref,
                  gate_ref, pr_ref, bb_ref, mb_ref, kv_ref, rp_ref, h_ref, *, L, S):
    nch = S // L
    _mlstm_gate_rows(gr_ref, br_ref, gate_ref, L)

    def local(ci, carry):
        _mlstm_local(ci, q_ref, k_ref, v_ref, gate_ref, pr_ref, bb_ref, mb_ref, kv_ref, rp_ref, L)
        return carry

    lax.fori_loop(0, nch, local, 0, unroll=4)
    step = functools.partial(_mlstm_carry, q_ref=q_ref, pr_ref=pr_ref, bb_ref=bb_ref, mb_ref=mb_ref,
                             kv_ref=kv_ref, rp_ref=rp_ref, h_ref=h_ref, L=L)

    def body(ci, carry):
        return step(ci, 0, carry[0]), step(nch - 1 - ci, 1, carry[1])

    def init(d):
        n_rep = jnp.broadcast_to(n0_ref[d], (DK_M, DK_M)).T
        return jnp.concatenate([c0_ref[d], n_rep], axis=1), m0_ref[d]

    fin = lax.fori_loop(0, nch, body, (init(0), init(1)), unroll=2)
    for d in range(2):
        cn, m = fin[d]
        c_out_ref[d] = cn[:, :DV_M]
        n_out_ref[d] = cn[:, DV_M:].T[0:1, :]
        m_out_ref[d] = m

    hm = h_ref[0] + h_ref[1]
    y = hm * lax.rsqrt(jnp.mean(hm * hm, axis=-1, keepdims=True) + EPS) * hn_ref[...]
    y_ref[...] = (y * _sigmoid(og_ref[...].astype(F32))).astype(y_ref.dtype)


def _mlstm(a16, grow, brow, c0, n0, m0, hnorm, B, S):
    L = min(MLSTM_CHUNK, S)
    nch = S // L
    cb = lambda off: off // LANES
    kern = functools.partial(_mlstm_kernel, L=L, S=S)
    return pl.pallas_call(
        kern,
        grid=(B, H_M),
        in_specs=[
            pl.BlockSpec((S, LANES), lambda b, h: (b, cb(A16_MQ) + h)),
            pl.BlockSpec((S, LANES), lambda b, h: (b, cb(A16_MK) + h)),
            pl.BlockSpec((S, LANES), lambda b, h: (b, cb(A16_MV) + h)),
            pl.BlockSpec((S, LANES), lambda b, h: (b, cb(A16_MO) + h)),
            pl.BlockSpec((None, 4, None, nch, L), lambda b, h: (h, 0, b, 0, 0)),
            pl.BlockSpec((None, 4, 1, 1), lambda b, h: (h, 0, 0, 0)),
            pl.BlockSpec((None, 2, None, DK_M, DV_M), lambda b, h: (b, 0, h, 0, 0)),
            pl.BlockSpec((None, 2, None, 1, DK_M), lambda b, h: (b, 0, h, 0, 0)),
            pl.BlockSpec((None, 2, None, 1, LANES), lambda b, h: (b, 0, h, 0, 0)),
            pl.BlockSpec((1, LANES), lambda b, h: (0, h)),
        ],
        out_specs=[
            pl.BlockSpec((S, LANES), lambda b, h: (b, h)),
            pl.BlockSpec((None, 2, None, DK_M, DV_M), lambda b, h: (b, 0, h, 0, 0)),
            pl.BlockSpec((None, 2, None, 1, DK_M), lambda b, h: (b, 0, h, 0, 0)),
            pl.BlockSpec((None, 2, None, 1, LANES), lambda b, h: (b, 0, h, 0, 0)),
        ],
        out_shape=[
            jax.ShapeDtypeStruct((B * S, BRANCH_W), BF16),
            jax.ShapeDtypeStruct((B, 2, H_M, DK_M, DV_M), F32),
            jax.ShapeDtypeStruct((B, 2, H_M, 1, DK_M), F32),
            jax.ShapeDtypeStruct((B, 2, H_M, 1, LANES), F32),
        ],
        scratch_shapes=[pltpu.VMEM((2, 3, nch, L), F32),
                        pltpu.VMEM((2, S, 2 * DV_M), F32), pltpu.VMEM((2, S, LANES), F32),
                        pltpu.VMEM((2, S, LANES), F32), pltpu.VMEM((2, nch, DK_M, 2 * DV_M), F32),
                        pltpu.VMEM((2, nch, 8, LANES), F32), pltpu.VMEM((2, S, DV_M), F32)],
        compiler_params=_cparams("parallel", "parallel"),
        name="mlstm",
    )(a16, a16, a16, a16, grow, brow, c0, n0, m0, hnorm)


def _gla_local(c, q2_ref, k2_ref, v_ref, la_ref, oa_ref, qt_ref, u_ref, dec_ref, L):
    nb = L // GLA_SUB
    ti = lax.broadcasted_iota(jnp.int32, (L, L), 0)
    si = lax.broadcasted_iota(jnp.int32, (L, L), 1)
    row_blk = lax.broadcasted_iota(jnp.int32, (L, LANES), 0) // GLA_SUB
    lo_half = lax.broadcasted_iota(jnp.int32, (L, LANES), 1) < DK_G
    eye = (lax.broadcasted_iota(jnp.int32, (DK_G, LANES), 0)
           == lax.broadcasted_iota(jnp.int32, (DK_G, LANES), 1))
    rows = pl.ds(pl.multiple_of(c * L, L), L)
    q2 = q2_ref[rows, :]
    k2 = k2_ref[rows, :]
    v = v_ref[rows, :]
    row = lax.broadcasted_iota(jnp.int32, (L, LANES), 0)
    for d in range(2):
        rev = d == 1
        mask = (si >= ti) if rev else (si <= ti)
        g2 = la_ref[d, rows, :]
        step = 1
        while step < L:
            if rev:
                g2 = g2 + jnp.where(row < L - step, pltpu.roll(g2, L - step, 0), 0.0)
            else:
                g2 = g2 + jnp.where(row >= step, pltpu.roll(g2, step, 0), 0.0)
            step *= 2
        qt_ref[d, rows, :] = (q2 * jnp.exp(g2))[:, :DK_G].astype(BF16)
        a_parts, b_parts = [], []
        for p in range(nb // 2):
            ia, ib = 2 * p, 2 * p + 1
            ra = ia * GLA_SUB + (GLA_SUB - 1 if rev else 0)
            rb = ib * GLA_SUB + (GLA_SUB - 1 if rev else 0)
            ref2 = jnp.where(lo_half, g2[ra:ra + 1, :], g2[rb:rb + 1, :])
            blk = jnp.where(lo_half, ia, ib)
            in_blk = row_blk == blk
            key_ok = (row_blk >= blk) if rev else (row_blk <= blk)
            a_parts.append(jnp.where(in_blk, q2 * jnp.exp(jnp.minimum(g2 - ref2, 0.0)), 0.0))
            b_parts.append(
                jnp.where(key_ok, k2 * jnp.exp(jnp.minimum(ref2 - g2, GLA_EXP_CLAMP)), 0.0))
        a_big = jnp.concatenate(a_parts, axis=1).astype(BF16)
        b_big = jnp.concatenate(b_parts, axis=1).astype(BF16)
        att = jnp.where(mask, _dot_nt(a_big, b_big), 0.0)
        oa_ref[d, rows, :] = _dot(att.astype(BF16), v)
        gl_row = 0 if rev else L - 1
        glast = g2[gl_row:gl_row + 1, :]
        kd = (k2 * jnp.exp(glast - g2))[:, :DK_G]
        u_ref[d, c] = _dot_tn(kd.astype(BF16), v)
        glast_col = jnp.sum(jnp.where(eye, glast, 0.0), axis=1, keepdims=True)
        dec_ref[d, c] = jnp.broadcast_to(jnp.exp(glast_col), (DK_G, DV_G))


def _gla_kernel(qk_ref, v_ref, sm_ref, wup_ref, bup_ref, s0_ref, gr_ref, hn_ref,
                y_ref, s_out_ref, la_ref, q2_ref, k2_ref, oa_ref, oi_ref, qt_ref, u_ref, dec_ref,
                *, L, S):
    nch = S // L
    sm = sm_ref[...].astype(BF16)
    for d in range(2):
        la_ref[d] = _log_sigmoid(_dot(sm, wup_ref[d]) + bup_ref[d]) * (1.0 / GLA_TAU)
    qk = qk_ref[...].astype(F32)
    qk_sw = pltpu.roll(qk, DK_G, 1)
    lo_half = lax.broadcasted_iota(jnp.int32, qk.shape, 1) < DK_G
    q2_ref[...] = jnp.where(lo_half, qk, qk_sw) * (DK_G ** -0.5)
    k2_ref[...] = jnp.where(lo_half, qk_sw, qk)

    def local(ci, carry):
        _gla_local(ci, q2_ref, k2_ref, v_ref, la_ref, oa_ref, qt_ref, u_ref, dec_ref, L)
        return carry

    lax.fori_loop(0, nch, local, 0, unroll=8)

    def body(ci, carry):
        out = []
        for d, c in ((0, ci), (1, nch - 1 - ci)):
            rows = pl.ds(pl.multiple_of(c * L, L), L)
            st = carry[d]
            oi_ref[d, rows, :] = _dot(qt_ref[d, rows, :], st.astype(BF16))
            out.append(dec_ref[d, c] * st + u_ref[d, c])
        return tuple(out)

    st_f, st_b = lax.fori_loop(0, nch, body, (s0_ref[0], s0_ref[1]), unroll=4)
    s_out_ref[0] = st_f
    s_out_ref[1] = st_b

    og = (oa_ref[0] + oi_ref[0]) + (oa_ref[1] + oi_ref[1])
    y = og * lax.rsqrt(jnp.mean(og * og, axis=-1, keepdims=True) + EPS) * hn_ref[...]
    gr = gr_ref[...].astype(F32)
    y_ref[...] = (y * (gr * _sigmoid(gr))).astype(y_ref.dtype)


def _gla(a32, a16, wup, bup, s0, hnorm, B, S):
    L = min(GLA_CHUNK, S)
    nch = S // L
    cb = lambda off: off // LANES
    kern = functools.partial(_gla_kernel, L=L, S=S)
    return pl.pallas_call(
        kern,
        grid=(B, H_G),
        in_specs=[
            pl.BlockSpec((S, LANES), lambda b, h: (b, cb(A16_GQK) + h)),
            pl.BlockSpec((S, LANES), lambda b, h: (b, cb(A16_GV) + h)),
            pl.BlockSpec((S, LANES), lambda b, h: (b, cb(A32_SM))),
            pl.BlockSpec((None, 2, LANES, LANES), lambda b, h: (h, 0, 0, 0)),
            pl.BlockSpec((None, 2, 1, LANES), lambda b, h: (h, 0, 0, 0)),
            pl.BlockSpec((None, 2, None, DK_G, DV_G), lambda b, h: (b, 0, h, 0, 0)),
            pl.BlockSpec((S, LANES), lambda b, h: (b, cb(A16_GR) + h)),
            pl.BlockSpec((1, LANES), lambda b, h: (0, h)),
        ],
        out_specs=[
            pl.BlockSpec((S, LANES), lambda b, h: (b, h)),
            pl.BlockSpec((None, 2, None, DK_G, DV_G), lambda b, h: (b, 0, h, 0, 0)),
        ],
        out_shape=[
            jax.ShapeDtypeStruct((B * S, BRANCH_W), BF16),
            jax.ShapeDtypeStruct((B, 2, H_G, DK_G, DV_G), F32),
        ],
        scratch_shapes=[pltpu.VMEM((2, S, LANES), F32), pltpu.VMEM((S, LANES), F32),
                        pltpu.VMEM((S, LANES), F32), pltpu.VMEM((2, S, DV_G), F32),
                        pltpu.VMEM((2, S, DV_G), F32), pltpu.VMEM((2, S, DK_G), BF16),
                        pltpu.VMEM((2, nch, DK_G, DV_G), F32), pltpu.VMEM((2, nch, DK_G, DV_G), F32)],
        compiler_params=_cparams("parallel", "parallel"),
        name="gla",
    )(a16, a16, a32, wup, bup, s0, a16, hnorm)


def _rope(x, cos, sin_signed):
    lane = lax.broadcasted_iota(jnp.int32, x.shape, 1)
    first = (lane % DQK_D) < (DQK_D // 2)
    partner = jnp.where(first, pltpu.roll(x, LANES - DQK_D // 2, 1), pltpu.roll(x, DQK_D // 2, 1))
    return x * cos + partner * sin_signed


def _attn_kernel(*refs, S, P, TQ, lam_init, has_ctx):
    if has_ctx:
        (q_ref, k_ref, v_ref, ck_ref, cv_ref, cos_ref, sin_ref, lam_ref, hn_ref,
         y_ref, kk_ref, vv_ref) = refs
    else:
        q_ref, k_ref, v_ref, lam_ref, hn_ref, y_ref, kk_ref, vv_ref = refs
    qi = pl.program_id(2)

    @pl.when(qi == 0)
    def _():
        k = k_ref[...]
        if has_ctx:
            k = _rope(k, cos_ref[...], sin_ref[...])
            kk_ref[S:S + P, :] = ck_ref[...].astype(BF16)
            vv_ref[S:S + P, :] = cv_ref[...].astype(BF16)
        kk_ref[0:S, :] = k.astype(BF16)
        vv_ref[0:S, :] = v_ref[...].astype(BF16)

    kk = kk_ref[...]
    vv = vv_ref[...]
    lv = lam_ref[...]
    lam = (jnp.exp(jnp.sum(lv[0:1, :] * lv[1:2, :], axis=-1, keepdims=True))
           - jnp.exp(jnp.sum(lv[2:3, :] * lv[3:4, :], axis=-1, keepdims=True)) + lam_init)
    q = q_ref[...].astype(F32)
    if has_ctx:
        r0 = pl.multiple_of(qi * TQ, TQ)
        q = _rope(q, cos_ref[pl.ds(r0, TQ), :], sin_ref[pl.ds(r0, TQ), :])
    q = q * (DQK_D ** -0.5 * math.log2(math.e))
    lane = lax.broadcasted_iota(jnp.int32, q.shape, 1)
    es, ls = [], []
    for comp in range(2):
        sel = (lane < DQK_D) if comp == 0 else (lane >= DQK_D)
        s = _dot_nt(jnp.where(sel, q, 0.0).astype(BF16), kk)
        e = jnp.exp2(s - jnp.max(s, axis=-1, keepdims=True))
        es.append(e)
        ls.append(jnp.sum(e, axis=-1, keepdims=True))
    w = es[0] - es[1] * (lam * ls[0] / ls[1])
    o = _dot(w.astype(BF16), vv) * (1.0 / ls[0])
    y = o * lax.rsqrt(jnp.mean(o * o, axis=-1, keepdims=True) + EPS) * hn_ref[...]
    y_ref[...] = (y * (1.0 - lam_init)).astype(y_ref.dtype)


def _attn(a16, a32, lamv, hnorm, B, S, lam_init, ctx=None):
    TQ = min(256, S)
    nq = S // TQ
    has_ctx = ctx is not None
    P = ctx[0].shape[2] if has_ctx else 0
    cb = lambda off: off // LANES
    kern = functools.partial(_attn_kernel, S=S, P=P, TQ=TQ, lam_init=lam_init, has_ctx=has_ctx)
    in_specs = [
        pl.BlockSpec((TQ, LANES), lambda b, h, i: (b * nq + i, cb(A16_DQ) + h)),
        pl.BlockSpec((S, LANES), lambda b, h, i: (b, cb(A32_DK) + h)),
        pl.BlockSpec((S, LANES), lambda b, h, i: (b, cb(A32_DV) + h)),
    ]
    args = [a16, a32, a32]
    if has_ctx:
        ck, cv, layer, cos, sin = ctx
        in_specs += [
            pl.BlockSpec((None, None, P, LANES), lambda b, h, i: (b, layer, 0, h)),
            pl.BlockSpec((None, None, P, LANES), lambda b, h, i: (b, layer, 0, h)),
            pl.BlockSpec((S, LANES), lambda b, h, i: (0, 0)),
            pl.BlockSpec((S, LANES), lambda b, h, i: (0, 0)),
        ]
        args += [ck, cv, cos, sin]
    in_specs += [
        pl.BlockSpec((4, DQK_D), lambda b, h, i: (0, 0)),
        pl.BlockSpec((1, LANES), lambda b, h, i: (0, h)),
    ]
    args += [lamv, hnorm]
    return pl.pallas_call(
        kern,
        grid=(B, H_D, nq),
        in_specs=in_specs,
        out_specs=pl.BlockSpec((TQ, LANES), lambda b, h, i: (b * nq + i, h)),
        out_shape=jax.ShapeDtypeStruct((B * S, BRANCH_W), BF16),
        scratch_shapes=[pltpu.VMEM((S + P, LANES), BF16), pltpu.VMEM((S + P, LANES), BF16)],
        compiler_params=_cparams("parallel", "parallel", "arbitrary"),
        name="diff_attn",
    )(*args)


def _conv_kernel(ca_ref, cb_ref, w_ref, g_ref, b_ref, y_ref, pad_ref, cv_ref, *, S):
    ca = ca_ref[...].astype(F32)
    cbv = cb_ref[...].astype(F32)
    zeros = jnp.zeros((CONV_PAD, BRANCH_W), F32)
    pad_ref[0:CONV_PAD, :] = zeros
    pad_ref[CONV_PAD + S:2 * CONV_PAD + S, :] = zeros
    pad_ref[CONV_PAD:CONV_PAD + S, :] = ca * _sigmoid(cbv)
    off = CONV_PAD - CONV_W // 2

    win_rows = CONV_ROWS + 2 * CONV_PAD

    def body(i, carry):
        base = pl.multiple_of(i * CONV_ROWS, CONV_ROWS)
        for lb in range(BRANCH_W // LANES):
            cols = slice(lb * LANES, (lb + 1) * LANES)
            win = pad_ref[pl.ds(base, win_rows), cols]
            acc = jnp.zeros((CONV_ROWS, LANES), F32)
            for r in range(8):
                rolled = win if r == 0 else pltpu.roll(win, win_rows - r, 0)
                for a in range(2 * CONV_PAD // 8):
                    j = 8 * a + r - off
                    if 0 <= j < CONV_W:
                        acc = acc + rolled[8 * a:8 * a + CONV_ROWS, :] * w_ref[j:j + 1, cols]
            cv_ref[:, cols] = acc
        acc = cv_ref[...]
        mu = jnp.mean(acc, axis=-1, keepdims=True)
        xc = acc - mu
        yn = xc * lax.rsqrt(jnp.mean(xc * xc, axis=-1, keepdims=True) + EPS) * g_ref[...] + b_ref[...]
        y_ref[pl.ds(base, CONV_ROWS), :] = (yn * _sigmoid(yn)).astype(y_ref.dtype)
        return carry

    lax.fori_loop(0, S // CONV_ROWS, body, 0)


def _conv(a16, w_dw, ln_g, ln_b, B, S):
    cb = lambda off: off // BRANCH_W
    kern = functools.partial(_conv_kernel, S=S)
    return pl.pallas_call(
        kern,
        grid=(B,),
        in_specs=[
            pl.BlockSpec((S, BRANCH_W), lambda b: (b, cb(A16_CA))),
            pl.BlockSpec((S, BRANCH_W), lambda b: (b, cb(A16_CB))),
            pl.BlockSpec((CONV_W + 1, BRANCH_W), lambda b: (0, 0)),
            pl.BlockSpec((1, BRANCH_W), lambda b: (0, 0)),
            pl.BlockSpec((1, BRANCH_W), lambda b: (0, 0)),
        ],
        out_specs=pl.BlockSpec((S, BRANCH_W), lambda b: (b, 0)),
        out_shape=jax.ShapeDtypeStruct((B * S, BRANCH_W), BF16),
        scratch_shapes=[pltpu.VMEM((S + 2 * CONV_PAD, BRANCH_W), F32),
                        pltpu.VMEM((CONV_ROWS, BRANCH_W), F32)],
        compiler_params=_cparams("parallel"),
        name="glu_conv_ln",
    )(a16, a16, w_dw, ln_g, ln_b)


def _merge_kernel(x_ref, mod_ref, ym_ref, yd_ref, yg_ref, yc_ref, g0_ref, g1_ref, g2_ref, g3_ref,
                  wb_ref, wo_ref, n2_ref, wr_ref, x1_ref, h2_ref, route_ref):
    ys = (ym_ref, yd_ref, yg_ref, yc_ref)
    gs = (g0_ref, g1_ref, g2_ref, g3_ref)
    merged = None
    for nbr in range(N_BRANCH):
        br = _dot(ys[nbr][...], wb_ref[nbr])
        term = _sigmoid(gs[nbr][...].astype(F32)) * br
        merged = term if merged is None else merged + term
    out = _dot(merged.astype(BF16), wo_ref[...])
    x1 = x_ref[...] + mod_ref[2:3, :] * out
    x1_ref[...] = x1
    y = x1 * lax.rsqrt(jnp.mean(x1 * x1, axis=-1, keepdims=True) + EPS) * n2_ref[...]
    h2 = y * (1.0 + mod_ref[4:5, :]) + mod_ref[3:4, :]
    h2_ref[...] = h2.reshape(h2_ref.shape)
    wr = wr_ref[...]
    h_hi, w_hi = h2.astype(BF16), wr.astype(BF16)
    h_lo = (h2 - h_hi.astype(F32)).astype(BF16)
    w_lo = (wr - w_hi.astype(F32)).astype(BF16)
    logits = _dot(h_hi, w_hi) + (_dot(h_hi, w_lo) + _dot(h_lo, w_hi))
    lane = lax.broadcasted_iota(jnp.int32, logits.shape, 1)
    neg = -jnp.inf
    big = jnp.int32(LANES)
    is_g = lane < N_GROUPS
    gl = jnp.where(is_g, logits, neg)
    gmax = jnp.max(gl, axis=-1, keepdims=True)
    gidx = jnp.min(jnp.where(is_g & (gl == gmax), lane, big), axis=-1, keepdims=True)
    g_p = 1.0 / jnp.sum(jnp.where(is_g, jnp.exp(gl - gmax), 0.0), axis=-1, keepdims=True)
    e_lane = lane - N_GROUPS
    in_grp = (e_lane >= 0) & (e_lane < N_EXPERTS) & ((e_lane // EXPERTS_PER_GROUP) == gidx)
    el = jnp.where(in_grp, logits, neg)
    v1 = jnp.max(el, axis=-1, keepdims=True)
    i1 = jnp.min(jnp.where(in_grp & (el == v1), lane, big), axis=-1, keepdims=True)
    el2 = jnp.where(lane == i1, neg, el)
    v2 = jnp.max(el2, axis=-1, keepdims=True)
    i2 = jnp.min(jnp.where(in_grp & (lane != i1) & (el2 == v2), lane, big), axis=-1, keepdims=True)
    e2 = jnp.exp(v2 - v1)
    w1 = g_p / (1.0 + e2)
    w2 = g_p * e2 / (1.0 + e2)
    id1 = (i1 - N_GROUPS).astype(F32)
    id2 = (i2 - N_GROUPS).astype(F32)
    route_ref[...] = jnp.where(lane == 0, id1, jnp.where(lane == 1, id2,
                               jnp.where(lane == 2, w1, jnp.where(lane == 3, w2, 0.0))))


def _merge(x2d, mod, a16, ym, yd, yg, yc, wb, wo, n2, wr, rows_per_mod, tm):
    T = x2d.shape[0]
    gcb = A16_GATE // D_MODEL
    row = lambda i: (i, 0)
    return pl.pallas_call(
        _merge_kernel,
        grid=(T // tm,),
        in_specs=[
            pl.BlockSpec((tm, D_MODEL), row),
            pl.BlockSpec((None, N_MOD, D_MODEL), lambda i: ((i * tm) // rows_per_mod, 0, 0)),
            pl.BlockSpec((tm, BRANCH_W), row),
            pl.BlockSpec((tm, BRANCH_W), row),
            pl.BlockSpec((tm, BRANCH_W), row),
            pl.BlockSpec((tm, BRANCH_W), row),
            pl.BlockSpec((tm, D_MODEL), lambda i: (i, gcb + 0)),
            pl.BlockSpec((tm, D_MODEL), lambda i: (i, gcb + 1)),
            pl.BlockSpec((tm, D_MODEL), lambda i: (i, gcb + 2)),
            pl.BlockSpec((tm, D_MODEL), lambda i: (i, gcb + 3)),
            pl.BlockSpec((N_BRANCH, BRANCH_W, D_MODEL), lambda i: (0, 0, 0)),
            pl.BlockSpec((D_MODEL, D_MODEL), lambda i: (0, 0)),
            pl.BlockSpec((1, D_MODEL), lambda i: (0, 0)),
            pl.BlockSpec((D_MODEL, LANES), lambda i: (0, 0)),
        ],
        out_specs=[
            pl.BlockSpec((tm, D_MODEL), row),
            pl.BlockSpec((tm, TOK_SUB, LANES), lambda i: (i, 0, 0)),
            pl.BlockSpec((tm, LANES), row),
        ],
        out_shape=[
            jax.ShapeDtypeStruct((T, D_MODEL), F32),
            jax.ShapeDtypeStruct((T, TOK_SUB, LANES), F32),
            jax.ShapeDtypeStruct((T, LANES), F32),
        ],
        compiler_params=_cparams("parallel"),
        name="merge_outproj_route",
    )(x2d, mod, ym, yd, yg, yc, a16, a16, a16, a16, wb, wo, n2, wr)


def _gather_rows(idx_ref, src_hbm, dst, sem, n):
    def body(j, carry):
        for u in range(ROW_DMA_UNROLL):
            r = j * ROW_DMA_UNROLL + u
            pltpu.make_async_copy(src_hbm.at[idx_ref[0, r]], dst.at[r], sem).start(priority=u % 2)
        return carry

    lax.fori_loop(0, n // ROW_DMA_UNROLL, body, 0)


def _scatter_rows(idx_ref, src, dst_hbm, sem, n):
    def body(j, carry):
        for u in range(ROW_DMA_UNROLL):
            r = j * ROW_DMA_UNROLL + u
            pltpu.make_async_copy(src.at[r], dst_hbm.at[idx_ref[0, r]], sem).start(priority=u % 2)
        return carry

    lax.fori_loop(0, n // ROW_DMA_UNROLL, body, 0)


def _wait_rows(buf, sem):
    pltpu.make_async_copy(buf, buf, sem).wait()


def _moe_dispatch_kernel(d0_ref, d1_ref, h_ref, xg_in, xg_out, sem, *, tm):
    del xg_in
    _scatter_rows(d0_ref, h_ref, xg_out, sem.at[0], tm)
    _scatter_rows(d1_ref, h_ref, xg_out, sem.at[1], tm)
    _wait_rows(h_ref, sem.at[0])
    _wait_rows(h_ref, sem.at[1])


def _moe_dispatch(h3, dest, xg_init, tm):
    T = h3.shape[0]
    kern = functools.partial(_moe_dispatch_kernel, tm=tm)
    return pl.pallas_call(
        kern,
        grid=(T // tm,),
        in_specs=[
            pl.BlockSpec((None, None, 1, tm), lambda i: (0, i, 0, 0), memory_space=pltpu.SMEM),
            pl.BlockSpec((None, None, 1, tm), lambda i: (1, i, 0, 0), memory_space=pltpu.SMEM),
            pl.BlockSpec((tm, TOK_SUB, LANES), lambda i: (i, 0, 0)),
            pl.BlockSpec(memory_space=pl.ANY),
        ],
        out_specs=pl.BlockSpec(memory_space=pl.ANY),
        out_shape=jax.ShapeDtypeStruct(xg_init.shape, F32),
        input_output_aliases={3: 0},
        scratch_shapes=[pltpu.SemaphoreType.DMA((2,))],
        compiler_params=_cparams("arbitrary"),
        name="moe_dispatch",
    )(dest, dest, h3, xg_init)


def _moe_ffn_kernel(te_ref, x_ref, w1_ref, w3_ref, w2_ref, o_ref):
    del te_ref
    x = x_ref[...].reshape(x_ref.shape[0], D_MODEL).astype(BF16)
    a = _dot(x, w1_ref[...].astype(BF16))
    b = _dot(x, w3_ref[...].astype(BF16))
    s = (a * _sigmoid(a)) * b
    y = _dot(s.astype(BF16), w2_ref[...].astype(BF16))
    o_ref[...] = y.reshape(o_ref.shape)


def _moe_ffn(xg, tile_e, w1, w3, w2, rows):
    ntiles = xg.shape[0] // rows
    tile = pl.BlockSpec((rows, TOK_SUB, LANES), lambda i, te: (i, 0, 0))
    return pl.pallas_call(
        _moe_ffn_kernel,
        grid_spec=pltpu.PrefetchScalarGridSpec(
            num_scalar_prefetch=1,
            grid=(ntiles,),
            in_specs=[
                tile,
                pl.BlockSpec((None, D_MODEL, D_EXPERT), lambda i, te: (te[i], 0, 0)),
                pl.BlockSpec((None, D_MODEL, D_EXPERT), lambda i, te: (te[i], 0, 0)),
                pl.BlockSpec((None, D_EXPERT, D_MODEL), lambda i, te: (te[i], 0, 0)),
            ],
            out_specs=tile,
        ),
        out_shape=jax.ShapeDtypeStruct(xg.shape, F32),
        compiler_params=_cparams("parallel"),
        name="moe_grouped_experts",
    )(tile_e, xg, w1, w3, w2)


def _moe_combine_kernel(d0_ref, d1_ref, y_hbm, route_ref, x1_ref, mod_ref, fn_ref, o_ref,
                        ga, gb, sem, *, tm, final_norm):
    _gather_rows(d0_ref, y_hbm, ga, sem.at[0], tm)
    _gather_rows(d1_ref, y_hbm, gb, sem.at[1], tm)
    rt = route_ref[...]
    _wait_rows(ga, sem.at[0])
    _wait_rows(gb, sem.at[1])
    y = rt[:, 2:3] * ga[...].reshape(tm, D_MODEL) + rt[:, 3:4] * gb[...].reshape(tm, D_MODEL)
    x2 = x1_ref[...] + mod_ref[5:6, :] * y
    if final_norm:
        x2 = x2 * lax.rsqrt(jnp.mean(x2 * x2, axis=-1, keepdims=True) + EPS) * fn_ref[...]
    o_ref[...] = x2


def _moe_combine(yg, dest, route, x1, mod, fn, rows_per_mod, tm, final_norm):
    T = x1.shape[0]
    kern = functools.partial(_moe_combine_kernel, tm=tm, final_norm=final_norm)
    return pl.pallas_call(
        kern,
        grid=(T // tm,),
        in_specs=[
            pl.BlockSpec((None, None, 1, tm), lambda i: (0, i, 0, 0), memory_space=pltpu.SMEM),
            pl.BlockSpec((None, None, 1, tm), lambda i: (1, i, 0, 0), memory_space=pltpu.SMEM),
            pl.BlockSpec(memory_space=pl.ANY),
            pl.BlockSpec((tm, LANES), lambda i: (i, 0)),
            pl.BlockSpec((tm, D_MODEL), lambda i: (i, 0)),
            pl.BlockSpec((None, N_MOD, D_MODEL), lambda i: ((i * tm) // rows_per_mod, 0, 0)),
            pl.BlockSpec((1, D_MODEL), lambda i: (0, 0)),
        ],
        out_specs=pl.BlockSpec((tm, D_MODEL), lambda i: (i, 0)),
        out_shape=jax.ShapeDtypeStruct((T, D_MODEL), F32),
        scratch_shapes=[pltpu.VMEM((tm, TOK_SUB, LANES), F32), pltpu.VMEM((tm, TOK_SUB, LANES), F32),
                        pltpu.SemaphoreType.DMA((2,))],
        compiler_params=_cparams("arbitrary"),
        name="moe_combine",
    )(dest, dest, yg, route, x1, mod, fn)


def _moe_rows(T):
    rows = 512 if T >= 4096 else 256
    ntiles = (2 * T + N_EXPERTS * (rows - 1) + rows - 1) // rows
    return rows, ntiles * rows


def _route_plan(route, T):
    tile, n_rows = _moe_rows(T)
    ef = route[:, 0:2].astype(jnp.int32).reshape(-1)
    oh = (ef[:, None] == jnp.arange(N_EXPERTS, dtype=jnp.int32)[None, :]).astype(jnp.int32)
    csum = jnp.cumsum(oh, axis=0)
    rank = jnp.sum((csum - oh) * oh, axis=1)
    counts = csum[-1]
    padded = ((counts + tile - 1) // tile) * tile
    seg_end = jnp.cumsum(padded)
    dest = jnp.sum(oh * (seg_end - padded)[None, :], axis=1) + rank
    tile_row = jnp.arange(n_rows // tile, dtype=jnp.int32) * tile
    tile_e = jnp.minimum(jnp.sum((tile_row[:, None] >= seg_end[None, :]).astype(jnp.int32), axis=1),
                         N_EXPERTS - 1)
    return jnp.transpose(dest.reshape(T, 2)), tile_e


def _split_w_in(w):
    sizes = (H_M * DK_M, H_M * DK_M, H_M * DV_M, H_M * DV_M, 2 * H_M, 2 * H_M,
             H_D * 2 * DQK_D, H_D * 2 * DQK_D, H_D * DV_D,
             H_G * DK_G, H_G * DK_G, H_G * DV_G, 2 * GATE_RANK, H_G * DV_G,
             BRANCH_W, BRANCH_W, N_BRANCH * D_MODEL)
    outs, acc = [], 0
    for s in sizes:
        outs.append(w[:, acc:acc + s])
        acc += s
    return outs


def _pack_layer_params(p):
    (m_q, m_k, m_v, m_o, m_i, m_f, d_q, d_k, d_v, g_q, g_k, g_v, g_a, g_r, c_a, c_b, gate) = \
        _split_w_in(p['w_in'])
    gqk = jnp.concatenate([g_q.reshape(D_MODEL, H_G, DK_G), g_k.reshape(D_MODEL, H_G, DK_G)],
                          axis=2).reshape(D_MODEL, 2 * H_G * DK_G)
    small = jnp.concatenate(
        [m_i, m_f, g_a, jnp.zeros((D_MODEL, LANES - 4 * H_M - 2 * GATE_RANK), F32)], axis=1)
    w16 = jnp.concatenate([m_q, m_k, m_v, m_o, gate, g_r, c_a, c_b, gqk, g_v, d_q],
                          axis=1).astype(BF16)
    w32 = jnp.concatenate([d_k, d_v, small], axis=1).astype(BF16)
    bi = p['b_m_i'].reshape(2, H_M)
    bf = p['b_m_f'].reshape(2, H_M)
    bcol = jnp.stack([bi[0], bi[1], bf[0], bf[1]], axis=-1)
    wup = p['w_gla_up'].reshape(2, GATE_RANK, H_G, DK_G)
    wup_pad = jnp.zeros((H_G, 2, LANES, LANES), F32)
    bup = p['b_gla_gate'].reshape(2, H_G, DK_G)
    for d in range(2):
        blk = jnp.transpose(wup[d], (1, 0, 2))
        blk = jnp.concatenate([blk, blk], axis=-1)
        r0 = SM_GA + d * GATE_RANK
        wup_pad = wup_pad.at[:, d, r0:r0 + GATE_RANK, :].set(blk)
    bup2 = jnp.transpose(jnp.concatenate([bup, bup], axis=-1), (1, 0, 2))[:, :, None, :]
    wr = jnp.concatenate([p['w_group_router'], p['w_expert_router'],
                          jnp.zeros((D_MODEL, LANES - N_GROUPS - N_EXPERTS), F32)], axis=1)
    return dict(
        w16=w16, w32=w32, brow=bcol.reshape(H_M, 4, 1, 1),
        wup=wup_pad.astype(BF16), bup=bup2,
        wdw=jnp.concatenate([p['w_dw'], jnp.zeros((1, BRANCH_W), F32)], axis=0),
        ln_g=p['conv_ln_g'].reshape(1, BRANCH_W), ln_b=p['conv_ln_b'].reshape(1, BRANCH_W),
        hn_m=p['hnorm_m'].reshape(1, BRANCH_W), hn_d=p['hnorm_d'].reshape(1, BRANCH_W),
        hn_g=p['hnorm_g'].reshape(1, BRANCH_W),
        lamv=jnp.stack([p['lam_q1'], p['lam_k1'], p['lam_q2'], p['lam_k2']], axis=0),
        wb=p['w_branch'].astype(BF16), wo=p['w_out'].astype(BF16),
        n1=p['norm1'].reshape(1, D_MODEL), n2=p['norm2'].reshape(1, D_MODEL), wr=wr,
    )


def _rope_tables(S):
    rows = S // GRID_W
    r, col = jnp.meshgrid(jnp.arange(rows, dtype=F32), jnp.arange(GRID_W, dtype=F32), indexing='ij')
    r, col = r.reshape(-1), col.reshape(-1)
    n_freq = DQK_D // 4
    inv = ROPE_BASE ** (-jnp.arange(n_freq, dtype=F32) / n_freq)
    ang = jnp.concatenate([r[:, None] * inv, col[:, None] * inv], axis=-1)
    cos, sin = jnp.cos(ang), jnp.sin(ang)
    cos_t = jnp.tile(cos, (1, LANES // (DQK_D // 2)))
    sin_t = jnp.tile(jnp.concatenate([-sin, sin], axis=-1), (1, LANES // DQK_D))
    return cos_t, sin_t


def _pick_tile(T, cap):
    t = min(T, cap)
    while T % t:
        t //= 2
    return t


def _layer(x2d, mod, pk, B, S, lam_init, ctx, final_norm, fn, xg):
    T = B * S
    rows_per_mod = T // mod.shape[0]
    tm = _pick_tile(rows_per_mod, 1024)
    a16 = _inproj(x2d, mod, pk['n1'], pk['w16'], BF16, rows_per_mod, tm, N_A16 // 3)
    a32 = _inproj(x2d, mod, pk['n1'], pk['w32'], F32, rows_per_mod, tm, N_A32)

    L = min(MLSTM_CHUNK, S)
    sm = a32[:, A32_SM:A32_SM + 4 * H_M]
    grow = jnp.transpose(sm.reshape(B, S // L, L, 4, H_M), (4, 3, 0, 1, 2))
    if ctx is None:
        c0 = jnp.zeros((B, 2, H_M, DK_M, DV_M), F32)
        n0 = jnp.zeros((B, 2, H_M, 1, DK_M), F32)
        m0 = jnp.zeros((B, 2, H_M, 1, LANES), F32)
        s0 = jnp.zeros((B, 2, H_G, DK_G, DV_G), F32)
        attn_ctx = None
    else:
        c0 = ctx['C']
        n0 = ctx['n'][:, :, :, None, :]
        m0 = jnp.broadcast_to(ctx['m'][:, :, :, None, None], (B, 2, H_M, 1, LANES))
        s0 = ctx['S']
        attn_ctx = (ctx['k'], ctx['v'], ctx['layer'], ctx['cos'], ctx['sin'])
    ym, c_f, n_f, m_f = _mlstm(a16, grow, pk['brow'], c0, n0, m0, pk['hn_m'], B, S)
    yd = _attn(a16, a32, pk['lamv'], pk['hn_d'], B, S, lam_init, attn_ctx)
    yg, s_f = _gla(a32, a16, pk['wup'], pk['bup'], s0, pk['hn_g'], B, S)
    yc = _conv(a16, pk['wdw'], pk['ln_g'], pk['ln_b'], B, S)
    x1, h3, route = _merge(x2d, mod, a16, ym, yd, yg, yc, pk['wb'], pk['wo'], pk['n2'], pk['wr'],
                           rows_per_mod, _pick_tile(rows_per_mod, 512))
    dest, tile_e = _route_plan(route, T)
    tmd = _pick_tile(T, 1024)
    moe_rows, n_rows = _moe_rows(T)
    if xg is None:
        xg = jnp.zeros((n_rows, TOK_SUB, LANES), F32)
    xg = _moe_dispatch(h3, dest.reshape(2, T // tmd, 1, tmd), xg, tmd)
    y_grouped = _moe_ffn(xg, tile_e + pk['expert_base'], pk['w1'], pk['w3'], pk['w2'], moe_rows)
    tmc = _pick_tile(rows_per_mod, 512)
    x2 = _moe_combine(y_grouped, dest.reshape(2, T // tmc, 1, tmc), route, x1, mod, fn,
                      rows_per_mod, tmc, final_norm)
    state = None
    if ctx is None:
        state = (a32[:, A32_DK:A32_DK + H_D * 2 * DQK_D].reshape(B, S, H_D, 2 * DQK_D),
                 a32[:, A32_DV:A32_DV + H_D * DV_D].reshape(B, S, H_D, DV_D),
                 c_f, n_f[:, :, :, 0, :], m_f[:, :, :, 0, 0], s_f)
    return x2, state, xg


def kernel(x_prompt, x_sample, c, cache_diff_k, cache_diff_v, state_mlstm_C, state_mlstm_n, state_mlstm_m, state_gla_S, c_ctx, w_mod, b_mod, norm1, w_in, b_m_i, b_m_f, lam_q1, lam_k1, lam_q2, lam_k2, w_gla_up, b_gla_gate, w_dw, conv_ln_g, conv_ln_b, hnorm_m, hnorm_d, hnorm_g, w_branch, w_out, norm2, w_group_router, w_expert_router, w_e1, w_e3, w_e2, final_norm):
    Bp, Sp, _ = x_prompt.shape
    Bs, Ss, _ = x_sample.shape
    P = cache_diff_k.shape[2]
    n_cond = 8 * ((1 + Bs + 7) // 8)
    cond = jnp.concatenate([c_ctx[None, :], c, jnp.zeros((n_cond - 1 - Bs, D_MODEL), F32)], axis=0)
    mod_all = _modulation(cond, w_mod, b_mod).reshape(DEPTH, n_cond, N_MOD, D_MODEL)
    cos_t, sin_t = _rope_tables(Ss)
    ck4 = cache_diff_k.reshape(Bs, DEPTH, P, H_D * 2 * DQK_D)
    cv4 = cache_diff_v.reshape(Bs, DEPTH, P, H_D * DV_D)
    fn = final_norm.reshape(1, D_MODEL)
    yp = x_prompt.reshape(Bp * Sp, D_MODEL)
    ys = x_sample.reshape(Bs * Ss, D_MODEL)
    states = []
    xg_p = xg_s = None
    for l in range(DEPTH):
        p = {'w_in': w_in[l], 'b_m_i': b_m_i[l], 'b_m_f': b_m_f[l], 'lam_q1': lam_q1[l],
             'lam_k1': lam_k1[l], 'lam_q2': lam_q2[l], 'lam_k2': lam_k2[l],
             'w_gla_up': w_gla_up[l], 'b_gla_gate': b_gla_gate[l], 'w_dw': w_dw[l],
             'conv_ln_g': conv_ln_g[l], 'conv_ln_b': conv_ln_b[l], 'hnorm_m': hnorm_m[l],
             'hnorm_d': hnorm_d[l], 'hnorm_g': hnorm_g[l], 'w_branch': w_branch[l],
             'w_out': w_out[l], 'norm1': norm1[l], 'norm2': norm2[l],
             'w_group_router': w_group_router[l], 'w_expert_router': w_expert_router[l]}
        pk = _pack_layer_params(p)
        pk.update(w1=w_e1.reshape(DEPTH * N_EXPERTS, D_MODEL, D_EXPERT),
                  w3=w_e3.reshape(DEPTH * N_EXPERTS, D_MODEL, D_EXPERT),
                  w2=w_e2.reshape(DEPTH * N_EXPERTS, D_EXPERT, D_MODEL), expert_base=l * N_EXPERTS)
        lam_init = 0.8 - 0.6 * math.exp(-0.3 * l)
        last = l == DEPTH - 1
        yp, st, xg_p = _layer(yp, mod_all[l, 0:1], pk, Bp, Sp, lam_init, None, last, fn, xg_p)
        states.append(st)
        ctx = {'k': ck4, 'v': cv4, 'layer': l, 'cos': cos_t, 'sin': sin_t,
               'C': state_mlstm_C[:, l], 'n': state_mlstm_n[:, l], 'm': state_mlstm_m[:, l],
               'S': state_gla_S[:, l]}
        ys, _, xg_s = _layer(ys, mod_all[l, 1:1 + Bs], pk, Bs, Ss, lam_init, ctx, last, fn, xg_s)
    stack = lambda i: jnp.stack([s[i] for s in states], axis=1)
    return (yp.reshape(Bp, Sp, D_MODEL), ys.reshape(Bs, Ss, D_MODEL),
            stack(0), stack(1), stack(2), stack(3), stack(4), stack(5))
```

```python
import functools
import math

import jax
import jax.numpy as jnp
from jax import lax
from jax.experimental import pallas as pl
from jax.experimental.pallas import tpu as pltpu

F32 = jnp.float32
BF16 = jnp.bfloat16

D_MODEL = 1024
DEPTH = 2
GRID_W = 64
BRANCH_W = 512
N_BRANCH = 4
H_M, DK_M, DV_M = 4, 128, 128
H_D, DQK_D, DV_D = 4, 64, 128
H_G, DK_G, DV_G = 4, 64, 128
GATE_RANK = 16
GLA_TAU = 16.0
CONV_W = 31
N_GROUPS, EXPERTS_PER_GROUP, D_EXPERT = 4, 4, 512
N_EXPERTS = N_GROUPS * EXPERTS_PER_GROUP
ROPE_BASE = 10000.0
EPS = 1e-6
N_MOD = 6

LANES = 128
VMEM_LIMIT = 48 * 1024 * 1024

A16_MQ, A16_MK, A16_MV, A16_MO = 0, 512, 1024, 1536
A16_GATE, A16_GR, A16_CA, A16_CB = 2048, 6144, 6656, 7168
A16_GQK, A16_GV, A16_DQ = 7680, 8192, 8704
N_A16 = 9216
A32_DK, A32_DV, A32_SM = 0, 512, 1024
N_A32 = 1152
SM_MI, SM_MF, SM_GA = 0, 8, 16

MLSTM_CHUNK = 128
GLA_CHUNK = 64
GLA_SUB = 16
GLA_EXP_CLAMP = 80.0
CONV_ROWS = 128
CONV_PAD = 16
TOK_SUB = D_MODEL // LANES
ROW_DMA_UNROLL = 8


def _cparams(*sem):
    return pltpu.CompilerParams(dimension_semantics=sem, vmem_limit_bytes=VMEM_LIMIT)


def _log_sigmoid(x):
    return jnp.minimum(x, 0.0) - jnp.log1p(jnp.exp(-jnp.abs(x)))


def _sigmoid(x):
    return 0.5 * jnp.tanh(0.5 * x) + 0.5


def _dot(a, b):
    return jnp.dot(a, b, preferred_element_type=F32)


def _dot_nt(a, b):
    return lax.dot_general(a, b, (((1,), (1,)), ((), ())), preferred_element_type=F32)


def _dot_tn(a, b):
    return lax.dot_general(a, b, (((0,), (0,)), ((), ())), preferred_element_type=F32)


def _mod_kernel(c_ref, w_ref, b_ref, o_ref):
    c = c_ref[...]
    a = (c * _sigmoid(c)).astype(BF16)
    o_ref[...] = _dot(a, w_ref[...].astype(BF16)) + b_ref[...]


def _modulation(cond, w_mod, b_mod):
    R = cond.shape[0]
    tn = 512
    nmod = N_MOD * D_MODEL
    return pl.pallas_call(
        _mod_kernel,
        grid=(DEPTH, nmod // tn),
        in_specs=[
            pl.BlockSpec((R, D_MODEL), lambda l, j: (0, 0)),
            pl.BlockSpec((None, D_MODEL, tn), lambda l, j: (l, 0, j)),
            pl.BlockSpec((None, 1, tn), lambda l, j: (l, 0, j)),
        ],
        out_specs=pl.BlockSpec((None, R, tn), lambda l, j: (l, 0, j)),
        out_shape=jax.ShapeDtypeStruct((DEPTH, R, nmod), F32),
        compiler_params=_cparams("parallel", "parallel"),
        name="adaln_mod",
    )(cond, w_mod, b_mod.reshape(DEPTH, 1, nmod))


def _inproj_kernel(x_ref, mod_ref, g_ref, w16_ref, w32_ref, o16_ref, o32_ref, h_ref, *, n16):
    j = pl.program_id(1)

    @pl.when(j == 0)
    def _():
        x = x_ref[...]
        y = x * lax.rsqrt(jnp.mean(x * x, axis=-1, keepdims=True) + EPS) * g_ref[...]
        h_ref[...] = (y * (1.0 + mod_ref[1:2, :]) + mod_ref[0:1, :]).astype(BF16)

    @pl.when(j < n16)
    def _():
        o16_ref[...] = _dot(h_ref[...], w16_ref[...]).astype(o16_ref.dtype)

    @pl.when(j == n16)
    def _():
        o32_ref[...] = _dot(h_ref[...], w32_ref[...])


def _inproj(x2d, mod, g, w16, w32, rows_per_mod, tm, n16):
    T = x2d.shape[0]
    tn = w16.shape[1] // n16
    n32 = w32.shape[1]
    col16 = lambda j: jnp.minimum(j, n16 - 1)
    return pl.pallas_call(
        functools.partial(_inproj_kernel, n16=n16),
        grid=(T // tm, n16 + 1),
        in_specs=[
            pl.BlockSpec((tm, D_MODEL), lambda i, j: (i, 0)),
            pl.BlockSpec((None, N_MOD, D_MODEL), lambda i, j: ((i * tm) // rows_per_mod, 0, 0)),
            pl.BlockSpec((1, D_MODEL), lambda i, j: (0, 0)),
            pl.BlockSpec((D_MODEL, tn), lambda i, j: (0, col16(j))),
            pl.BlockSpec((D_MODEL, n32), lambda i, j: (0, 0)),
        ],
        out_specs=[
            pl.BlockSpec((tm, tn), lambda i, j: (i, col16(j))),
            pl.BlockSpec((tm, n32), lambda i, j: (i, 0)),
        ],
        out_shape=[
            jax.ShapeDtypeStruct((T, w16.shape[1]), BF16),
            jax.ShapeDtypeStruct((T, n32), F32),
        ],
        scratch_shapes=[pltpu.VMEM((tm, D_MODEL), BF16)],
        compiler_params=_cparams("parallel", "arbitrary"),
        name="norm_inproj",
    )(x2d, mod, g, w16, w32)


def _mlstm_local(c, q_ref, k_ref, v_ref, gate_ref, pr_ref, bb_ref, mb_ref, kv_ref, rp_ref, L):
    scale = DK_M ** -0.5
    ti = lax.broadcasted_iota(jnp.int32, (L, L), 0)
    si = lax.broadcasted_iota(jnp.int32, (L, L), 1)
    sub = lax.broadcasted_iota(jnp.int32, (8, LANES), 0)
    rows = pl.ds(pl.multiple_of(c * L, L), L)
    q = q_ref[rows, :]
    v_ext = jnp.concatenate([v_ref[rows, :], jnp.ones((L, LANES), BF16)], axis=1)
    k_t = k_ref[rows, :].astype(F32).T
    qk = _dot(q, k_t.astype(BF16)) * scale
    for d in range(2):
        rev = d == 1
        mask = (si >= ti) if rev else (si <= ti)
        i_row = gate_ref[d, 0, pl.ds(c, 1), :]
        f_row = gate_ref[d, 1, pl.ds(c, 1), :]
        b_row = gate_ref[d, 2, pl.ds(c, 1), :]
        b_col = jnp.sum(jnp.where(mask, f_row, 0.0), axis=1, keepdims=True)
        log_d = jnp.where(mask, b_col + (i_row - b_row), -jnp.inf)
        m_loc = jnp.max(log_d, axis=1, keepdims=True)
        smat = qk * jnp.exp(log_d - m_loc)
        pr_ref[d, rows, :] = _dot(smat.astype(BF16), v_ext)
        bb_ref[d, rows, :] = jnp.broadcast_to(b_col, (L, LANES))
        mb_ref[d, rows, :] = jnp.broadcast_to(m_loc, (L, LANES))
        b_last = jnp.sum(f_row, axis=1, keepdims=True)
        ls_row = b_last - b_row + i_row
        m2 = jnp.max(ls_row, axis=1, keepdims=True)
        kw_t = (k_t * jnp.exp(ls_row - m2)).astype(BF16)
        kv_ref[d, c] = scale * _dot(kw_t, v_ext)
        rp_ref[d, c] = jnp.where(sub == 0, b_last, m2)


def _mlstm_carry(c, d, carry, q_ref, pr_ref, bb_ref, mb_ref, kv_ref, rp_ref, h_ref, L):
    cn, m = carry
    two = lambda x: jnp.concatenate([x, x], axis=1)
    rows = pl.ds(pl.multiple_of(c * L, L), L)
    bb = bb_ref[d, rows, :]
    mb = mb_ref[d, rows, :]
    m_t = jnp.maximum(bb + m, mb)
    a_int = jnp.exp(bb + m - m_t)
    e_loc = jnp.exp(mb - m_t)
    nd = two(a_int) * _dot(q_ref[rows, :], cn.astype(BF16)) + two(e_loc) * pr_ref[d, rows, :]
    h_ref[d, rows, :] = nd[:, :DV_M] / jnp.maximum(jnp.abs(nd[:, DV_M:]), jnp.exp(-m_t))
    rp = rp_ref[d, c]
    b_last, m2 = rp[0:1, :], rp[1:2, :]
    m_new = jnp.maximum(b_last + m, m2)
    a_c = jnp.exp(b_last + m - m_new)
    e2 = jnp.exp(m2 - m_new)
    return two(a_c) * cn + two(e2) * kv_ref[d, c], m_new


def _mlstm_gate_rows(gr_ref, br_ref, gate_ref, L):
    ui = lax.broadcasted_iota(jnp.int32, (L, L), 0)
    si = lax.broadcasted_iota(jnp.int32, (L, L), 1)
    for d in range(2):
        src = ((ui >= si) if d == 1 else (ui <= si)).astype(BF16)
        f = _log_sigmoid(gr_ref[2 + d] + br_ref[2 + d])
        f_hi = f.astype(BF16)
        f_r1 = f - f_hi.astype(F32)
        f_mid = f_r1.astype(BF16)
        f_lo = (f_r1 - f_mid.astype(F32)).astype(BF16)
        gate_ref[d, 0] = gr_ref[d] + br_ref[d]
        gate_ref[d, 1] = f
        gate_ref[d, 2] = _dot(f_hi, src) + _dot(f_mid, src) + _dot(f_lo, src)


def _mlstm_kernel(q_ref, k_ref, v_ref, og_ref, gr_ref, br_ref, c0_ref, n0_ref, m0_ref, hn_ref,
                  y_ref, c_out_ref, n_out_ref, m_out_ref,
                  gate_ref, pr_ref, bb_ref, mb_ref, kv_ref, rp_ref, h_ref, *, L, S):
    nch = S // L
    _mlstm_gate_rows(gr_ref, br_ref, gate_ref, L)

    def local(ci, carry):
        _mlstm_local(ci, q_ref, k_ref, v_ref, gate_ref, pr_ref, bb_ref, mb_ref, kv_ref, rp_ref, L)
        return carry

    lax.fori_loop(0, nch, local, 0, unroll=4)
    step = functools.partial(_mlstm_carry, q_ref=q_ref, pr_ref=pr_ref, bb_ref=bb_ref, mb_ref=mb_ref,
                             kv_ref=kv_ref, rp_ref=rp_ref, h_ref=h_ref, L=L)

    def body(ci, carry):
        return step(ci, 0, carry[0]), step(nch - 1 - ci, 1, carry[1])

    def init(d):
        n_rep = jnp.broadcast_to(n0_ref[d], (DK_M, DK_M)).T
        return jnp.concatenate([c0_ref[d], n_rep], axis=1), m0_ref[d]

    fin = lax.fori_loop(0, nch, body, (init(0), init(1)), unroll=2)
    for d in range(2):
        cn, m = fin[d]
        c_out_ref[d] = cn[:, :DV_M]
        n_out_ref[d] = cn[:, DV_M:].T[0:1, :]
        m_out_ref[d] = m

    hm = h_ref[0] + h_ref[1]
    y = hm * lax.rsqrt(jnp.mean(hm * hm, axis=-1, keepdims=True) + EPS) * hn_ref[...]
    y_ref[...] = (y * _sigmoid(og_ref[...].astype(F32))).astype(y_ref.dtype)


def _mlstm(a16, grow, brow, c0, n0, m0, hnorm, B, S):
    L = min(MLSTM_CHUNK, S)
    nch = S // L
    cb = lambda off: off // LANES
    kern = functools.partial(_mlstm_kernel, L=L, S=S)
    return pl.pallas_call(
        kern,
        grid=(B, H_M),
        in_specs=[
            pl.BlockSpec((S, LANES), lambda b, h: (b, cb(A16_MQ) + h)),
            pl.BlockSpec((S, LANES), lambda b, h: (b, cb(A16_MK) + h)),
            pl.BlockSpec((S, LANES), lambda b, h: (b, cb(A16_MV) + h)),
            pl.BlockSpec((S, LANES), lambda b, h: (b, cb(A16_MO) + h)),
            pl.BlockSpec((None, 4, None, nch, L), lambda b, h: (h, 0, b, 0, 0)),
            pl.BlockSpec((None, 4, 1, 1), lambda b, h: (h, 0, 0, 0)),
            pl.BlockSpec((None, 2, None, DK_M, DV_M), lambda b, h: (b, 0, h, 0, 0)),
            pl.BlockSpec((None, 2, None, 1, DK_M), lambda b, h: (b, 0, h, 0, 0)),
            pl.BlockSpec((None, 2, None, 1, LANES), lambda b, h: (b, 0, h, 0, 0)),
            pl.BlockSpec((1, LANES), lambda b, h: (0, h)),
        ],
        out_specs=[
            pl.BlockSpec((S, LANES), lambda b, h: (b, h)),
            pl.BlockSpec((None, 2, None, DK_M, DV_M), lambda b, h: (b, 0, h, 0, 0)),
            pl.BlockSpec((None, 2, None, 1, DK_M), lambda b, h: (b, 0, h, 0, 0)),
            pl.BlockSpec((None, 2, None, 1, LANES), lambda b, h: (b, 0, h, 0, 0)),
        ],
        out_shape=[
            jax.ShapeDtypeStruct((B * S, BRANCH_W), BF16),
            jax.ShapeDtypeStruct((B, 2, H_M, DK_M, DV_M), F32),
            jax.ShapeDtypeStruct((B, 2, H_M, 1, DK_M), F32),
            jax.ShapeDtypeStruct((B, 2, H_M, 1, LANES), F32),
        ],
        scratch_shapes=[pltpu.VMEM((2, 3, nch, L), F32),
                        pltpu.VMEM((2, S, 2 * DV_M), F32), pltpu.VMEM((2, S, LANES), F32),
                        pltpu.VMEM((2, S, LANES), F32), pltpu.VMEM((2, nch, DK_M, 2 * DV_M), F32),
                        pltpu.VMEM((2, nch, 8, LANES), F32), pltpu.VMEM((2, S, DV_M), F32)],
        compiler_params=_cparams("parallel", "parallel"),
        name="mlstm",
    )(a16, a16, a16, a16, grow, brow, c0, n0, m0, hnorm)


def _gla_local(c, q2_ref, k2_ref, v_ref, la_ref, oa_ref, qt_ref, u_ref, dec_ref, L):
    nb = L // GLA_SUB
    ti = lax.broadcasted_iota(jnp.int32, (L, L), 0)
    si = lax.broadcasted_iota(jnp.int32, (L, L), 1)
    row_blk = lax.broadcasted_iota(jnp.int32, (L, LANES), 0) // GLA_SUB
    lo_half = lax.broadcasted_iota(jnp.int32, (L, LANES), 1) < DK_G
    eye = (lax.broadcasted_iota(jnp.int32, (DK_G, LANES), 0)
           == lax.broadcasted_iota(jnp.int32, (DK_G, LANES), 1))
    rows = pl.ds(pl.multiple_of(c * L, L), L)
    q2 = q2_ref[rows, :]
    k2 = k2_ref[rows, :]
    v = v_ref[rows, :]
    row = lax.broadcasted_iota(jnp.int32, (L, LANES), 0)
    for d in range(2):
        rev = d == 1
        mask = (si >= ti) if rev else (si <= ti)
        g2 = la_ref[d, rows, :]
        step = 1
        while step < L:
            if rev:
                g2 = g2 + jnp.where(row < L - step, pltpu.roll(g2, L - step, 0), 0.0)
            else:
                g2 = g2 + jnp.where(row >= step, pltpu.roll(g2, step, 0), 0.0)
            step *= 2
        qt_ref[d, rows, :] = (q2 * jnp.exp(g2))[:, :DK_G].astype(BF16)
        a_parts, b_parts = [], []
        for p in range(nb // 2):
            ia, ib = 2 * p, 2 * p + 1
            ra = ia * GLA_SUB + (GLA_SUB - 1 if rev else 0)
            rb = ib * GLA_SUB + (GLA_SUB - 1 if rev else 0)
            ref2 = jnp.where(lo_half, g2[ra:ra + 1, :], g2[rb:rb + 1, :])
            blk = jnp.where(lo_half, ia, ib)
            in_blk = row_blk == blk
            key_ok = (row_blk >= blk) if rev else (row_blk <= blk)
            a_parts.append(jnp.where(in_blk, q2 * jnp.exp(jnp.minimum(g2 - ref2, 0.0)), 0.0))
            b_parts.append(
                jnp.where(key_ok, k2 * jnp.exp(jnp.minimum(ref2 - g2, GLA_EXP_CLAMP)), 0.0))
        a_big = jnp.concatenate(a_parts, axis=1).astype(BF16)
        b_big = jnp.concatenate(b_parts, axis=1).astype(BF16)
        att = jnp.where(mask, _dot_nt(a_big, b_big), 0.0)
        oa_ref[d, rows, :] = _dot(att.astype(BF16), v)
        gl_row = 0 if rev else L - 1
        glast = g2[gl_row:gl_row + 1, :]
        kd = (k2 * jnp.exp(glast - g2))[:, :DK_G]
        u_ref[d, c] = _dot_tn(kd.astype(BF16), v)
        glast_col = jnp.sum(jnp.where(eye, glast, 0.0), axis=1, keepdims=True)
        dec_ref[d, c] = jnp.broadcast_to(jnp.exp(glast_col), (DK_G, DV_G))


def _gla_kernel(qk_ref, v_ref, sm_ref, wup_ref, bup_ref, s0_ref, gr_ref, hn_ref,
                y_ref, s_out_ref, la_ref, q2_ref, k2_ref, oa_ref, oi_ref, qt_ref, u_ref, dec_ref,
                *, L, S):
    nch = S // L
    sm = sm_ref[...].astype(BF16)
    for d in range(2):
        la_ref[d] = _log_sigmoid(_dot(sm, wup_ref[d]) + bup_ref[d]) * (1.0 / GLA_TAU)
    qk = qk_ref[...].astype(F32)
    qk_sw = pltpu.roll(qk, DK_G, 1)
    lo_half = lax.broadcasted_iota(jnp.int32, qk.shape, 1) < DK_G
    q2_ref[...] = jnp.where(lo_half, qk, qk_sw) * (DK_G ** -0.5)
    k2_ref[...] = jnp.where(lo_half, qk_sw, qk)

    def local(ci, carry):
        _gla_local(ci, q2_ref, k2_ref, v_ref, la_ref, oa_ref, qt_ref, u_ref, dec_ref, L)
        return carry

    lax.fori_loop(0, nch, local, 0, unroll=8)

    def body(ci, carry):
        out = []
        for d, c in ((0, ci), (1, nch - 1 - ci)):
            rows = pl.ds(pl.multiple_of(c * L, L), L)
            st = carry[d]
            oi_ref[d, rows, :] = _dot(qt_ref[d, rows, :], st.astype(BF16))
            out.append(dec_ref[d, c] * st + u_ref[d, c])
        return tuple(out)

    st_f, st_b = lax.fori_loop(0, nch, body, (s0_ref[0], s0_ref[1]), unroll=4)
    s_out_ref[0] = st_f
    s_out_ref[1] = st_b

    og = (oa_ref[0] + oi_ref[0]) + (oa_ref[1] + oi_ref[1])
    y = og * lax.rsqrt(jnp.mean(og * og, axis=-1, keepdims=True) + EPS) * hn_ref[...]
    gr = gr_ref[...].astype(F32)
    y_ref[...] = (y * (gr * _sigmoid(gr))).astype(y_ref.dtype)


def _gla(a32, a16, wup, bup, s0, hnorm, B, S):
    L = min(GLA_CHUNK, S)
    nch = S // L
    cb = lambda off: off // LANES
    kern = functools.partial(_gla_kernel, L=L, S=S)
    return pl.pallas_call(
        kern,
        grid=(B, H_G),
        in_specs=[
            pl.BlockSpec((S, LANES), lambda b, h: (b, cb(A16_GQK) + h)),
            pl.BlockSpec((S, LANES), lambda b, h: (b, cb(A16_GV) + h)),
            pl.BlockSpec((S, LANES), lambda b, h: (b, cb(A32_SM))),
            pl.BlockSpec((None, 2, LANES, LANES), lambda b, h: (h, 0, 0, 0)),
            pl.BlockSpec((None, 2, 1, LANES), lambda b, h: (h, 0, 0, 0)),
            pl.BlockSpec((None, 2, None, DK_G, DV_G), lambda b, h: (b, 0, h, 0, 0)),
            pl.BlockSpec((S, LANES), lambda b, h: (b, cb(A16_GR) + h)),
            pl.BlockSpec((1, LANES), lambda b, h: (0, h)),
        ],
        out_specs=[
            pl.BlockSpec((S, LANES), lambda b, h: (b, h)),
            pl.BlockSpec((None, 2, None, DK_G, DV_G), lambda b, h: (b, 0, h, 0, 0)),
        ],
        out_shape=[
            jax.ShapeDtypeStruct((B * S, BRANCH_W), BF16),
            jax.ShapeDtypeStruct((B, 2, H_G, DK_G, DV_G), F32),
        ],
        scratch_shapes=[pltpu.VMEM((2, S, LANES), F32), pltpu.VMEM((S, LANES), F32),
                        pltpu.VMEM((S, LANES), F32), pltpu.VMEM((2, S, DV_G), F32),
                        pltpu.VMEM((2, S, DV_G), F32), pltpu.VMEM((2, S, DK_G), BF16),
                        pltpu.VMEM((2, nch, DK_G, DV_G), F32), pltpu.VMEM((2, nch, DK_G, DV_G), F32)],
        compiler_params=_cparams("parallel", "parallel"),
        name="gla",
    )(a16, a16, a32, wup, bup, s0, a16, hnorm)


def _rope(x, cos, sin_signed):
    lane = lax.broadcasted_iota(jnp.int32, x.shape, 1)
    first = (lane % DQK_D) < (DQK_D // 2)
    partner = jnp.where(first, pltpu.roll(x, LANES - DQK_D // 2, 1), pltpu.roll(x, DQK_D // 2, 1))
    return x * cos + partner * sin_signed


def _attn_kernel(*refs, S, P, TQ, lam_init, has_ctx):
    if has_ctx:
        (q_ref, k_ref, v_ref, ck_ref, cv_ref, cos_ref, sin_ref, lam_ref, hn_ref,
         y_ref, kk_ref, vv_ref) = refs
    else:
        q_ref, k_ref, v_ref, lam_ref, hn_ref, y_ref, kk_ref, vv_ref = refs
    qi = pl.program_id(2)

    @pl.when(qi == 0)
    def _():
        k = k_ref[...]
        if has_ctx:
            k = _rope(k, cos_ref[...], sin_ref[...])
            kk_ref[S:S + P, :] = ck_ref[...].astype(BF16)
            vv_ref[S:S + P, :] = cv_ref[...].astype(BF16)
        kk_ref[0:S, :] = k.astype(BF16)
        vv_ref[0:S, :] = v_ref[...].astype(BF16)

    kk = kk_ref[...]
    vv = vv_ref[...]
    lv = lam_ref[...]
    lam = (jnp.exp(jnp.sum(lv[0:1, :] * lv[1:2, :], axis=-1, keepdims=True))
           - jnp.exp(jnp.sum(lv[2:3, :] * lv[3:4, :], axis=-1, keepdims=True)) + lam_init)
    q = q_ref[...].astype(F32)
    if has_ctx:
        r0 = pl.multiple_of(qi * TQ, TQ)
        q = _rope(q, cos_ref[pl.ds(r0, TQ), :], sin_ref[pl.ds(r0, TQ), :])
    q = q * (DQK_D ** -0.5 * math.log2(math.e))
    lane = lax.broadcasted_iota(jnp.int32, q.shape, 1)
    es, ls = [], []
    for comp in range(2):
        sel = (lane < DQK_D) if comp == 0 else (lane >= DQK_D)
        s = _dot_nt(jnp.where(sel, q, 0.0).astype(BF16), kk)
        e = jnp.exp2(s - jnp.max(s, axis=-1, keepdims=True))
        es.append(e)
        ls.append(jnp.sum(e, axis=-1, keepdims=True))
    w = es[0] - es[1] * (lam * ls[0] / ls[1])
    o = _dot(w.astype(BF16), vv) * (1.0 / ls[0])
    y = o * lax.rsqrt(jnp.mean(o * o, axis=-1, keepdims=True) + EPS) * hn_ref[...]
    y_ref[...] = (y * (1.0 - lam_init)).astype(y_ref.dtype)


def _attn(a16, a32, lamv, hnorm, B, S, lam_init, ctx=None):
    TQ = min(256, S)
    nq = S // TQ
    has_ctx = ctx is not None
    P = ctx[0].shape[2] if has_ctx else 0
    cb = lambda off: off // LANES
    kern = functools.partial(_attn_kernel, S=S, P=P, TQ=TQ, lam_init=lam_init, has_ctx=has_ctx)
    in_specs = [
        pl.BlockSpec((TQ, LANES), lambda b, h, i: (b * nq + i, cb(A16_DQ) + h)),
        pl.BlockSpec((S, LANES), lambda b, h, i: (b, cb(A32_DK) + h)),
        pl.BlockSpec((S, LANES), lambda b, h, i: (b, cb(A32_DV) + h)),
    ]
    args = [a16, a32, a32]
    if has_ctx:
        ck, cv, layer, cos, sin = ctx
        in_specs += [
            pl.BlockSpec((None, None, P, LANES), lambda b, h, i: (b, layer, 0, h)),
            pl.BlockSpec((None, None, P, LANES), lambda b, h, i: (b, layer, 0, h)),
            pl.BlockSpec((S, LANES), lambda b, h, i: (0, 0)),
            pl.BlockSpec((S, LANES), lambda b, h, i: (0, 0)),
        ]
        args += [ck, cv, cos, sin]
    in_specs += [
        pl.BlockSpec((4, DQK_D), lambda b, h, i: (0, 0)),
        pl.BlockSpec((1, LANES), lambda b, h, i: (0, h)),
    ]
    args += [lamv, hnorm]
    return pl.pallas_call(
        kern,
        grid=(B, H_D, nq),
        in_specs=in_specs,
        out_specs=pl.BlockSpec((TQ, LANES), lambda b, h, i: (b * nq + i, h)),
        out_shape=jax.ShapeDtypeStruct((B * S, BRANCH_W), BF16),
        scratch_shapes=[pltpu.VMEM((S + P, LANES), BF16), pltpu.VMEM((S + P, LANES), BF16)],
        compiler_params=_cparams("parallel", "parallel", "arbitrary"),
        name="diff_attn",
    )(*args)


def _conv_kernel(ca_ref, cb_ref, w_ref, g_ref, b_ref, y_ref, pad_ref, cv_ref, *, S):
    ca = ca_ref[...].astype(F32)
    cbv = cb_ref[...].astype(F32)
    zeros = jnp.zeros((CONV_PAD, BRANCH_W), F32)
    pad_ref[0:CONV_PAD, :] = zeros
    pad_ref[CONV_PAD + S:2 * CONV_PAD + S, :] = zeros
    pad_ref[CONV_PAD:CONV_PAD + S, :] = ca * _sigmoid(cbv)
    off = CONV_PAD - CONV_W // 2

    win_rows = CONV_ROWS + 2 * CONV_PAD

    def body(i, carry):
        base = pl.multiple_of(i * CONV_ROWS, CONV_ROWS)
        for lb in range(BRANCH_W // LANES):
            cols = slice(lb * LANES, (lb + 1) * LANES)
            win = pad_ref[pl.ds(base, win_rows), cols]
            acc = jnp.zeros((CONV_ROWS, LANES), F32)
            for r in range(8):
                rolled = win if r == 0 else pltpu.roll(win, win_rows - r, 0)
                for a in range(2 * CONV_PAD // 8):
                    j = 8 * a + r - off
                    if 0 <= j < CONV_W:
                        acc = acc + rolled[8 * a:8 * a + CONV_ROWS, :] * w_ref[j:j + 1, cols]
            cv_ref[:, cols] = acc
        acc = cv_ref[...]
        mu = jnp.mean(acc, axis=-1, keepdims=True)
        xc = acc - mu
        yn = xc * lax.rsqrt(jnp.mean(xc * xc, axis=-1, keepdims=True) + EPS) * g_ref[...] + b_ref[...]
        y_ref[pl.ds(base, CONV_ROWS), :] = (yn * _sigmoid(yn)).astype(y_ref.dtype)
        return carry

    lax.fori_loop(0, S // CONV_ROWS, body, 0)


def _conv(a16, w_dw, ln_g, ln_b, B, S):
    cb = lambda off: off // BRANCH_W
    kern = functools.partial(_conv_kernel, S=S)
    return pl.pallas_call(
        kern,
        grid=(B,),
        in_specs=[
            pl.BlockSpec((S, BRANCH_W), lambda b: (b, cb(A16_CA))),
            pl.BlockSpec((S, BRANCH_W), lambda b: (b, cb(A16_CB))),
            pl.BlockSpec((CONV_W + 1, BRANCH_W), lambda b: (0, 0)),
            pl.BlockSpec((1, BRANCH_W), lambda b: (0, 0)),
            pl.BlockSpec((1, BRANCH_W), lambda b: (0, 0)),
        ],
        out_specs=pl.BlockSpec((S, BRANCH_W), lambda b: (b, 0)),
        out_shape=jax.ShapeDtypeStruct((B * S, BRANCH_W), BF16),
        scratch_shapes=[pltpu.VMEM((S + 2 * CONV_PAD, BRANCH_W), F32),
                        pltpu.VMEM((CONV_ROWS, BRANCH_W), F32)],
        compiler_params=_cparams("parallel"),
        name="glu_conv_ln",
    )(a16, a16, w_dw, ln_g, ln_b)


def _merge_kernel(x_ref, mod_ref, ym_ref, yd_ref, yg_ref, yc_ref, g0_ref, g1_ref, g2_ref, g3_ref,
                  wb_ref, wo_ref, n2_ref, wr_ref, x1_ref, h2_ref, route_ref):
    ys = (ym_ref, yd_ref, yg_ref, yc_ref)
    gs = (g0_ref, g1_ref, g2_ref, g3_ref)
    merged = None
    for nbr in range(N_BRANCH):
        br = _dot(ys[nbr][...], wb_ref[nbr])
        term = _sigmoid(gs[nbr][...].astype(F32)) * br
        merged = term if merged is None else merged + term
    out = _dot(merged.astype(BF16), wo_ref[...])
    x1 = x_ref[...] + mod_ref[2:3, :] * out
    x1_ref[...] = x1
    y = x1 * lax.rsqrt(jnp.mean(x1 * x1, axis=-1, keepdims=True) + EPS) * n2_ref[...]
    h2 = y * (1.0 + mod_ref[4:5, :]) + mod_ref[3:4, :]
    h2_ref[...] = h2.reshape(h2_ref.shape)
    wr = wr_ref[...]
    h_hi, w_hi = h2.astype(BF16), wr.astype(BF16)
    h_lo = (h2 - h_hi.astype(F32)).astype(BF16)
    w_lo = (wr - w_hi.astype(F32)).astype(BF16)
    logits = _dot(h_hi, w_hi) + (_dot(h_hi, w_lo) + _dot(h_lo, w_hi))
    lane = lax.broadcasted_iota(jnp.int32, logits.shape, 1)
    neg = -jnp.inf
    big = jnp.int32(LANES)
    is_g = lane < N_GROUPS
    gl = jnp.where(is_g, logits, neg)
    gmax = jnp.max(gl, axis=-1, keepdims=True)
    gidx = jnp.min(jnp.where(is_g & (gl == gmax), lane, big), axis=-1, keepdims=True)
    g_p = 1.0 / jnp.sum(jnp.where(is_g, jnp.exp(gl - gmax), 0.0), axis=-1, keepdims=True)
    e_lane = lane - N_GROUPS
    in_grp = (e_lane >= 0) & (e_lane < N_EXPERTS) & ((e_lane // EXPERTS_PER_GROUP) == gidx)
    el = jnp.where(in_grp, logits, neg)
    v1 = jnp.max(el, axis=-1, keepdims=True)
    i1 = jnp.min(jnp.where(in_grp & (el == v1), lane, big), axis=-1, keepdims=True)
    el2 = jnp.where(lane == i1, neg, el)
    v2 = jnp.max(el2, axis=-1, keepdims=True)
    i2 = jnp.min(jnp.where(in_grp & (lane != i1) & (el2 == v2), lane, big), axis=-1, keepdims=True)
    e2 = jnp.exp(v2 - v1)
    w1 = g_p / (1.0 + e2)
    w2 = g_p * e2 / (1.0 + e2)
    id1 = (i1 - N_GROUPS).astype(F32)
    id2 = (i2 - N_GROUPS).astype(F32)
    route_ref[...] = jnp.where(lane == 0, id1, jnp.where(lane == 1, id2,
                               jnp.where(lane == 2, w1, jnp.where(lane == 3, w2, 0.0))))


def _merge(x2d, mod, a16, ym, yd, yg, yc, wb, wo, n2, wr, rows_per_mod, tm):
    T = x2d.shape[0]
    gcb = A16_GATE // D_MODEL
    row = lambda i: (i, 0)
    return pl.pallas_call(
        _merge_kernel,
        grid=(T // tm,),
        in_specs=[
            pl.BlockSpec((tm, D_MODEL), row),
            pl.BlockSpec((None, N_MOD, D_MODEL), lambda i: ((i * tm) // rows_per_mod, 0, 0)),
            pl.BlockSpec((tm, BRANCH_W), row),
            pl.BlockSpec((tm, BRANCH_W), row),
            pl.BlockSpec((tm, BRANCH_W), row),
            pl.BlockSpec((tm, BRANCH_W), row),
            pl.BlockSpec((tm, D_MODEL), lambda i: (i, gcb + 0)),
            pl.BlockSpec((tm, D_MODEL), lambda i: (i, gcb + 1)),
            pl.BlockSpec((tm, D_MODEL), lambda i: (i, gcb + 2)),
            pl.BlockSpec((tm, D_MODEL), lambda i: (i, gcb + 3)),
            pl.BlockSpec((N_BRANCH, BRANCH_W, D_MODEL), lambda i: (0, 0, 0)),
            pl.BlockSpec((D_MODEL, D_MODEL), lambda i: (0, 0)),
            pl.BlockSpec((1, D_MODEL), lambda i: (0, 0)),
            pl.BlockSpec((D_MODEL, LANES), lambda i: (0, 0)),
        ],
        out_specs=[
            pl.BlockSpec((tm, D_MODEL), row),
            pl.BlockSpec((tm, TOK_SUB, LANES), lambda i: (i, 0, 0)),
            pl.BlockSpec((tm, LANES), row),
        ],
        out_shape=[
            jax.ShapeDtypeStruct((T, D_MODEL), F32),
            jax.ShapeDtypeStruct((T, TOK_SUB, LANES), F32),
            jax.ShapeDtypeStruct((T, LANES), F32),
        ],
        compiler_params=_cparams("parallel"),
        name="merge_outproj_route",
    )(x2d, mod, ym, yd, yg, yc, a16, a16, a16, a16, wb, wo, n2, wr)


def _gather_rows(idx_ref, src_hbm, dst, sem, n):
    def body(j, carry):
        for u in range(ROW_DMA_UNROLL):
            r = j * ROW_DMA_UNROLL + u
            pltpu.make_async_copy(src_hbm.at[idx_ref[0, r]], dst.at[r], sem).start(priority=u % 2)
        return carry

    lax.fori_loop(0, n // ROW_DMA_UNROLL, body, 0)


def _scatter_rows(idx_ref, src, dst_hbm, sem, n):
    def body(j, carry):
        for u in range(ROW_DMA_UNROLL):
            r = j * ROW_DMA_UNROLL + u
            pltpu.make_async_copy(src.at[r], dst_hbm.at[idx_ref[0, r]], sem).start(priority=u % 2)
        return carry

    lax.fori_loop(0, n // ROW_DMA_UNROLL, body, 0)


def _wait_rows(buf, sem):
    pltpu.make_async_copy(buf, buf, sem).wait()


def _moe_dispatch_kernel(d0_ref, d1_ref, h_ref, xg_in, xg_out, sem, *, tm):
    del xg_in
    _scatter_rows(d0_ref, h_ref, xg_out, sem.at[0], tm)
    _scatter_rows(d1_ref, h_ref, xg_out, sem.at[1], tm)
    _wait_rows(h_ref, sem.at[0])
    _wait_rows(h_ref, sem.at[1])


def _moe_dispatch(h3, dest, xg_init, tm):
    T = h3.shape[0]
    kern = functools.partial(_moe_dispatch_kernel, tm=tm)
    return pl.pallas_call(
        kern,
        grid=(T // tm,),
        in_specs=[
            pl.BlockSpec((None, None, 1, tm), lambda i: (0, i, 0, 0), memory_space=pltpu.SMEM),
            pl.BlockSpec((None, None, 1, tm), lambda i: (1, i, 0, 0), memory_space=pltpu.SMEM),
            pl.BlockSpec((tm, TOK_SUB, LANES), lambda i: (i, 0, 0)),
            pl.BlockSpec(memory_space=pl.ANY),
        ],
        out_specs=pl.BlockSpec(memory_space=pl.ANY),
        out_shape=jax.ShapeDtypeStruct(xg_init.shape, F32),
        input_output_aliases={3: 0},
        scratch_shapes=[pltpu.SemaphoreType.DMA((2,))],
        compiler_params=_cparams("arbitrary"),
        name="moe_dispatch",
    )(dest, dest, h3, xg_init)


def _moe_ffn_kernel(te_ref, x_ref, w1_ref, w3_ref, w2_ref, o_ref):
    del te_ref
    x = x_ref[...].reshape(x_ref.shape[0], D_MODEL).astype(BF16)
    a = _dot(x, w1_ref[...].astype(BF16))
    b = _dot(x, w3_ref[...].astype(BF16))
    s = (a * _sigmoid(a)) * b
    y = _dot(s.astype(BF16), w2_ref[...].astype(BF16))
    o_ref[...] = y.reshape(o_ref.shape)


def _moe_ffn(xg, tile_e, w1, w3, w2, rows):
    ntiles = xg.shape[0] // rows
    tile = pl.BlockSpec((rows, TOK_SUB, LANES), lambda i, te: (i, 0, 0))
    return pl.pallas_call(
        _moe_ffn_kernel,
        grid_spec=pltpu.PrefetchScalarGridSpec(
            num_scalar_prefetch=1,
            grid=(ntiles,),
            in_specs=[
                tile,
                pl.BlockSpec((None, D_MODEL, D_EXPERT), lambda i, te: (te[i], 0, 0)),
                pl.BlockSpec((None, D_MODEL, D_EXPERT), lambda i, te: (te[i], 0, 0)),
                pl.BlockSpec((None, D_EXPERT, D_MODEL), lambda i, te: (te[i], 0, 0)),
            ],
            out_specs=tile,
        ),
        out_shape=jax.ShapeDtypeStruct(xg.shape, F32),
        compiler_params=_cparams("parallel"),
        name="moe_grouped_experts",
    )(tile_e, xg, w1, w3, w2)


def _moe_combine_kernel(d0_ref, d1_ref, y_hbm, route_ref, x1_ref, mod_ref, fn_ref, o_ref,
                        ga, gb, sem, *, tm, final_norm):
    _gather_rows(d0_ref, y_hbm, ga, sem.at[0], tm)
    _gather_rows(d1_ref, y_hbm, gb, sem.at[1], tm)
    rt = route_ref[...]
    _wait_rows(ga, sem.at[0])
    _wait_rows(gb, sem.at[1])
    y = rt[:, 2:3] * ga[...].reshape(tm, D_MODEL) + rt[:, 3:4] * gb[...].reshape(tm, D_MODEL)
    x2 = x1_ref[...] + mod_ref[5:6, :] * y
    if final_norm:
        x2 = x2 * lax.rsqrt(jnp.mean(x2 * x2, axis=-1, keepdims=True) + EPS) * fn_ref[...]
    o_ref[...] = x2


def _moe_combine(yg, dest, route, x1, mod, fn, rows_per_mod, tm, final_norm):
    T = x1.shape[0]
    kern = functools.partial(_moe_combine_kernel, tm=tm, final_norm=final_norm)
    return pl.pallas_call(
        kern,
        grid=(T // tm,),
        in_specs=[
            pl.BlockSpec((None, None, 1, tm), lambda i: (0, i, 0, 0), memory_space=pltpu.SMEM),
            pl.BlockSpec((None, None, 1, tm), lambda i: (1, i, 0, 0), memory_space=pltpu.SMEM),
            pl.BlockSpec(memory_space=pl.ANY),
            pl.BlockSpec((tm, LANES), lambda i: (i, 0)),
            pl.BlockSpec((tm, D_MODEL), lambda i: (i, 0)),
            pl.BlockSpec((None, N_MOD, D_MODEL), lambda i: ((i * tm) // rows_per_mod, 0, 0)),
            pl.BlockSpec((1, D_MODEL), lambda i: (0, 0)),
        ],
        out_specs=pl.BlockSpec((tm, D_MODEL), lambda i: (i, 0)),
        out_shape=jax.ShapeDtypeStruct((T, D_MODEL), F32),
        scratch_shapes=[pltpu.VMEM((tm, TOK_SUB, LANES), F32), pltpu.VMEM((tm, TOK_SUB, LANES), F32),
                        pltpu.SemaphoreType.DMA((2,))],
        compiler_params=_cparams("arbitrary"),
        name="moe_combine",
    )(dest, dest, yg, route, x1, mod, fn)


def _moe_rows(T):
    rows = 512 if T >= 4096 else 256
    ntiles = (2 * T + N_EXPERTS * (rows - 1) + rows - 1) // rows
    return rows, ntiles * rows


def _route_plan(route, T):
    tile, n_rows = _moe_rows(T)
    ef = route[:, 0:2].astype(jnp.int32).reshape(-1)
    oh = (ef[:, None] == jnp.arange(N_EXPERTS, dtype=jnp.int32)[None, :]).astype(jnp.int32)
    csum = jnp.cumsum(oh, axis=0)
    rank = jnp.sum((csum - oh) * oh, axis=1)
    counts = csum[-1]
    padded = ((counts + tile - 1) // tile) * tile
    seg_end = jnp.cumsum(padded)
    dest = jnp.sum(oh * (seg_end - padded)[None, :], axis=1) + rank
    tile_row = jnp.arange(n_rows // tile, dtype=jnp.int32) * tile
    tile_e = jnp.minimum(jnp.sum((tile_row[:, None] >= seg_end[None, :]).astype(jnp.int32), axis=1),
                         N_EXPERTS - 1)
    return jnp.transpose(dest.reshape(T, 2)), tile_e


def _split_w_in(w):
    sizes = (H_M * DK_M, H_M * DK_M, H_M * DV_M, H_M * DV_M, 2 * H_M, 2 * H_M,
             H_D * 2 * DQK_D, H_D * 2 * DQK_D, H_D * DV_D,
             H_G * DK_G, H_G * DK_G, H_G * DV_G, 2 * GATE_RANK, H_G * DV_G,
             BRANCH_W, BRANCH_W, N_BRANCH * D_MODEL)
    outs, acc = [], 0
    for s in sizes:
        outs.append(w[:, acc:acc + s])
        acc += s
    return outs


def _pack_layer_params(p):
    (m_q, m_k, m_v, m_o, m_i, m_f, d_q, d_k, d_v, g_q, g_k, g_v, g_a, g_r, c_a, c_b, gate) = \
        _split_w_in(p['w_in'])
    gqk = jnp.concatenate([g_q.reshape(D_MODEL, H_G, DK_G), g_k.reshape(D_MODEL, H_G, DK_G)],
                          axis=2).reshape(D_MODEL, 2 * H_G * DK_G)
    small = jnp.concatenate(
        [m_i, m_f, g_a, jnp.zeros((D_MODEL, LANES - 4 * H_M - 2 * GATE_RANK), F32)], axis=1)
    w16 = jnp.concatenate([m_q, m_k, m_v, m_o, gate, g_r, c_a, c_b, gqk, g_v, d_q],
                          axis=1).astype(BF16)
    w32 = jnp.concatenate([d_k, d_v, small], axis=1).astype(BF16)
    bi = p['b_m_i'].reshape(2, H_M)
    bf = p['b_m_f'].reshape(2, H_M)
    bcol = jnp.stack([bi[0], bi[1], bf[0], bf[1]], axis=-1)
    wup = p['w_gla_up'].reshape(2, GATE_RANK, H_G, DK_G)
    wup_pad = jnp.zeros((H_G, 2, LANES, LANES), F32)
    bup = p['b_gla_gate'].reshape(2, H_G, DK_G)
    for d in range(2):
        blk = jnp.transpose(wup[d], (1, 0, 2))
        blk = jnp.concatenate([blk, blk], axis=-1)
        r0 = SM_GA + d * GATE_RANK
        wup_pad = wup_pad.at[:, d, r0:r0 + GATE_RANK, :].set(blk)
    bup2 = jnp.transpose(jnp.concatenate([bup, bup], axis=-1), (1, 0, 2))[:, :, None, :]
    wr = jnp.concatenate([p['w_group_router'], p['w_expert_router'],
                          jnp.zeros((D_MODEL, LANES - N_GROUPS - N_EXPERTS), F32)], axis=1)
    return dict(
        w16=w16, w32=w32, brow=bcol.reshape(H_M, 4, 1, 1),
        wup=wup_pad.astype(BF16), bup=bup2,
        wdw=jnp.concatenate([p['w_dw'], jnp.zeros((1, BRANCH_W), F32)], axis=0),
        ln_g=p['conv_ln_g'].reshape(1, BRANCH_W), ln_b=p['conv_ln_b'].reshape(1, BRANCH_W),
        hn_m=p['hnorm_m'].reshape(1, BRANCH_W), hn_d=p['hnorm_d'].reshape(1, BRANCH_W),
        hn_g=p['hnorm_g'].reshape(1, BRANCH_W),
        lamv=jnp.stack([p['lam_q1'], p['lam_k1'], p['lam_q2'], p['lam_k2']], axis=0),
        wb=p['w_branch'].astype(BF16), wo=p['w_out'].astype(BF16),
        n1=p['norm1'].reshape(1, D_MODEL), n2=p['norm2'].reshape(1, D_MODEL), wr=wr,
    )


def _rope_tables(S):
    rows = S // GRID_W
    r, col = jnp.meshgrid(jnp.arange(rows, dtype=F32), jnp.arange(GRID_W, dtype=F32), indexing='ij')
    r, col = r.reshape(-1), col.reshape(-1)
    n_freq = DQK_D // 4
    inv = ROPE_BASE ** (-jnp.arange(n_freq, dtype=F32) / n_freq)
    ang = jnp.concatenate([r[:, None] * inv, col[:, None] * inv], axis=-1)
    cos, sin = jnp.cos(ang), jnp.sin(ang)
    cos_t = jnp.tile(cos, (1, LANES // (DQK_D // 2)))
    sin_t = jnp.tile(jnp.concatenate([-sin, sin], axis=-1), (1, LANES // DQK_D))
    return cos_t, sin_t


def _pick_tile(T, cap):
    t = min(T, cap)
    while T % t:
        t //= 2
    return t


def _layer(x2d, mod, pk, B, S, lam_init, ctx, final_norm, fn, xg):
    T = B * S
    rows_per_mod = T // mod.shape[0]
    tm = _pick_tile(rows_per_mod, 1024)
    a16, a32 = _inproj(x2d, mod, pk['n1'], pk['w16'], pk['w32'], rows_per_mod, tm, 4)

    L = min(MLSTM_CHUNK, S)
    sm = a32[:, A32_SM:A32_SM + 4 * H_M]
    grow = jnp.transpose(sm.reshape(B, S // L, L, 4, H_M), (4, 3, 0, 1, 2))
    if ctx is None:
        c0 = jnp.zeros((B, 2, H_M, DK_M, DV_M), F32)
        n0 = jnp.zeros((B, 2, H_M, 1, DK_M), F32)
        m0 = jnp.zeros((B, 2, H_M, 1, LANES), F32)
        s0 = jnp.zeros((B, 2, H_G, DK_G, DV_G), F32)
        attn_ctx = None
    else:
        c0 = ctx['C']
        n0 = ctx['n'][:, :, :, None, :]
        m0 = jnp.broadcast_to(ctx['m'][:, :, :, None, None], (B, 2, H_M, 1, LANES))
        s0 = ctx['S']
        attn_ctx = (ctx['k'], ctx['v'], ctx['layer'], ctx['cos'], ctx['sin'])
    ym, c_f, n_f, m_f = _mlstm(a16, grow, pk['brow'], c0, n0, m0, pk['hn_m'], B, S)
    yd = _attn(a16, a32, pk['lamv'], pk['hn_d'], B, S, lam_init, attn_ctx)
    yg, s_f = _gla(a32, a16, pk['wup'], pk['bup'], s0, pk['hn_g'], B, S)
    yc = _conv(a16, pk['wdw'], pk['ln_g'], pk['ln_b'], B, S)
    x1, h3, route = _merge(x2d, mod, a16, ym, yd, yg, yc, pk['wb'], pk['wo'], pk['n2'], pk['wr'],
                           rows_per_mod, _pick_tile(rows_per_mod, 512))
    dest, tile_e = _route_plan(route, T)
    tmd = _pick_tile(T, 1024)
    moe_rows, n_rows = _moe_rows(T)
    if xg is None:
        xg = jnp.zeros((n_rows, TOK_SUB, LANES), F32)
    xg = _moe_dispatch(h3, dest.reshape(2, T // tmd, 1, tmd), xg, tmd)
    y_grouped = _moe_ffn(xg, tile_e + pk['expert_base'], pk['w1'], pk['w3'], pk['w2'], moe_rows)
    tmc = _pick_tile(rows_per_mod, 512)
    x2 = _moe_combine(y_grouped, dest.reshape(2, T // tmc, 1, tmc), route, x1, mod, fn,
                      rows_per_mod, tmc, final_norm)
    state = None
    if ctx is None:
        state = (a32[:, A32_DK:A32_DK + H_D * 2 * DQK_D].reshape(B, S, H_D, 2 * DQK_D),
                 a32[:, A32_DV:A32_DV + H_D * DV_D].reshape(B, S, H_D, DV_D),
                 c_f, n_f[:, :, :, 0, :], m_f[:, :, :, 0, 0], s_f)
    return x2, state, xg


def kernel(x_prompt, x_sample, c, cache_diff_k, cache_diff_v, state_mlstm_C, state_mlstm_n, state_mlstm_m, state_gla_S, c_ctx, w_mod, b_mod, norm1, w_in, b_m_i, b_m_f, lam_q1, lam_k1, lam_q2, lam_k2, w_gla_up, b_gla_gate, w_dw, conv_ln_g, conv_ln_b, hnorm_m, hnorm_d, hnorm_g, w_branch, w_out, norm2, w_group_router, w_expert_router, w_e1, w_e3, w_e2, final_norm):
    Bp, Sp, _ = x_prompt.shape
    Bs, Ss, _ = x_sample.shape
    P = cache_diff_k.shape[2]
    n_cond = 8 * ((1 + Bs + 7) // 8)
    cond = jnp.concatenate([c_ctx[None, :], c, jnp.zeros((n_cond - 1 - Bs, D_MODEL), F32)], axis=0)
    mod_all = _modulation(cond, w_mod, b_mod).reshape(DEPTH, n_cond, N_MOD, D_MODEL)
    cos_t, sin_t = _rope_tables(Ss)
    ck4 = cache_diff_k.reshape(Bs, DEPTH, P, H_D * 2 * DQK_D)
    cv4 = cache_diff_v.reshape(Bs, DEPTH, P, H_D * DV_D)
    fn = final_norm.reshape(1, D_MODEL)
    yp = x_prompt.reshape(Bp * Sp, D_MODEL)
    ys = x_sample.reshape(Bs * Ss, D_MODEL)
    states = []
    xg_p = xg_s = None
    for l in range(DEPTH):
        p = {'w_in': w_in[l], 'b_m_i': b_m_i[l], 'b_m_f': b_m_f[l], 'lam_q1': lam_q1[l],
             'lam_k1': lam_k1[l], 'lam_q2': lam_q2[l], 'lam_k2': lam_k2[l],
             'w_gla_up': w_gla_up[l], 'b_gla_gate': b_gla_gate[l], 'w_dw': w_dw[l],
             'conv_ln_g': conv_ln_g[l], 'conv_ln_b': conv_ln_b[l], 'hnorm_m': hnorm_m[l],
             'hnorm_d': hnorm_d[l], 'hnorm_g': hnorm_g[l], 'w_branch': w_branch[l],
             'w_out': w_out[l], 'norm1': norm1[l], 'norm2': norm2[l],
             'w_group_router': w_group_router[l], 'w_expert_router': w_expert_router[l]}
        pk = _pack_layer_params(p)
        pk.update(w1=w_e1.reshape(DEPTH * N_EXPERTS, D_MODEL, D_EXPERT),
                  w3=w_e3.reshape(DEPTH * N_EXPERTS, D_MODEL, D_EXPERT),
                  w2=w_e2.reshape(DEPTH * N_EXPERTS, D_EXPERT, D_MODEL), expert_base=l * N_EXPERTS)
        lam_init = 0.8 - 0.6 * math.exp(-0.3 * l)
        last = l == DEPTH - 1
        yp, st, xg_p = _layer(yp, mod_all[l, 0:1], pk, Bp, Sp, lam_init, None, last, fn, xg_p)
        states.append(st)
        ctx = {'k': ck4, 'v': cv4, 'layer': l, 'cos': cos_t, 'sin': sin_t,
               'C': state_mlstm_C[:, l], 'n': state_mlstm_n[:, l], 'm': state_mlstm_m[:, l],
               'S': state_gla_S[:, l]}
        ys, _, xg_s = _layer(ys, mod_all[l, 1:1 + Bs], pk, Bs, Ss, lam_init, ctx, last, fn, xg_s)
    stack = lambda i: jnp.stack([s[i] for s in states], axis=1)
    return (yp.reshape(Bp, Sp, D_MODEL), ys.reshape(Bs, Ss, D_MODEL),
            stack(0), stack(1), stack(2), stack(3), stack(4), stack(5))
```
